```python
import math
import jax, jax.numpy as jnp
from jax import lax
import numpy as np

D_MODEL = 1024
BATCH = 8
SEQ = 8192
DEPTH = 4

HEAD_DIM = 64
N_HEADS = 8
N_KV_HEADS = 2
GQA_GROUP = N_HEADS // N_KV_HEADS
ATTN_WIDTH = N_HEADS * HEAD_DIM
KV_WIDTH = N_KV_HEADS * HEAD_DIM
CONV_WIDTH = D_MODEL - ATTN_WIDTH
MIX_WIDTH = ATTN_WIDTH + CONV_WIDTH
IN_WIDTH = ATTN_WIDTH + 2 * KV_WIDTH + 2 * CONV_WIDTH
WINDOW = 128
BLOCK = 128
CONV_KERNEL = 31
NUM_BUCKETS = 32
MAX_DISTANCE = 128
D_FF = 4 * D_MODEL
EPS = 1e-6
NEG = -1e30

kernel_name = "hymba_conformer_swa_sink_hybrid"


def rms_norm(x, g):
    xf = x.astype(jnp.float32)
    y = xf * lax.rsqrt(jnp.mean(xf * xf, axis=-1, keepdims=True) + EPS)
    return (y * g.astype(jnp.float32)).astype(x.dtype)


def layer_norm(x, g, b):
    xf = x.astype(jnp.float32)
    mu = jnp.mean(xf, axis=-1, keepdims=True)
    var = jnp.mean(jnp.square(xf - mu), axis=-1, keepdims=True)
    y = (xf - mu) * lax.rsqrt(var + EPS)
    return (y * g.astype(jnp.float32) + b.astype(jnp.float32)).astype(x.dtype)


def t5_causal_bucket(n):
    n = np.asarray(n)
    max_exact = NUM_BUCKETS // 2
    large = max_exact + (np.log(np.maximum(n, 1) / max_exact)
                         / np.log(MAX_DISTANCE / max_exact)
                         * (NUM_BUCKETS - max_exact)).astype(np.int32)
    large = np.minimum(large, NUM_BUCKETS - 1)
    return np.where(n < max_exact, n, large).astype(np.int32)


def band_structure(seq_len):
    n_blocks = seq_len // BLOCK
    qi = np.arange(BLOCK)[:, None]
    kj = np.arange(2 * BLOCK)[None, :]
    dist = qi + BLOCK - kj
    in_window = (dist >= 0) & (dist < WINDOW)
    bucket = t5_causal_bucket(np.clip(dist, 0, None))
    k_abs = (np.arange(n_blocks)[:, None] - 1) * BLOCK + np.arange(2 * BLOCK)[None, :]
    valid = in_window[None, :, :] & (k_abs >= 0)[:, None, :]
    return bucket, valid


def sliding_window_gqa(q, k, v, sinks, rel_bias):
    B, T = q.shape[0], q.shape[1]
    nb = T // BLOCK
    bucket, valid = band_structure(T)
    bias = jnp.transpose(rel_bias[bucket].astype(jnp.float32), (2, 0, 1))
    bias = bias.reshape(N_KV_HEADS, GQA_GROUP, BLOCK, 2 * BLOCK)

    qb = q.reshape(B, nb, BLOCK, N_KV_HEADS, GQA_GROUP, HEAD_DIM)

    def band(t):
        tp = jnp.pad(t, ((0, 0), (BLOCK, 0), (0, 0), (0, 0)))
        prev = tp[:, :T].reshape(B, nb, BLOCK, N_KV_HEADS, HEAD_DIM)
        cur = t.reshape(B, nb, BLOCK, N_KV_HEADS, HEAD_DIM)
        return jnp.concatenate([prev, cur], axis=2)

    kb, vb = band(k), band(v)
    scale = 1.0 / math.sqrt(HEAD_DIM)
    s = jnp.einsum('bnqkgd,bnskd->bnkgqs', qb, kb).astype(jnp.float32) * scale
    s = s + bias[None, None]
    s = jnp.where(jnp.asarray(valid)[None, :, None, None, :, :], s, NEG)
    sink = sinks.astype(jnp.float32).reshape(N_KV_HEADS, GQA_GROUP)[None, None, :, :, None, None]
    m = jnp.maximum(jnp.max(s, axis=-1, keepdims=True), sink)
    p = jnp.exp(s - m)
    p = p / (jnp.sum(p, axis=-1, keepdims=True) + jnp.exp(sink - m))
    o = jnp.einsum('bnkgqs,bnskd->bnqkgd', p.astype(v.dtype), vb)
    return o.reshape(B, T, ATTN_WIDTH)


def conformer_conv(u, gate, conv_w, conv_b, ln_g, ln_b):
    h = u * jax.nn.sigmoid(gate)
    hp = jnp.pad(h, ((0, 0), (CONV_KERNEL - 1, 0), (0, 0)))
    y = lax.conv_general_dilated(
        hp, conv_w[:, None, :].astype(h.dtype), window_strides=(1,), padding='VALID',
        dimension_numbers=('NWC', 'WIO', 'NWC'), feature_group_count=CONV_WIDTH)
    y = y + conv_b
    y = layer_norm(y, ln_g, ln_b)
    return jax.nn.silu(y)


def _fwd_setup_inputs(seed: int = 0) -> dict:
    key = jax.random.key(seed)
    ks = jax.random.split(key, 20)
    f32 = jnp.float32
    nrm = lambda k, shape, s: (jax.random.normal(k, shape, f32) * s).astype(f32)
    return {
        "x": nrm(ks[0], (BATCH, SEQ, D_MODEL), 1.0),
        "rel_bias": nrm(ks[1], (NUM_BUCKETS, N_HEADS), 0.5),
        "norm_mix_g": 1.0 + nrm(ks[2], (DEPTH, D_MODEL), 0.02),
        "w_in": nrm(ks[3], (DEPTH, D_MODEL, IN_WIDTH), D_MODEL ** -0.5),
        "q_norm_g": 1.0 + nrm(ks[4], (DEPTH, HEAD_DIM), 0.02),
        "k_norm_g": 1.0 + nrm(ks[5], (DEPTH, HEAD_DIM), 0.02),
        "sinks": nrm(ks[6], (DEPTH, N_HEADS), 0.5),
        "conv_w": nrm(ks[7], (DEPTH, CONV_KERNEL, CONV_WIDTH), CONV_KERNEL ** -0.5),
        "conv_b": nrm(ks[8], (DEPTH, CONV_WIDTH), 0.02),
        "conv_ln_g": 1.0 + nrm(ks[9], (DEPTH, CONV_WIDTH), 0.02),
        "conv_ln_b": nrm(ks[10], (DEPTH, CONV_WIDTH), 0.02),
        "attn_out_g": 1.0 + nrm(ks[11], (DEPTH, ATTN_WIDTH), 0.02),
        "conv_out_g": 1.0 + nrm(ks[12], (DEPTH, CONV_WIDTH), 0.02),
        "w_out": nrm(ks[13], (DEPTH, MIX_WIDTH, D_MODEL), (MIX_WIDTH * 2 * DEPTH) ** -0.5),
        "norm_mlp_g": 1.0 + nrm(ks[14], (DEPTH, D_MODEL), 0.02),
        "w_mlp_up": nrm(ks[15], (DEPTH, D_MODEL, D_FF), D_MODEL ** -0.5),
        "w_mlp_down": nrm(ks[16], (DEPTH, D_FF, D_MODEL), (D_FF * 2 * DEPTH) ** -0.5),
    }


def _fwd_reference(x, rel_bias, norm_mix_g, w_in, q_norm_g, k_norm_g, sinks, conv_w, conv_b,
              conv_ln_g, conv_ln_b, attn_out_g, conv_out_g, w_out, norm_mlp_g,
              w_mlp_up, w_mlp_down):
    B, T, _ = x.shape
    splits = [ATTN_WIDTH, ATTN_WIDTH + KV_WIDTH, ATTN_WIDTH + 2 * KV_WIDTH,
              ATTN_WIDTH + 2 * KV_WIDTH + CONV_WIDTH]
    for l in range(DEPTH):
        h = rms_norm(x, norm_mix_g[l])
        z = h @ w_in[l]
        q, k, v, u, gate = jnp.split(z, splits, axis=-1)
        q = rms_norm(q.reshape(B, T, N_HEADS, HEAD_DIM), q_norm_g[l])
        k = rms_norm(k.reshape(B, T, N_KV_HEADS, HEAD_DIM), k_norm_g[l])
        v = v.reshape(B, T, N_KV_HEADS, HEAD_DIM)
        a = sliding_window_gqa(q, k, v, sinks[l], rel_bias)
        c = conformer_conv(u, gate, conv_w[l], conv_b[l], conv_ln_g[l], conv_ln_b[l])
        mix = jnp.concatenate([rms_norm(a, attn_out_g[l]), rms_norm(c, conv_out_g[l])], axis=-1)
        x = x + mix @ w_out[l]
        h = rms_norm(x, norm_mlp_g[l])
        x = x + jnp.square(jax.nn.relu(h @ w_mlp_up[l])) @ w_mlp_down[l]
    return x


import jax as _jax
import jax.numpy as _jnp

TWIN_FORMAT = 'train_step'
FWD_PARAMS = ['x', 'rel_bias', 'norm_mix_g', 'w_in', 'q_norm_g', 'k_norm_g', 'sinks', 'conv_w', 'conv_b', 'conv_ln_g', 'conv_ln_b', 'attn_out_g', 'conv_out_g', 'w_out', 'norm_mlp_g', 'w_mlp_up', 'w_mlp_down']
TWIN_WEIGHTS = ['rel_bias', 'norm_mix_g', 'w_in', 'q_norm_g', 'k_norm_g', 'sinks', 'conv_w', 'conv_b', 'conv_ln_g', 'conv_ln_b', 'attn_out_g', 'conv_out_g', 'w_out', 'norm_mlp_g', 'w_mlp_up', 'w_mlp_down']
TWIN_DIFF_INPUT = 'x'
TWIN_INPUTS = ['x', 'rel_bias', 'norm_mix_g', 'w_in', 'q_norm_g', 'k_norm_g', 'sinks', 'conv_w', 'conv_b', 'conv_ln_g', 'conv_ln_b', 'attn_out_g', 'conv_out_g', 'w_out', 'norm_mlp_g', 'w_mlp_up', 'w_mlp_down', 'loss_target', 'm_rel_bias', 'm_norm_mix_g', 'm_w_in', 'm_q_norm_g', 'm_k_norm_g', 'm_sinks', 'm_conv_w', 'm_conv_b', 'm_conv_ln_g', 'm_conv_ln_b', 'm_attn_out_g', 'm_conv_out_g', 'm_w_out', 'm_norm_mlp_g', 'm_w_mlp_up', 'm_w_mlp_down', 'v_rel_bias', 'v_norm_mix_g', 'v_w_in', 'v_q_norm_g', 'v_k_norm_g', 'v_sinks', 'v_conv_w', 'v_conv_b', 'v_conv_ln_g', 'v_conv_ln_b', 'v_attn_out_g', 'v_conv_out_g', 'v_w_out', 'v_norm_mlp_g', 'v_w_mlp_up', 'v_w_mlp_down']
TWIN_OUTPUTS = ['loss', 'grad_x', 'grad_rel_bias', 'grad_norm_mix_g', 'grad_w_in', 'grad_q_norm_g', 'grad_k_norm_g', 'grad_sinks', 'grad_conv_w', 'grad_conv_b', 'grad_conv_ln_g', 'grad_conv_ln_b', 'grad_attn_out_g', 'grad_conv_out_g', 'grad_w_out', 'grad_norm_mlp_g', 'grad_w_mlp_up', 'grad_w_mlp_down', 'delta_rel_bias', 'delta_norm_mix_g', 'delta_w_in', 'delta_q_norm_g', 'delta_k_norm_g', 'delta_sinks', 'delta_conv_w', 'delta_conv_b', 'delta_conv_ln_g', 'delta_conv_ln_b', 'delta_attn_out_g', 'delta_conv_out_g', 'delta_w_out', 'delta_norm_mlp_g', 'delta_w_mlp_up', 'delta_w_mlp_down', 'new_m_rel_bias', 'new_m_norm_mix_g', 'new_m_w_in', 'new_m_q_norm_g', 'new_m_k_norm_g', 'new_m_sinks', 'new_m_conv_w', 'new_m_conv_b', 'new_m_conv_ln_g', 'new_m_conv_ln_b', 'new_m_attn_out_g', 'new_m_conv_out_g', 'new_m_w_out', 'new_m_norm_mlp_g', 'new_m_w_mlp_up', 'new_m_w_mlp_down', 'new_v_rel_bias', 'new_v_norm_mix_g', 'new_v_w_in', 'new_v_q_norm_g', 'new_v_k_norm_g', 'new_v_sinks', 'new_v_conv_w', 'new_v_conv_b', 'new_v_conv_ln_g', 'new_v_conv_ln_b', 'new_v_attn_out_g', 'new_v_conv_out_g', 'new_v_w_out', 'new_v_norm_mlp_g', 'new_v_w_mlp_up', 'new_v_w_mlp_down']
TWIN_LEAF_KINDS = {'loss': 'loss', 'grad_x': 'grad_x', 'grad_rel_bias': 'grad_w', 'grad_norm_mix_g': 'grad_w', 'grad_w_in': 'grad_w', 'grad_q_norm_g': 'grad_w', 'grad_k_norm_g': 'grad_w', 'grad_sinks': 'grad_w', 'grad_conv_w': 'grad_w', 'grad_conv_b': 'grad_w', 'grad_conv_ln_g': 'grad_w', 'grad_conv_ln_b': 'grad_w', 'grad_attn_out_g': 'grad_w', 'grad_conv_out_g': 'grad_w', 'grad_w_out': 'grad_w', 'grad_norm_mlp_g': 'grad_w', 'grad_w_mlp_up': 'grad_w', 'grad_w_mlp_down': 'grad_w', 'delta_rel_bias': 'delta_w', 'delta_norm_mix_g': 'delta_w', 'delta_w_in': 'delta_w', 'delta_q_norm_g': 'delta_w', 'delta_k_norm_g': 'delta_w', 'delta_sinks': 'delta_w', 'delta_conv_w': 'delta_w', 'delta_conv_b': 'delta_w', 'delta_conv_ln_g': 'delta_w', 'delta_conv_ln_b': 'delta_w', 'delta_attn_out_g': 'delta_w', 'delta_conv_out_g': 'delta_w', 'delta_w_out': 'delta_w', 'delta_norm_mlp_g': 'delta_w', 'delta_w_mlp_up': 'delta_w', 'delta_w_mlp_down': 'delta_w', 'new_m_rel_bias': 'new_m', 'new_m_norm_mix_g': 'new_m', 'new_m_w_in': 'new_m', 'new_m_q_norm_g': 'new_m', 'new_m_k_norm_g': 'new_m', 'new_m_sinks': 'new_m', 'new_m_conv_w': 'new_m', 'new_m_conv_b': 'new_m', 'new_m_conv_ln_g': 'new_m', 'new_m_conv_ln_b': 'new_m', 'new_m_attn_out_g': 'new_m', 'new_m_conv_out_g': 'new_m', 'new_m_w_out': 'new_m', 'new_m_norm_mlp_g': 'new_m', 'new_m_w_mlp_up': 'new_m', 'new_m_w_mlp_down': 'new_m', 'new_v_rel_bias': 'new_v', 'new_v_norm_mix_g': 'new_v', 'new_v_w_in': 'new_v', 'new_v_q_norm_g': 'new_v', 'new_v_k_norm_g': 'new_v', 'new_v_sinks': 'new_v', 'new_v_conv_w': 'new_v', 'new_v_conv_b': 'new_v', 'new_v_conv_ln_g': 'new_v', 'new_v_conv_ln_b': 'new_v', 'new_v_attn_out_g': 'new_v', 'new_v_conv_out_g': 'new_v', 'new_v_w_out': 'new_v', 'new_v_norm_mlp_g': 'new_v', 'new_v_w_mlp_up': 'new_v', 'new_v_w_mlp_down': 'new_v'}


def _forward(args):
    return _fwd_reference(*[args[k] for k in FWD_PARAMS])


def _output_shape():
    def fwd():
        inp = _fwd_setup_inputs(0)
        return _fwd_reference(*[inp[k] for k in FWD_PARAMS])
    out = _jax.eval_shape(fwd)
    return out.shape, out.dtype

N_MICROBATCH = 1
ADAM_LR = 0.001
ADAM_B1 = 0.9
ADAM_B2 = 0.999
ADAM_EPS = 1e-08
ADAM_WD = 0.01
ADAM_STEP = 10
PER_EXAMPLE_BATCH_AXIS = {'x': 0, 'loss_target': 0}
SHARED_INPUTS = []
_WEIGHT_DTYPES = {'rel_bias': _jnp.float32, 'norm_mix_g': _jnp.float32, 'w_in': _jnp.float32, 'q_norm_g': _jnp.float32, 'k_norm_g': _jnp.float32, 'sinks': _jnp.float32, 'conv_w': _jnp.float32, 'conv_b': _jnp.float32, 'conv_ln_g': _jnp.float32, 'conv_ln_b': _jnp.float32, 'attn_out_g': _jnp.float32, 'conv_out_g': _jnp.float32, 'w_out': _jnp.float32, 'norm_mlp_g': _jnp.float32, 'w_mlp_up': _jnp.float32, 'w_mlp_down': _jnp.float32}
MOMENT_SCALE = {'rel_bias': 5.851488e-01, 'norm_mix_g': 7.866703e+00, 'w_in': 5.879638e+00, 'q_norm_g': 1.275319e+00, 'k_norm_g': 1.272267e+00, 'sinks': 2.985620e-01, 'conv_w': 3.562067e+00, 'conv_b': 3.028355e+01, 'conv_ln_g': 1.126312e+01, 'conv_ln_b': 1.600742e+01, 'attn_out_g': 1.402665e+01, 'conv_out_g': 1.149710e+01, 'w_out': 2.408499e+01, 'norm_mlp_g': 2.552258e+01, 'w_mlp_up': 2.218825e+00, 'w_mlp_down': 2.789885e+01}


def _to_microbatches(a, axis):
    t = _jnp.moveaxis(a, axis, 0)
    t = t.reshape((N_MICROBATCH, t.shape[0] // N_MICROBATCH) + t.shape[1:])
    return _jnp.moveaxis(t, 1, axis + 1)


def setup_inputs(seed: int = 0) -> dict:
    inp = _fwd_setup_inputs(seed)
    key = _jax.random.fold_in(_jax.random.key(seed), 7919)
    shape, _ = _output_shape()
    out = dict(inp)
    out["loss_target"] = _jax.random.normal(_jax.random.fold_in(key, 0), shape, _jnp.float32)
    for i, name in enumerate(TWIN_WEIGHTS):
        w = inp[name].astype(_jnp.float32)
        if MOMENT_SCALE is None:
            s = _jnp.sqrt(_jnp.mean(_jnp.square(w)) + 1e-30)
        else:
            s = MOMENT_SCALE[name]
        km, kv = _jax.random.split(_jax.random.fold_in(key, i + 1))
        out[name] = w
        out["m_" + name] = s * _jax.random.normal(km, w.shape, _jnp.float32)
        out["v_" + name] = (s * s) * _jax.random.uniform(kv, w.shape, _jnp.float32, 0.5, 1.5)
    if N_MICROBATCH > 1:
        for name, axis in PER_EXAMPLE_BATCH_AXIS.items():
            out[name] = _to_microbatches(out[name], axis)
    return {'x': out['x'], 'rel_bias': out['rel_bias'], 'norm_mix_g': out['norm_mix_g'], 'w_in': out['w_in'], 'q_norm_g': out['q_norm_g'], 'k_norm_g': out['k_norm_g'], 'sinks': out['sinks'], 'conv_w': out['conv_w'], 'conv_b': out['conv_b'], 'conv_ln_g': out['conv_ln_g'], 'conv_ln_b': out['conv_ln_b'], 'attn_out_g': out['attn_out_g'], 'conv_out_g': out['conv_out_g'], 'w_out': out['w_out'], 'norm_mlp_g': out['norm_mlp_g'], 'w_mlp_up': out['w_mlp_up'], 'w_mlp_down': out['w_mlp_down'], 'loss_target': out['loss_target'], 'm_rel_bias': out['m_rel_bias'], 'm_norm_mix_g': out['m_norm_mix_g'], 'm_w_in': out['m_w_in'], 'm_q_norm_g': out['m_q_norm_g'], 'm_k_norm_g': out['m_k_norm_g'], 'm_sinks': out['m_sinks'], 'm_conv_w': out['m_conv_w'], 'm_conv_b': out['m_conv_b'], 'm_conv_ln_g': out['m_conv_ln_g'], 'm_conv_ln_b': out['m_conv_ln_b'], 'm_attn_out_g': out['m_attn_out_g'], 'm_conv_out_g': out['m_conv_out_g'], 'm_w_out': out['m_w_out'], 'm_norm_mlp_g': out['m_norm_mlp_g'], 'm_w_mlp_up': out['m_w_mlp_up'], 'm_w_mlp_down': out['m_w_mlp_down'], 'v_rel_bias': out['v_rel_bias'], 'v_norm_mix_g': out['v_norm_mix_g'], 'v_w_in': out['v_w_in'], 'v_q_norm_g': out['v_q_norm_g'], 'v_k_norm_g': out['v_k_norm_g'], 'v_sinks': out['v_sinks'], 'v_conv_w': out['v_conv_w'], 'v_conv_b': out['v_conv_b'], 'v_conv_ln_g': out['v_conv_ln_g'], 'v_conv_ln_b': out['v_conv_ln_b'], 'v_attn_out_g': out['v_attn_out_g'], 'v_conv_out_g': out['v_conv_out_g'], 'v_w_out': out['v_w_out'], 'v_norm_mlp_g': out['v_norm_mlp_g'], 'v_w_mlp_up': out['v_w_mlp_up'], 'v_w_mlp_down': out['v_w_mlp_down']}


def _loss(weights, diff, rest, loss_target):
    with _jax.named_scope("forward"):
        args = {**rest, TWIN_DIFF_INPUT: diff, **{k: w.astype(_WEIGHT_DTYPES[k]) for k, w in weights.items()}}
        y = _forward(args)
    with _jax.named_scope("loss_head"):
        err = _jnp.square(y.astype(_jnp.float32) - loss_target)
        return 0.5 * _jnp.sum(_jnp.mean(err, axis=-1)) if err.ndim else 0.5 * err


def _adamw(w, g, m, v):
    m = ADAM_B1 * m + (1.0 - ADAM_B1) * g
    v = ADAM_B2 * v + (1.0 - ADAM_B2) * _jnp.square(g)
    m_hat = m / (1.0 - ADAM_B1 ** ADAM_STEP)
    v_hat = v / (1.0 - ADAM_B2 ** ADAM_STEP)
    delta = -ADAM_LR * (m_hat / (_jnp.sqrt(v_hat) + ADAM_EPS) + ADAM_WD * w)
    return delta, m, v


def reference(x, rel_bias, norm_mix_g, w_in, q_norm_g, k_norm_g, sinks, conv_w, conv_b, conv_ln_g, conv_ln_b, attn_out_g, conv_out_g, w_out, norm_mlp_g, w_mlp_up, w_mlp_down, loss_target, m_rel_bias, m_norm_mix_g, m_w_in, m_q_norm_g, m_k_norm_g, m_sinks, m_conv_w, m_conv_b, m_conv_ln_g, m_conv_ln_b, m_attn_out_g, m_conv_out_g, m_w_out, m_norm_mlp_g, m_w_mlp_up, m_w_mlp_down, v_rel_bias, v_norm_mix_g, v_w_in, v_q_norm_g, v_k_norm_g, v_sinks, v_conv_w, v_conv_b, v_conv_ln_g, v_conv_ln_b, v_attn_out_g, v_conv_out_g, v_w_out, v_norm_mlp_g, v_w_mlp_up, v_w_mlp_down):
    given = dict(x=x, rel_bias=rel_bias, norm_mix_g=norm_mix_g, w_in=w_in, q_norm_g=q_norm_g, k_norm_g=k_norm_g, sinks=sinks, conv_w=conv_w, conv_b=conv_b, conv_ln_g=conv_ln_g, conv_ln_b=conv_ln_b, attn_out_g=attn_out_g, conv_out_g=conv_out_g, w_out=w_out, norm_mlp_g=norm_mlp_g, w_mlp_up=w_mlp_up, w_mlp_down=w_mlp_down, loss_target=loss_target, m_rel_bias=m_rel_bias, m_norm_mix_g=m_norm_mix_g, m_w_in=m_w_in, m_q_norm_g=m_q_norm_g, m_k_norm_g=m_k_norm_g, m_sinks=m_sinks, m_conv_w=m_conv_w, m_conv_b=m_conv_b, m_conv_ln_g=m_conv_ln_g, m_conv_ln_b=m_conv_ln_b, m_attn_out_g=m_attn_out_g, m_conv_out_g=m_conv_out_g, m_w_out=m_w_out, m_norm_mlp_g=m_norm_mlp_g, m_w_mlp_up=m_w_mlp_up, m_w_mlp_down=m_w_mlp_down, v_rel_bias=v_rel_bias, v_norm_mix_g=v_norm_mix_g, v_w_in=v_w_in, v_q_norm_g=v_q_norm_g, v_k_norm_g=v_k_norm_g, v_sinks=v_sinks, v_conv_w=v_conv_w, v_conv_b=v_conv_b, v_conv_ln_g=v_conv_ln_g, v_conv_ln_b=v_conv_ln_b, v_attn_out_g=v_attn_out_g, v_conv_out_g=v_conv_out_g, v_w_out=v_w_out, v_norm_mlp_g=v_norm_mlp_g, v_w_mlp_up=v_w_mlp_up, v_w_mlp_down=v_w_mlp_down)
    weights = {n: given[n] for n in TWIN_WEIGHTS}
    shared = {n: given[n] for n in SHARED_INPUTS}
    per_example = {n: given[n] for n in ['x']}
    grad_fn = _jax.value_and_grad(_loss, argnums=(0, 1))

    def one_microbatch(ex, loss_target):
        ex = dict(ex)
        diff = ex.pop(TWIN_DIFF_INPUT)
        return grad_fn(weights, diff, {**shared, **ex}, loss_target)

    if N_MICROBATCH == 1:
        loss, (grad_w, grad_x) = one_microbatch(per_example, given["loss_target"])
    else:
        def body(carry, xs):
            loss_sum, grad_sum = carry
            l_k, (gw_k, gx_k) = one_microbatch(xs[0], xs[1])
            with _jax.named_scope("update"):
                return (loss_sum + l_k, _jax.tree.map(_jnp.add, grad_sum, gw_k)), gx_k

        init = (_jnp.zeros((), _jnp.float32), _jax.tree.map(_jnp.zeros_like, weights))
        (loss, grad_w), grad_x = _jax.lax.scan(body, init, (per_example, given["loss_target"]))
    with _jax.named_scope("update"):
        delta_w, new_m, new_v = {}, {}, {}
        for n in TWIN_WEIGHTS:
            delta_w[n], new_m[n], new_v[n] = _adamw(weights[n], grad_w[n], given["m_" + n], given["v_" + n])
    return (loss, grad_x, *[grad_w[n] for n in TWIN_WEIGHTS], *[delta_w[n] for n in TWIN_WEIGHTS],
            *[new_m[n] for n in TWIN_WEIGHTS], *[new_v[n] for n in TWIN_WEIGHTS])
```

```python
import math

import numpy as np
import jax
import jax.numpy as jnp
from jax import lax
from jax.experimental import pallas as pl
from jax.experimental.pallas import tpu as pltpu

F32, BF16 = jnp.float32, jnp.bfloat16
D_MODEL = 1024
DEPTH = 4
HEAD_DIM = 64
N_HEADS = 8
N_KV = 2
GQA = N_HEADS // N_KV
ATTN_W = N_HEADS * HEAD_DIM
KV_W = N_KV * HEAD_DIM
CONV_W = D_MODEL - ATTN_W
IN_W = ATTN_W + 2 * KV_W + 2 * CONV_W
BLOCK = 128
CONV_K = 31
HALO = 32
N_BUCKETS = 32
MAX_DIST = 128
D_FF = 4 * D_MODEL
FF_CHUNK = 512
N_FF = D_FF // FF_CHUNK
EPS = 1e-6
NEG = -1e30
N_DEV = 8
ADAM_LR, ADAM_B1, ADAM_B2, ADAM_EPS, ADAM_WD, ADAM_STEP = 0.001, 0.9, 0.999, 1e-08, 0.01, 10
VMEM_LIMIT = 56 * 1024 * 1024
MESH = pl.DeviceIdType.MESH

RESIDENT = pl.BlockSpec(memory_space=pltpu.VMEM)
IN_SMEM = pl.BlockSpec(memory_space=pltpu.SMEM)
ANY = pl.BlockSpec(memory_space=pl.ANY)


def _call(body, **kw):
    return pl.pallas_call(body, **kw)


def _params(*sem):
    return pltpu.CompilerParams(dimension_semantics=sem, vmem_limit_bytes=VMEM_LIMIT)


def _dot(a, b):
    return lax.dot_general(a, b, (((1,), (0,)), ((), ())), preferred_element_type=F32)


def _dot_nt(a, b):
    return lax.dot_general(a, b, (((1,), (1,)), ((), ())), preferred_element_type=F32)


def _dot_tn(a, b):
    return lax.dot_general(a, b, (((0,), (0,)), ((), ())), preferred_element_type=F32)


def _sig(x):
    return 1.0 / (1.0 + jnp.exp(-x))


def _rms(x):
    r = lax.rsqrt(jnp.mean(x * x, axis=-1, keepdims=True) + EPS)
    return x * r, r


def _rms_bwd(dy_g, xh, r):
    return r * (dy_g - xh * jnp.mean(dy_g * xh, axis=-1, keepdims=True))


def _rows(tm, w):
    return pl.BlockSpec((tm, w), lambda i: (i, 0))


def _acc_rows(w, rows=1):
    return pl.BlockSpec((rows, w), lambda i: (0, 0))


def _colsum(x):
    return jnp.sum(x, axis=0, keepdims=True)


def _fwd_in(x, g, wt):
    T = x.shape[0]
    tm = 512

    def body(x_ref, g_ref, w_ref, q_ref, kv_ref, u_ref, gt_ref):
        xh, _ = _rms(x_ref[...])
        h = (xh * g_ref[...]).astype(BF16)
        z = _dot_nt(h, w_ref[...])
        q_ref[...] = z[:, :ATTN_W]
        kv_ref[...] = z[:, ATTN_W:ATTN_W + 2 * KV_W]
        u_ref[...] = z[:, ATTN_W + 2 * KV_W:ATTN_W + 2 * KV_W + CONV_W]
        gt_ref[...] = z[:, ATTN_W + 2 * KV_W + CONV_W:]

    widths = (ATTN_W, 2 * KV_W, CONV_W, CONV_W)
    return _call(
        body, name="fwd_in", grid=(T // tm,),
        in_specs=[_rows(tm, D_MODEL), RESIDENT, RESIDENT],
        out_specs=[_rows(tm, w) for w in widths],
        out_shape=[jax.ShapeDtypeStruct((T, w), F32) for w in widths],
        compiler_params=_params("parallel"),
    )(x, g, wt)


def _band_masks(n):
    qi = lax.broadcasted_iota(jnp.int32, (BLOCK, 2 * BLOCK), 0)
    kj = lax.broadcasted_iota(jnp.int32, (BLOCK, 2 * BLOCK), 1)
    dist = qi + BLOCK - kj
    valid = (dist >= 0) & (dist < BLOCK) & ((kj >= BLOCK) | (n > 0))
    return jnp.concatenate([valid] * GQA, axis=0)


def _head_rows(g):
    row = lax.broadcasted_iota(jnp.int32, (GQA * BLOCK, 1), 0)
    return (row >= g * BLOCK) & (row < (g + 1) * BLOCK)


def _attn_group(zq, kv, kh, bias_ref, sinks_ref, qg, kg, valid4):
    scale = 1.0 / math.sqrt(HEAD_DIM)
    kraw = kv[:, kh * HEAD_DIM:(kh + 1) * HEAD_DIM]
    khat, rk = _rms(kraw)
    kn = (khat * kg).astype(BF16)
    v = kv[:, KV_W + kh * HEAD_DIM:KV_W + (kh + 1) * HEAD_DIM].astype(BF16)
    qraw = jnp.concatenate(
        [zq[:, (kh * GQA + g) * HEAD_DIM:(kh * GQA + g + 1) * HEAD_DIM] for g in range(GQA)], axis=0)
    qhat, rq = _rms(qraw)
    qn = (qhat * qg).astype(BF16)
    s = _dot_nt(qn, kn) * scale + bias_ref[kh * GQA:(kh + 1) * GQA].reshape(GQA * BLOCK, 2 * BLOCK)
    s = jnp.where(valid4, s, NEG)
    sink = jnp.zeros((GQA * BLOCK, 1), F32)
    for g in range(GQA):
        sink = jnp.where(_head_rows(g), sinks_ref[kh * GQA + g], sink)
    m = jnp.maximum(jnp.max(s, axis=-1, keepdims=True), sink)
    p = jnp.exp(s - m)
    es = jnp.exp(sink - m)
    den = jnp.sum(p, axis=-1, keepdims=True) + es
    pn = p / den
    return dict(khat=khat, rk=rk, kn=kn, v=v, qhat=qhat, rq=rq, qn=qn, pn=pn, psink=es / den)


def _unstack_heads(o):
    return [o[g * BLOCK:(g + 1) * BLOCK] for g in range(GQA)]


def _attn_fwd(zq, zkv, bias, sinks, qg, kg, og):
    T = zq.shape[0]
    nb = T // BLOCK

    def body(q_ref, kvc_ref, kvp_ref, bias_ref, sinks_ref, qg_ref, kg_ref, og_ref, a_ref, mix_ref):
        n = pl.program_id(0)
        zq_v = q_ref[...]
        kv = jnp.concatenate([kvp_ref[...], kvc_ref[...]], axis=0)
        valid4 = _band_masks(n)
        outs = []
        for kh in range(N_KV):
            c = _attn_group(zq_v, kv, kh, bias_ref, sinks_ref, qg_ref[...], kg_ref[...], valid4)
            outs += _unstack_heads(_dot(c["pn"].astype(BF16), c["v"]))
        a = jnp.concatenate(outs, axis=-1)
        a_ref[...] = a
        ah, _ = _rms(a)
        mix_ref[...] = (ah * og_ref[...]).astype(BF16)

    return _call(
        body, name="attn_fwd", grid=(nb,),
        in_specs=[_rows(BLOCK, ATTN_W), _rows(BLOCK, 2 * KV_W),
                  pl.BlockSpec((BLOCK, 2 * KV_W), lambda n: (jnp.maximum(n - 1, 0), 0)),
                  RESIDENT, IN_SMEM, RESIDENT, RESIDENT, RESIDENT],
        out_specs=[_rows(BLOCK, ATTN_W), _rows(BLOCK, ATTN_W)],
        out_shape=[jax.ShapeDtypeStruct((T, ATTN_W), F32), jax.ShapeDtypeStruct((T, ATTN_W), BF16)],
        compiler_params=_params("parallel"),
    )(zq, zkv, zkv, bias, sinks, qg, kg, og)


def _conv_post(y, lg, lb, og):
    mu = jnp.mean(y, axis=-1, keepdims=True)
    yc = y - mu
    rstd = lax.rsqrt(jnp.mean(yc * yc, axis=-1, keepdims=True) + EPS)
    yn = yc * rstd
    ln = yn * lg + lb
    sg = _sig(ln)
    c = ln * sg
    ch, r = _rms(c)
    return yn, rstd, ln, sg, ch, r


def _conv_fwd(zu, zg, cw, cb, lg, lb, og):
    T = zu.shape[0]
    tt = 512
    halo_spec = pl.BlockSpec((HALO, CONV_W), lambda i: (jnp.maximum(i * (tt // HALO) - 1, 0), 0))

    def body(u_ref, g_ref, uh_ref, gh_ref, cw_ref, cb_ref, lg_ref, lb_ref, og_ref, y_ref, mix_ref, buf):
        i = pl.program_id(0)
        hal = uh_ref[...] * _sig(gh_ref[...])
        buf[0:HALO, :] = jnp.where(i > 0, hal, 0.0)
        buf[HALO:HALO + tt, :] = u_ref[...] * _sig(g_ref[...])
        acc = jnp.zeros((tt, CONV_W), F32)
        for j in range(CONV_K):
            acc = acc + cw_ref[j:j + 1, :] * buf[HALO - (CONV_K - 1) + j:HALO - (CONV_K - 1) + j + tt, :]
        y = acc + cb_ref[...]
        y_ref[...] = y
        _, _, _, _, ch, _ = _conv_post(y, lg_ref[...], lb_ref[...], og_ref[...])
        mix_ref[...] = (ch * og_ref[...]).astype(BF16)

    return _call(
        body, name="conv_fwd", grid=(T // tt,),
        in_specs=[_rows(tt, CONV_W), _rows(tt, CONV_W), halo_spec, halo_spec] + [RESIDENT] * 5,
        out_specs=[_rows(tt, CONV_W), _rows(tt, CONV_W)],
        out_shape=[jax.ShapeDtypeStruct((T, CONV_W), F32), jax.ShapeDtypeStruct((T, CONV_W), BF16)],
        scratch_shapes=[pltpu.VMEM((HALO + tt, CONV_W), F32)],
        compiler_params=_params("parallel"),
    )(zu, zg, zu, zg, cw, cb, lg, lb, og)


def _fwd_out(x, mix_a, mix_c, w_out):
    T = x.shape[0]
    tm = 512

    def body(x_ref, a_ref, c_ref, w_ref, o_ref):
        o_ref[...] = (x_ref[...] + _dot(a_ref[...], w_ref[0:ATTN_W, :])
                      + _dot(c_ref[...], w_ref[ATTN_W:, :]))

    return _call(
        body, name="fwd_out", grid=(T // tm,),
        in_specs=[_rows(tm, D_MODEL), _rows(tm, ATTN_W), _rows(tm, CONV_W), RESIDENT],
        out_specs=_rows(tm, D_MODEL),
        out_shape=jax.ShapeDtypeStruct((T, D_MODEL), F32),
        compiler_params=_params("parallel"),
    )(x, mix_a, mix_c, w_out)


def _chunked(tm):
    return pl.BlockSpec((N_FF, tm, FF_CHUNK), lambda i: (0, i, 0))


def _mlp_fwd(x, g, wup_t, wdown):
    T = x.shape[0]
    tm = 512

    def body(x_ref, g_ref, wu_ref, wd_ref, o_ref, up_ref):
        xv = x_ref[...]
        xh, _ = _rms(xv)
        h = (xh * g_ref[...]).astype(BF16)
        acc = xv
        for c in range(N_FF):
            rows = slice(c * FF_CHUNK, (c + 1) * FF_CHUNK)
            up = _dot_nt(h, wu_ref[rows, :])
            up_ref[c] = up.astype(BF16)
            act = jnp.square(jnp.maximum(up, 0.0))
            acc = acc + _dot(act.astype(BF16), wd_ref[rows, :])
        o_ref[...] = acc

    return _call(
        body, name="mlp_fwd", grid=(T // tm,),
        in_specs=[_rows(tm, D_MODEL), RESIDENT, RESIDENT, RESIDENT],
        out_specs=[_rows(tm, D_MODEL), _chunked(tm)],
        out_shape=[jax.ShapeDtypeStruct((T, D_MODEL), F32), jax.ShapeDtypeStruct((N_FF, T, FF_CHUNK), BF16)],
        compiler_params=_params("parallel"),
    )(x, g, wup_t, wdown)


def _loss_head(y, target):
    T = y.shape[0]
    tm = 512

    def body(y_ref, t_ref, l_ref, d_ref):
        @pl.when(pl.program_id(0) == 0)
        def _():
            l_ref[...] = jnp.zeros_like(l_ref)
        e = y_ref[...] - t_ref[...]
        d_ref[...] = e / D_MODEL
        l_ref[...] += 0.5 * jnp.sum(jnp.mean(e * e, axis=-1, keepdims=True))

    return _call(
        body, name="loss_head", grid=(T // tm,),
        in_specs=[_rows(tm, D_MODEL), _rows(tm, D_MODEL)],
        out_specs=[_acc_rows(128, 8), _rows(tm, D_MODEL)],
        out_shape=[jax.ShapeDtypeStruct((8, 128), F32), jax.ShapeDtypeStruct((T, D_MODEL), F32)],
        compiler_params=_params("arbitrary"),
    )(y, target)


def _bias_table(rel_bias, bucket):
    def body(rb_ref, bk_ref, o_ref):
        bk = bk_ref[...]
        for h in range(N_HEADS):
            acc = jnp.zeros((BLOCK, 2 * BLOCK), F32)
            for b in range(N_BUCKETS):
                acc = jnp.where(bk == b, rb_ref[b, h], acc)
            o_ref[h] = acc

    return _call(
        body, name="bias_table", in_specs=[IN_SMEM, RESIDENT], out_specs=RESIDENT,
        out_shape=jax.ShapeDtypeStruct((N_HEADS, BLOCK, 2 * BLOCK), F32),
    )(rel_bias, bucket)


def _mlp_bwd(dx2, x1, up, g, wup_t, wdown):
    T = x1.shape[0]
    tm = 512

    def body(d_ref, x_ref, up_ref, g_ref, wu_ref, wd_ref, dx_ref, dup_ref, h_ref, dg_ref):
        @pl.when(pl.program_id(0) == 0)
        def _():
            dg_ref[...] = jnp.zeros_like(dg_ref)
        d2 = d_ref[...]
        d2b = d2.astype(BF16)
        xh, r = _rms(x_ref[...])
        gv = g_ref[...]
        h_ref[...] = (xh * gv).astype(BF16)
        dh = jnp.zeros((tm, D_MODEL), F32)
        for c in range(N_FF):
            rows = slice(c * FF_CHUNK, (c + 1) * FF_CHUNK)
            dact = _dot_nt(d2b, wd_ref[rows, :])
            dup = (dact * (2.0 * jnp.maximum(up_ref[c].astype(F32), 0.0))).astype(BF16)
            dup_ref[c] = dup
            dh = dh + _dot(dup, wu_ref[rows, :])
        dg_ref[...] += _colsum(dh * xh)
        dx_ref[...] = d2 + _rms_bwd(dh * gv, xh, r)

    return _call(
        body, name="mlp_bwd", grid=(T // tm,),
        in_specs=[_rows(tm, D_MODEL), _rows(tm, D_MODEL), _chunked(tm), RESIDENT, RESIDENT, RESIDENT],
        out_specs=[_rows(tm, D_MODEL), _chunked(tm), _rows(tm, D_MODEL), _acc_rows(D_MODEL)],
        out_shape=[jax.ShapeDtypeStruct((T, D_MODEL), F32), jax.ShapeDtypeStruct((N_FF, T, FF_CHUNK), BF16),
                   jax.ShapeDtypeStruct((T, D_MODEL), BF16), jax.ShapeDtypeStruct((1, D_MODEL), F32)],
        compiler_params=_params("arbitrary"),
    )(dx2, x1, up, g, wup_t, wdown)


def _wgrad(a, b, name, chunked=False, square_relu=False):
    if chunked:
        nr, T, tr = a.shape
    else:
        T, R = a.shape
        tr = min(R, FF_CHUNK)
        nr = R // tr
    tk = 1024
    nk = T // tk

    def body(a_ref, b_ref, o_ref, acc):
        k = pl.program_id(1)

        @pl.when(k == 0)
        def _():
            acc[...] = jnp.zeros_like(acc)
        av = a_ref[...]
        if square_relu:
            av = jnp.square(jnp.maximum(av.astype(F32), 0.0))
        acc[...] += _dot_tn(av.astype(BF16), b_ref[...].astype(BF16))

        @pl.when(k == nk - 1)
        def _():
            o_ref[...] = acc[...].astype(BF16)

    a_spec = (pl.BlockSpec((None, tk, tr), lambda r, k: (r, k, 0)) if chunked
              else pl.BlockSpec((tk, tr), lambda r, k: (k, r)))
    return _call(
        body, name=name, grid=(nr, nk),
        in_specs=[a_spec, pl.BlockSpec((tk, D_MODEL), lambda r, k: (k, 0))],
        out_specs=pl.BlockSpec((tr, D_MODEL), lambda r, k: (r, 0)),
        out_shape=jax.ShapeDtypeStruct((nr * tr, D_MODEL), BF16),
        scratch_shapes=[pltpu.VMEM((tr, D_MODEL), F32)],
        compiler_params=_params("parallel", "arbitrary"),
    )(a, b)


def _bwd_out(dx1, w_out):
    T = dx1.shape[0]
    tm = 512

    def body(d_ref, w_ref, da_ref, dc_ref, db_ref):
        db = d_ref[...].astype(BF16)
        db_ref[...] = db
        dm = _dot_nt(db, w_ref[...])
        da_ref[...] = dm[:, :ATTN_W]
        dc_ref[...] = dm[:, ATTN_W:]

    return _call(
        body, name="bwd_out", grid=(T // tm,),
        in_specs=[_rows(tm, D_MODEL), RESIDENT],
        out_specs=[_rows(tm, ATTN_W), _rows(tm, CONV_W), _rows(tm, D_MODEL)],
        out_shape=[jax.ShapeDtypeStruct((T, ATTN_W), F32), jax.ShapeDtypeStruct((T, CONV_W), F32),
                   jax.ShapeDtypeStruct((T, D_MODEL), BF16)],
        compiler_params=_params("parallel"),
    )(dx1, w_out)


def _conv_bwd(dmix, y, zu, zg, cw, lg, lb, og):
    T = y.shape[0]
    tt = 512
    nt = T // tt
    per = tt // HALO
    prev_spec = pl.BlockSpec((HALO, CONV_W), lambda i: (jnp.maximum(i * per - 1, 0), 0))
    next_spec = pl.BlockSpec((HALO, CONV_W), lambda i: (jnp.minimum((i + 1) * per, nt * per - 1), 0))

    def body(dm_ref, dmn_ref, y_ref, yn_ref, u_ref, g_ref, uh_ref, gh_ref, cw_ref, lg_ref, lb_ref, og_ref,
             du_ref, dg_ref, pg_ref, hbuf, dybuf):
        i = pl.program_id(0)

        @pl.when(i == 0)
        def _():
            pg_ref[...] = jnp.zeros_like(pg_ref)
        lgv, lbv, ogv = lg_ref[...], lb_ref[...], og_ref[...]

        def chain(yv, dm):
            yn, rstd, ln, sg, ch, r = _conv_post(yv, lgv, lbv, ogv)
            dc = _rms_bwd(dm * ogv, ch, r)
            dln = dc * sg * (1.0 + ln * (1.0 - sg))
            dyn = dln * lgv
            dy = rstd * (dyn - jnp.mean(dyn, axis=-1, keepdims=True)
                         - yn * jnp.mean(dyn * yn, axis=-1, keepdims=True))
            return dy, dm * ch, dln * yn, dln

        dy, p_og, p_lg, p_lb = chain(y_ref[...], dm_ref[...])
        dyh, _, _, _ = chain(yn_ref[...], dmn_ref[...])
        dybuf[0:tt, :] = dy
        dybuf[tt:tt + HALO, :] = jnp.where(i < nt - 1, dyh, 0.0)
        sgt = _sig(g_ref[...])
        uv = u_ref[...]
        hbuf[0:HALO, :] = jnp.where(i > 0, uh_ref[...] * _sig(gh_ref[...]), 0.0)
        hbuf[HALO:HALO + tt, :] = uv * sgt
        dh = jnp.zeros((tt, CONV_W), F32)
        lo = HALO - (CONV_K - 1)
        for j in range(CONV_K):
            dh = dh + cw_ref[j:j + 1, :] * dybuf[CONV_K - 1 - j:CONV_K - 1 - j + tt, :]
            pg_ref[j:j + 1, :] += _colsum(dy * hbuf[lo + j:lo + j + tt, :])
        pg_ref[32:33, :] += _colsum(dy)
        pg_ref[33:34, :] += _colsum(p_lg)
        pg_ref[34:35, :] += _colsum(p_lb)
        pg_ref[35:36, :] += _colsum(p_og)
        du_ref[...] = (dh * sgt).astype(BF16)
        dg_ref[...] = (dh * uv * sgt * (1.0 - sgt)).astype(BF16)

    return _call(
        body, name="conv_bwd", grid=(nt,),
        in_specs=[_rows(tt, CONV_W), next_spec, _rows(tt, CONV_W), next_spec, _rows(tt, CONV_W), _rows(tt, CONV_W),
                  prev_spec, prev_spec] + [RESIDENT] * 4,
        out_specs=[_rows(tt, CONV_W), _rows(tt, CONV_W), _acc_rows(CONV_W, 40)],
        out_shape=[jax.ShapeDtypeStruct((T, CONV_W), BF16), jax.ShapeDtypeStruct((T, CONV_W), BF16),
                   jax.ShapeDtypeStruct((40, CONV_W), F32)],
        scratch_shapes=[pltpu.VMEM((HALO + tt, CONV_W), F32), pltpu.VMEM((tt + HALO, CONV_W), F32)],
        compiler_params=_params("arbitrary"),
    )(dmix, dmix, y, y, zu, zg, zu, zg, cw, lg, lb, og)


def _attn_bwd(dmix, a, zq, zkv, bias, sinks, qg, kg, og):
    T = zq.shape[0]
    nb = T // BLOCK
    last = nb - 1
    cur = lambda w: pl.BlockSpec((BLOCK, w), lambda n: (jnp.minimum(n, last), 0))
    prev = lambda w: pl.BlockSpec((BLOCK, w), lambda n: (jnp.clip(n - 1, 0, last), 0))
    scale = 1.0 / math.sqrt(HEAD_DIM)

    def body(dm_ref, a_ref, q_ref, kvc_ref, kvp_ref, bias_ref, sinks_ref, qg_ref, kg_ref, og_ref,
             dq_ref, dkv_ref, db_ref, dog_ref, dqg_ref, dkg_ref, dsk_ref, carry):
        n = pl.program_id(0)

        @pl.when(n == 0)
        def _():
            for ref in (db_ref, dog_ref, dqg_ref, dkg_ref, dsk_ref, carry):
                ref[...] = jnp.zeros_like(ref)

        @pl.when(n < nb)
        def _():
            ah, ra = _rms(a_ref[...])
            dm = dm_ref[...]
            ogv, qgv, kgv = og_ref[...], qg_ref[...], kg_ref[...]
            dog_ref[...] += _colsum(dm * ah)
            da = _rms_bwd(dm * ogv, ah, ra)
            zq_v = q_ref[...]
            kv = jnp.concatenate([kvp_ref[...], kvc_ref[...]], axis=0)
            valid4 = _band_masks(n)
            lane =lax.broadcasted_iota(jnp.int32, (1, 128), 1)
            dqs, dks, dvs = [], [], []
            dsk = jnp.zeros((1, 128), F32)
            for kh in range(N_KV):
                c = _attn_group(zq_v, kv, kh, bias_ref, sinks_ref, qgv, kgv, valid4)
                do = jnp.concatenate(
                    [da[:, (kh * GQA + g) * HEAD_DIM:(kh * GQA + g + 1) * HEAD_DIM] for g in range(GQA)], axis=0)
                dob = do.astype(BF16)
                pn = c["pn"]
                dvs.append(_dot_tn(pn.astype(BF16), dob))
                dp = _dot_nt(dob, c["v"])
                dl = jnp.sum(pn * dp, axis=-1, keepdims=True)
                ds = pn * (dp - dl)
                dsr = -c["psink"] * dl
                for g in range(GQA):
                    dsk = dsk + jnp.where(lane == kh * GQA + g, jnp.sum(jnp.where(_head_rows(g), dsr, 0.0)), 0.0)
                db_ref[kh * GQA:(kh + 1) * GQA] += ds.reshape(GQA, BLOCK, 2 * BLOCK)
                dsb = (ds * scale).astype(BF16)
                dqn = _dot(dsb, c["kn"])
                dkn = _dot_tn(dsb, c["qn"])
                dqg_ref[...] += _colsum(dqn * c["qhat"])
                dkg_ref[...] += _colsum(dkn * c["khat"])
                dqs += _unstack_heads(_rms_bwd(dqn * qgv, c["qhat"], c["rq"]))
                dks.append(_rms_bwd(dkn * kgv, c["khat"], c["rk"]))
            dsk_ref[...] += dsk
            dq_ref[...] = jnp.concatenate(dqs, axis=-1).astype(BF16)
            dkv = jnp.concatenate(dks + dvs, axis=-1)
            dkv_ref[...] = (carry[...] + dkv[:BLOCK]).astype(BF16)
            carry[...] = dkv[BLOCK:]

        @pl.when(n == nb)
        def _():
            dkv_ref[...] = carry[...].astype(BF16)

    small = lambda w: pl.BlockSpec((1, w), lambda n: (0, 0))
    return _call(
        body, name="attn_bwd", grid=(nb + 1,),
        in_specs=[cur(ATTN_W), cur(ATTN_W), cur(ATTN_W), cur(2 * KV_W), prev(2 * KV_W),
                  RESIDENT, IN_SMEM, RESIDENT, RESIDENT, RESIDENT],
        out_specs=[cur(ATTN_W), prev(2 * KV_W),
                   pl.BlockSpec((N_HEADS, BLOCK, 2 * BLOCK), lambda n: (0, 0, 0)),
                   small(ATTN_W), small(HEAD_DIM), small(HEAD_DIM), small(128)],
        out_shape=[jax.ShapeDtypeStruct((T, ATTN_W), BF16), jax.ShapeDtypeStruct((T, 2 * KV_W), BF16),
                   jax.ShapeDtypeStruct((N_HEADS, BLOCK, 2 * BLOCK), F32),
                   jax.ShapeDtypeStruct((1, ATTN_W), F32), jax.ShapeDtypeStruct((1, HEAD_DIM), F32),
                   jax.ShapeDtypeStruct((1, HEAD_DIM), F32), jax.ShapeDtypeStruct((1, 128), F32)],
        scratch_shapes=[pltpu.VMEM((BLOCK, 2 * KV_W), F32)],
        compiler_params=_params("arbitrary"),
    )(dmix, a, zq, zkv, zkv, bias, sinks, qg, kg, og)


def _bwd_in(dx1, x, dq, dkv, du, dgt, g, wt):
    T = x.shape[0]
    tm = 512

    def body(d1_ref, x_ref, dq_ref, dkv_ref, du_ref, dgt_ref, g_ref, w_ref, dx_ref, h_ref, dg_ref):
        @pl.when(pl.program_id(0) == 0)
        def _():
            dg_ref[...] = jnp.zeros_like(dg_ref)
        xh, r = _rms(x_ref[...])
        gv = g_ref[...]
        h_ref[...] = (xh * gv).astype(BF16)
        dz = jnp.concatenate([dq_ref[...], dkv_ref[...], du_ref[...], dgt_ref[...]], axis=-1)
        dh = _dot(dz, w_ref[...])
        dg_ref[...] += _colsum(dh * xh)
        dx_ref[...] = d1_ref[...] + _rms_bwd(dh * gv, xh, r)

    return _call(
        body, name="bwd_in", grid=(T // tm,),
        in_specs=[_rows(tm, D_MODEL), _rows(tm, D_MODEL), _rows(tm, ATTN_W), _rows(tm, 2 * KV_W),
                  _rows(tm, CONV_W), _rows(tm, CONV_W), RESIDENT, RESIDENT],
        out_specs=[_rows(tm, D_MODEL), _rows(tm, D_MODEL), _acc_rows(D_MODEL)],
        out_shape=[jax.ShapeDtypeStruct((T, D_MODEL), F32), jax.ShapeDtypeStruct((T, D_MODEL), BF16),
                   jax.ShapeDtypeStruct((1, D_MODEL), F32)],
        compiler_params=_params("arbitrary"),
    )(dx1, x, dq, dkv, du, dgt, g, wt)


def _bias_grad(db, bucket):
    def body(db_ref, bk_ref, o_ref):
        tot = db_ref[0]
        for l in range(1, DEPTH):
            tot = tot + db_ref[l]
        bk = bk_ref[...]
        lane = lax.broadcasted_iota(jnp.int32, (N_HEADS, 128), 1)
        out = jnp.zeros((N_HEADS, 128), F32)
        for b in range(N_BUCKETS):
            v = jnp.sum(jnp.sum(jnp.where(bk == b, tot, 0.0), axis=2), axis=1, keepdims=True)
            out = jnp.where(lane == b, v, out)
        o_ref[...] = out

    return _call(
        body, name="bias_grad", in_specs=[RESIDENT, RESIDENT], out_specs=RESIDENT,
        out_shape=jax.ShapeDtypeStruct((N_HEADS, 128), F32),
        compiler_params=pltpu.CompilerParams(vmem_limit_bytes=VMEM_LIMIT),
    )(db, bucket)


def _place():
    return lax.axis_index("x"), lax.axis_index("y"), lax.axis_index("c")


def _all_gather(shards):
    na = len(shards)

    def body(*refs):
        ins, outs = refs[:na], refs[na:2 * na]
        send_sems, recv_sems, local_sems = refs[2 * na:]
        x, y, c = _place()
        me, sibling = (x, y, c), (x, y, 1 - c)
        chips = [(1 - x, y), (x, 1 - y), (1 - x, 1 - y)]

        def rows(a, p):
            r = ins[a].shape[1]
            return outs[a].at[:, pl.ds((4 * p[0] + 2 * p[1] + p[2]) * r, r), :]

        def copy(a, k, block, to, src=None):
            return pltpu.make_async_remote_copy(
                src_ref=rows(a, block) if src is None else src, dst_ref=rows(a, block),
                send_sem=send_sems.at[a, k], recv_sem=recv_sems.at[a, k], device_id=to, device_id_type=MESH)

        mine = [pltpu.make_async_copy(ins[a], rows(a, me), local_sems.at[a]) for a in range(na)]
        for cp in mine:
            cp.start()
        first = []
        for a in range(na):
            first.append(copy(a, 0, me, sibling, src=ins[a]))
            first += [copy(a, 1 + j, me, (*chip, c), src=ins[a]) for j, chip in enumerate(chips)]
        for cp in first:
            cp.start()
        passed = []
        for j, chip in enumerate(chips):
            for a in range(na):
                copy(a, 1 + j, (*chip, c), me).wait_recv()
                cp = copy(a, 4 + j, (*chip, c), sibling)
                cp.start()
                passed.append(cp)
        for a in range(na):
            copy(a, 0, sibling, me).wait_recv()
            for j, chip in enumerate(chips):
                copy(a, 4 + j, (*chip, 1 - c), me).wait_recv()
        for cp in first + passed:
            cp.wait_send()
        for cp in mine:
            cp.wait()

    return _call(
        body, name="all_gather",
        in_specs=[ANY] * na, out_specs=[ANY] * na,
        out_shape=[jax.ShapeDtypeStruct((s.shape[0], N_DEV * s.shape[1], s.shape[2]), s.dtype) for s in shards],
        scratch_shapes=[pltpu.SemaphoreType.DMA((na, 7)), pltpu.SemaphoreType.DMA((na, 7)),
                        pltpu.SemaphoreType.DMA((na,))],
    )(*shards)


def _all_to_all(fulls):
    na = len(fulls)

    def body(*refs):
        ins, outs = refs[:na], refs[na:2 * na]
        send_sems, recv_sems, local_sems = refs[2 * na:]
        x, y, c = _place()
        me = 4 * x + 2 * y + c

        def peer(k):
            return (x ^ (k >> 2), y ^ ((k >> 1) & 1), c ^ (k & 1))

        def block(a, p):
            r = ins[a].shape[1] // N_DEV
            return ins[a].at[:, pl.ds((4 * p[0] + 2 * p[1] + p[2]) * r, r), :]

        def copy(a, k):
            return pltpu.make_async_remote_copy(
                src_ref=block(a, peer(k)), dst_ref=outs[a].at[me],
                send_sem=send_sems.at[a, k - 1], recv_sem=recv_sems.at[a, k - 1],
                device_id=peer(k), device_id_type=MESH)

        def landing(a, k):
            p = peer(k)
            return pltpu.make_async_remote_copy(
                src_ref=block(a, p), dst_ref=outs[a].at[4 * p[0] + 2 * p[1] + p[2]],
                send_sem=send_sems.at[a, k - 1], recv_sem=recv_sems.at[a, k - 1],
                device_id=p, device_id_type=MESH)

        mine = [pltpu.make_async_copy(block(a, (x, y, c)), outs[a].at[me], local_sems.at[a]) for a in range(na)]
        for cp in mine:
            cp.start()
        sends = [copy(a, k) for k in range(1, N_DEV) for a in range(na)]
        for cp in sends:
            cp.start()
        for k in range(1, N_DEV):
            for a in range(na):
                landing(a, k).wait_recv()
        for cp in sends:
            cp.wait_send()
        for cp in mine:
            cp.wait()

    return _call(
        body, name="all_to_all",
        in_specs=[ANY] * na, out_specs=[ANY] * na,
        out_shape=[jax.ShapeDtypeStruct((N_DEV, f.shape[0], f.shape[1] // N_DEV, f.shape[2]), f.dtype) for f in fulls],
        scratch_shapes=[pltpu.SemaphoreType.DMA((na, 7)), pltpu.SemaphoreType.DMA((na, 7)),
                        pltpu.SemaphoreType.DMA((na,))],
    )(*fulls)


def _adam_math(w, g, m, v):
    m = ADAM_B1 * m + (1.0 - ADAM_B1) * g
    v = ADAM_B2 * v + (1.0 - ADAM_B2) * jnp.square(g)
    m_hat = m / (1.0 - ADAM_B1 ** ADAM_STEP)
    v_hat = v / (1.0 - ADAM_B2 ** ADAM_STEP)
    delta = -ADAM_LR * (m_hat / (jnp.sqrt(v_hat) + ADAM_EPS) + ADAM_WD * w)
    return delta, m, v


def _adam_big(parts, w, m, v):
    nl, r, cdim = w.shape
    tr = r if r <= 256 else 256
    spec = pl.BlockSpec((None, tr, cdim), lambda l, i: (l, i, 0))

    def body(p_ref, w_ref, m_ref, v_ref, g_ref, d_ref, nm_ref, nv_ref):
        g = p_ref[0].astype(F32)
        for s in range(1, N_DEV):
            g = g + p_ref[s].astype(F32)
        d, nm, nv = _adam_math(w_ref[...], g, m_ref[...], v_ref[...])
        g_ref[...] = g
        d_ref[...] = d
        nm_ref[...] = nm
        nv_ref[...] = nv

    return _call(
        body, name="adam_big", grid=(nl, r // tr),
        in_specs=[pl.BlockSpec((N_DEV, None, tr, cdim), lambda l, i: (0, l, i, 0)), spec, spec, spec],
        out_specs=[spec] * 4,
        out_shape=[jax.ShapeDtypeStruct(w.shape, F32)] * 4,
        compiler_params=_params("parallel", "parallel"),
    )(parts, w, m, v)


def _sum_slots(parts):
    def body(p_ref, o_ref):
        g = p_ref[0]
        for s in range(1, N_DEV):
            g = g + p_ref[s]
        o_ref[...] = g

    return _call(body, name="sum_slots", in_specs=[RESIDENT], out_specs=RESIDENT,
                 out_shape=jax.ShapeDtypeStruct(parts.shape[1:], F32))(parts)


def _adam_small(w, g, m, v):
    def body(w_ref, g_ref, m_ref, v_ref, d_ref, nm_ref, nv_ref):
        d, nm, nv = _adam_math(w_ref[...], g_ref[...], m_ref[...], v_ref[...])
        d_ref[...] = d
        nm_ref[...] = nm
        nv_ref[...] = nv

    return _call(body, name="adam_small", in_specs=[RESIDENT] * 4, out_specs=[RESIDENT] * 3,
                 out_shape=[jax.ShapeDtypeStruct(w.shape, F32)] * 3)(w, g, m, v)


def _pack(arrays):
    parts = []
    for a in arrays:
        flat = a.reshape(-1)
        n = flat.shape[0]
        padded = -(-n // 1024) * 1024
        parts.append(jnp.pad(flat, (0, padded - n)).reshape(padded // 128, 128))
    return jnp.concatenate(parts, axis=0)


def _unpack(packed, shapes):
    out, row = [], 0
    for shp in shapes:
        n = int(np.prod(shp))
        rows = -(-n // 1024) * 8
        out.append(packed[row:row + rows].reshape(-1)[:n].reshape(shp))
        row += rows
    return out


def _t5_bucket_table():
    qi = np.arange(BLOCK)[:, None]
    kj = np.arange(2 * BLOCK)[None, :]
    n = np.clip(qi + BLOCK - kj, 0, None)
    max_exact = N_BUCKETS // 2
    large = max_exact + (np.log(np.maximum(n, 1) / max_exact) / np.log(MAX_DIST / max_exact)
                         * (N_BUCKETS - max_exact)).astype(np.int32)
    large = np.minimum(large, N_BUCKETS - 1)
    return np.where(n < max_exact, n, large).astype(np.int32)


SMALL_NAMES = ("rel_bias", "norm_mix_g", "q_norm_g", "k_norm_g", "sinks", "conv_b", "conv_ln_g", "conv_ln_b",
               "attn_out_g", "conv_out_g", "norm_mlp_g")


def _local_step(x, target, small, conv_w, wt_in, w_out, wt_up, w_down):
    bucket = jnp.asarray(_t5_bucket_table())
    bias = _bias_table(small["rel_bias"], bucket)
    row = lambda a, l: a[l][None, :]
    cw_pad = jnp.pad(conv_w, ((0, 0), (0, HALO - CONV_K), (0, 0)))
    saved = []
    for l in range(DEPTH):
        zq, zkv, zu, zg = _fwd_in(x, row(small["norm_mix_g"], l), wt_in[l])
        a, mix_a = _attn_fwd(zq, zkv, bias, small["sinks"][l], row(small["q_norm_g"], l), row(small["k_norm_g"], l),
                             row(small["attn_out_g"], l))
        y, mix_c = _conv_fwd(zu, zg, cw_pad[l], row(small["conv_b"], l), row(small["conv_ln_g"], l),
                             row(small["conv_ln_b"], l), row(small["conv_out_g"], l))
        x1 = _fwd_out(x, mix_a, mix_c, w_out[l])
        x2, up = _mlp_fwd(x1, row(small["norm_mlp_g"], l), wt_up[l], w_down[l])
        saved.append((x, zq, zkv, zu, zg, a, mix_a, y, mix_c, x1, up))
        x = x2
    loss_part, dx = _loss_head(x, target)

    gs = {n: [None] * DEPTH for n in SMALL_NAMES if n != "rel_bias"}
    g_conv_w, g_in, g_out, g_up, g_down, dbias = ([None] * DEPTH for _ in range(6))
    for l in reversed(range(DEPTH)):
        x0, zq, zkv, zu, zg, a, mix_a, y, mix_c, x1, up = saved[l]
        dx1, dup, h2, gs["norm_mlp_g"][l] = _mlp_bwd(dx, x1, up, row(small["norm_mlp_g"], l), wt_up[l], w_down[l])
        g_up[l] = _wgrad(dup, h2, "wgrad_up", chunked=True)
        g_down[l] = _wgrad(up, dx, "wgrad_down", chunked=True, square_relu=True)
        dma, dmc, dx1b = _bwd_out(dx1, w_out[l])
        g_out[l] = jnp.concatenate([_wgrad(mix_a, dx1b, "wgrad_out_a"), _wgrad(mix_c, dx1b, "wgrad_out_c")], axis=0)
        du, dgt, pg = _conv_bwd(dmc, y, zu, zg, cw_pad[l], row(small["conv_ln_g"], l), row(small["conv_ln_b"], l),
                                row(small["conv_out_g"], l))
        g_conv_w[l] = pg[:CONV_K]
        gs["conv_b"][l], gs["conv_ln_g"][l], gs["conv_ln_b"][l], gs["conv_out_g"][l] = pg[32], pg[33], pg[34], pg[35]
        dq, dkv, dbias[l], dog, dqg, dkg, dsk = _attn_bwd(
            dma, a, zq, zkv, bias, small["sinks"][l], row(small["q_norm_g"], l), row(small["k_norm_g"], l),
            row(small["attn_out_g"], l))
        gs["attn_out_g"][l], gs["q_norm_g"][l], gs["k_norm_g"][l], gs["sinks"][l] = dog[0], dqg[0], dkg[0], dsk[0, :N_HEADS]
        dx, h, gs["norm_mix_g"][l] = _bwd_in(dx1, x0, dq, dkv, du, dgt, row(small["norm_mix_g"], l), wt_in[l])
        g_in[l] = jnp.concatenate([_wgrad(dq, h, "wgrad_in_q"), _wgrad(dkv, h, "wgrad_in_kv"),
                                   _wgrad(du, h, "wgrad_in_u"), _wgrad(dgt, h, "wgrad_in_g")], axis=0)
    small_grads = {n: jnp.stack([jnp.reshape(v, (-1,)) for v in vals]) for n, vals in gs.items()}
    small_grads["rel_bias"] = _bias_grad(jnp.stack(dbias), bucket)[:, :N_BUCKETS].T
    big = tuple(jnp.stack(g) for g in (g_in, g_out, g_up, g_down))
    return loss_part, dx, small_grads, jnp.stack(g_conv_w), big


def kernel(x, rel_bias, norm_mix_g, w_in, q_norm_g, k_norm_g, sinks, conv_w, conv_b, conv_ln_g, conv_ln_b, attn_out_g, conv_out_g, w_out, norm_mlp_g, w_mlp_up, w_mlp_down, loss_target, m_rel_bias, m_norm_mix_g, m_w_in, m_q_norm_g, m_k_norm_g, m_sinks, m_conv_w, m_conv_b, m_conv_ln_g, m_conv_ln_b, m_attn_out_g, m_conv_out_g, m_w_out, m_norm_mlp_g, m_w_mlp_up, m_w_mlp_down, v_rel_bias, v_norm_mix_g, v_w_in, v_q_norm_g, v_k_norm_g, v_sinks, v_conv_w, v_conv_b, v_conv_ln_g, v_conv_ln_b, v_attn_out_g, v_conv_out_g, v_w_out, v_norm_mlp_g, v_w_mlp_up, v_w_mlp_down):
    args = dict(locals())
    small = {n: args[n] for n in SMALL_NAMES}
    tr = lambda a: jnp.swapaxes(a, 1, 2)
    me = 4 * lax.axis_index("x") + 2 * lax.axis_index("y") + lax.axis_index("c")

    shards = (tr(w_in).astype(BF16), w_out.astype(BF16), tr(w_mlp_up).astype(BF16), w_mlp_down.astype(BF16))
    cw_sh = jnp.pad(tr(conv_w), ((0, 0), (0, 0), (0, HALO - CONV_K)))
    wt_in, w_out_f, wt_up, w_down_f, cw_all = _all_gather((*shards, cw_sh))
    conv_w_full = tr(cw_all)[:, :CONV_K, :]

    loss_part, dx, small_grads, g_conv_w, big = _local_step(
        x[0], loss_target[0], small, conv_w_full, wt_in, w_out_f, wt_up, w_down_f)

    parts = _all_to_all(big)
    big_w = (tr(w_in), w_out, tr(w_mlp_up), w_mlp_down)
    big_m = (tr(m_w_in), m_w_out, tr(m_w_mlp_up), m_w_mlp_down)
    big_v = (tr(v_w_in), v_w_out, tr(v_w_mlp_up), v_w_mlp_down)
    big_out = [_adam_big(p, w, m, v) for p, w, m, v in zip(parts, big_w, big_m, big_v)]
    for k in (0, 2):
        big_out[k] = [tr(o) for o in big_out[k]]

    order = [n for n in SMALL_NAMES]
    packed = _pack([small_grads[n] for n in order] + [g_conv_w])
    gathered, = _all_gather((packed[None],))
    summed = _sum_slots(gathered[0].reshape(N_DEV, packed.shape[0], 128))
    shapes = [small[n].shape for n in order] + [(DEPTH, CONV_K, CONV_W)]
    sg = _unpack(summed, shapes)
    g_small = dict(zip(order, sg[:-1]))
    g_small["conv_w"] = lax.dynamic_slice_in_dim(sg[-1], me * (CONV_W // N_DEV), CONV_W // N_DEV, axis=2)
    names = order + ["conv_w"]
    shapes = [args[n].shape for n in names]
    d_p, m_p, v_p = _adam_small(_pack([args[n] for n in names]), _pack([g_small[n] for n in names]),
                                _pack([args["m_" + n] for n in names]), _pack([args["v_" + n] for n in names]))
    res = {"grad": g_small, "delta": dict(zip(names, _unpack(d_p, shapes))),
           "new_m": dict(zip(names, _unpack(m_p, shapes))), "new_v": dict(zip(names, _unpack(v_p, shapes)))}
    for k, n in enumerate(("w_in", "w_out", "w_mlp_up", "w_mlp_down")):
        for kind, val in zip(("grad", "delta", "new_m", "new_v"), big_out[k]):
            res[kind][n] = val

    loss = lax.psum(loss_part[0, 0], ("x", "y", "c"))
    weights = ("rel_bias", "norm_mix_g", "w_in", "q_norm_g", "k_norm_g", "sinks", "conv_w", "conv_b", "conv_ln_g",
               "conv_ln_b", "attn_out_g", "conv_out_g", "w_out", "norm_mlp_g", "w_mlp_up", "w_mlp_down")
    return (loss, dx[None], *[res[kind][n] for kind in ("grad", "delta", "new_m", "new_v") for n in weights])
```

```python
import math

import numpy as np
import jax
import jax.numpy as jnp
from jax import lax
from jax.experimental import pallas as pl
from jax.experimental.pallas import tpu as pltpu

F32, BF16 = jnp.float32, jnp.bfloat16
D_MODEL = 1024
DEPTH = 4
HEAD_DIM = 64
N_HEADS = 8
N_KV = 2
GQA = N_HEADS // N_KV
ATTN_W = N_HEADS * HEAD_DIM
KV_W = N_KV * HEAD_DIM
CONV_W = D_MODEL - ATTN_W
IN_W = ATTN_W + 2 * KV_W + 2 * CONV_W
BLOCK = 128
CONV_K = 31
HALO = 32
N_BUCKETS = 32
MAX_DIST = 128
D_FF = 4 * D_MODEL
FF_CHUNK = 512
N_FF = D_FF // FF_CHUNK
EPS = 1e-6
NEG = -1e30
N_DEV = 8
ADAM_LR, ADAM_B1, ADAM_B2, ADAM_EPS, ADAM_WD, ADAM_STEP = 0.001, 0.9, 0.999, 1e-08, 0.01, 10
VMEM_LIMIT = 56 * 1024 * 1024
MESH = pl.DeviceIdType.MESH

RESIDENT = pl.BlockSpec(memory_space=pltpu.VMEM)
IN_SMEM = pl.BlockSpec(memory_space=pltpu.SMEM)
ANY = pl.BlockSpec(memory_space=pl.ANY)


def _call(body, **kw):
    return pl.pallas_call(body, **kw)


def _params(*sem):
    return pltpu.CompilerParams(dimension_semantics=sem, vmem_limit_bytes=VMEM_LIMIT)


def _dot(a, b):
    return lax.dot_general(a, b, (((1,), (0,)), ((), ())), preferred_element_type=F32)


def _dot_nt(a, b):
    return lax.dot_general(a, b, (((1,), (1,)), ((), ())), preferred_element_type=F32)


def _dot_tn(a, b):
    return lax.dot_general(a, b, (((0,), (0,)), ((), ())), preferred_element_type=F32)


def _sig(x):
    return 1.0 / (1.0 + jnp.exp(-x))


def _rms(x):
    r = lax.rsqrt(jnp.mean(x * x, axis=-1, keepdims=True) + EPS)
    return x * r, r


def _rms_bwd(dy_g, xh, r):
    return r * (dy_g - xh * jnp.mean(dy_g * xh, axis=-1, keepdims=True))


def _rows(tm, w):
    return pl.BlockSpec((tm, w), lambda i: (i, 0))


def _acc_rows(w, rows=1):
    return pl.BlockSpec((rows, w), lambda i: (0, 0))


def _colsum(x):
    return jnp.sum(x, axis=0, keepdims=True)


def _fwd_in(x, g, wt):
    T = x.shape[0]
    tm = 512

    def body(x_ref, g_ref, w_ref, q_ref, kv_ref, u_ref, gt_ref):
        xh, _ = _rms(x_ref[...])
        h = (xh * g_ref[...]).astype(BF16)
        z = _dot_nt(h, w_ref[...])
        q_ref[...] = z[:, :ATTN_W]
        kv_ref[...] = z[:, ATTN_W:ATTN_W + 2 * KV_W]
        u_ref[...] = z[:, ATTN_W + 2 * KV_W:ATTN_W + 2 * KV_W + CONV_W]
        gt_ref[...] = z[:, ATTN_W + 2 * KV_W + CONV_W:]

    widths = (ATTN_W, 2 * KV_W, CONV_W, CONV_W)
    return _call(
        body, name="fwd_in", grid=(T // tm,),
        in_specs=[_rows(tm, D_MODEL), RESIDENT, RESIDENT],
        out_specs=[_rows(tm, w) for w in widths],
        out_shape=[jax.ShapeDtypeStruct((T, w), F32) for w in widths],
        compiler_params=_params("parallel"),
    )(x, g, wt)


def _band_masks(n):
    qi = lax.broadcasted_iota(jnp.int32, (BLOCK, 2 * BLOCK), 0)
    kj = lax.broadcasted_iota(jnp.int32, (BLOCK, 2 * BLOCK), 1)
    dist = qi + BLOCK - kj
    valid = (dist >= 0) & (dist < BLOCK) & ((kj >= BLOCK) | (n > 0))
    return jnp.concatenate([valid] * GQA, axis=0)


def _head_rows(g):
    row = lax.broadcasted_iota(jnp.int32, (GQA * BLOCK, 1), 0)
    return (row >= g * BLOCK) & (row < (g + 1) * BLOCK)


def _attn_group(zq, kv, kh, bias_ref, sinks_ref, qg, kg, valid4):
    scale = 1.0 / math.sqrt(HEAD_DIM)
    kraw = kv[:, kh * HEAD_DIM:(kh + 1) * HEAD_DIM]
    khat, rk = _rms(kraw)
    kn = (khat * kg).astype(BF16)
    v = kv[:, KV_W + kh * HEAD_DIM:KV_W + (kh + 1) * HEAD_DIM].astype(BF16)
    qraw = jnp.concatenate(
        [zq[:, (kh * GQA + g) * HEAD_DIM:(kh * GQA + g + 1) * HEAD_DIM] for g in range(GQA)], axis=0)
    qhat, rq = _rms(qraw)
    qn = (qhat * qg).astype(BF16)
    s = _dot_nt(qn, kn) * scale + bias_ref[kh * GQA:(kh + 1) * GQA].reshape(GQA * BLOCK, 2 * BLOCK)
    s = jnp.where(valid4, s, NEG)
    sink = jnp.zeros((GQA * BLOCK, 1), F32)
    for g in range(GQA):
        sink = jnp.where(_head_rows(g), sinks_ref[kh * GQA + g], sink)
    m = jnp.maximum(jnp.max(s, axis=-1, keepdims=True), sink)
    p = jnp.exp(s - m)
    es = jnp.exp(sink - m)
    den = jnp.sum(p, axis=-1, keepdims=True) + es
    pn = p / den
    return dict(khat=khat, rk=rk, kn=kn, v=v, qhat=qhat, rq=rq, qn=qn, pn=pn, psink=es / den)


def _unstack_heads(o):
    return [o[g * BLOCK:(g + 1) * BLOCK] for g in range(GQA)]


def _attn_fwd(zq, zkv, bias, sinks, qg, kg, og):
    T = zq.shape[0]
    nb = T // BLOCK

    def body(q_ref, kvc_ref, kvp_ref, bias_ref, sinks_ref, qg_ref, kg_ref, og_ref, a_ref, mix_ref):
        n = pl.program_id(0)
        zq_v = q_ref[...]
        kv = jnp.concatenate([kvp_ref[...], kvc_ref[...]], axis=0)
        valid4 = _band_masks(n)
        outs = []
        for kh in range(N_KV):
            c = _attn_group(zq_v, kv, kh, bias_ref, sinks_ref, qg_ref[...], kg_ref[...], valid4)
            outs += _unstack_heads(_dot(c["pn"].astype(BF16), c["v"]))
        a = jnp.concatenate(outs, axis=-1)
        a_ref[...] = a
        ah, _ = _rms(a)
        mix_ref[...] = (ah * og_ref[...]).astype(BF16)

    return _call(
        body, name="attn_fwd", grid=(nb,),
        in_specs=[_rows(BLOCK, ATTN_W), _rows(BLOCK, 2 * KV_W),
                  pl.BlockSpec((BLOCK, 2 * KV_W), lambda n: (jnp.maximum(n - 1, 0), 0)),
                  RESIDENT, IN_SMEM, RESIDENT, RESIDENT, RESIDENT],
        out_specs=[_rows(BLOCK, ATTN_W), _rows(BLOCK, ATTN_W)],
        out_shape=[jax.ShapeDtypeStruct((T, ATTN_W), F32), jax.ShapeDtypeStruct((T, ATTN_W), BF16)],
        compiler_params=_params("parallel"),
    )(zq, zkv, zkv, bias, sinks, qg, kg, og)


def _conv_post(y, lg, lb, og):
    mu = jnp.mean(y, axis=-1, keepdims=True)
    yc = y - mu
    rstd = lax.rsqrt(jnp.mean(yc * yc, axis=-1, keepdims=True) + EPS)
    yn = yc * rstd
    ln = yn * lg + lb
    sg = _sig(ln)
    c = ln * sg
    ch, r = _rms(c)
    return yn, rstd, ln, sg, ch, r


def _conv_fwd(zu, zg, cw, cb, lg, lb, og):
    T = zu.shape[0]
    tt = 512
    halo_spec = pl.BlockSpec((HALO, CONV_W), lambda i: (jnp.maximum(i * (tt // HALO) - 1, 0), 0))

    def body(u_ref, g_ref, uh_ref, gh_ref, cw_ref, cb_ref, lg_ref, lb_ref, og_ref, y_ref, mix_ref, buf):
        i = pl.program_id(0)
        hal = uh_ref[...] * _sig(gh_ref[...])
        buf[0:HALO, :] = jnp.where(i > 0, hal, 0.0)
        buf[HALO:HALO + tt, :] = u_ref[...] * _sig(g_ref[...])
        acc = jnp.zeros((tt, CONV_W), F32)
        for j in range(CONV_K):
            acc = acc + cw_ref[j:j + 1, :] * buf[HALO - (CONV_K - 1) + j:HALO - (CONV_K - 1) + j + tt, :]
        y = acc + cb_ref[...]
        y_ref[...] = y
        _, _, _, _, ch, _ = _conv_post(y, lg_ref[...], lb_ref[...], og_ref[...])
        mix_ref[...] = (ch * og_ref[...]).astype(BF16)

    return _call(
        body, name="conv_fwd", grid=(T // tt,),
        in_specs=[_rows(tt, CONV_W), _rows(tt, CONV_W), halo_spec, halo_spec] + [RESIDENT] * 5,
        out_specs=[_rows(tt, CONV_W), _rows(tt, CONV_W)],
        out_shape=[jax.ShapeDtypeStruct((T, CONV_W), F32), jax.ShapeDtypeStruct((T, CONV_W), BF16)],
        scratch_shapes=[pltpu.VMEM((HALO + tt, CONV_W), F32)],
        compiler_params=_params("parallel"),
    )(zu, zg, zu, zg, cw, cb, lg, lb, og)


def _fwd_out(x, mix_a, mix_c, w_out):
    T = x.shape[0]
    tm = 512

    def body(x_ref, a_ref, c_ref, w_ref, o_ref):
        o_ref[...] = (x_ref[...] + _dot(a_ref[...], w_ref[0:ATTN_W, :])
                      + _dot(c_ref[...], w_ref[ATTN_W:, :]))

    return _call(
        body, name="fwd_out", grid=(T // tm,),
        in_specs=[_rows(tm, D_MODEL), _rows(tm, ATTN_W), _rows(tm, CONV_W), RESIDENT],
        out_specs=_rows(tm, D_MODEL),
        out_shape=jax.ShapeDtypeStruct((T, D_MODEL), F32),
        compiler_params=_params("parallel"),
    )(x, mix_a, mix_c, w_out)


def _chunked(tm):
    return pl.BlockSpec((N_FF, tm, FF_CHUNK), lambda i: (0, i, 0))


def _mlp_fwd(x, g, wup_t, wdown):
    T = x.shape[0]
    tm = 512

    def body(x_ref, g_ref, wu_ref, wd_ref, o_ref, up_ref):
        xv = x_ref[...]
        xh, _ = _rms(xv)
        h = (xh * g_ref[...]).astype(BF16)
        acc = xv
        for c in range(N_FF):
            rows = slice(c * FF_CHUNK, (c + 1) * FF_CHUNK)
            up = _dot_nt(h, wu_ref[rows, :])
            up_ref[c] = up.astype(BF16)
            act = jnp.square(jnp.maximum(up, 0.0))
            acc = acc + _dot(act.astype(BF16), wd_ref[rows, :])
        o_ref[...] = acc

    return _call(
        body, name="mlp_fwd", grid=(T // tm,),
        in_specs=[_rows(tm, D_MODEL), RESIDENT, RESIDENT, RESIDENT],
        out_specs=[_rows(tm, D_MODEL), _chunked(tm)],
        out_shape=[jax.ShapeDtypeStruct((T, D_MODEL), F32), jax.ShapeDtypeStruct((N_FF, T, FF_CHUNK), BF16)],
        compiler_params=_params("parallel"),
    )(x, g, wup_t, wdown)


def _loss_head(y, target):
    T = y.shape[0]
    tm = 512

    def body(y_ref, t_ref, l_ref, d_ref):
        @pl.when(pl.program_id(0) == 0)
        def _():
            l_ref[...] = jnp.zeros_like(l_ref)
        e = y_ref[...] - t_ref[...]
        d_ref[...] = e / D_MODEL
        l_ref[...] += 0.5 * jnp.sum(jnp.mean(e * e, axis=-1, keepdims=True))

    return _call(
        body, name="loss_head", grid=(T // tm,),
        in_specs=[_rows(tm, D_MODEL), _rows(tm, D_MODEL)],
        out_specs=[_acc_rows(128, 8), _rows(tm, D_MODEL)],
        out_shape=[jax.ShapeDtypeStruct((8, 128), F32), jax.ShapeDtypeStruct((T, D_MODEL), F32)],
        compiler_params=_params("arbitrary"),
    )(y, target)


def _bias_table(rel_bias, bucket):
    def body(rb_ref, bk_ref, o_ref):
        bk = bk_ref[...]
        for h in range(N_HEADS):
            acc = jnp.zeros((BLOCK, 2 * BLOCK), F32)
            for b in range(N_BUCKETS):
                acc = jnp.where(bk == b, rb_ref[b, h], acc)
            o_ref[h] = acc

    return _call(
        body, name="bias_table", in_specs=[IN_SMEM, RESIDENT], out_specs=RESIDENT,
        out_shape=jax.ShapeDtypeStruct((N_HEADS, BLOCK, 2 * BLOCK), F32),
    )(rel_bias, bucket)


def _mlp_bwd(dx2, x1, up, g, wup_t, wdown):
    T = x1.shape[0]
    tm = 512

    def body(d_ref, x_ref, up_ref, g_ref, wu_ref, wd_ref, dx_ref, dup_ref, h_ref, dg_ref):
        @pl.when(pl.program_id(0) == 0)
        def _():
            dg_ref[...] = jnp.zeros_like(dg_ref)
        d2 = d_ref[...]
        d2b = d2.astype(BF16)
        xh, r = _rms(x_ref[...])
        gv = g_ref[...]
        h_ref[...] = (xh * gv).astype(BF16)
        dh = jnp.zeros((tm, D_MODEL), F32)
        for c in range(N_FF):
            rows = slice(c * FF_CHUNK, (c + 1) * FF_CHUNK)
            dact = _dot_nt(d2b, wd_ref[rows, :])
            dup = (dact * (2.0 * jnp.maximum(up_ref[c].astype(F32), 0.0))).astype(BF16)
            dup_ref[c] = dup
            dh = dh + _dot(dup, wu_ref[rows, :])
        dg_ref[...] += _colsum(dh * xh)
        dx_ref[...] = d2 + _rms_bwd(dh * gv, xh, r)

    return _call(
        body, name="mlp_bwd", grid=(T // tm,),
        in_specs=[_rows(tm, D_MODEL), _rows(tm, D_MODEL), _chunked(tm), RESIDENT, RESIDENT, RESIDENT],
        out_specs=[_rows(tm, D_MODEL), _chunked(tm), _rows(tm, D_MODEL), _acc_rows(D_MODEL)],
        out_shape=[jax.ShapeDtypeStruct((T, D_MODEL), F32), jax.ShapeDtypeStruct((N_FF, T, FF_CHUNK), BF16),
                   jax.ShapeDtypeStruct((T, D_MODEL), BF16), jax.ShapeDtypeStruct((1, D_MODEL), F32)],
        compiler_params=_params("arbitrary"),
    )(dx2, x1, up, g, wup_t, wdown)


def _wgrad(a, b, name, chunked=False, square_relu=False):
    if chunked:
        nr, T, tr = a.shape
    else:
        T, R = a.shape
        tr = min(R, FF_CHUNK)
        nr = R // tr
    tk = 1024
    nk = T // tk

    def body(a_ref, b_ref, o_ref, acc):
        k = pl.program_id(1)

        @pl.when(k == 0)
        def _():
            acc[...] = jnp.zeros_like(acc)
        av = a_ref[...]
        if square_relu:
            av = jnp.square(jnp.maximum(av.astype(F32), 0.0))
        acc[...] += _dot_tn(av.astype(BF16), b_ref[...].astype(BF16))

        @pl.when(k == nk - 1)
        def _():
            o_ref[...] = acc[...].astype(BF16)

    a_spec = (pl.BlockSpec((None, tk, tr), lambda r, k: (r, k, 0)) if chunked
              else pl.BlockSpec((tk, tr), lambda r, k: (k, r)))
    return _call(
        body, name=name, grid=(nr, nk),
        in_specs=[a_spec, pl.BlockSpec((tk, D_MODEL), lambda r, k: (k, 0))],
        out_specs=pl.BlockSpec((tr, D_MODEL), lambda r, k: (r, 0)),
        out_shape=jax.ShapeDtypeStruct((nr * tr, D_MODEL), BF16),
        scratch_shapes=[pltpu.VMEM((tr, D_MODEL), F32)],
        compiler_params=_params("parallel", "arbitrary"),
    )(a, b)


def _bwd_out(dx1, w_out):
    T = dx1.shape[0]
    tm = 512

    def body(d_ref, w_ref, da_ref, dc_ref, db_ref):
        db = d_ref[...].astype(BF16)
        db_ref[...] = db
        dm = _dot_nt(db, w_ref[...])
        da_ref[...] = dm[:, :ATTN_W]
        dc_ref[...] = dm[:, ATTN_W:]

    return _call(
        body, name="bwd_out", grid=(T // tm,),
        in_specs=[_rows(tm, D_MODEL), RESIDENT],
        out_specs=[_rows(tm, ATTN_W), _rows(tm, CONV_W), _rows(tm, D_MODEL)],
        out_shape=[jax.ShapeDtypeStruct((T, ATTN_W), F32), jax.ShapeDtypeStruct((T, CONV_W), F32),
                   jax.ShapeDtypeStruct((T, D_MODEL), BF16)],
        compiler_params=_params("parallel"),
    )(dx1, w_out)


def _conv_bwd(dmix, y, zu, zg, cw, lg, lb, og):
    T = y.shape[0]
    tt = 512
    nt = T // tt
    per = tt // HALO
    prev_spec = pl.BlockSpec((HALO, CONV_W), lambda i: (jnp.maximum(i * per - 1, 0), 0))
    next_spec = pl.BlockSpec((HALO, CONV_W), lambda i: (jnp.minimum((i + 1) * per, nt * per - 1), 0))

    def body(dm_ref, dmn_ref, y_ref, yn_ref, u_ref, g_ref, uh_ref, gh_ref, cw_ref, lg_ref, lb_ref, og_ref,
             du_ref, dg_ref, pg_ref, hbuf, dybuf):
        i = pl.program_id(0)

        @pl.when(i == 0)
        def _():
            pg_ref[...] = jnp.zeros_like(pg_ref)
        lgv, lbv, ogv = lg_ref[...], lb_ref[...], og_ref[...]

        def chain(yv, dm):
            yn, rstd, ln, sg, ch, r = _conv_post(yv, lgv, lbv, ogv)
            dc = _rms_bwd(dm * ogv, ch, r)
            dln = dc * sg * (1.0 + ln * (1.0 - sg))
            dyn = dln * lgv
            dy = rstd * (dyn - jnp.mean(dyn, axis=-1, keepdims=True)
                         - yn * jnp.mean(dyn * yn, axis=-1, keepdims=True))
            return dy, dm * ch, dln * yn, dln

        dy, p_og, p_lg, p_lb = chain(y_ref[...], dm_ref[...])
        dyh, _, _, _ = chain(yn_ref[...], dmn_ref[...])
        dybuf[0:tt, :] = dy
        dybuf[tt:tt + HALO, :] = jnp.where(i < nt - 1, dyh, 0.0)
        sgt = _sig(g_ref[...])
        uv = u_ref[...]
        hbuf[0:HALO, :] = jnp.where(i > 0, uh_ref[...] * _sig(gh_ref[...]), 0.0)
        hbuf[HALO:HALO + tt, :] = uv * sgt
        dh = jnp.zeros((tt, CONV_W), F32)
        lo = HALO - (CONV_K - 1)
        for j in range(CONV_K):
            dh = dh + cw_ref[j:j + 1, :] * dybuf[CONV_K - 1 - j:CONV_K - 1 - j + tt, :]
            pg_ref[j:j + 1, :] += _colsum(dy * hbuf[lo + j:lo + j + tt, :])
        pg_ref[32:33, :] += _colsum(dy)
        pg_ref[33:34, :] += _colsum(p_lg)
        pg_ref[34:35, :] += _colsum(p_lb)
        pg_ref[35:36, :] += _colsum(p_og)
        du_ref[...] = (dh * sgt).astype(BF16)
        dg_ref[...] = (dh * uv * sgt * (1.0 - sgt)).astype(BF16)

    return _call(
        body, name="conv_bwd", grid=(nt,),
        in_specs=[_rows(tt, CONV_W), next_spec, _rows(tt, CONV_W), next_spec, _rows(tt, CONV_W), _rows(tt, CONV_W),
                  prev_spec, prev_spec] + [RESIDENT] * 4,
        out_specs=[_rows(tt, CONV_W), _rows(tt, CONV_W), _acc_rows(CONV_W, 40)],
        out_shape=[jax.ShapeDtypeStruct((T, CONV_W), BF16), jax.ShapeDtypeStruct((T, CONV_W), BF16),
                   jax.ShapeDtypeStruct((40, CONV_W), F32)],
        scratch_shapes=[pltpu.VMEM((HALO + tt, CONV_W), F32), pltpu.VMEM((tt + HALO, CONV_W), F32)],
        compiler_params=_params("arbitrary"),
    )(dmix, dmix, y, y, zu, zg, zu, zg, cw, lg, lb, og)


def _attn_bwd(dmix, a, zq, zkv, bias, sinks, qg, kg, og):
    T = zq.shape[0]
    nb = T // BLOCK
    last = nb - 1
    cur = lambda w: pl.BlockSpec((BLOCK, w), lambda n: (jnp.minimum(n, last), 0))
    prev = lambda w: pl.BlockSpec((BLOCK, w), lambda n: (jnp.clip(n - 1, 0, last), 0))
    scale = 1.0 / math.sqrt(HEAD_DIM)

    def body(dm_ref, a_ref, q_ref, kvc_ref, kvp_ref, bias_ref, sinks_ref, qg_ref, kg_ref, og_ref,
             dq_ref, dkv_ref, db_ref, dog_ref, dqg_ref, dkg_ref, dsk_ref, carry):
        n = pl.program_id(0)

        @pl.when(n == 0)
        def _():
            for ref in (db_ref, dog_ref, dqg_ref, dkg_ref, dsk_ref, carry):
                ref[...] = jnp.zeros_like(ref)

        @pl.when(n < nb)
        def _():
            ah, ra = _rms(a_ref[...])
            dm = dm_ref[...]
            ogv, qgv, kgv = og_ref[...], qg_ref[...], kg_ref[...]
            dog_ref[...] += _colsum(dm * ah)
            da = _rms_bwd(dm * ogv, ah, ra)
            zq_v = q_ref[...]
            kv = jnp.concatenate([kvp_ref[...], kvc_ref[...]], axis=0)
            valid4 = _band_masks(n)
            lane =lax.broadcasted_iota(jnp.int32, (1, 128), 1)
            dqs, dks, dvs = [], [], []
            dsk = jnp.zeros((1, 128), F32)
            for kh in range(N_KV):
                c = _attn_group(zq_v, kv, kh, bias_ref, sinks_ref, qgv, kgv, valid4)
                do = jnp.concatenate(
                    [da[:, (kh * GQA + g) * HEAD_DIM:(kh * GQA + g + 1) * HEAD_DIM] for g in range(GQA)], axis=0)
                dob = do.astype(BF16)
                pn = c["pn"]
                dvs.append(_dot_tn(pn.astype(BF16), dob))
                dp = _dot_nt(dob, c["v"])
                dl = jnp.sum(pn * dp, axis=-1, keepdims=True)
                ds = pn * (dp - dl)
                dsr = -c["psink"] * dl
                for g in range(GQA):
                    dsk = dsk + jnp.where(lane == kh * GQA + g, jnp.sum(jnp.where(_head_rows(g), dsr, 0.0)), 0.0)
                db_ref[kh * GQA:(kh + 1) * GQA] += ds.reshape(GQA, BLOCK, 2 * BLOCK)
                dsb = (ds * scale).astype(BF16)
                dqn = _dot(dsb, c["kn"])
                dkn = _dot_tn(dsb, c["qn"])
                dqg_ref[...] += _colsum(dqn * c["qhat"])
                dkg_ref[...] += _colsum(dkn * c["khat"])
                dqs += _unstack_heads(_rms_bwd(dqn * qgv, c["qhat"], c["rq"]))
                dks.append(_rms_bwd(dkn * kgv, c["khat"], c["rk"]))
            dsk_ref[...] += dsk
            dq_ref[...] = jnp.concatenate(dqs, axis=-1).astype(BF16)
            dkv = jnp.concatenate(dks + dvs, axis=-1)
            dkv_ref[...] = (carry[...] + dkv[:BLOCK]).astype(BF16)
            carry[...] = dkv[BLOCK:]

        @pl.when(n == nb)
        def _():
            dkv_ref[...] = carry[...].astype(BF16)

    small = lambda w: pl.BlockSpec((1, w), lambda n: (0, 0))
    return _call(
        body, name="attn_bwd", grid=(nb + 1,),
        in_specs=[cur(ATTN_W), cur(ATTN_W), cur(ATTN_W), cur(2 * KV_W), prev(2 * KV_W),
                  RESIDENT, IN_SMEM, RESIDENT, RESIDENT, RESIDENT],
        out_specs=[cur(ATTN_W), prev(2 * KV_W),
                   pl.BlockSpec((N_HEADS, BLOCK, 2 * BLOCK), lambda n: (0, 0, 0)),
                   small(ATTN_W), small(HEAD_DIM), small(HEAD_DIM), small(128)],
        out_shape=[jax.ShapeDtypeStruct((T, ATTN_W), BF16), jax.ShapeDtypeStruct((T, 2 * KV_W), BF16),
                   jax.ShapeDtypeStruct((N_HEADS, BLOCK, 2 * BLOCK), F32),
                   jax.ShapeDtypeStruct((1, ATTN_W), F32), jax.ShapeDtypeStruct((1, HEAD_DIM), F32),
                   jax.ShapeDtypeStruct((1, HEAD_DIM), F32), jax.ShapeDtypeStruct((1, 128), F32)],
        scratch_shapes=[pltpu.VMEM((BLOCK, 2 * KV_W), F32)],
        compiler_params=_params("arbitrary"),
    )(dmix, a, zq, zkv, zkv, bias, sinks, qg, kg, og)


def _bwd_in(dx1, x, dq, dkv, du, dgt, g, wt):
    T = x.shape[0]
    tm = 512

    def body(d1_ref, x_ref, dq_ref, dkv_ref, du_ref, dgt_ref, g_ref, w_ref, dx_ref, h_ref, dg_ref):
        @pl.when(pl.program_id(0) == 0)
        def _():
            dg_ref[...] = jnp.zeros_like(dg_ref)
        xh, r = _rms(x_ref[...])
        gv = g_ref[...]
        h_ref[...] = (xh * gv).astype(BF16)
        dz = jnp.concatenate([dq_ref[...], dkv_ref[...], du_ref[...], dgt_ref[...]], axis=-1)
        dh = _dot(dz, w_ref[...])
        dg_ref[...] += _colsum(dh * xh)
        dx_ref[...] = d1_ref[...] + _rms_bwd(dh * gv, xh, r)

    return _call(
        body, name="bwd_in", grid=(T // tm,),
        in_specs=[_rows(tm, D_MODEL), _rows(tm, D_MODEL), _rows(tm, ATTN_W), _rows(tm, 2 * KV_W),
                  _rows(tm, CONV_W), _rows(tm, CONV_W), RESIDENT, RESIDENT],
        out_specs=[_rows(tm, D_MODEL), _rows(tm, D_MODEL), _acc_rows(D_MODEL)],
        out_shape=[jax.ShapeDtypeStruct((T, D_MODEL), F32), jax.ShapeDtypeStruct((T, D_MODEL), BF16),
                   jax.ShapeDtypeStruct((1, D_MODEL), F32)],
        compiler_params=_params("arbitrary"),
    )(dx1, x, dq, dkv, du, dgt, g, wt)


def _bias_grad(db, bucket):
    def body(db_ref, bk_ref, o_ref):
        tot = db_ref[0]
        for l in range(1, DEPTH):
            tot = tot + db_ref[l]
        bk = bk_ref[...]
        lane = lax.broadcasted_iota(jnp.int32, (N_HEADS, 128), 1)
        out = jnp.zeros((N_HEADS, 128), F32)
        for b in range(N_BUCKETS):
            v = jnp.sum(jnp.sum(jnp.where(bk == b, tot, 0.0), axis=2), axis=1, keepdims=True)
            out = jnp.where(lane == b, v, out)
        o_ref[...] = out

    return _call(
        body, name="bias_grad", in_specs=[RESIDENT, RESIDENT], out_specs=RESIDENT,
        out_shape=jax.ShapeDtypeStruct((N_HEADS, 128), F32),
        compiler_params=pltpu.CompilerParams(vmem_limit_bytes=VMEM_LIMIT),
    )(db, bucket)


def _place():
    return lax.axis_index("x"), lax.axis_index("y"), lax.axis_index("c")


def _all_gather(shards):
    na = len(shards)

    def body(*refs):
        ins, outs = refs[:na], refs[na:2 * na]
        send_sems, recv_sems, local_sems = refs[2 * na:]
        x, y, c = _place()
        me, sibling = (x, y, c), (x, y, 1 - c)
        chips = [(1 - x, y), (x, 1 - y), (1 - x, 1 - y)]

        def rows(a, p):
            r = ins[a].shape[1]
            return outs[a].at[:, pl.ds((4 * p[0] + 2 * p[1] + p[2]) * r, r), :]

        def copy(a, k, block, to, src=None):
            return pltpu.make_async_remote_copy(
                src_ref=rows(a, block) if src is None else src, dst_ref=rows(a, block),
                send_sem=send_sems.at[a, k], recv_sem=recv_sems.at[a, k], device_id=to, device_id_type=MESH)

        mine = [pltpu.make_async_copy(ins[a], rows(a, me), local_sems.at[a]) for a in range(na)]
        for cp in mine:
            cp.start()
        first = []
        for a in range(na):
            first.append(copy(a, 0, me, sibling, src=ins[a]))
            first += [copy(a, 1 + j, me, (*chip, c), src=ins[a]) for j, chip in enumerate(chips)]
        for cp in first:
            cp.start()
        passed = []
        for j, chip in enumerate(chips):
            for a in range(na):
                copy(a, 1 + j, (*chip, c), me).wait_recv()
                cp = copy(a, 4 + j, (*chip, c), sibling)
                cp.start()
                passed.append(cp)
        for a in range(na):
            copy(a, 0, sibling, me).wait_recv()
            for j, chip in enumerate(chips):
                copy(a, 4 + j, (*chip, 1 - c), me).wait_recv()
        for cp in first + passed:
            cp.wait_send()
        for cp in mine:
            cp.wait()

    return _call(
        body, name="all_gather",
        in_specs=[ANY] * na, out_specs=[ANY] * na,
        out_shape=[jax.ShapeDtypeStruct((s.shape[0], N_DEV * s.shape[1], s.shape[2]), s.dtype) for s in shards],
        scratch_shapes=[pltpu.SemaphoreType.DMA((na, 7)), pltpu.SemaphoreType.DMA((na, 7)),
                        pltpu.SemaphoreType.DMA((na,))],
    )(*shards)


def _all_to_all(fulls):
    na = len(fulls)

    def body(*refs):
        ins, outs = refs[:na], refs[na:2 * na]
        send_sems, recv_sems, local_sems = refs[2 * na:]
        x, y, c = _place()
        me = 4 * x + 2 * y + c

        def peer(k):
            return (x ^ (k >> 2), y ^ ((k >> 1) & 1), c ^ (k & 1))

        def block(a, p):
            r = ins[a].shape[1] // N_DEV
            return ins[a].at[:, pl.ds((4 * p[0] + 2 * p[1] + p[2]) * r, r), :]

        def copy(a, k):
            return pltpu.make_async_remote_copy(
                src_ref=block(a, peer(k)), dst_ref=outs[a].at[me],
                send_sem=send_sems.at[a, k - 1], recv_sem=recv_sems.at[a, k - 1],
                device_id=peer(k), device_id_type=MESH)

        def landing(a, k):
            p = peer(k)
            return pltpu.make_async_remote_copy(
                src_ref=block(a, p), dst_ref=outs[a].at[4 * p[0] + 2 * p[1] + p[2]],
                send_sem=send_sems.at[a, k - 1], recv_sem=recv_sems.at[a, k - 1],
                device_id=p, device_id_type=MESH)

        mine = [pltpu.make_async_copy(block(a, (x, y, c)), outs[a].at[me], local_sems.at[a]) for a in range(na)]
        for cp in mine:
            cp.start()
        sends = [copy(a, k) for k in range(1, N_DEV) for a in range(na)]
        for cp in sends:
            cp.start()
        for k in range(1, N_DEV):
            for a in range(na):
                landing(a, k).wait_recv()
        for cp in sends:
            cp.wait_send()
        for cp in mine:
            cp.wait()

    return _call(
        body, name="all_to_all",
        in_specs=[ANY] * na, out_specs=[ANY] * na,
        out_shape=[jax.ShapeDtypeStruct((N_DEV, f.shape[0], f.shape[1] // N_DEV, f.shape[2]), f.dtype) for f in fulls],
        scratch_shapes=[pltpu.SemaphoreType.DMA((na, 7)), pltpu.SemaphoreType.DMA((na, 7)),
                        pltpu.SemaphoreType.DMA((na,))],
    )(*fulls)


IN_HBM = pl.BlockSpec(memory_space=pltpu.HBM)
IN_SEM = pl.BlockSpec(memory_space=pltpu.SEMAPHORE)
DATAFLOW = pltpu.SideEffectType.DATAFLOW_SIDE_EFFECTING


def _exchange_copies(srcs, lands, send_sems, recv_sems, scatter):
    x, y, c = _place()
    me = 4 * x + 2 * y + c
    copies = []
    for k in range(1, N_DEV):
        p = (x ^ (k >> 2), y ^ ((k >> 1) & 1), c ^ (k & 1))
        for a, (src, land) in enumerate(zip(srcs, lands)):
            r = land.shape[0] // N_DEV
            if scatter:
                src = src.at[pl.ds((4 * p[0] + 2 * p[1] + p[2]) * r, r), :]
            copies.append(pltpu.make_async_remote_copy(
                src_ref=src, dst_ref=land.at[pl.ds(me * r, r), :], send_sem=send_sems.at[a * (N_DEV - 1) + k - 1],
                recv_sem=recv_sems.at[a * (N_DEV - 1) + k - 1], device_id=p, device_id_type=MESH))
    return copies


def _exchange_start(srcs, name, scatter):
    na = len(srcs)
    lands = [lax.empty((s.shape[0] * (1 if scatter else N_DEV), s.shape[1]), s.dtype) for s in srcs]

    def body(*refs):
        ins, lnd = refs[:na], refs[na:2 * na]
        send_sems, recv_sems = refs[2 * na], refs[2 * na + 1]
        token = refs[-1]
        for cp in _exchange_copies(ins, lnd, send_sems, recv_sems, scatter):
            cp.start()
        token[...] = jnp.zeros_like(token)

    hbm = lambda a: pltpu.with_memory_space_constraint(a, pltpu.HBM)
    out = _call(
        body, name=name,
        out_shape=(pltpu.SemaphoreType.DMA((na * (N_DEV - 1),)), pltpu.SemaphoreType.DMA((na * (N_DEV - 1),)),
                   *[pltpu.HBM(a.shape, a.dtype) for a in (*srcs, *lands)], jax.ShapeDtypeStruct((8, 128), F32)),
        in_specs=[IN_HBM] * (2 * na),
        out_specs=(IN_SEM, IN_SEM, *[IN_HBM] * (2 * na), RESIDENT),
        input_output_aliases={i: 2 + i for i in range(2 * na)},
        compiler_params=pltpu.CompilerParams(has_side_effects=DATAFLOW),
    )(*[hbm(a) for a in (*srcs, *lands)])
    return (out[0], out[1], out[2:2 + na], out[2 + na:2 + 2 * na]), out[-1][0, 0]


def _exchange_wait(state, after, name, scatter):
    send_sems, recv_sems, srcs, lands = state
    na = len(srcs)

    def body(*refs):
        ins, lnd = refs[:na], refs[na:2 * na]
        for cp in _exchange_copies(ins, lnd, refs[2 * na], refs[2 * na + 1], scatter):
            cp.wait_send()
            cp.wait_recv()

    out = _call(
        body, name=name,
        out_shape=tuple(pltpu.HBM(a.shape, a.dtype) for a in (*srcs, *lands)),
        in_specs=[IN_HBM] * (2 * na) + [IN_SEM, IN_SEM, ANY],
        out_specs=[IN_HBM] * (2 * na),
        input_output_aliases={i: i for i in range(2 * na)},
        compiler_params=pltpu.CompilerParams(has_side_effects=DATAFLOW),
    )(*srcs, *lands, send_sems, recv_sems, after)
    return out[:na], out[na:]


def _adam_math(w, g, m, v):
    m = ADAM_B1 * m + (1.0 - ADAM_B1) * g
    v = ADAM_B2 * v + (1.0 - ADAM_B2) * jnp.square(g)
    m_hat = m / (1.0 - ADAM_B1 ** ADAM_STEP)
    v_hat = v / (1.0 - ADAM_B2 ** ADAM_STEP)
    delta = -ADAM_LR * (m_hat / (jnp.sqrt(v_hat) + ADAM_EPS) + ADAM_WD * w)
    return delta, m, v


def _adam_big(parts, w, m, v):
    nl, r, cdim = w.shape
    tr = r if r <= 256 else 256
    spec = pl.BlockSpec((None, tr, cdim), lambda l, i: (l, i, 0))

    def body(p_ref, w_ref, m_ref, v_ref, g_ref, d_ref, nm_ref, nv_ref):
        g = p_ref[0].astype(F32)
        for s in range(1, N_DEV):
            g = g + p_ref[s].astype(F32)
        d, nm, nv = _adam_math(w_ref[...], g, m_ref[...], v_ref[...])
        g_ref[...] = g
        d_ref[...] = d
        nm_ref[...] = nm
        nv_ref[...] = nv

    return _call(
        body, name="adam_big", grid=(nl, r // tr),
        in_specs=[pl.BlockSpec((None, N_DEV, tr, cdim), lambda l, i: (l, 0, i, 0)), spec, spec, spec],
        out_specs=[spec] * 4,
        out_shape=[jax.ShapeDtypeStruct(w.shape, F32)] * 4,
        compiler_params=_params("parallel", "parallel"),
    )(parts, w, m, v)


def _sum_slots(parts):
    def body(p_ref, o_ref):
        g = p_ref[0]
        for s in range(1, N_DEV):
            g = g + p_ref[s]
        o_ref[...] = g

    return _call(body, name="sum_slots", in_specs=[RESIDENT], out_specs=RESIDENT,
                 out_shape=jax.ShapeDtypeStruct(parts.shape[1:], F32))(parts)


def _adam_small(w, g, m, v):
    def body(w_ref, g_ref, m_ref, v_ref, d_ref, nm_ref, nv_ref):
        d, nm, nv = _adam_math(w_ref[...], g_ref[...], m_ref[...], v_ref[...])
        d_ref[...] = d
        nm_ref[...] = nm
        nv_ref[...] = nv

    return _call(body, name="adam_small", in_specs=[RESIDENT] * 4, out_specs=[RESIDENT] * 3,
                 out_shape=[jax.ShapeDtypeStruct(w.shape, F32)] * 3)(w, g, m, v)


def _pack(arrays):
    parts = []
    for a in arrays:
        flat = a.reshape(-1)
        n = flat.shape[0]
        padded = -(-n // 1024) * 1024
        parts.append(jnp.pad(flat, (0, padded - n)).reshape(padded // 128, 128))
    return jnp.concatenate(parts, axis=0)


def _unpack(packed, shapes):
    out, row = [], 0
    for shp in shapes:
        n = int(np.prod(shp))
        rows = -(-n // 1024) * 8
        out.append(packed[row:row + rows].reshape(-1)[:n].reshape(shp))
        row += rows
    return out


def _t5_bucket_table():
    qi = np.arange(BLOCK)[:, None]
    kj = np.arange(2 * BLOCK)[None, :]
    n = np.clip(qi + BLOCK - kj, 0, None)
    max_exact = N_BUCKETS // 2
    large = max_exact + (np.log(np.maximum(n, 1) / max_exact) / np.log(MAX_DIST / max_exact)
                         * (N_BUCKETS - max_exact)).astype(np.int32)
    large = np.minimum(large, N_BUCKETS - 1)
    return np.where(n < max_exact, n, large).astype(np.int32)


SMALL_NAMES = ("rel_bias", "norm_mix_g", "q_norm_g", "k_norm_g", "sinks", "conv_b", "conv_ln_g", "conv_ln_b",
               "attn_out_g", "conv_out_g", "norm_mlp_g")


def _local_step(x, target, small, conv_w, get_w, put_g, token):
    bucket = jnp.asarray(_t5_bucket_table())
    bias = _bias_table(small["rel_bias"], bucket)
    row = lambda a, l: a[l][None, :]
    cw_pad = jnp.pad(conv_w, ((0, 0), (0, HALO - CONV_K), (0, 0)))
    saved, weights = [], []
    for l in range(DEPTH):
        weights.append(get_w(l, x))
        wt_in, w_out, wt_up, w_down = weights[l]
        zq, zkv, zu, zg = _fwd_in(x, row(small["norm_mix_g"], l) + (token if l == 0 else 0.0), wt_in)
        a, mix_a = _attn_fwd(zq, zkv, bias, small["sinks"][l], row(small["q_norm_g"], l), row(small["k_norm_g"], l),
                             row(small["attn_out_g"], l))
        y, mix_c = _conv_fwd(zu, zg, cw_pad[l], row(small["conv_b"], l), row(small["conv_ln_g"], l),
                             row(small["conv_ln_b"], l), row(small["conv_out_g"], l))
        x1 = _fwd_out(x, mix_a, mix_c, w_out)
        x2, up = _mlp_fwd(x1, row(small["norm_mlp_g"], l), wt_up, w_down)
        saved.append((x, zq, zkv, zu, zg, a, mix_a, y, mix_c, x1, up))
        x = x2
    loss_part, dx = _loss_head(x, target)

    gs = {n: [None] * DEPTH for n in SMALL_NAMES if n != "rel_bias"}
    g_conv_w, dbias = [None] * DEPTH, [None] * DEPTH
    token = 0.0
    for l in reversed(range(DEPTH)):
        x0, zq, zkv, zu, zg, a, mix_a, y, mix_c, x1, up = saved[l]
        wt_in, w_out, wt_up, w_down = weights[l]
        dx1, dup, h2, gs["norm_mlp_g"][l] = _mlp_bwd(dx, x1, up, row(small["norm_mlp_g"], l) + token, wt_up, w_down)
        g_up = _wgrad(dup, h2, "wgrad_up", chunked=True)
        g_down = _wgrad(up, dx, "wgrad_down", chunked=True, square_relu=True)
        dma, dmc, dx1b = _bwd_out(dx1, w_out)
        g_out = jnp.concatenate([_wgrad(mix_a, dx1b, "wgrad_out_a"), _wgrad(mix_c, dx1b, "wgrad_out_c")], axis=0)
        du, dgt, pg = _conv_bwd(dmc, y, zu, zg, cw_pad[l], row(small["conv_ln_g"], l), row(small["conv_ln_b"], l),
                                row(small["conv_out_g"], l))
        g_conv_w[l] = pg[:CONV_K]
        gs["conv_b"][l], gs["conv_ln_g"][l], gs["conv_ln_b"][l], gs["conv_out_g"][l] = pg[32], pg[33], pg[34], pg[35]
        dq, dkv, dbias[l], dog, dqg, dkg, dsk = _attn_bwd(
            dma, a, zq, zkv, bias, small["sinks"][l], row(small["q_norm_g"], l), row(small["k_norm_g"], l),
            row(small["attn_out_g"], l))
        gs["attn_out_g"][l], gs["q_norm_g"][l], gs["k_norm_g"][l], gs["sinks"][l] = dog[0], dqg[0], dkg[0], dsk[0, :N_HEADS]
        dx, h, gs["norm_mix_g"][l] = _bwd_in(dx1, x0, dq, dkv, du, dgt, row(small["norm_mix_g"], l), wt_in)
        g_in = jnp.concatenate([_wgrad(dq, h, "wgrad_in_q"), _wgrad(dkv, h, "wgrad_in_kv"),
                                _wgrad(du, h, "wgrad_in_u"), _wgrad(dgt, h, "wgrad_in_g")], axis=0)
        token = put_g(l, (g_in, g_out, g_up, g_down))
    small_grads = {n: jnp.stack([jnp.reshape(v, (-1,)) for v in vals]) for n, vals in gs.items()}
    small_grads["rel_bias"] = _bias_grad(jnp.stack(dbias), bucket)[:, :N_BUCKETS].T
    return loss_part, dx, small_grads, jnp.stack(g_conv_w)


def kernel(x, rel_bias, norm_mix_g, w_in, q_norm_g, k_norm_g, sinks, conv_w, conv_b, conv_ln_g, conv_ln_b, attn_out_g, conv_out_g, w_out, norm_mlp_g, w_mlp_up, w_mlp_down, loss_target, m_rel_bias, m_norm_mix_g, m_w_in, m_q_norm_g, m_k_norm_g, m_sinks, m_conv_w, m_conv_b, m_conv_ln_g, m_conv_ln_b, m_attn_out_g, m_conv_out_g, m_w_out, m_norm_mlp_g, m_w_mlp_up, m_w_mlp_down, v_rel_bias, v_norm_mix_g, v_w_in, v_q_norm_g, v_k_norm_g, v_sinks, v_conv_w, v_conv_b, v_conv_ln_g, v_conv_ln_b, v_attn_out_g, v_conv_out_g, v_w_out, v_norm_mlp_g, v_w_mlp_up, v_w_mlp_down):
    args = dict(locals())
    small = {n: args[n] for n in SMALL_NAMES}
    tr = lambda a: jnp.swapaxes(a, 1, 2)
    me = 4 * lax.axis_index("x") + 2 * lax.axis_index("y") + lax.axis_index("c")

    shards = (tr(w_in).astype(BF16), w_out.astype(BF16), tr(w_mlp_up).astype(BF16), w_mlp_down.astype(BF16))
    gathers, token = [], 0.0
    for l in range(DEPTH):
        state, tok = _exchange_start([s[l] for s in shards], f"gather_start_{l}", scatter=False)
        gathers.append(state)
        token = token + tok

    def own_rows(land, block):
        return lax.dynamic_update_slice(land, block, (me * block.shape[0], 0))

    def get_w(l, after):
        srcs, lands = _exchange_wait(gathers[l], after, f"gather_wait_{l}", scatter=False)
        return tuple(own_rows(land, src) for land, src in zip(lands, srcs))

    scatters = [None] * DEPTH

    def put_g(l, grads):
        scatters[l], tok = _exchange_start(list(grads), f"scatter_start_{l}", scatter=True)
        return tok

    cw_sh = jnp.pad(tr(conv_w), ((0, 0), (0, 0), (0, HALO - CONV_K)))
    cw_all, = _all_gather((cw_sh,))
    conv_w_full = tr(cw_all)[:, :CONV_K, :]

    loss_part, dx, small_grads, g_conv_w = _local_step(x[0], loss_target[0], small, conv_w_full, get_w, put_g, token)

    parts = [[] for _ in shards]
    for l in range(DEPTH):
        srcs, lands = _exchange_wait(scatters[l], dx, f"scatter_wait_{l}", scatter=True)
        for k, (land, src) in enumerate(zip(lands, srcs)):
            r = land.shape[0] // N_DEV
            mine = lax.dynamic_slice(src, (me * r, 0), (r, land.shape[1]))
            parts[k].append(own_rows(land, mine).reshape(N_DEV, r, land.shape[1]))
    parts = [jnp.stack(p) for p in parts]
    big_w = (tr(w_in), w_out, tr(w_mlp_up), w_mlp_down)
    big_m = (tr(m_w_in), m_w_out, tr(m_w_mlp_up), m_w_mlp_down)
    big_v = (tr(v_w_in), v_w_out, tr(v_w_mlp_up), v_w_mlp_down)
    big_out = [_adam_big(p, w, m, v) for p, w, m, v in zip(parts, big_w, big_m, big_v)]
    for k in (0, 2):
        big_out[k] = [tr(o) for o in big_out[k]]

    order = [n for n in SMALL_NAMES]
    packed = _pack([small_grads[n] for n in order] + [g_conv_w])
    gathered, = _all_gather((packed[None],))
    summed = _sum_slots(gathered[0].reshape(N_DEV, packed.shape[0], 128))
    shapes = [small[n].shape for n in order] + [(DEPTH, CONV_K, CONV_W)]
    sg = _unpack(summed, shapes)
    g_small = dict(zip(order, sg[:-1]))
    g_small["conv_w"] = lax.dynamic_slice_in_dim(sg[-1], me * (CONV_W // N_DEV), CONV_W // N_DEV, axis=2)
    names = order + ["conv_w"]
    shapes = [args[n].shape for n in names]
    d_p, m_p, v_p = _adam_small(_pack([args[n] for n in names]), _pack([g_small[n] for n in names]),
                                _pack([args["m_" + n] for n in names]), _pack([args["v_" + n] for n in names]))
    res = {"grad": g_small, "delta": dict(zip(names, _unpack(d_p, shapes))),
           "new_m": dict(zip(names, _unpack(m_p, shapes))), "new_v": dict(zip(names, _unpack(v_p, shapes)))}
    for k, n in enumerate(("w_in", "w_out", "w_mlp_up", "w_mlp_down")):
        for kind, val in zip(("grad", "delta", "new_m", "new_v"), big_out[k]):
            res[kind][n] = val

    loss = lax.psum(loss_part[0, 0], ("x", "y", "c"))
    weights = ("rel_bias", "norm_mix_g", "w_in", "q_norm_g", "k_norm_g", "sinks", "conv_w", "conv_b", "conv_ln_g",
               "conv_ln_b", "attn_out_g", "conv_out_g", "w_out", "norm_mlp_g", "w_mlp_up", "w_mlp_down")
    return (loss, dx[None], *[res[kind][n] for kind in ("grad", "delta", "new_m", "new_v") for n in weights])
```

```python
import math

import numpy as np
import jax
import jax.numpy as jnp
from jax import lax
from jax.experimental import pallas as pl
from jax.experimental.pallas import tpu as pltpu

F32, BF16 = jnp.float32, jnp.bfloat16
D_MODEL = 1024
DEPTH = 4
HEAD_DIM = 64
N_HEADS = 8
N_KV = 2
GQA = N_HEADS // N_KV
ATTN_W = N_HEADS * HEAD_DIM
KV_W = N_KV * HEAD_DIM
CONV_W = D_MODEL - ATTN_W
IN_W = ATTN_W + 2 * KV_W + 2 * CONV_W
BLOCK = 128
CONV_K = 31
HALO = 32
N_BUCKETS = 32
MAX_DIST = 128
D_FF = 4 * D_MODEL
FF_CHUNK = 512
N_FF = D_FF // FF_CHUNK
EPS = 1e-6
NEG = -1e30
N_DEV = 8
ADAM_LR, ADAM_B1, ADAM_B2, ADAM_EPS, ADAM_WD, ADAM_STEP = 0.001, 0.9, 0.999, 1e-08, 0.01, 10
VMEM_LIMIT = 56 * 1024 * 1024
MESH = pl.DeviceIdType.MESH

RESIDENT = pl.BlockSpec(memory_space=pltpu.VMEM)
IN_SMEM = pl.BlockSpec(memory_space=pltpu.SMEM)
ANY = pl.BlockSpec(memory_space=pl.ANY)


def _call(body, **kw):
    return pl.pallas_call(body, **kw)


def _params(*sem):
    return pltpu.CompilerParams(dimension_semantics=sem, vmem_limit_bytes=VMEM_LIMIT)


def _dot(a, b):
    return lax.dot_general(a, b, (((1,), (0,)), ((), ())), preferred_element_type=F32)


def _dot_nt(a, b):
    return lax.dot_general(a, b, (((1,), (1,)), ((), ())), preferred_element_type=F32)


def _dot_tn(a, b):
    return lax.dot_general(a, b, (((0,), (0,)), ((), ())), preferred_element_type=F32)


def _sig(x):
    return 1.0 / (1.0 + jnp.exp(-x))


def _rms(x):
    r = lax.rsqrt(jnp.mean(x * x, axis=-1, keepdims=True) + EPS)
    return x * r, r


def _rms_bwd(dy_g, xh, r):
    return r * (dy_g - xh * jnp.mean(dy_g * xh, axis=-1, keepdims=True))


def _rows(tm, w):
    return pl.BlockSpec((tm, w), lambda i: (i, 0))


def _acc_rows(w, rows=1):
    return pl.BlockSpec((rows, w), lambda i: (0, 0))


def _colsum(x):
    return jnp.sum(x, axis=0, keepdims=True)


def _fwd_in(x, g, wt):
    T = x.shape[0]
    tm = 512

    def body(x_ref, g_ref, w_ref, q_ref, kv_ref, u_ref, gt_ref):
        xh, _ = _rms(x_ref[...])
        h = (xh * g_ref[...]).astype(BF16)
        z = _dot_nt(h, w_ref[...])
        q_ref[...] = z[:, :ATTN_W]
        kv_ref[...] = z[:, ATTN_W:ATTN_W + 2 * KV_W]
        u_ref[...] = z[:, ATTN_W + 2 * KV_W:ATTN_W + 2 * KV_W + CONV_W]
        gt_ref[...] = z[:, ATTN_W + 2 * KV_W + CONV_W:]

    widths = (ATTN_W, 2 * KV_W, CONV_W, CONV_W)
    return _call(
        body, name="fwd_in", grid=(T // tm,),
        in_specs=[_rows(tm, D_MODEL), RESIDENT, RESIDENT],
        out_specs=[_rows(tm, w) for w in widths],
        out_shape=[jax.ShapeDtypeStruct((T, w), F32) for w in widths],
        compiler_params=_params("parallel"),
    )(x, g, wt)


ATTN_SUB = 2


def _tri4():
    qi = lax.broadcasted_iota(jnp.int32, (BLOCK, BLOCK), 0)
    kj = lax.broadcasted_iota(jnp.int32, (BLOCK, BLOCK), 1)
    return jnp.concatenate([kj <= qi] * GQA, axis=0)


def _kv_block(kv, kh, kg):
    khat, rk = _rms(kv[:, kh * HEAD_DIM:(kh + 1) * HEAD_DIM])
    return dict(khat=khat, rk=rk, kn=(khat * kg).astype(BF16),
                v=kv[:, KV_W + kh * HEAD_DIM:KV_W + (kh + 1) * HEAD_DIM].astype(BF16))


def _head_rows(g):
    row = lax.broadcasted_iota(jnp.int32, (GQA * BLOCK, 1), 0)
    return (row >= g * BLOCK) & (row < (g + 1) * BLOCK)


def _attn_group(zq, kp, kc, kh, bias_ref, sinks_ref, qg, tri, has_prev):
    scale = 1.0 / math.sqrt(HEAD_DIM)
    qraw = jnp.concatenate(
        [zq[:, (kh * GQA + g) * HEAD_DIM:(kh * GQA + g + 1) * HEAD_DIM] for g in range(GQA)], axis=0)
    qhat, rq = _rms(qraw)
    qf = qhat * qg
    qn = qf.astype(BF16)
    qs = (qf * scale).astype(BF16)
    s = jnp.where(tri, _dot_nt(qs, kc["kn"]), _dot_nt(qs, kp["kn"]))
    s = s + bias_ref[kh * GQA:(kh + 1) * GQA].reshape(GQA * BLOCK, BLOCK)
    if has_prev is not None:
        s = jnp.where(tri | has_prev, s, NEG)
    sink =jnp.zeros((GQA * BLOCK, 1), F32)
    for g in range(GQA):
        sink = jnp.where(_head_rows(g), sinks_ref[kh * GQA + g], sink)
    m = jnp.maximum(jnp.max(s, axis=-1, keepdims=True), sink)
    p = jnp.exp(s - m)
    es = jnp.exp(sink - m)
    den = jnp.sum(p, axis=-1, keepdims=True) + es
    pn = p / den
    return dict(qhat=qhat, rq=rq, qn=qn, pn=pn, psink=es / den)


def _split(pb, tri):
    zero = jnp.zeros_like(pb)
    return jnp.where(tri, pb, zero), jnp.where(tri, zero, pb)


def _unstack_heads(o):
    return [o[g * BLOCK:(g + 1) * BLOCK] for g in range(GQA)]


def _attn_fwd(zq, zkv, bias, sinks, qg, kg, og):
    T = zq.shape[0]
    rows = ATTN_SUB * BLOCK

    def body(q_ref, kvc_ref, kvp_ref, bias_ref, sinks_ref, qg_ref, kg_ref, og_ref, a_ref, mix_ref):
        n = pl.program_id(0)
        tri = _tri4()
        qgv, kgv = qg_ref[...], kg_ref[...]
        kvs = [kvp_ref[...]] + [kvc_ref[i * BLOCK:(i + 1) * BLOCK, :] for i in range(ATTN_SUB)]
        keys = [[_kv_block(kv, kh, kgv) for kh in range(N_KV)] for kv in kvs]
        for i in range(ATTN_SUB):
            zq_v = q_ref[i * BLOCK:(i + 1) * BLOCK, :]
            has_prev = (n > 0) if i == 0 else None
            outs = []
            for kh in range(N_KV):
                kp, kc = keys[i][kh], keys[i + 1][kh]
                c = _attn_group(zq_v, kp, kc, kh, bias_ref, sinks_ref, qgv, tri, has_prev)
                p_c, p_p = _split(c["pn"].astype(BF16), tri)
                outs += _unstack_heads(_dot(p_c, kc["v"]) + _dot(p_p, kp["v"]))
            a = jnp.concatenate(outs, axis=-1)
            a_ref[i * BLOCK:(i + 1) * BLOCK, :] = a
            ah, _ = _rms(a)
            mix_ref[i * BLOCK:(i + 1) * BLOCK, :] = (ah * og_ref[...]).astype(BF16)

    return _call(
        body, name="attn_fwd", grid=(T // rows,),
        in_specs=[_rows(rows, ATTN_W), _rows(rows, 2 * KV_W),
                  pl.BlockSpec((BLOCK, 2 * KV_W), lambda n: (jnp.maximum(n * ATTN_SUB - 1, 0), 0)),
                  RESIDENT, IN_SMEM, RESIDENT, RESIDENT, RESIDENT],
        out_specs=[_rows(rows, ATTN_W), _rows(rows, ATTN_W)],
        out_shape=[jax.ShapeDtypeStruct((T, ATTN_W), F32), jax.ShapeDtypeStruct((T, ATTN_W), BF16)],
        compiler_params=_params("parallel"),
    )(zq, zkv, zkv, bias, sinks, qg, kg, og)


def _conv_post(y, lg, lb, og):
    mu = jnp.mean(y, axis=-1, keepdims=True)
    yc = y - mu
    rstd = lax.rsqrt(jnp.mean(yc * yc, axis=-1, keepdims=True) + EPS)
    yn = yc * rstd
    ln = yn * lg + lb
    sg = _sig(ln)
    c = ln * sg
    ch, r = _rms(c)
    return yn, rstd, ln, sg, ch, r


CONV_CHUNK = 64


SLAB_ROWS = CONV_CHUNK + 8 * ((CONV_K - 1) // 8)


def _shifted_taps(buf, slab, r0, base):
    for b in range(8):
        taps = range(b, CONV_K, 8)
        span = CONV_CHUNK + 8 * (len(taps) - 1)
        slab[0:span, :] = buf[r0 + base + b:r0 + base + b + span, :]
        for a, j in enumerate(taps):
            yield j, slab[8 * a:8 * a + CONV_CHUNK, :]


def _fold8(x):
    return jnp.sum(x.reshape(x.shape[0] // 8, 8, x.shape[1]), axis=0)


def _conv_fwd(zu, zg, cw, cb, lg, lb, og):
    T = zu.shape[0]
    tt = 512
    halo_spec = pl.BlockSpec((HALO, CONV_W), lambda i: (jnp.maximum(i * (tt // HALO) - 1, 0), 0))

    def body(u_ref, g_ref, uh_ref, gh_ref, cw_ref, cb_ref, lg_ref, lb_ref, og_ref, y_ref, mix_ref, buf, slab):
        i = pl.program_id(0)
        hal = uh_ref[...] * _sig(gh_ref[...])
        buf[0:HALO, :] = jnp.where(i > 0, hal, 0.0)
        buf[HALO:HALO + tt, :] = u_ref[...] * _sig(g_ref[...])
        cbv, lgv, lbv, ogv = cb_ref[...], lg_ref[...], lb_ref[...], og_ref[...]
        for r0 in range(0, tt, CONV_CHUNK):
            rs = slice(r0, r0 + CONV_CHUNK)
            acc = jnp.zeros((CONV_CHUNK, CONV_W), F32)
            for j, win in _shifted_taps(buf, slab, r0, HALO - (CONV_K - 1)):
                acc = acc + cw_ref[j:j + 1, :] * win
            y = acc + cbv
            y_ref[rs, :] = y
            mix_ref[rs, :] = (_conv_post(y, lgv, lbv, ogv)[4] * ogv).astype(BF16)

    return _call(
        body, name="conv_fwd", grid=(T // tt,),
        in_specs=[_rows(tt, CONV_W), _rows(tt, CONV_W), halo_spec, halo_spec] + [RESIDENT] * 5,
        out_specs=[_rows(tt, CONV_W), _rows(tt, CONV_W)],
        out_shape=[jax.ShapeDtypeStruct((T, CONV_W), F32), jax.ShapeDtypeStruct((T, CONV_W), BF16)],
        scratch_shapes=[pltpu.VMEM((HALO + tt, CONV_W), F32), pltpu.VMEM((SLAB_ROWS, CONV_W), F32)],
        compiler_params=_params("parallel"),
    )(zu, zg, zu, zg, cw, cb, lg, lb, og)


def _fwd_out(x, mix_a, mix_c, w_out):
    T = x.shape[0]
    tm = 512

    def body(x_ref, a_ref, c_ref, w_ref, o_ref):
        o_ref[...] = (x_ref[...] + _dot(a_ref[...], w_ref[0:ATTN_W, :])
                      + _dot(c_ref[...], w_ref[ATTN_W:, :]))

    return _call(
        body, name="fwd_out", grid=(T // tm,),
        in_specs=[_rows(tm, D_MODEL), _rows(tm, ATTN_W), _rows(tm, CONV_W), RESIDENT],
        out_specs=_rows(tm, D_MODEL),
        out_shape=jax.ShapeDtypeStruct((T, D_MODEL), F32),
        compiler_params=_params("parallel"),
    )(x, mix_a, mix_c, w_out)


def _chunked(tm):
    return pl.BlockSpec((N_FF, tm, FF_CHUNK), lambda i: (0, i, 0))


def _mlp_fwd(x, g, wup_t, wdown):
    T = x.shape[0]
    tm = 512

    def body(x_ref, g_ref, wu_ref, wd_ref, o_ref, up_ref):
        xv = x_ref[...]
        xh, _ = _rms(xv)
        h = (xh * g_ref[...]).astype(BF16)
        acc = xv
        for c in range(N_FF):
            rows = slice(c * FF_CHUNK, (c + 1) * FF_CHUNK)
            up = _dot_nt(h, wu_ref[rows, :])
            up_ref[c] = up.astype(BF16)
            act = jnp.square(jnp.maximum(up, 0.0))
            acc = acc + _dot(act.astype(BF16), wd_ref[rows, :])
        o_ref[...] = acc

    return _call(
        body, name="mlp_fwd", grid=(T // tm,),
        in_specs=[_rows(tm, D_MODEL), RESIDENT, RESIDENT, RESIDENT],
        out_specs=[_rows(tm, D_MODEL), _chunked(tm)],
        out_shape=[jax.ShapeDtypeStruct((T, D_MODEL), F32), jax.ShapeDtypeStruct((N_FF, T, FF_CHUNK), BF16)],
        compiler_params=_params("parallel"),
    )(x, g, wup_t, wdown)


def _loss_head(y, target):
    T = y.shape[0]
    tm = 512

    def body(y_ref, t_ref, l_ref, d_ref):
        @pl.when(pl.program_id(0) == 0)
        def _():
            l_ref[...] = jnp.zeros_like(l_ref)
        e = y_ref[...] - t_ref[...]
        d_ref[...] = e / D_MODEL
        l_ref[...] += 0.5 * jnp.sum(jnp.mean(e * e, axis=-1, keepdims=True))

    return _call(
        body, name="loss_head", grid=(T // tm,),
        in_specs=[_rows(tm, D_MODEL), _rows(tm, D_MODEL)],
        out_specs=[_acc_rows(128, 8), _rows(tm, D_MODEL)],
        out_shape=[jax.ShapeDtypeStruct((8, 128), F32), jax.ShapeDtypeStruct((T, D_MODEL), F32)],
        compiler_params=_params("arbitrary"),
    )(y, target)


def _bias_table(rel_bias, bucket):
    def body(rb_ref, bk_ref, o_ref):
        bk = bk_ref[...]
        for h in range(N_HEADS):
            acc = jnp.zeros((BLOCK, BLOCK), F32)
            for b in range(N_BUCKETS):
                acc = jnp.where(bk == b, rb_ref[b, h], acc)
            o_ref[h] = acc

    return _call(
        body, name="bias_table", in_specs=[IN_SMEM, RESIDENT], out_specs=RESIDENT,
        out_shape=jax.ShapeDtypeStruct((N_HEADS, BLOCK, BLOCK), F32),
    )(rel_bias, bucket)


def _mlp_bwd(dx2, x1, up, g, wup_t, wdown):
    T = x1.shape[0]
    tm = 512

    def body(d_ref, x_ref, up_ref, g_ref, wu_ref, wd_ref, dx_ref, dup_ref, h_ref, dg_ref):
        @pl.when(pl.program_id(0) == 0)
        def _():
            dg_ref[...] = jnp.zeros_like(dg_ref)
        d2 = d_ref[...]
        d2b = d2.astype(BF16)
        xh, r = _rms(x_ref[...])
        gv = g_ref[...]
        h_ref[...] = (xh * gv).astype(BF16)
        dh = jnp.zeros((tm, D_MODEL), F32)
        for c in range(N_FF):
            rows = slice(c * FF_CHUNK, (c + 1) * FF_CHUNK)
            dact = _dot_nt(d2b, wd_ref[rows, :])
            dup = (dact * (2.0 * jnp.maximum(up_ref[c].astype(F32), 0.0))).astype(BF16)
            dup_ref[c] = dup
            dh = dh + _dot(dup, wu_ref[rows, :])
        dg_ref[...] += _colsum(dh * xh)
        dx_ref[...] = d2 + _rms_bwd(dh * gv, xh, r)

    return _call(
        body, name="mlp_bwd", grid=(T // tm,),
        in_specs=[_rows(tm, D_MODEL), _rows(tm, D_MODEL), _chunked(tm), RESIDENT, RESIDENT, RESIDENT],
        out_specs=[_rows(tm, D_MODEL), _chunked(tm), _rows(tm, D_MODEL), _acc_rows(D_MODEL)],
        out_shape=[jax.ShapeDtypeStruct((T, D_MODEL), F32), jax.ShapeDtypeStruct((N_FF, T, FF_CHUNK), BF16),
                   jax.ShapeDtypeStruct((T, D_MODEL), BF16), jax.ShapeDtypeStruct((1, D_MODEL), F32)],
        compiler_params=_params("arbitrary"),
    )(dx2, x1, up, g, wup_t, wdown)


def _wgrad(a, b, name, chunked=False, square_relu=False):
    if chunked:
        nr, T, tr = a.shape
    else:
        T, R = a.shape
        tr = min(R, FF_CHUNK)
        nr = R // tr
    tk = 1024
    nk = T // tk

    def body(a_ref, b_ref, o_ref, acc):
        k = pl.program_id(1)

        @pl.when(k == 0)
        def _():
            acc[...] = jnp.zeros_like(acc)
        av = a_ref[...]
        if square_relu:
            av = jnp.square(jnp.maximum(av.astype(F32), 0.0))
        acc[...] += _dot_tn(av.astype(BF16), b_ref[...].astype(BF16))

        @pl.when(k == nk - 1)
        def _():
            o_ref[...] = acc[...].astype(BF16)

    a_spec = (pl.BlockSpec((None, tk, tr), lambda r, k: (r, k, 0)) if chunked
              else pl.BlockSpec((tk, tr), lambda r, k: (k, r)))
    return _call(
        body, name=name, grid=(nr, nk),
        in_specs=[a_spec, pl.BlockSpec((tk, D_MODEL), lambda r, k: (k, 0))],
        out_specs=pl.BlockSpec((tr, D_MODEL), lambda r, k: (r, 0)),
        out_shape=jax.ShapeDtypeStruct((nr * tr, D_MODEL), BF16),
        scratch_shapes=[pltpu.VMEM((tr, D_MODEL), F32)],
        compiler_params=_params("parallel", "arbitrary"),
    )(a, b)


def _bwd_out(dx1, w_out):
    T = dx1.shape[0]
    tm = 512

    def body(d_ref, w_ref, da_ref, dc_ref, db_ref):
        db = d_ref[...].astype(BF16)
        db_ref[...] = db
        dm = _dot_nt(db, w_ref[...])
        da_ref[...] = dm[:, :ATTN_W]
        dc_ref[...] = dm[:, ATTN_W:]

    return _call(
        body, name="bwd_out", grid=(T // tm,),
        in_specs=[_rows(tm, D_MODEL), RESIDENT],
        out_specs=[_rows(tm, ATTN_W), _rows(tm, CONV_W), _rows(tm, D_MODEL)],
        out_shape=[jax.ShapeDtypeStruct((T, ATTN_W), F32), jax.ShapeDtypeStruct((T, CONV_W), F32),
                   jax.ShapeDtypeStruct((T, D_MODEL), BF16)],
        compiler_params=_params("parallel"),
    )(dx1, w_out)


def _conv_bwd(dmix, y, zu, zg, cw, lg, lb, og):
    T = y.shape[0]
    tt = 512
    nt = T // tt
    per = tt // HALO
    prev_spec = pl.BlockSpec((HALO, CONV_W), lambda i: (jnp.maximum(i * per - 1, 0), 0))
    next_spec = pl.BlockSpec((HALO, CONV_W), lambda i: (jnp.minimum((i + 1) * per, nt * per - 1), 0))

    def body(dm_ref, dmn_ref, y_ref, yn_ref, u_ref, g_ref, uh_ref, gh_ref, cw_ref, lg_ref, lb_ref, og_ref,
             du_ref, dg_ref, pg_ref, hbuf, dybuf, dwacc, pacc, slab):
        i = pl.program_id(0)

        @pl.when(i == 0)
        def _():
            dwacc[...] = jnp.zeros_like(dwacc)
            pacc[...] = jnp.zeros_like(pacc)
        lgv, lbv, ogv = lg_ref[...], lb_ref[...], og_ref[...]

        def chain(yv, dm):
            yn, rstd, ln, sg, ch, r = _conv_post(yv, lgv, lbv, ogv)
            dc = _rms_bwd(dm * ogv, ch, r)
            dln = dc * sg * (1.0 + ln * (1.0 - sg))
            dyn = dln * lgv
            dy = rstd * (dyn - jnp.mean(dyn, axis=-1, keepdims=True)
                         - yn * jnp.mean(dyn * yn, axis=-1, keepdims=True))
            return dy, dm * ch, dln * yn, dln

        hbuf[0:HALO, :] = jnp.where(i > 0, uh_ref[...] * _sig(gh_ref[...]), 0.0)
        for r0 in range(0, tt, CONV_CHUNK):
            rs = slice(r0, r0 + CONV_CHUNK)
            hbuf[HALO + r0:HALO + r0 + CONV_CHUNK, :] = u_ref[rs, :] * _sig(g_ref[rs, :])
            dy, p_og, p_lg, p_lb = chain(y_ref[rs, :], dm_ref[rs, :])
            dybuf[rs, :] = dy
            for k, part in enumerate((dy, p_lg, p_lb, p_og)):
                pacc[8 * k:8 * k + 8, :] += _fold8(part)
        dyh, _, _, _ = chain(yn_ref[...], dmn_ref[...])
        dybuf[tt:tt + HALO, :] = jnp.where(i < nt - 1, dyh, 0.0)
        for r0 in range(0, tt, CONV_CHUNK):
            rs = slice(r0, r0 + CONV_CHUNK)
            dy = dybuf[rs, :]
            dh = jnp.zeros((CONV_CHUNK, CONV_W), F32)
            for j, win in _shifted_taps(dybuf, slab, r0, 0):
                dh = dh + cw_ref[CONV_K - 1 - j:CONV_K - j, :] * win
            for j, win in _shifted_taps(hbuf, slab, r0, HALO - (CONV_K - 1)):
                dwacc[8 * j:8 * j + 8, :] += _fold8(dy * win)
            sgt = _sig(g_ref[rs, :])
            du_ref[rs, :] = (dh * sgt).astype(BF16)
            dg_ref[rs, :] = (dh * u_ref[rs, :] * sgt * (1.0 - sgt)).astype(BF16)

        @pl.when(i == nt - 1)
        def _():
            pg_ref[...] = jnp.zeros_like(pg_ref)
            for j in range(CONV_K):
                pg_ref[j:j + 1, :] = _colsum(dwacc[8 * j:8 * j + 8, :])
            for k in range(4):
                pg_ref[32 + k:33 + k, :] = _colsum(pacc[8 * k:8 * k + 8, :])

    return _call(
        body, name="conv_bwd", grid=(nt,),
        in_specs=[_rows(tt, CONV_W), next_spec, _rows(tt, CONV_W), next_spec, _rows(tt, CONV_W), _rows(tt, CONV_W),
                  prev_spec, prev_spec] + [RESIDENT] * 4,
        out_specs=[_rows(tt, CONV_W), _rows(tt, CONV_W), _acc_rows(CONV_W, 40)],
        out_shape=[jax.ShapeDtypeStruct((T, CONV_W), BF16), jax.ShapeDtypeStruct((T, CONV_W), BF16),
                   jax.ShapeDtypeStruct((40, CONV_W), F32)],
        scratch_shapes=[pltpu.VMEM((HALO + tt, CONV_W), F32), pltpu.VMEM((tt + HALO, CONV_W), F32),
                        pltpu.VMEM((8 * HALO, CONV_W), F32), pltpu.VMEM((32, CONV_W), F32),
                        pltpu.VMEM((SLAB_ROWS, CONV_W), F32)],
        compiler_params=_params("arbitrary"),
    )(dmix, dmix, y, y, zu, zg, zu, zg, cw, lg, lb, og)


def _attn_bwd(dmix, a, zq, zkv, bias, sinks, qg, kg, og):
    T = zq.shape[0]
    rows = ATTN_SUB * BLOCK
    ns = T // rows
    nb = T // BLOCK
    cur = lambda w: pl.BlockSpec((rows, w), lambda n: (jnp.minimum(n, ns - 1), 0))
    scale = 1.0 / math.sqrt(HEAD_DIM)
    done = rows - BLOCK

    def body(dm_ref, a_ref, q_ref, kvc_ref, kvp_ref, bias_ref, sinks_ref, qg_ref, kg_ref, og_ref,
             dq_ref, dkv_ref, db_ref, dog_ref, dqg_ref, dkg_ref, dsk_ref, carry):
        n = pl.program_id(0)

        @pl.when(n == 0)
        def _():
            for ref in (db_ref, dog_ref, dqg_ref, dkg_ref, dsk_ref, carry):
                ref[...] = jnp.zeros_like(ref)

        @pl.when(n < ns)
        def _():
            tri = _tri4()
            ogv, qgv, kgv = og_ref[...], qg_ref[...], kg_ref[...]
            lane = lax.broadcasted_iota(jnp.int32, (1, 128), 1)
            kvs = [kvp_ref[...]] + [kvc_ref[i * BLOCK:(i + 1) * BLOCK, :] for i in range(ATTN_SUB)]
            keys = [[_kv_block(kv, kh, kgv) for kh in range(N_KV)] for kv in kvs]
            dkn = [[jnp.zeros((BLOCK, HEAD_DIM), F32)] * N_KV for _ in kvs]
            dv = [[jnp.zeros((BLOCK, HEAD_DIM), F32)] * N_KV for _ in kvs]
            dsk = jnp.zeros((1, 128), F32)
            for i in range(ATTN_SUB):
                blk = slice(i * BLOCK, (i + 1) * BLOCK)
                ah, ra = _rms(a_ref[blk, :])
                dm = dm_ref[blk, :]
                dog_ref[...] += _colsum(dm * ah)
                da = _rms_bwd(dm * ogv, ah, ra)
                zq_v = q_ref[blk, :]
                has_prev = (n > 0) if i == 0 else None
                dqs = []
                for kh in range(N_KV):
                    kp, kc = keys[i][kh], keys[i + 1][kh]
                    c = _attn_group(zq_v, kp, kc, kh, bias_ref, sinks_ref, qgv, tri, has_prev)
                    do = jnp.concatenate(
                        [da[:, (kh * GQA + g) * HEAD_DIM:(kh * GQA + g + 1) * HEAD_DIM] for g in range(GQA)], axis=0)
                    dob = do.astype(BF16)
                    pn = c["pn"]
                    p_c, p_p = _split(pn.astype(BF16), tri)
                    dv[i + 1][kh] = dv[i + 1][kh] + _dot_tn(p_c, dob)
                    dv[i][kh] = dv[i][kh] + _dot_tn(p_p, dob)
                    dp = jnp.where(tri, _dot_nt(dob, kc["v"]), _dot_nt(dob, kp["v"]))
                    dl = jnp.sum(pn * dp, axis=-1, keepdims=True)
                    ds = pn * (dp - dl)
                    dsr = -c["psink"] * dl
                    for g in range(GQA):
                        dsk = dsk + jnp.where(lane == kh * GQA + g, jnp.sum(jnp.where(_head_rows(g), dsr, 0.0)), 0.0)
                    db_ref[kh * GQA:(kh + 1) * GQA] += ds.reshape(GQA, BLOCK, BLOCK)
                    ds_c, ds_p = _split((ds * scale).astype(BF16), tri)
                    dqn = _dot(ds_c, kc["kn"]) + _dot(ds_p, kp["kn"])
                    dkn[i + 1][kh] = dkn[i + 1][kh] + _dot_tn(ds_c, c["qn"])
                    dkn[i][kh] = dkn[i][kh] + _dot_tn(ds_p, c["qn"])
                    dqg_ref[...] += _colsum(dqn * c["qhat"])
                    dqs += _unstack_heads(_rms_bwd(dqn * qgv, c["qhat"], c["rq"]))
                dq_ref[blk, :] = jnp.concatenate(dqs, axis=-1).astype(BF16)
            dsk_ref[...] += dsk
            dkv = []
            for j in range(ATTN_SUB + 1):
                dk = []
                for kh in range(N_KV):
                    key = keys[j][kh]
                    dkg_ref[...] += _colsum(dkn[j][kh] * key["khat"])
                    dk.append(_rms_bwd(dkn[j][kh] * kgv, key["khat"], key["rk"]))
                dkv.append(jnp.concatenate(dk + dv[j], axis=-1))
            if done:
                dkv_ref[0:done, :] = carry[0:done, :].astype(BF16)
            dkv_ref[done:rows, :] = (carry[done:rows, :] + dkv[0]).astype(BF16)
            for j in range(1, ATTN_SUB + 1):
                carry[(j - 1) * BLOCK:j * BLOCK, :] = dkv[j]

        @pl.when(n == ns)
        def _():
            dkv_ref[...] = carry[...].astype(BF16)

    small = lambda w: pl.BlockSpec((1, w), lambda n: (0, 0))
    return _call(
        body, name="attn_bwd", grid=(ns + 1,),
        in_specs=[cur(ATTN_W), cur(ATTN_W), cur(ATTN_W), cur(2 * KV_W),
                  pl.BlockSpec((BLOCK, 2 * KV_W), lambda n: (jnp.clip(n * ATTN_SUB - 1, 0, nb - 1), 0)),
                  RESIDENT, IN_SMEM, RESIDENT, RESIDENT, RESIDENT],
        out_specs=[cur(ATTN_W), pl.BlockSpec((rows, 2 * KV_W), lambda n: (jnp.maximum(n - 1, 0), 0)),
                   pl.BlockSpec((N_HEADS, BLOCK, BLOCK), lambda n: (0, 0, 0)),
                   small(ATTN_W), small(HEAD_DIM), small(HEAD_DIM), small(128)],
        out_shape=[jax.ShapeDtypeStruct((T, ATTN_W), BF16), jax.ShapeDtypeStruct((T, 2 * KV_W), BF16),
                   jax.ShapeDtypeStruct((N_HEADS, BLOCK, BLOCK), F32),
                   jax.ShapeDtypeStruct((1, ATTN_W), F32), jax.ShapeDtypeStruct((1, HEAD_DIM), F32),
                   jax.ShapeDtypeStruct((1, HEAD_DIM), F32), jax.ShapeDtypeStruct((1, 128), F32)],
        scratch_shapes=[pltpu.VMEM((rows, 2 * KV_W), F32)],
        compiler_params=_params("arbitrary"),
    )(dmix, a, zq, zkv, zkv, bias, sinks, qg, kg, og)


def _bwd_in(dx1, x, dq, dkv, du, dgt, g, wt):
    T = x.shape[0]
    tm = 512

    def body(d1_ref, x_ref, dq_ref, dkv_ref, du_ref, dgt_ref, g_ref, w_ref, dx_ref, h_ref, dg_ref):
        @pl.when(pl.program_id(0) == 0)
        def _():
            dg_ref[...] = jnp.zeros_like(dg_ref)
        xh, r = _rms(x_ref[...])
        gv = g_ref[...]
        h_ref[...] = (xh * gv).astype(BF16)
        dz = jnp.concatenate([dq_ref[...], dkv_ref[...], du_ref[...], dgt_ref[...]], axis=-1)
        dh = _dot(dz, w_ref[...])
        dg_ref[...] += _colsum(dh * xh)
        dx_ref[...] = d1_ref[...] + _rms_bwd(dh * gv, xh, r)

    return _call(
        body, name="bwd_in", grid=(T // tm,),
        in_specs=[_rows(tm, D_MODEL), _rows(tm, D_MODEL), _rows(tm, ATTN_W), _rows(tm, 2 * KV_W),
                  _rows(tm, CONV_W), _rows(tm, CONV_W), RESIDENT, RESIDENT],
        out_specs=[_rows(tm, D_MODEL), _rows(tm, D_MODEL), _acc_rows(D_MODEL)],
        out_shape=[jax.ShapeDtypeStruct((T, D_MODEL), F32), jax.ShapeDtypeStruct((T, D_MODEL), BF16),
                   jax.ShapeDtypeStruct((1, D_MODEL), F32)],
        compiler_params=_params("arbitrary"),
    )(dx1, x, dq, dkv, du, dgt, g, wt)


def _bias_grad(db, bucket):
    def body(db_ref, bk_ref, o_ref):
        tot = db_ref[0]
        for l in range(1, DEPTH):
            tot = tot + db_ref[l]
        bk = bk_ref[...]
        lane = lax.broadcasted_iota(jnp.int32, (N_HEADS, 128), 1)
        out = jnp.zeros((N_HEADS, 128), F32)
        for b in range(N_BUCKETS):
            v = jnp.sum(jnp.sum(jnp.where(bk == b, tot, 0.0), axis=2), axis=1, keepdims=True)
            out = jnp.where(lane == b, v, out)
        o_ref[...] = out

    return _call(
        body, name="bias_grad", in_specs=[RESIDENT, RESIDENT], out_specs=RESIDENT,
        out_shape=jax.ShapeDtypeStruct((N_HEADS, 128), F32),
        compiler_params=pltpu.CompilerParams(vmem_limit_bytes=VMEM_LIMIT),
    )(db, bucket)


def _place():
    return lax.axis_index("x"), lax.axis_index("y"), lax.axis_index("c")


def _all_gather(shards):
    na = len(shards)

    def body(*refs):
        ins, outs = refs[:na], refs[na:2 * na]
        send_sems, recv_sems, local_sems = refs[2 * na:]
        x, y, c = _place()
        me, sibling = (x, y, c), (x, y, 1 - c)
        chips = [(1 - x, y), (x, 1 - y), (1 - x, 1 - y)]

        def rows(a, p):
            r = ins[a].shape[1]
            return outs[a].at[:, pl.ds((4 * p[0] + 2 * p[1] + p[2]) * r, r), :]

        def copy(a, k, block, to, src=None):
            return pltpu.make_async_remote_copy(
                src_ref=rows(a, block) if src is None else src, dst_ref=rows(a, block),
                send_sem=send_sems.at[a, k], recv_sem=recv_sems.at[a, k], device_id=to, device_id_type=MESH)

        mine = [pltpu.make_async_copy(ins[a], rows(a, me), local_sems.at[a]) for a in range(na)]
        for cp in mine:
            cp.start()
        first = []
        for a in range(na):
            first.append(copy(a, 0, me, sibling, src=ins[a]))
            first += [copy(a, 1 + j, me, (*chip, c), src=ins[a]) for j, chip in enumerate(chips)]
        for cp in first:
            cp.start()
        passed = []
        for j, chip in enumerate(chips):
            for a in range(na):
                copy(a, 1 + j, (*chip, c), me).wait_recv()
                cp = copy(a, 4 + j, (*chip, c), sibling)
                cp.start()
                passed.append(cp)
        for a in range(na):
            copy(a, 0, sibling, me).wait_recv()
            for j, chip in enumerate(chips):
                copy(a, 4 + j, (*chip, 1 - c), me).wait_recv()
        for cp in first + passed:
            cp.wait_send()
        for cp in mine:
            cp.wait()

    return _call(
        body, name="all_gather",
        in_specs=[ANY] * na, out_specs=[ANY] * na,
        out_shape=[jax.ShapeDtypeStruct((s.shape[0], N_DEV * s.shape[1], s.shape[2]), s.dtype) for s in shards],
        scratch_shapes=[pltpu.SemaphoreType.DMA((na, 7)), pltpu.SemaphoreType.DMA((na, 7)),
                        pltpu.SemaphoreType.DMA((na,))],
    )(*shards)


IN_HBM = pl.BlockSpec(memory_space=pltpu.HBM)
IN_SEM = pl.BlockSpec(memory_space=pltpu.SEMAPHORE)
DATAFLOW = pltpu.SideEffectType.DATAFLOW_SIDE_EFFECTING


def _exchange_copies(srcs, lands, send_sems, recv_sems, scatter):
    x, y, c = _place()
    me = 4 * x + 2 * y + c
    copies = []
    for k in range(1, N_DEV):
        p = (x ^ (k >> 2), y ^ ((k >> 1) & 1), c ^ (k & 1))
        for a, (src, land) in enumerate(zip(srcs, lands)):
            r = land.shape[0] // N_DEV
            if scatter:
                src = src.at[pl.ds((4 * p[0] + 2 * p[1] + p[2]) * r, r), :]
            copies.append(pltpu.make_async_remote_copy(
                src_ref=src, dst_ref=land.at[pl.ds(me * r, r), :], send_sem=send_sems.at[a * (N_DEV - 1) + k - 1],
                recv_sem=recv_sems.at[a * (N_DEV - 1) + k - 1], device_id=p, device_id_type=MESH))
    return copies


def _exchange_start(srcs, name, scatter):
    na = len(srcs)
    lands = [lax.empty((s.shape[0] * (1 if scatter else N_DEV), s.shape[1]), s.dtype) for s in srcs]

    def body(*refs):
        ins, lnd = refs[:na], refs[na:2 * na]
        send_sems, recv_sems = refs[2 * na], refs[2 * na + 1]
        token = refs[-1]
        for cp in _exchange_copies(ins, lnd, send_sems, recv_sems, scatter):
            cp.start()
        token[...] = jnp.zeros_like(token)

    hbm = lambda a: pltpu.with_memory_space_constraint(a, pltpu.HBM)
    out = _call(
        body, name=name,
        out_shape=(pltpu.SemaphoreType.DMA((na * (N_DEV - 1),)), pltpu.SemaphoreType.DMA((na * (N_DEV - 1),)),
                   *[pltpu.HBM(a.shape, a.dtype) for a in (*srcs, *lands)], jax.ShapeDtypeStruct((8, 128), F32)),
        in_specs=[IN_HBM] * (2 * na),
        out_specs=(IN_SEM, IN_SEM, *[IN_HBM] * (2 * na), RESIDENT),
        input_output_aliases={i: 2 + i for i in range(2 * na)},
        compiler_params=pltpu.CompilerParams(has_side_effects=DATAFLOW),
    )(*[hbm(a) for a in (*srcs, *lands)])
    return (out[0], out[1], out[2:2 + na], out[2 + na:2 + 2 * na]), out[-1]


def _exchange_wait(state, after, name, scatter):
    send_sems, recv_sems, srcs, lands = state
    na = len(srcs)

    def body(*refs):
        ins, lnd = refs[:na], refs[na:2 * na]
        for cp in _exchange_copies(ins, lnd, refs[2 * na], refs[2 * na + 1], scatter):
            cp.wait_send()
            cp.wait_recv()

    out = _call(
        body, name=name,
        out_shape=tuple(pltpu.HBM(a.shape, a.dtype) for a in (*srcs, *lands)),
        in_specs=[IN_HBM] * (2 * na) + [IN_SEM, IN_SEM, ANY],
        out_specs=[IN_HBM] * (2 * na),
        input_output_aliases={i: i for i in range(2 * na)},
        compiler_params=pltpu.CompilerParams(has_side_effects=DATAFLOW),
    )(*srcs, *lands, send_sems, recv_sems, after)
    return out[:na], out[na:]


def _adam_math(w, g, m, v):
    m = ADAM_B1 * m + (1.0 - ADAM_B1) * g
    v = ADAM_B2 * v + (1.0 - ADAM_B2) * jnp.square(g)
    m_hat = m / (1.0 - ADAM_B1 ** ADAM_STEP)
    v_hat = v / (1.0 - ADAM_B2 ** ADAM_STEP)
    delta = -ADAM_LR * (m_hat / (jnp.sqrt(v_hat) + ADAM_EPS) + ADAM_WD * w)
    return delta, m, v


def _adam_big(parts, w, m, v):
    nl, r, cdim = w.shape
    tr = r if r <= 256 else 256
    spec = pl.BlockSpec((None, tr, cdim), lambda l, i: (l, i, 0))

    def body(p_ref, w_ref, m_ref, v_ref, g_ref, d_ref, nm_ref, nv_ref):
        g = p_ref[0].astype(F32)
        for s in range(1, N_DEV):
            g = g + p_ref[s].astype(F32)
        d, nm, nv = _adam_math(w_ref[...], g, m_ref[...], v_ref[...])
        g_ref[...] = g
        d_ref[...] = d
        nm_ref[...] = nm
        nv_ref[...] = nv

    return _call(
        body, name="adam_big", grid=(nl, r // tr),
        in_specs=[pl.BlockSpec((None, N_DEV, tr, cdim), lambda l, i: (l, 0, i, 0)), spec, spec, spec],
        out_specs=[spec] * 4,
        out_shape=[jax.ShapeDtypeStruct(w.shape, F32)] * 4,
        compiler_params=_params("parallel", "parallel"),
    )(parts, w, m, v)


def _sum_slots(parts):
    def body(p_ref, o_ref):
        g = p_ref[0]
        for s in range(1, N_DEV):
            g = g + p_ref[s]
        o_ref[...] = g

    return _call(body, name="sum_slots", in_specs=[RESIDENT], out_specs=RESIDENT,
                 out_shape=jax.ShapeDtypeStruct(parts.shape[1:], F32))(parts)


def _adam_small(w, g, m, v):
    def body(w_ref, g_ref, m_ref, v_ref, d_ref, nm_ref, nv_ref):
        d, nm, nv = _adam_math(w_ref[...], g_ref[...], m_ref[...], v_ref[...])
        d_ref[...] = d
        nm_ref[...] = nm
        nv_ref[...] = nv

    return _call(body, name="adam_small", in_specs=[RESIDENT] * 4, out_specs=[RESIDENT] * 3,
                 out_shape=[jax.ShapeDtypeStruct(w.shape, F32)] * 3)(w, g, m, v)


def _pack(arrays):
    parts = []
    for a in arrays:
        flat = a.reshape(-1)
        n = flat.shape[0]
        padded = -(-n // 1024) * 1024
        parts.append(jnp.pad(flat, (0, padded - n)).reshape(padded // 128, 128))
    return jnp.concatenate(parts, axis=0)


def _unpack(packed, shapes):
    out, row = [], 0
    for shp in shapes:
        n = int(np.prod(shp))
        rows = -(-n // 1024) * 8
        out.append(packed[row:row + rows].reshape(-1)[:n].reshape(shp))
        row += rows
    return out


def _t5_bucket_table():
    qi = np.arange(BLOCK)[:, None]
    kj = np.arange(BLOCK)[None, :]
    n = (qi - kj) % BLOCK
    max_exact = N_BUCKETS // 2
    large = max_exact + (np.log(np.maximum(n, 1) / max_exact) / np.log(MAX_DIST / max_exact)
                         * (N_BUCKETS - max_exact)).astype(np.int32)
    large = np.minimum(large, N_BUCKETS - 1)
    return np.where(n < max_exact, n, large).astype(np.int32)


SMALL_NAMES = ("rel_bias", "norm_mix_g", "q_norm_g", "k_norm_g", "sinks", "conv_b", "conv_ln_g", "conv_ln_b",
               "attn_out_g", "conv_out_g", "norm_mlp_g")


def _local_step(x, target, small, conv_w, get_w, put_g, token):
    bucket = jnp.asarray(_t5_bucket_table())
    bias = _bias_table(small["rel_bias"], bucket)
    row = lambda a, l: a[l][None, :]
    cw_pad = jnp.pad(conv_w, ((0, 0), (0, HALO - CONV_K), (0, 0)))
    saved, weights = [], []
    for l in range(DEPTH):
        wt_in, w_out = get_w(l, 0, x)
        zq, zkv, zu, zg = _fwd_in(x, row(small["norm_mix_g"], l) + (token if l == 0 else 0.0), wt_in)
        a, mix_a = _attn_fwd(zq, zkv, bias, small["sinks"][l], row(small["q_norm_g"], l), row(small["k_norm_g"], l),
                             row(small["attn_out_g"], l))
        y, mix_c = _conv_fwd(zu, zg, cw_pad[l], row(small["conv_b"], l), row(small["conv_ln_g"], l),
                             row(small["conv_ln_b"], l), row(small["conv_out_g"], l))
        x1 = _fwd_out(x, mix_a, mix_c, w_out)
        wt_up, w_down = get_w(l, 1, x1)
        weights.append((wt_in, w_out, wt_up, w_down))
        x2, up = _mlp_fwd(x1, row(small["norm_mlp_g"], l), wt_up, w_down)
        saved.append((x, zq, zkv, zu, zg, a, mix_a, y, mix_c, x1, up))
        x = x2
    loss_part, dx = _loss_head(x, target)

    gs = {n: [None] * DEPTH for n in SMALL_NAMES if n != "rel_bias"}
    g_conv_w, dbias = [None] * DEPTH, [None] * DEPTH
    token = 0.0
    for l in reversed(range(DEPTH)):
        x0, zq, zkv, zu, zg, a, mix_a, y, mix_c, x1, up = saved[l]
        wt_in, w_out, wt_up, w_down = weights[l]
        dx1, dup, h2, gs["norm_mlp_g"][l] = _mlp_bwd(dx, x1, up, row(small["norm_mlp_g"], l) + token, wt_up, w_down)
        g_up = _wgrad(dup, h2, "wgrad_up", chunked=True)
        g_down = _wgrad(up, dx, "wgrad_down", chunked=True, square_relu=True)
        dma, dmc, dx1b = _bwd_out(dx1, w_out)
        g_out = jnp.concatenate([_wgrad(mix_a, dx1b, "wgrad_out_a"), _wgrad(mix_c, dx1b, "wgrad_out_c")], axis=0)
        du, dgt, pg = _conv_bwd(dmc, y, zu, zg, cw_pad[l], row(small["conv_ln_g"], l), row(small["conv_ln_b"], l),
                                row(small["conv_out_g"], l))
        g_conv_w[l] = pg[:CONV_K]
        gs["conv_b"][l], gs["conv_ln_g"][l], gs["conv_ln_b"][l], gs["conv_out_g"][l] = pg[32], pg[33], pg[34], pg[35]
        dq, dkv, dbias[l], dog, dqg, dkg, dsk = _attn_bwd(
            dma, a, zq, zkv, bias, small["sinks"][l], row(small["q_norm_g"], l), row(small["k_norm_g"], l),
            row(small["attn_out_g"], l))
        gs["attn_out_g"][l], gs["q_norm_g"][l], gs["k_norm_g"][l], gs["sinks"][l] = dog[0], dqg[0], dkg[0], dsk[0, :N_HEADS]
        dx, h, gs["norm_mix_g"][l] = _bwd_in(dx1, x0, dq, dkv, du, dgt, row(small["norm_mix_g"], l), wt_in)
        g_in = jnp.concatenate([_wgrad(dq, h, "wgrad_in_q"), _wgrad(dkv, h, "wgrad_in_kv"),
                                _wgrad(du, h, "wgrad_in_u"), _wgrad(dgt, h, "wgrad_in_g")], axis=0)
        token = put_g(l, (g_in, g_out, g_up, g_down))
    small_grads = {n: jnp.stack([jnp.reshape(v, (-1,)) for v in vals]) for n, vals in gs.items()}
    small_grads["rel_bias"] = _bias_grad(jnp.stack(dbias), bucket)[:, :N_BUCKETS].T
    return loss_part, dx, small_grads, jnp.stack(g_conv_w)


def kernel(x, rel_bias, norm_mix_g, w_in, q_norm_g, k_norm_g, sinks, conv_w, conv_b, conv_ln_g, conv_ln_b, attn_out_g, conv_out_g, w_out, norm_mlp_g, w_mlp_up, w_mlp_down, loss_target, m_rel_bias, m_norm_mix_g, m_w_in, m_q_norm_g, m_k_norm_g, m_sinks, m_conv_w, m_conv_b, m_conv_ln_g, m_conv_ln_b, m_attn_out_g, m_conv_out_g, m_w_out, m_norm_mlp_g, m_w_mlp_up, m_w_mlp_down, v_rel_bias, v_norm_mix_g, v_w_in, v_q_norm_g, v_k_norm_g, v_sinks, v_conv_w, v_conv_b, v_conv_ln_g, v_conv_ln_b, v_attn_out_g, v_conv_out_g, v_w_out, v_norm_mlp_g, v_w_mlp_up, v_w_mlp_down):
    args = dict(locals())
    small = {n: args[n] for n in SMALL_NAMES}
    tr = lambda a: jnp.swapaxes(a, 1, 2)
    me = 4 * lax.axis_index("x") + 2 * lax.axis_index("y") + lax.axis_index("c")

    shards = (tr(w_in).astype(BF16), w_out.astype(BF16), tr(w_mlp_up).astype(BF16), w_mlp_down.astype(BF16))
    cw_sh = jnp.pad(tr(conv_w), ((0, 0), (0, 0), (0, HALO - CONV_K))).reshape(DEPTH * (CONV_W // N_DEV), HALO)
    plan = [("gather_0a", [cw_sh, shards[0][0], shards[1][0]]), ("gather_0b", [shards[2][0], shards[3][0]])]
    plan += [(f"gather_{l}", [s[l] for s in shards]) for l in range(1, DEPTH)]
    gathers, token = {}, jnp.zeros((8, 128), F32)
    for name, arrays in plan:
        gathers[name], tok = _exchange_start(arrays, name + "_start", scatter=False)
        token = token + tok

    def own_rows(land, block):
        return lax.dynamic_update_slice(land, block, (me * block.shape[0], 0))

    def gathered(name, after):
        srcs, lands = _exchange_wait(gathers[name], after, name + "_wait", scatter=False)
        return [own_rows(land, src) for land, src in zip(lands, srcs)]

    cw_all, *first = gathered("gather_0a", token)
    conv_w_full = jnp.transpose(cw_all.reshape(N_DEV, DEPTH, CONV_W // N_DEV, HALO), (1, 3, 0, 2))
    conv_w_full = conv_w_full.reshape(DEPTH, HALO, CONV_W)[:, :CONV_K, :]
    later = {}

    def get_w(l, half, after):
        if l == 0:
            return tuple(first) if half == 0 else tuple(gathered("gather_0b", after))
        if half == 0:
            later[l] = gathered(f"gather_{l}", after)
        return tuple(later[l][2 * half:2 * half + 2])

    scatters = [None] * DEPTH

    def put_g(l, grads):
        scatters[l], tok = _exchange_start(list(grads), f"scatter_start_{l}", scatter=True)
        return tok[0, 0]

    loss_part, dx, small_grads, g_conv_w = _local_step(
        x[0], loss_target[0], small, conv_w_full, get_w, put_g, token[0, 0])

    parts = [[] for _ in shards]
    for l in range(DEPTH):
        srcs, lands = _exchange_wait(scatters[l], dx, f"scatter_wait_{l}", scatter=True)
        for k, (land, src) in enumerate(zip(lands, srcs)):
            r = land.shape[0] // N_DEV
            mine = lax.dynamic_slice(src, (me * r, 0), (r, land.shape[1]))
            parts[k].append(own_rows(land, mine).reshape(N_DEV, r, land.shape[1]))
    parts = [jnp.stack(p) for p in parts]
    big_w = (tr(w_in), w_out, tr(w_mlp_up), w_mlp_down)
    big_m = (tr(m_w_in), m_w_out, tr(m_w_mlp_up), m_w_mlp_down)
    big_v = (tr(v_w_in), v_w_out, tr(v_w_mlp_up), v_w_mlp_down)
    big_out = [_adam_big(p, w, m, v) for p, w, m, v in zip(parts, big_w, big_m, big_v)]
    for k in (0, 2):
        big_out[k] = [tr(o) for o in big_out[k]]

    order = [n for n in SMALL_NAMES]
    packed = _pack([small_grads[n] for n in order] + [g_conv_w])
    gathered, = _all_gather((packed[None],))
    summed = _sum_slots(gathered[0].reshape(N_DEV, packed.shape[0], 128))
    shapes = [small[n].shape for n in order] + [(DEPTH, CONV_K, CONV_W)]
    sg = _unpack(summed, shapes)
    g_small = dict(zip(order, sg[:-1]))
    g_small["conv_w"] = lax.dynamic_slice_in_dim(sg[-1], me * (CONV_W // N_DEV), CONV_W // N_DEV, axis=2)
    names = order + ["conv_w"]
    shapes = [args[n].shape for n in names]
    d_p, m_p, v_p = _adam_small(_pack([args[n] for n in names]), _pack([g_small[n] for n in names]),
                                _pack([args["m_" + n] for n in names]), _pack([args["v_" + n] for n in names]))
    res = {"grad": g_small, "delta": dict(zip(names, _unpack(d_p, shapes))),
           "new_m": dict(zip(names, _unpack(m_p, shapes))), "new_v": dict(zip(names, _unpack(v_p, shapes)))}
    for k, n in enumerate(("w_in", "w_out", "w_mlp_up", "w_mlp_down")):
        for kind, val in zip(("grad", "delta", "new_m", "new_v"), big_out[k]):
            res[kind][n] = val

    loss = lax.psum(loss_part[0, 0], ("x", "y", "c"))
    weights = ("rel_bias", "norm_mix_g", "w_in", "q_norm_g", "k_norm_g", "sinks", "conv_w", "conv_b", "conv_ln_g",
               "conv_ln_b", "attn_out_g", "conv_out_g", "w_out", "norm_mlp_g", "w_mlp_up", "w_mlp_down")
    return (loss, dx[None], *[res[kind][n] for kind in ("grad", "delta", "new_m", "new_v") for n in weights])
```

```python
import math

import numpy as np
import jax
import jax.numpy as jnp
from jax import lax
from jax.experimental import pallas as pl
from jax.experimental.pallas import tpu as pltpu

F32, BF16 = jnp.float32, jnp.bfloat16
D_MODEL = 1024
DEPTH = 4
HEAD_DIM = 64
N_HEADS = 8
N_KV = 2
GQA = N_HEADS // N_KV
ATTN_W = N_HEADS * HEAD_DIM
KV_W = N_KV * HEAD_DIM
CONV_W = D_MODEL - ATTN_W
IN_W = ATTN_W + 2 * KV_W + 2 * CONV_W
BLOCK = 128
CONV_K = 31
HALO = 32
N_BUCKETS = 32
MAX_DIST = 128
D_FF = 4 * D_MODEL
FF_CHUNK = 512
N_FF = D_FF // FF_CHUNK
EPS = 1e-6
NEG = -1e30
N_DEV = 8
ADAM_LR, ADAM_B1, ADAM_B2, ADAM_EPS, ADAM_WD, ADAM_STEP = 0.001, 0.9, 0.999, 1e-08, 0.01, 10
VMEM_LIMIT = 56 * 1024 * 1024
MESH = pl.DeviceIdType.MESH

RESIDENT = pl.BlockSpec(memory_space=pltpu.VMEM)
IN_SMEM = pl.BlockSpec(memory_space=pltpu.SMEM)
ANY = pl.BlockSpec(memory_space=pl.ANY)


def _call(body, **kw):
    return pl.pallas_call(body, **kw)


def _params(*sem):
    return pltpu.CompilerParams(dimension_semantics=sem, vmem_limit_bytes=VMEM_LIMIT)


def _dot(a, b):
    return lax.dot_general(a, b, (((1,), (0,)), ((), ())), preferred_element_type=F32)


def _dot_nt(a, b):
    return lax.dot_general(a, b, (((1,), (1,)), ((), ())), preferred_element_type=F32)


def _dot_tn(a, b):
    return lax.dot_general(a, b, (((0,), (0,)), ((), ())), preferred_element_type=F32)


def _sig(x):
    return 1.0 / (1.0 + jnp.exp(-x))


def _rms(x):
    r = lax.rsqrt(jnp.mean(x * x, axis=-1, keepdims=True) + EPS)
    return x * r, r


def _rms_bwd(dy_g, xh, r):
    return r * (dy_g - xh * jnp.mean(dy_g * xh, axis=-1, keepdims=True))


def _rows(tm, w):
    return pl.BlockSpec((tm, w), lambda i: (i, 0))


def _acc_rows(w, rows=1):
    return pl.BlockSpec((rows, w), lambda i: (0, 0))


def _colsum(x):
    return jnp.sum(x, axis=0, keepdims=True)


def _fwd_in(x, g, wt):
    T = x.shape[0]
    tm = 512

    def body(x_ref, g_ref, w_ref, q_ref, kv_ref, u_ref, gt_ref):
        xh, _ = _rms(x_ref[...])
        h = (xh * g_ref[...]).astype(BF16)
        z = _dot_nt(h, w_ref[...])
        q_ref[...] = z[:, :ATTN_W]
        kv_ref[...] = z[:, ATTN_W:ATTN_W + 2 * KV_W]
        u_ref[...] = z[:, ATTN_W + 2 * KV_W:ATTN_W + 2 * KV_W + CONV_W]
        gt_ref[...] = z[:, ATTN_W + 2 * KV_W + CONV_W:]

    widths = (ATTN_W, 2 * KV_W, CONV_W, CONV_W)
    return _call(
        body, name="fwd_in", grid=(T // tm,),
        in_specs=[_rows(tm, D_MODEL), RESIDENT, RESIDENT],
        out_specs=[_rows(tm, w) for w in widths],
        out_shape=[jax.ShapeDtypeStruct((T, w), F32) for w in widths],
        compiler_params=_params("parallel"),
    )(x, g, wt)


ATTN_SUB = 2


def _tri4():
    qi = lax.broadcasted_iota(jnp.int32, (BLOCK, BLOCK), 0)
    kj = lax.broadcasted_iota(jnp.int32, (BLOCK, BLOCK), 1)
    return jnp.concatenate([kj <= qi] * GQA, axis=0)


def _kv_block(kv, kh, kg):
    khat, rk = _rms(kv[:, kh * HEAD_DIM:(kh + 1) * HEAD_DIM])
    return dict(khat=khat, rk=rk, kn=(khat * kg).astype(BF16),
                v=kv[:, KV_W + kh * HEAD_DIM:KV_W + (kh + 1) * HEAD_DIM].astype(BF16))


def _head_rows(g):
    row = lax.broadcasted_iota(jnp.int32, (GQA * BLOCK, 1), 0)
    return (row >= g * BLOCK) & (row < (g + 1) * BLOCK)


def _attn_group(zq, kp, kc, kh, bias_ref, sinks_ref, qg, tri, has_prev):
    scale = 1.0 / math.sqrt(HEAD_DIM)
    qraw = jnp.concatenate(
        [zq[:, (kh * GQA + g) * HEAD_DIM:(kh * GQA + g + 1) * HEAD_DIM] for g in range(GQA)], axis=0)
    qhat, rq = _rms(qraw)
    qf = qhat * qg
    qn = qf.astype(BF16)
    qs = (qf * scale).astype(BF16)
    s = jnp.where(tri, _dot_nt(qs, kc["kn"]), _dot_nt(qs, kp["kn"]))
    s = s + bias_ref[kh * GQA:(kh + 1) * GQA].reshape(GQA * BLOCK, BLOCK)
    if has_prev is not None:
        s = jnp.where(tri | has_prev, s, NEG)
    sink =jnp.zeros((GQA * BLOCK, 1), F32)
    for g in range(GQA):
        sink = jnp.where(_head_rows(g), sinks_ref[kh * GQA + g], sink)
    m = jnp.maximum(jnp.max(s, axis=-1, keepdims=True), sink)
    p = jnp.exp(s - m)
    es = jnp.exp(sink - m)
    den = jnp.sum(p, axis=-1, keepdims=True) + es
    pn = p / den
    return dict(qhat=qhat, rq=rq, qn=qn, pn=pn, psink=es / den)


def _split(pb, tri):
    zero = jnp.zeros_like(pb)
    return jnp.where(tri, pb, zero), jnp.where(tri, zero, pb)


def _unstack_heads(o):
    return [o[g * BLOCK:(g + 1) * BLOCK] for g in range(GQA)]


def _attn_fwd(zq, zkv, bias, sinks, qg, kg, og):
    T = zq.shape[0]
    rows = ATTN_SUB * BLOCK

    def body(q_ref, kvc_ref, kvp_ref, bias_ref, sinks_ref, qg_ref, kg_ref, og_ref, a_ref, mix_ref):
        n = pl.program_id(0)
        tri = _tri4()
        qgv, kgv = qg_ref[...], kg_ref[...]
        kvs = [kvp_ref[...]] + [kvc_ref[i * BLOCK:(i + 1) * BLOCK, :] for i in range(ATTN_SUB)]
        keys = [[_kv_block(kv, kh, kgv) for kh in range(N_KV)] for kv in kvs]
        for i in range(ATTN_SUB):
            zq_v = q_ref[i * BLOCK:(i + 1) * BLOCK, :]
            has_prev = (n > 0) if i == 0 else None
            outs = []
            for kh in range(N_KV):
                kp, kc = keys[i][kh], keys[i + 1][kh]
                c = _attn_group(zq_v, kp, kc, kh, bias_ref, sinks_ref, qgv, tri, has_prev)
                p_c, p_p = _split(c["pn"].astype(BF16), tri)
                outs += _unstack_heads(_dot(p_c, kc["v"]) + _dot(p_p, kp["v"]))
            a = jnp.concatenate(outs, axis=-1)
            a_ref[i * BLOCK:(i + 1) * BLOCK, :] = a
            ah, _ = _rms(a)
            mix_ref[i * BLOCK:(i + 1) * BLOCK, :] = (ah * og_ref[...]).astype(BF16)

    return _call(
        body, name="attn_fwd", grid=(T // rows,),
        in_specs=[_rows(rows, ATTN_W), _rows(rows, 2 * KV_W),
                  pl.BlockSpec((BLOCK, 2 * KV_W), lambda n: (jnp.maximum(n * ATTN_SUB - 1, 0), 0)),
                  RESIDENT, IN_SMEM, RESIDENT, RESIDENT, RESIDENT],
        out_specs=[_rows(rows, ATTN_W), _rows(rows, ATTN_W)],
        out_shape=[jax.ShapeDtypeStruct((T, ATTN_W), F32), jax.ShapeDtypeStruct((T, ATTN_W), BF16)],
        compiler_params=_params("parallel"),
    )(zq, zkv, zkv, bias, sinks, qg, kg, og)


def _conv_post(y, lg, lb, og):
    mu = jnp.mean(y, axis=-1, keepdims=True)
    yc = y - mu
    rstd = lax.rsqrt(jnp.mean(yc * yc, axis=-1, keepdims=True) + EPS)
    yn = yc * rstd
    ln = yn * lg + lb
    sg = _sig(ln)
    c = ln * sg
    ch, r = _rms(c)
    return yn, rstd, ln, sg, ch, r


CONV_CHUNK = 64


SLAB_ROWS = CONV_CHUNK + 8 * ((CONV_K - 1) // 8)


def _shifted_taps(buf, slab, r0, base):
    for b in range(8):
        taps = range(b, CONV_K, 8)
        span = CONV_CHUNK + 8 * (len(taps) - 1)
        slab[0:span, :] = buf[r0 + base + b:r0 + base + b + span, :]
        for a, j in enumerate(taps):
            yield j, slab[8 * a:8 * a + CONV_CHUNK, :]


def _fold8(x):
    return jnp.sum(x.reshape(x.shape[0] // 8, 8, x.shape[1]), axis=0)


def _conv_fwd(zu, zg, cw, cb, lg, lb, og):
    T = zu.shape[0]
    tt = 512
    halo_spec = pl.BlockSpec((HALO, CONV_W), lambda i: (jnp.maximum(i * (tt // HALO) - 1, 0), 0))

    def body(u_ref, g_ref, uh_ref, gh_ref, cw_ref, cb_ref, lg_ref, lb_ref, og_ref, y_ref, mix_ref, buf, slab):
        i = pl.program_id(0)
        hal = uh_ref[...] * _sig(gh_ref[...])
        buf[0:HALO, :] = jnp.where(i > 0, hal, 0.0)
        buf[HALO:HALO + tt, :] = u_ref[...] * _sig(g_ref[...])
        cbv, lgv, lbv, ogv = cb_ref[...], lg_ref[...], lb_ref[...], og_ref[...]
        for r0 in range(0, tt, CONV_CHUNK):
            rs = slice(r0, r0 + CONV_CHUNK)
            acc = jnp.zeros((CONV_CHUNK, CONV_W), F32)
            for j, win in _shifted_taps(buf, slab, r0, HALO - (CONV_K - 1)):
                acc = acc + cw_ref[j:j + 1, :] * win
            y = acc + cbv
            y_ref[rs, :] = y
            mix_ref[rs, :] = (_conv_post(y, lgv, lbv, ogv)[4] * ogv).astype(BF16)

    return _call(
        body, name="conv_fwd", grid=(T // tt,),
        in_specs=[_rows(tt, CONV_W), _rows(tt, CONV_W), halo_spec, halo_spec] + [RESIDENT] * 5,
        out_specs=[_rows(tt, CONV_W), _rows(tt, CONV_W)],
        out_shape=[jax.ShapeDtypeStruct((T, CONV_W), F32), jax.ShapeDtypeStruct((T, CONV_W), BF16)],
        scratch_shapes=[pltpu.VMEM((HALO + tt, CONV_W), F32), pltpu.VMEM((SLAB_ROWS, CONV_W), F32)],
        compiler_params=_params("parallel"),
    )(zu, zg, zu, zg, cw, cb, lg, lb, og)


def _fwd_out(x, mix_a, mix_c, w_out):
    T = x.shape[0]
    tm = 512

    def body(x_ref, a_ref, c_ref, w_ref, o_ref):
        o_ref[...] = (x_ref[...] + _dot(a_ref[...], w_ref[0:ATTN_W, :])
                      + _dot(c_ref[...], w_ref[ATTN_W:, :]))

    return _call(
        body, name="fwd_out", grid=(T // tm,),
        in_specs=[_rows(tm, D_MODEL), _rows(tm, ATTN_W), _rows(tm, CONV_W), RESIDENT],
        out_specs=_rows(tm, D_MODEL),
        out_shape=jax.ShapeDtypeStruct((T, D_MODEL), F32),
        compiler_params=_params("parallel"),
    )(x, mix_a, mix_c, w_out)


def _chunked(tm):
    return pl.BlockSpec((N_FF, tm, FF_CHUNK), lambda i: (0, i, 0))


def _mlp_fwd(x, g, wup_t, wdown):
    T = x.shape[0]
    tm = 512

    def body(x_ref, g_ref, wu_ref, wd_ref, o_ref, up_ref):
        xv = x_ref[...]
        xh, _ = _rms(xv)
        h = (xh * g_ref[...]).astype(BF16)
        acc = xv
        for c in range(N_FF):
            rows = slice(c * FF_CHUNK, (c + 1) * FF_CHUNK)
            up = _dot_nt(h, wu_ref[rows, :])
            up_ref[c] = up.astype(BF16)
            act = jnp.square(jnp.maximum(up, 0.0))
            acc = acc + _dot(act.astype(BF16), wd_ref[rows, :])
        o_ref[...] = acc

    return _call(
        body, name="mlp_fwd", grid=(T // tm,),
        in_specs=[_rows(tm, D_MODEL), RESIDENT, RESIDENT, RESIDENT],
        out_specs=[_rows(tm, D_MODEL), _chunked(tm)],
        out_shape=[jax.ShapeDtypeStruct((T, D_MODEL), F32), jax.ShapeDtypeStruct((N_FF, T, FF_CHUNK), BF16)],
        compiler_params=_params("parallel"),
    )(x, g, wup_t, wdown)


def _loss_head(y, target):
    T = y.shape[0]
    tm = 512

    def body(y_ref, t_ref, l_ref, d_ref):
        @pl.when(pl.program_id(0) == 0)
        def _():
            l_ref[...] = jnp.zeros_like(l_ref)
        e = y_ref[...] - t_ref[...]
        d_ref[...] = e / D_MODEL
        l_ref[...] += 0.5 * jnp.sum(jnp.mean(e * e, axis=-1, keepdims=True))

    return _call(
        body, name="loss_head", grid=(T // tm,),
        in_specs=[_rows(tm, D_MODEL), _rows(tm, D_MODEL)],
        out_specs=[_acc_rows(128, 8), _rows(tm, D_MODEL)],
        out_shape=[jax.ShapeDtypeStruct((8, 128), F32), jax.ShapeDtypeStruct((T, D_MODEL), F32)],
        compiler_params=_params("arbitrary"),
    )(y, target)


def _bias_table(rel_bias, bucket):
    def body(rb_ref, bk_ref, o_ref):
        bk = bk_ref[...]
        for h in range(N_HEADS):
            acc = jnp.zeros((BLOCK, BLOCK), F32)
            for b in range(N_BUCKETS):
                acc = jnp.where(bk == b, rb_ref[b, h], acc)
            o_ref[h] = acc

    return _call(
        body, name="bias_table", in_specs=[IN_SMEM, RESIDENT], out_specs=RESIDENT,
        out_shape=jax.ShapeDtypeStruct((N_HEADS, BLOCK, BLOCK), F32),
    )(rel_bias, bucket)


def _mlp_bwd(dx2, x1, up, g, wup_t, wdown):
    T = x1.shape[0]
    tm = 512

    def body(d_ref, x_ref, up_ref, g_ref, wu_ref, wd_ref, dx_ref, dup_ref, h_ref, dg_ref, db_ref):
        @pl.when(pl.program_id(0) == 0)
        def _():
            dg_ref[...] = jnp.zeros_like(dg_ref)
        d2 = d_ref[...]
        d2b = d2.astype(BF16)
        db_ref[...] = d2b
        xh, r = _rms(x_ref[...])
        gv = g_ref[...]
        h_ref[...] = (xh * gv).astype(BF16)
        dh = jnp.zeros((tm, D_MODEL), F32)
        for c in range(N_FF):
            rows = slice(c * FF_CHUNK, (c + 1) * FF_CHUNK)
            dact = _dot_nt(d2b, wd_ref[rows, :])
            dup = (dact * (2.0 * jnp.maximum(up_ref[c].astype(F32), 0.0))).astype(BF16)
            dup_ref[c] = dup
            dh = dh + _dot(dup, wu_ref[rows, :])
        dg_ref[...] += _colsum(dh * xh)
        dx_ref[...] = d2 + _rms_bwd(dh * gv, xh, r)

    return _call(
        body, name="mlp_bwd", grid=(T // tm,),
        in_specs=[_rows(tm, D_MODEL), _rows(tm, D_MODEL), _chunked(tm), RESIDENT, RESIDENT, RESIDENT],
        out_specs=[_rows(tm, D_MODEL), _chunked(tm), _rows(tm, D_MODEL), _acc_rows(D_MODEL), _rows(tm, D_MODEL)],
        out_shape=[jax.ShapeDtypeStruct((T, D_MODEL), F32), jax.ShapeDtypeStruct((N_FF, T, FF_CHUNK), BF16),
                   jax.ShapeDtypeStruct((T, D_MODEL), BF16), jax.ShapeDtypeStruct((1, D_MODEL), F32),
                   jax.ShapeDtypeStruct((T, D_MODEL), BF16)],
        compiler_params=_params("arbitrary"),
    )(dx2, x1, up, g, wup_t, wdown)


def _wgrad(a, b, name, chunked=False, square_relu=False):
    if chunked:
        nr, T, tr = a.shape
    else:
        T, R = a.shape
        tr = min(R, FF_CHUNK)
        nr = R // tr
    tk = min(T, 4096)
    nk = T // tk

    def body(a_ref, b_ref, o_ref, acc):
        k = pl.program_id(1)

        @pl.when(k == 0)
        def _():
            acc[...] = jnp.zeros_like(acc)
        av = a_ref[...]
        if square_relu:
            av = jnp.square(jnp.maximum(av.astype(F32), 0.0))
        acc[...] += _dot_tn(av.astype(BF16), b_ref[...].astype(BF16))

        @pl.when(k == nk - 1)
        def _():
            o_ref[...] = acc[...].astype(BF16)

    a_spec = (pl.BlockSpec((None, tk, tr), lambda r, k: (r, k, 0)) if chunked
              else pl.BlockSpec((tk, tr), lambda r, k: (k, r)))
    return _call(
        body, name=name, grid=(nr, nk),
        in_specs=[a_spec, pl.BlockSpec((tk, D_MODEL), lambda r, k: (k, 0))],
        out_specs=pl.BlockSpec((tr, D_MODEL), lambda r, k: (r, 0)),
        out_shape=jax.ShapeDtypeStruct((nr * tr, D_MODEL), BF16),
        scratch_shapes=[pltpu.VMEM((tr, D_MODEL), F32)],
        compiler_params=_params("parallel", "arbitrary"),
    )(a, b)


def _bwd_out(dx1, w_out):
    T = dx1.shape[0]
    tm = 512

    def body(d_ref, w_ref, da_ref, dc_ref, db_ref):
        db = d_ref[...].astype(BF16)
        db_ref[...] = db
        dm = _dot_nt(db, w_ref[...])
        da_ref[...] = dm[:, :ATTN_W]
        dc_ref[...] = dm[:, ATTN_W:]

    return _call(
        body, name="bwd_out", grid=(T // tm,),
        in_specs=[_rows(tm, D_MODEL), RESIDENT],
        out_specs=[_rows(tm, ATTN_W), _rows(tm, CONV_W), _rows(tm, D_MODEL)],
        out_shape=[jax.ShapeDtypeStruct((T, ATTN_W), F32), jax.ShapeDtypeStruct((T, CONV_W), F32),
                   jax.ShapeDtypeStruct((T, D_MODEL), BF16)],
        compiler_params=_params("parallel"),
    )(dx1, w_out)


def _conv_bwd(dmix, y, zu, zg, cw, lg, lb, og):
    T = y.shape[0]
    tt = 512
    nt = T // tt
    per = tt // HALO
    prev_spec = pl.BlockSpec((HALO, CONV_W), lambda i: (jnp.maximum(i * per - 1, 0), 0))
    next_spec = pl.BlockSpec((HALO, CONV_W), lambda i: (jnp.minimum((i + 1) * per, nt * per - 1), 0))

    def body(dm_ref, dmn_ref, y_ref, yn_ref, u_ref, g_ref, uh_ref, gh_ref, cw_ref, lg_ref, lb_ref, og_ref,
             du_ref, dg_ref, pg_ref, hbuf, dybuf, dwacc, pacc, slab):
        i = pl.program_id(0)

        @pl.when(i == 0)
        def _():
            dwacc[...] = jnp.zeros_like(dwacc)
            pacc[...] = jnp.zeros_like(pacc)
        lgv, lbv, ogv = lg_ref[...], lb_ref[...], og_ref[...]

        def chain(yv, dm):
            yn, rstd, ln, sg, ch, r = _conv_post(yv, lgv, lbv, ogv)
            dc = _rms_bwd(dm * ogv, ch, r)
            dln = dc * sg * (1.0 + ln * (1.0 - sg))
            dyn = dln * lgv
            dy = rstd * (dyn - jnp.mean(dyn, axis=-1, keepdims=True)
                         - yn * jnp.mean(dyn * yn, axis=-1, keepdims=True))
            return dy, dm * ch, dln * yn, dln

        hbuf[0:HALO, :] = jnp.where(i > 0, uh_ref[...] * _sig(gh_ref[...]), 0.0)
        for r0 in range(0, tt, CONV_CHUNK):
            rs = slice(r0, r0 + CONV_CHUNK)
            hbuf[HALO + r0:HALO + r0 + CONV_CHUNK, :] = u_ref[rs, :] * _sig(g_ref[rs, :])
            dy, p_og, p_lg, p_lb = chain(y_ref[rs, :], dm_ref[rs, :])
            dybuf[rs, :] = dy
            for k, part in enumerate((dy, p_lg, p_lb, p_og)):
                pacc[8 * k:8 * k + 8, :] += _fold8(part)
        dyh, _, _, _ = chain(yn_ref[...], dmn_ref[...])
        dybuf[tt:tt + HALO, :] = jnp.where(i < nt - 1, dyh, 0.0)
        for r0 in range(0, tt, CONV_CHUNK):
            rs = slice(r0, r0 + CONV_CHUNK)
            dy = dybuf[rs, :]
            dh = jnp.zeros((CONV_CHUNK, CONV_W), F32)
            for j, win in _shifted_taps(dybuf, slab, r0, 0):
                dh = dh + cw_ref[CONV_K - 1 - j:CONV_K - j, :] * win
            for j, win in _shifted_taps(hbuf, slab, r0, HALO - (CONV_K - 1)):
                dwacc[8 * j:8 * j + 8, :] += _fold8(dy * win)
            sgt = _sig(g_ref[rs, :])
            du_ref[rs, :] = (dh * sgt).astype(BF16)
            dg_ref[rs, :] = (dh * u_ref[rs, :] * sgt * (1.0 - sgt)).astype(BF16)

        @pl.when(i == nt - 1)
        def _():
            pg_ref[...] = jnp.zeros_like(pg_ref)
            for j in range(CONV_K):
                pg_ref[j:j + 1, :] = _colsum(dwacc[8 * j:8 * j + 8, :])
            for k in range(4):
                pg_ref[32 + k:33 + k, :] = _colsum(pacc[8 * k:8 * k + 8, :])

    return _call(
        body, name="conv_bwd", grid=(nt,),
        in_specs=[_rows(tt, CONV_W), next_spec, _rows(tt, CONV_W), next_spec, _rows(tt, CONV_W), _rows(tt, CONV_W),
                  prev_spec, prev_spec] + [RESIDENT] * 4,
        out_specs=[_rows(tt, CONV_W), _rows(tt, CONV_W), _acc_rows(CONV_W, 40)],
        out_shape=[jax.ShapeDtypeStruct((T, CONV_W), BF16), jax.ShapeDtypeStruct((T, CONV_W), BF16),
                   jax.ShapeDtypeStruct((40, CONV_W), F32)],
        scratch_shapes=[pltpu.VMEM((HALO + tt, CONV_W), F32), pltpu.VMEM((tt + HALO, CONV_W), F32),
                        pltpu.VMEM((8 * HALO, CONV_W), F32), pltpu.VMEM((32, CONV_W), F32),
                        pltpu.VMEM((SLAB_ROWS, CONV_W), F32)],
        compiler_params=_params("arbitrary"),
    )(dmix, dmix, y, y, zu, zg, zu, zg, cw, lg, lb, og)


def _attn_bwd(dmix, a, zq, zkv, bias, sinks, qg, kg, og):
    T = zq.shape[0]
    rows = ATTN_SUB * BLOCK
    ns = T // rows
    nb = T // BLOCK
    cur = lambda w: pl.BlockSpec((rows, w), lambda n: (jnp.minimum(n, ns - 1), 0))
    scale = 1.0 / math.sqrt(HEAD_DIM)
    done = rows - BLOCK

    def body(dm_ref, a_ref, q_ref, kvc_ref, kvp_ref, bias_ref, sinks_ref, qg_ref, kg_ref, og_ref,
             dq_ref, dkv_ref, db_ref, dog_ref, dqg_ref, dkg_ref, dsk_ref, carry):
        n = pl.program_id(0)

        @pl.when(n == 0)
        def _():
            for ref in (db_ref, dog_ref, dqg_ref, dkg_ref, dsk_ref, carry):
                ref[...] = jnp.zeros_like(ref)

        @pl.when(n < ns)
        def _():
            tri = _tri4()
            ogv, qgv, kgv = og_ref[...], qg_ref[...], kg_ref[...]
            lane = lax.broadcasted_iota(jnp.int32, (1, 128), 1)
            kvs = [kvp_ref[...]] + [kvc_ref[i * BLOCK:(i + 1) * BLOCK, :] for i in range(ATTN_SUB)]
            keys = [[_kv_block(kv, kh, kgv) for kh in range(N_KV)] for kv in kvs]
            dkn = [[jnp.zeros((BLOCK, HEAD_DIM), F32)] * N_KV for _ in kvs]
            dv = [[jnp.zeros((BLOCK, HEAD_DIM), F32)] * N_KV for _ in kvs]
            dsk = jnp.zeros((1, 128), F32)
            for i in range(ATTN_SUB):
                blk = slice(i * BLOCK, (i + 1) * BLOCK)
                ah, ra = _rms(a_ref[blk, :])
                dm = dm_ref[blk, :]
                dog_ref[...] += _colsum(dm * ah)
                da = _rms_bwd(dm * ogv, ah, ra)
                zq_v = q_ref[blk, :]
                has_prev = (n > 0) if i == 0 else None
                dqs = []
                for kh in range(N_KV):
                    kp, kc = keys[i][kh], keys[i + 1][kh]
                    c = _attn_group(zq_v, kp, kc, kh, bias_ref, sinks_ref, qgv, tri, has_prev)
                    do = jnp.concatenate(
                        [da[:, (kh * GQA + g) * HEAD_DIM:(kh * GQA + g + 1) * HEAD_DIM] for g in range(GQA)], axis=0)
                    dob = do.astype(BF16)
                    pn = c["pn"]
                    p_c, p_p = _split(pn.astype(BF16), tri)
                    dv[i + 1][kh] = dv[i + 1][kh] + _dot_tn(p_c, dob)
                    dv[i][kh] = dv[i][kh] + _dot_tn(p_p, dob)
                    dp = jnp.where(tri, _dot_nt(dob, kc["v"]), _dot_nt(dob, kp["v"]))
                    dl = jnp.sum(pn * dp, axis=-1, keepdims=True)
                    ds = pn * (dp - dl)
                    dsr = -c["psink"] * dl
                    for g in range(GQA):
                        dsk = dsk + jnp.where(lane == kh * GQA + g, jnp.sum(jnp.where(_head_rows(g), dsr, 0.0)), 0.0)
                    db_ref[kh * GQA:(kh + 1) * GQA] += ds.reshape(GQA, BLOCK, BLOCK)
                    ds_c, ds_p = _split((ds * scale).astype(BF16), tri)
                    dqn = _dot(ds_c, kc["kn"]) + _dot(ds_p, kp["kn"])
                    dkn[i + 1][kh] = dkn[i + 1][kh] + _dot_tn(ds_c, c["qn"])
                    dkn[i][kh] = dkn[i][kh] + _dot_tn(ds_p, c["qn"])
                    dqg_ref[...] += _colsum(dqn * c["qhat"])
                    dqs += _unstack_heads(_rms_bwd(dqn * qgv, c["qhat"], c["rq"]))
                dq_ref[blk, :] = jnp.concatenate(dqs, axis=-1).astype(BF16)
            dsk_ref[...] += dsk
            dkv = []
            for j in range(ATTN_SUB + 1):
                dk = []
                for kh in range(N_KV):
                    key = keys[j][kh]
                    dkg_ref[...] += _colsum(dkn[j][kh] * key["khat"])
                    dk.append(_rms_bwd(dkn[j][kh] * kgv, key["khat"], key["rk"]))
                dkv.append(jnp.concatenate(dk + dv[j], axis=-1))
            if done:
                dkv_ref[0:done, :] = carry[0:done, :].astype(BF16)
            dkv_ref[done:rows, :] = (carry[done:rows, :] + dkv[0]).astype(BF16)
            for j in range(1, ATTN_SUB + 1):
                carry[(j - 1) * BLOCK:j * BLOCK, :] = dkv[j]

        @pl.when(n == ns)
        def _():
            dkv_ref[...] = carry[...].astype(BF16)

    small = lambda w: pl.BlockSpec((1, w), lambda n: (0, 0))
    return _call(
        body, name="attn_bwd", grid=(ns + 1,),
        in_specs=[cur(ATTN_W), cur(ATTN_W), cur(ATTN_W), cur(2 * KV_W),
                  pl.BlockSpec((BLOCK, 2 * KV_W), lambda n: (jnp.clip(n * ATTN_SUB - 1, 0, nb - 1), 0)),
                  RESIDENT, IN_SMEM, RESIDENT, RESIDENT, RESIDENT],
        out_specs=[cur(ATTN_W), pl.BlockSpec((rows, 2 * KV_W), lambda n: (jnp.maximum(n - 1, 0), 0)),
                   pl.BlockSpec((N_HEADS, BLOCK, BLOCK), lambda n: (0, 0, 0)),
                   small(ATTN_W), small(HEAD_DIM), small(HEAD_DIM), small(128)],
        out_shape=[jax.ShapeDtypeStruct((T, ATTN_W), BF16), jax.ShapeDtypeStruct((T, 2 * KV_W), BF16),
                   jax.ShapeDtypeStruct((N_HEADS, BLOCK, BLOCK), F32),
                   jax.ShapeDtypeStruct((1, ATTN_W), F32), jax.ShapeDtypeStruct((1, HEAD_DIM), F32),
                   jax.ShapeDtypeStruct((1, HEAD_DIM), F32), jax.ShapeDtypeStruct((1, 128), F32)],
        scratch_shapes=[pltpu.VMEM((rows, 2 * KV_W), F32)],
        compiler_params=_params("arbitrary"),
    )(dmix, a, zq, zkv, zkv, bias, sinks, qg, kg, og)


def _bwd_in(dx1, x, dq, dkv, du, dgt, g, wt):
    T = x.shape[0]
    tm = 512

    def body(d1_ref, x_ref, dq_ref, dkv_ref, du_ref, dgt_ref, g_ref, w_ref, dx_ref, h_ref, dg_ref):
        @pl.when(pl.program_id(0) == 0)
        def _():
            dg_ref[...] = jnp.zeros_like(dg_ref)
        xh, r = _rms(x_ref[...])
        gv = g_ref[...]
        h_ref[...] = (xh * gv).astype(BF16)
        dz = jnp.concatenate([dq_ref[...], dkv_ref[...], du_ref[...], dgt_ref[...]], axis=-1)
        dh = _dot(dz, w_ref[...])
        dg_ref[...] += _colsum(dh * xh)
        dx_ref[...] = d1_ref[...] + _rms_bwd(dh * gv, xh, r)

    return _call(
        body, name="bwd_in", grid=(T // tm,),
        in_specs=[_rows(tm, D_MODEL), _rows(tm, D_MODEL), _rows(tm, ATTN_W), _rows(tm, 2 * KV_W),
                  _rows(tm, CONV_W), _rows(tm, CONV_W), RESIDENT, RESIDENT],
        out_specs=[_rows(tm, D_MODEL), _rows(tm, D_MODEL), _acc_rows(D_MODEL)],
        out_shape=[jax.ShapeDtypeStruct((T, D_MODEL), F32), jax.ShapeDtypeStruct((T, D_MODEL), BF16),
                   jax.ShapeDtypeStruct((1, D_MODEL), F32)],
        compiler_params=_params("arbitrary"),
    )(dx1, x, dq, dkv, du, dgt, g, wt)


def _bias_grad(db, bucket):
    def body(db_ref, bk_ref, o_ref):
        tot = db_ref[0]
        for l in range(1, DEPTH):
            tot = tot + db_ref[l]
        bk = bk_ref[...]
        lane = lax.broadcasted_iota(jnp.int32, (N_HEADS, 128), 1)
        out = jnp.zeros((N_HEADS, 128), F32)
        for b in range(N_BUCKETS):
            v = jnp.sum(jnp.sum(jnp.where(bk == b, tot, 0.0), axis=2), axis=1, keepdims=True)
            out = jnp.where(lane == b, v, out)
        o_ref[...] = out

    return _call(
        body, name="bias_grad", in_specs=[RESIDENT, RESIDENT], out_specs=RESIDENT,
        out_shape=jax.ShapeDtypeStruct((N_HEADS, 128), F32),
        compiler_params=pltpu.CompilerParams(vmem_limit_bytes=VMEM_LIMIT),
    )(db, bucket)


def _place():
    return lax.axis_index("x"), lax.axis_index("y"), lax.axis_index("c")


def _all_gather(shards):
    na = len(shards)

    def body(*refs):
        ins, outs = refs[:na], refs[na:2 * na]
        send_sems, recv_sems, local_sems = refs[2 * na:]
        x, y, c = _place()
        me, sibling = (x, y, c), (x, y, 1 - c)
        chips = [(1 - x, y), (x, 1 - y), (1 - x, 1 - y)]

        def rows(a, p):
            r = ins[a].shape[1]
            return outs[a].at[:, pl.ds((4 * p[0] + 2 * p[1] + p[2]) * r, r), :]

        def copy(a, k, block, to, src=None):
            return pltpu.make_async_remote_copy(
                src_ref=rows(a, block) if src is None else src, dst_ref=rows(a, block),
                send_sem=send_sems.at[a, k], recv_sem=recv_sems.at[a, k], device_id=to, device_id_type=MESH)

        mine = [pltpu.make_async_copy(ins[a], rows(a, me), local_sems.at[a]) for a in range(na)]
        for cp in mine:
            cp.start()
        first = []
        for a in range(na):
            first.append(copy(a, 0, me, sibling, src=ins[a]))
            first += [copy(a, 1 + j, me, (*chip, c), src=ins[a]) for j, chip in enumerate(chips)]
        for cp in first:
            cp.start()
        passed = []
        for j, chip in enumerate(chips):
            for a in range(na):
                copy(a, 1 + j, (*chip, c), me).wait_recv()
                cp = copy(a, 4 + j, (*chip, c), sibling)
                cp.start()
                passed.append(cp)
        for a in range(na):
            copy(a, 0, sibling, me).wait_recv()
            for j, chip in enumerate(chips):
                copy(a, 4 + j, (*chip, 1 - c), me).wait_recv()
        for cp in first + passed:
            cp.wait_send()
        for cp in mine:
            cp.wait()

    return _call(
        body, name="all_gather",
        in_specs=[ANY] * na, out_specs=[ANY] * na,
        out_shape=[jax.ShapeDtypeStruct((s.shape[0], N_DEV * s.shape[1], s.shape[2]), s.dtype) for s in shards],
        scratch_shapes=[pltpu.SemaphoreType.DMA((na, 7)), pltpu.SemaphoreType.DMA((na, 7)),
                        pltpu.SemaphoreType.DMA((na,))],
    )(*shards)


IN_HBM = pl.BlockSpec(memory_space=pltpu.HBM)
IN_SEM = pl.BlockSpec(memory_space=pltpu.SEMAPHORE)
DATAFLOW = pltpu.SideEffectType.DATAFLOW_SIDE_EFFECTING


def _exchange_copies(srcs, lands, send_sems, recv_sems, scatter):
    x, y, c = _place()
    me = 4 * x + 2 * y + c
    copies = []
    for k in range(1, N_DEV):
        p = (x ^ (k >> 2), y ^ ((k >> 1) & 1), c ^ (k & 1))
        for a, (src, land) in enumerate(zip(srcs, lands)):
            r = land.shape[0] // N_DEV
            if scatter:
                src = src.at[pl.ds((4 * p[0] + 2 * p[1] + p[2]) * r, r), :]
            copies.append(pltpu.make_async_remote_copy(
                src_ref=src, dst_ref=land.at[pl.ds(me * r, r), :], send_sem=send_sems.at[a * (N_DEV - 1) + k - 1],
                recv_sem=recv_sems.at[a * (N_DEV - 1) + k - 1], device_id=p, device_id_type=MESH))
    return copies


def _exchange_start(srcs, name, scatter):
    na = len(srcs)
    lands = [lax.empty((s.shape[0] * (1 if scatter else N_DEV), s.shape[1]), s.dtype) for s in srcs]

    def body(*refs):
        ins, lnd = refs[:na], refs[na:2 * na]
        send_sems, recv_sems = refs[2 * na], refs[2 * na + 1]
        token = refs[-1]
        for cp in _exchange_copies(ins, lnd, send_sems, recv_sems, scatter):
            cp.start()
        token[...] = jnp.zeros_like(token)

    hbm = lambda a: pltpu.with_memory_space_constraint(a, pltpu.HBM)
    out = _call(
        body, name=name,
        out_shape=(pltpu.SemaphoreType.DMA((na * (N_DEV - 1),)), pltpu.SemaphoreType.DMA((na * (N_DEV - 1),)),
                   *[pltpu.HBM(a.shape, a.dtype) for a in (*srcs, *lands)], jax.ShapeDtypeStruct((8, 128), F32)),
        in_specs=[IN_HBM] * (2 * na),
        out_specs=(IN_SEM, IN_SEM, *[IN_HBM] * (2 * na), RESIDENT),
        input_output_aliases={i: 2 + i for i in range(2 * na)},
        compiler_params=pltpu.CompilerParams(has_side_effects=DATAFLOW),
    )(*[hbm(a) for a in (*srcs, *lands)])
    return (out[0], out[1], out[2:2 + na], out[2 + na:2 + 2 * na]), out[-1]


def _exchange_wait(state, after, name, scatter):
    send_sems, recv_sems, srcs, lands = state
    na = len(srcs)

    def body(*refs):
        ins, lnd = refs[:na], refs[na:2 * na]
        for cp in _exchange_copies(ins, lnd, refs[2 * na], refs[2 * na + 1], scatter):
            cp.wait_send()
            cp.wait_recv()
        refs[-1][...] = jnp.zeros_like(refs[-1])

    out = _call(
        body, name=name,
        out_shape=(*[pltpu.HBM(a.shape, a.dtype) for a in (*srcs, *lands)], jax.ShapeDtypeStruct((8, 128), F32)),
        in_specs=[IN_HBM] * (2 * na) + [IN_SEM, IN_SEM, ANY],
        out_specs=(*[IN_HBM] * (2 * na), RESIDENT),
        input_output_aliases={i: i for i in range(2 * na)},
        compiler_params=pltpu.CompilerParams(has_side_effects=DATAFLOW),
    )(*srcs, *lands, send_sems, recv_sems, after)
    return out[:na], out[na:2 * na], out[-1]


def _adam_math(w, g, m, v):
    m = ADAM_B1 * m + (1.0 - ADAM_B1) * g
    v = ADAM_B2 * v + (1.0 - ADAM_B2) * jnp.square(g)
    m_hat = m / (1.0 - ADAM_B1 ** ADAM_STEP)
    v_hat = v / (1.0 - ADAM_B2 ** ADAM_STEP)
    delta = -ADAM_LR * (m_hat / (jnp.sqrt(v_hat) + ADAM_EPS) + ADAM_WD * w)
    return delta, m, v


def _adam_big(parts, w, m, v):
    nl, r, cdim = w.shape
    tr = r if r <= 256 else 256
    spec = pl.BlockSpec((None, tr, cdim), lambda l, i: (l, i, 0))

    def body(p_ref, w_ref, m_ref, v_ref, g_ref, d_ref, nm_ref, nv_ref):
        g = p_ref[0].astype(F32)
        for s in range(1, N_DEV):
            g = g + p_ref[s].astype(F32)
        d, nm, nv = _adam_math(w_ref[...], g, m_ref[...], v_ref[...])
        g_ref[...] = g
        d_ref[...] = d
        nm_ref[...] = nm
        nv_ref[...] = nv

    return _call(
        body, name="adam_big", grid=(nl, r // tr),
        in_specs=[pl.BlockSpec((None, N_DEV, tr, cdim), lambda l, i: (l, 0, i, 0)), spec, spec, spec],
        out_specs=[spec] * 4,
        out_shape=[jax.ShapeDtypeStruct(w.shape, F32)] * 4,
        compiler_params=_params("parallel", "parallel"),
    )(parts, w, m, v)


def _sum_slots(parts):
    def body(p_ref, o_ref):
        g = p_ref[0]
        for s in range(1, N_DEV):
            g = g + p_ref[s]
        o_ref[...] = g

    return _call(body, name="sum_slots", in_specs=[RESIDENT], out_specs=RESIDENT,
                 out_shape=jax.ShapeDtypeStruct(parts.shape[1:], F32))(parts)


def _adam_small(w, g, m, v):
    def body(w_ref, g_ref, m_ref, v_ref, d_ref, nm_ref, nv_ref):
        d, nm, nv = _adam_math(w_ref[...], g_ref[...], m_ref[...], v_ref[...])
        d_ref[...] = d
        nm_ref[...] = nm
        nv_ref[...] = nv

    return _call(body, name="adam_small", in_specs=[RESIDENT] * 4, out_specs=[RESIDENT] * 3,
                 out_shape=[jax.ShapeDtypeStruct(w.shape, F32)] * 3)(w, g, m, v)


def _pack(arrays):
    parts = []
    for a in arrays:
        flat = a.reshape(-1)
        n = flat.shape[0]
        padded = -(-n // 1024) * 1024
        parts.append(jnp.pad(flat, (0, padded - n)).reshape(padded // 128, 128))
    return jnp.concatenate(parts, axis=0)


def _unpack(packed, shapes):
    out, row = [], 0
    for shp in shapes:
        n = int(np.prod(shp))
        rows = -(-n // 1024) * 8
        out.append(packed[row:row + rows].reshape(-1)[:n].reshape(shp))
        row += rows
    return out


def _t5_bucket_table():
    qi = np.arange(BLOCK)[:, None]
    kj = np.arange(BLOCK)[None, :]
    n = (qi - kj) % BLOCK
    max_exact = N_BUCKETS // 2
    large = max_exact + (np.log(np.maximum(n, 1) / max_exact) / np.log(MAX_DIST / max_exact)
                         * (N_BUCKETS - max_exact)).astype(np.int32)
    large = np.minimum(large, N_BUCKETS - 1)
    return np.where(n < max_exact, n, large).astype(np.int32)


SMALL_NAMES = ("rel_bias", "norm_mix_g", "q_norm_g", "k_norm_g", "sinks", "conv_b", "conv_ln_g", "conv_ln_b",
               "attn_out_g", "conv_out_g", "norm_mlp_g")


def _local_step(x, target, small, conv_w, get_w, put_g, token):
    bucket = jnp.asarray(_t5_bucket_table())
    bias = _bias_table(small["rel_bias"], bucket)
    row = lambda a, l: a[l][None, :]
    cw_pad = jnp.pad(conv_w, ((0, 0), (0, HALO - CONV_K), (0, 0)))
    saved, weights = [], []
    for l in range(DEPTH):
        wt_in, w_out, tok = get_w(l, 0, x)
        zq, zkv, zu, zg = _fwd_in(x, row(small["norm_mix_g"], l) + tok + (token if l == 0 else 0.0), wt_in)
        a, mix_a = _attn_fwd(zq, zkv, bias, small["sinks"][l], row(small["q_norm_g"], l), row(small["k_norm_g"], l),
                             row(small["attn_out_g"], l))
        y, mix_c = _conv_fwd(zu, zg, cw_pad[l], row(small["conv_b"], l), row(small["conv_ln_g"], l),
                             row(small["conv_ln_b"], l), row(small["conv_out_g"], l))
        x1 = _fwd_out(x, mix_a, mix_c, w_out)
        wt_up, w_down, tok = get_w(l, 1, x1)
        weights.append((wt_in, w_out, wt_up, w_down))
        x2, up = _mlp_fwd(x1, row(small["norm_mlp_g"], l) + tok, wt_up, w_down)
        saved.append((x, zq, zkv, zu, zg, a, mix_a, y, mix_c, x1, up))
        x = x2
    loss_part, dx = _loss_head(x, target)

    gs = {n: [None] * DEPTH for n in SMALL_NAMES if n != "rel_bias"}
    g_conv_w, dbias = [None] * DEPTH, [None] * DEPTH
    token = 0.0
    for l in reversed(range(DEPTH)):
        x0, zq, zkv, zu, zg, a, mix_a, y, mix_c, x1, up = saved[l]
        wt_in, w_out, wt_up, w_down = weights[l]
        dx1, dup, h2, gs["norm_mlp_g"][l], dxb = _mlp_bwd(dx, x1, up, row(small["norm_mlp_g"], l) + token, wt_up, w_down)
        g_up = _wgrad(dup, h2, "wgrad_up", chunked=True)
        g_down = _wgrad(up, dxb, "wgrad_down", chunked=True, square_relu=True)
        token = put_g(l, 1, (g_up, g_down))
        dma, dmc, dx1b = _bwd_out(dx1, w_out)
        g_out = jnp.concatenate([_wgrad(mix_a, dx1b, "wgrad_out_a"), _wgrad(mix_c, dx1b, "wgrad_out_c")], axis=0)
        du, dgt, pg = _conv_bwd(dmc, y, zu, zg, cw_pad[l], row(small["conv_ln_g"], l) + token,
                                row(small["conv_ln_b"], l), row(small["conv_out_g"], l))
        g_conv_w[l] = pg[:CONV_K]
        gs["conv_b"][l], gs["conv_ln_g"][l], gs["conv_ln_b"][l], gs["conv_out_g"][l] = pg[32], pg[33], pg[34], pg[35]
        dq, dkv, dbias[l], dog, dqg, dkg, dsk = _attn_bwd(
            dma, a, zq, zkv, bias, small["sinks"][l], row(small["q_norm_g"], l), row(small["k_norm_g"], l),
            row(small["attn_out_g"], l))
        gs["attn_out_g"][l], gs["q_norm_g"][l], gs["k_norm_g"][l], gs["sinks"][l] = dog[0], dqg[0], dkg[0], dsk[0, :N_HEADS]
        dx, h, gs["norm_mix_g"][l] = _bwd_in(dx1, x0, dq, dkv, du, dgt, row(small["norm_mix_g"], l), wt_in)
        g_in = jnp.concatenate([_wgrad(dq, h, "wgrad_in_q"), _wgrad(dkv, h, "wgrad_in_kv"),
                                _wgrad(du, h, "wgrad_in_u"), _wgrad(dgt, h, "wgrad_in_g")], axis=0)
        token = put_g(l, 0, (g_in, g_out))
    small_grads = {n: jnp.stack([jnp.reshape(v, (-1,)) for v in vals]) for n, vals in gs.items()}
    small_grads["rel_bias"] = _bias_grad(jnp.stack(dbias), bucket)[:, :N_BUCKETS].T
    return loss_part, dx, small_grads, jnp.stack(g_conv_w)


def kernel(x, rel_bias, norm_mix_g, w_in, q_norm_g, k_norm_g, sinks, conv_w, conv_b, conv_ln_g, conv_ln_b, attn_out_g, conv_out_g, w_out, norm_mlp_g, w_mlp_up, w_mlp_down, loss_target, m_rel_bias, m_norm_mix_g, m_w_in, m_q_norm_g, m_k_norm_g, m_sinks, m_conv_w, m_conv_b, m_conv_ln_g, m_conv_ln_b, m_attn_out_g, m_conv_out_g, m_w_out, m_norm_mlp_g, m_w_mlp_up, m_w_mlp_down, v_rel_bias, v_norm_mix_g, v_w_in, v_q_norm_g, v_k_norm_g, v_sinks, v_conv_w, v_conv_b, v_conv_ln_g, v_conv_ln_b, v_attn_out_g, v_conv_out_g, v_w_out, v_norm_mlp_g, v_w_mlp_up, v_w_mlp_down):
    args = dict(locals())
    small = {n: args[n] for n in SMALL_NAMES}
    tr = lambda a: jnp.swapaxes(a, 1, 2)
    me = 4 * lax.axis_index("x") + 2 * lax.axis_index("y") + lax.axis_index("c")

    shards = (tr(w_in).astype(BF16), w_out.astype(BF16), tr(w_mlp_up).astype(BF16), w_mlp_down.astype(BF16))
    cw_sh = jnp.pad(tr(conv_w), ((0, 0), (0, 0), (0, HALO - CONV_K))).reshape(DEPTH * (CONV_W // N_DEV), HALO)
    gathers = {}

    def start_gather(l, half, dep):
        arrays = [s[l] for s in shards[2 * half:2 * half + 2]]
        if (l, half) == (0, 0):
            arrays = [cw_sh] + arrays
        if dep is not None:
            arrays = [a + dep[0, 0].astype(a.dtype) for a in arrays]
        gathers[l, half], tok = _exchange_start(arrays, f"gather_{l}{'ab'[half]}_start", scatter=False)
        return tok

    def own_rows(land, block):
        return lax.dynamic_update_slice(land, block, (me * block.shape[0], 0))

    landed = {}

    def get_w(l, half, after):
        if (l, half) not in landed:
            srcs, lands, tok = _exchange_wait(gathers[l, half], after, f"gather_{l}{'ab'[half]}_wait", scatter=False)
            if l + 1 < DEPTH:
                tok = tok + start_gather(l + 1, half, tok)
            landed[l, half] = (*[own_rows(land, src) for land, src in zip(lands, srcs)], tok[0, 0])
        return landed[l, half]

    token = start_gather(0, 0, None) + start_gather(0, 1, None)
    cw_all, *first = get_w(0, 0, token)
    landed[0, 0] = tuple(first)
    conv_w_full = jnp.transpose(cw_all.reshape(N_DEV, DEPTH, CONV_W // N_DEV, HALO), (1, 3, 0, 2))
    conv_w_full = conv_w_full.reshape(DEPTH, HALO, CONV_W)[:, :CONV_K, :]

    scatters = {}

    def put_g(l, half, grads):
        scatters[l, half], tok = _exchange_start(list(grads), f"scatter_{l}{'ab'[half]}_start", scatter=True)
        return tok[0, 0]

    loss_part, dx, small_grads, g_conv_w = _local_step(
        x[0], loss_target[0], small, conv_w_full, get_w, put_g, token[0, 0])

    big_w = (tr(w_in), w_out, tr(w_mlp_up), w_mlp_down)
    big_m = (tr(m_w_in), m_w_out, tr(m_w_mlp_up), m_w_mlp_down)
    big_v = (tr(v_w_in), v_w_out, tr(v_w_mlp_up), v_w_mlp_down)
    big_out, after = [None] * 4, dx
    for half in (1, 0):
        parts = [[], []]
        for l in reversed(range(DEPTH)):
            srcs, lands, _ = _exchange_wait(scatters[l, half], after, f"scatter_{l}{'ab'[half]}_wait", scatter=True)
            for k, (land, src) in enumerate(zip(lands, srcs)):
                r = land.shape[0] // N_DEV
                mine = lax.dynamic_slice(src, (me * r, 0), (r, land.shape[1]))
                parts[k].insert(0, own_rows(land, mine).reshape(N_DEV, r, land.shape[1]))
        for k in range(2):
            big_out[2 * half + k] = _adam_big(jnp.stack(parts[k]), big_w[2 * half + k], big_m[2 * half + k],
                                              big_v[2 * half + k])
        after = big_out[2 * half + 1][0]
    for k in (0, 2):
        big_out[k] = [tr(o) for o in big_out[k]]

    order = [n for n in SMALL_NAMES]
    packed = _pack([small_grads[n] for n in order] + [g_conv_w])
    slots, = _all_gather((packed[None],))
    summed = _sum_slots(slots[0].reshape(N_DEV, packed.shape[0], 128))
    shapes = [small[n].shape for n in order] + [(DEPTH, CONV_K, CONV_W)]
    sg = _unpack(summed, shapes)
    g_small = dict(zip(order, sg[:-1]))
    g_small["conv_w"] = lax.dynamic_slice_in_dim(sg[-1], me * (CONV_W // N_DEV), CONV_W // N_DEV, axis=2)
    names = order + ["conv_w"]
    shapes = [args[n].shape for n in names]
    d_p, m_p, v_p = _adam_small(_pack([args[n] for n in names]), _pack([g_small[n] for n in names]),
                                _pack([args["m_" + n] for n in names]), _pack([args["v_" + n] for n in names]))
    res = {"grad": g_small, "delta": dict(zip(names, _unpack(d_p, shapes))),
           "new_m": dict(zip(names, _unpack(m_p, shapes))), "new_v": dict(zip(names, _unpack(v_p, shapes)))}
    for k, n in enumerate(("w_in", "w_out", "w_mlp_up", "w_mlp_down")):
        for kind, val in zip(("grad", "delta", "new_m", "new_v"), big_out[k]):
            res[kind][n] = val

    loss = lax.psum(loss_part[0, 0], ("x", "y", "c"))
    weights = ("rel_bias", "norm_mix_g", "w_in", "q_norm_g", "k_norm_g", "sinks", "conv_w", "conv_b", "conv_ln_g",
               "conv_ln_b", "attn_out_g", "conv_out_g", "w_out", "norm_mlp_g", "w_mlp_up", "w_mlp_down")
    return (loss, dx[None], *[res[kind][n] for kind in ("grad", "delta", "new_m", "new_v") for n in weights])
```

```python
import math

import numpy as np
import jax
import jax.numpy as jnp
from jax import lax
from jax.experimental import pallas as pl
from jax.experimental.pallas import tpu as pltpu

F32, BF16 = jnp.float32, jnp.bfloat16
D_MODEL = 1024
DEPTH = 4
HEAD_DIM = 64
N_HEADS = 8
N_KV = 2
GQA = N_HEADS // N_KV
ATTN_W = N_HEADS * HEAD_DIM
KV_W = N_KV * HEAD_DIM
CONV_W = D_MODEL - ATTN_W
IN_W = ATTN_W + 2 * KV_W + 2 * CONV_W
BLOCK = 128
CONV_K = 31
HALO = 32
N_BUCKETS = 32
MAX_DIST = 128
D_FF = 4 * D_MODEL
FF_CHUNK = 512
N_FF = D_FF // FF_CHUNK
EPS = 1e-6
NEG = -1e30
N_DEV = 8
ADAM_LR, ADAM_B1, ADAM_B2, ADAM_EPS, ADAM_WD, ADAM_STEP = 0.001, 0.9, 0.999, 1e-08, 0.01, 10
VMEM_LIMIT = 56 * 1024 * 1024
MESH = pl.DeviceIdType.MESH

RESIDENT = pl.BlockSpec(memory_space=pltpu.VMEM)
IN_SMEM = pl.BlockSpec(memory_space=pltpu.SMEM)
ANY = pl.BlockSpec(memory_space=pl.ANY)


def _call(body, **kw):
    return pl.pallas_call(body, **kw)


def _params(*sem):
    return pltpu.CompilerParams(dimension_semantics=sem, vmem_limit_bytes=VMEM_LIMIT)


def _dot(a, b):
    return lax.dot_general(a, b, (((1,), (0,)), ((), ())), preferred_element_type=F32)


def _dot_nt(a, b):
    return lax.dot_general(a, b, (((1,), (1,)), ((), ())), preferred_element_type=F32)


def _dot_tn(a, b):
    return lax.dot_general(a, b, (((0,), (0,)), ((), ())), preferred_element_type=F32)


def _sig(x):
    return 1.0 / (1.0 + jnp.exp(-x))


def _rms(x):
    r = lax.rsqrt(jnp.mean(x * x, axis=-1, keepdims=True) + EPS)
    return x * r, r


def _rms_bwd(dy_g, xh, r):
    return r * (dy_g - xh * jnp.mean(dy_g * xh, axis=-1, keepdims=True))


def _rows(tm, w):
    return pl.BlockSpec((tm, w), lambda i: (i, 0))


def _acc_rows(w, rows=1):
    return pl.BlockSpec((rows, w), lambda i: (0, 0))


def _colsum(x):
    return jnp.sum(x, axis=0, keepdims=True)


def _fwd_in(x, g, wt):
    T = x.shape[0]
    tm = 512

    def body(x_ref, g_ref, w_ref, q_ref, kv_ref, u_ref, gt_ref):
        xh, _ = _rms(x_ref[...])
        h = (xh * g_ref[...]).astype(BF16)
        z = _dot_nt(h, w_ref[...])
        q_ref[...] = z[:, :ATTN_W]
        kv_ref[...] = z[:, ATTN_W:ATTN_W + 2 * KV_W]
        u_ref[...] = z[:, ATTN_W + 2 * KV_W:ATTN_W + 2 * KV_W + CONV_W]
        gt_ref[...] = z[:, ATTN_W + 2 * KV_W + CONV_W:]

    widths = (ATTN_W, 2 * KV_W, CONV_W, CONV_W)
    return _call(
        body, name="fwd_in", grid=(T // tm,),
        in_specs=[_rows(tm, D_MODEL), RESIDENT, RESIDENT],
        out_specs=[_rows(tm, w) for w in widths],
        out_shape=[jax.ShapeDtypeStruct((T, w), F32) for w in widths],
        compiler_params=_params("parallel"),
    )(x, g, wt)


ATTN_SUB = 2


def _rms0(x):
    r = lax.rsqrt(jnp.mean(x * x, axis=0, keepdims=True) + EPS)
    return x * r, r


def _rms0_bwd(dy_g, xh, r):
    return r * (dy_g - xh * jnp.mean(dy_g * xh, axis=0, keepdims=True))


def _tri_t():
    kj = lax.broadcasted_iota(jnp.int32, (BLOCK, BLOCK), 0)
    qi = lax.broadcasted_iota(jnp.int32, (BLOCK, BLOCK), 1)
    return jnp.concatenate([kj <= qi] * GQA, axis=1)


def _head_lanes(g):
    lane = lax.broadcasted_iota(jnp.int32, (1, GQA * BLOCK), 1)
    return (lane >= g * BLOCK) & (lane < (g + 1) * BLOCK)


def _heads_side_by_side(xt, kh):
    return jnp.concatenate(
        [xt[(kh * GQA + g) * HEAD_DIM:(kh * GQA + g + 1) * HEAD_DIM, :] for g in range(GQA)], axis=1)


def _kv_block(kv, kh, kg, kg_t):
    kvt = kv.T
    khat, rk = _rms(kv[:, kh * HEAD_DIM:(kh + 1) * HEAD_DIM])
    khat_t, _ = _rms0(kvt[kh * HEAD_DIM:(kh + 1) * HEAD_DIM, :])
    return dict(khat=khat, rk=rk, kn=(khat * kg).astype(BF16), kn_t=(khat_t * kg_t).astype(BF16),
                v=kv[:, KV_W + kh * HEAD_DIM:KV_W + (kh + 1) * HEAD_DIM].astype(BF16),
                v_t=kvt[KV_W + kh * HEAD_DIM:KV_W + (kh + 1) * HEAD_DIM, :].astype(BF16))


def _attn_group(zq_t, kp, kc, kh, bias_ref, sinks_ref, qg_t, tri, has_prev):
    scale = 1.0 / math.sqrt(HEAD_DIM)
    qhat, rq = _rms0(_heads_side_by_side(zq_t, kh))
    qf = qhat * qg_t
    qn = qf.astype(BF16)
    qs = (qf * scale).astype(BF16)
    s = jnp.where(tri, _dot(kc["kn"], qs), _dot(kp["kn"], qs)) + bias_ref[kh]
    if has_prev is not None:
        s = jnp.where(tri | has_prev, s, NEG)
    sink = jnp.zeros((1, GQA * BLOCK), F32)
    for g in range(GQA):
        sink = jnp.where(_head_lanes(g), sinks_ref[kh * GQA + g], sink)
    m = jnp.maximum(jnp.max(s, axis=0, keepdims=True), sink)
    p = jnp.exp(s - m)
    es = jnp.exp(sink - m)
    inv = 1.0 / (jnp.sum(p, axis=0, keepdims=True) + es)
    return dict(qhat=qhat, rq=rq, qn=qn, pn=p * inv, psink=es * inv)


def _split(pb, tri):
    zero = jnp.zeros_like(pb)
    return jnp.where(tri, pb, zero), jnp.where(tri, zero, pb)


def _attn_fwd(zq, zkv, bias, sinks, qg, kg, og):
    T = zq.shape[0]
    rows = ATTN_SUB * BLOCK

    def body(q_ref, kvc_ref, kvp_ref, bias_ref, sinks_ref, qgt_ref, kg_ref, kgt_ref, og_ref, a_ref, mix_ref):
        n = pl.program_id(0)
        tri = _tri_t()
        qgt, kgv, kgt = qgt_ref[...], kg_ref[...], kgt_ref[...]
        kvs = [kvp_ref[...]] + [kvc_ref[i * BLOCK:(i + 1) * BLOCK, :] for i in range(ATTN_SUB)]
        keys = [[_kv_block(kv, kh, kgv, kgt) for kh in range(N_KV)] for kv in kvs]
        for i in range(ATTN_SUB):
            zq_t = q_ref[i * BLOCK:(i + 1) * BLOCK, :].T
            has_prev = (n > 0) if i == 0 else None
            outs = []
            for kh in range(N_KV):
                kp, kc = keys[i][kh], keys[i + 1][kh]
                c = _attn_group(zq_t, kp, kc, kh, bias_ref, sinks_ref, qgt, tri, has_prev)
                p_c, p_p = _split(c["pn"].astype(BF16), tri)
                o_t = _dot(kc["v_t"], p_c) + _dot(kp["v_t"], p_p)
                outs += [o_t[:, g * BLOCK:(g + 1) * BLOCK] for g in range(GQA)]
            a = jnp.concatenate(outs, axis=0).T
            a_ref[i * BLOCK:(i + 1) * BLOCK, :] = a
            ah, _ = _rms(a)
            mix_ref[i * BLOCK:(i + 1) * BLOCK, :] = (ah * og_ref[...]).astype(BF16)

    return _call(
        body, name="attn_fwd", grid=(T // rows,),
        in_specs=[_rows(rows, ATTN_W), _rows(rows, 2 * KV_W),
                  pl.BlockSpec((BLOCK, 2 * KV_W), lambda n: (jnp.maximum(n * ATTN_SUB - 1, 0), 0)),
                  RESIDENT, IN_SMEM, RESIDENT, RESIDENT, RESIDENT, RESIDENT],
        out_specs=[_rows(rows, ATTN_W), _rows(rows, ATTN_W)],
        out_shape=[jax.ShapeDtypeStruct((T, ATTN_W), F32), jax.ShapeDtypeStruct((T, ATTN_W), BF16)],
        compiler_params=_params("parallel"),
    )(zq, zkv, zkv, bias, sinks, qg.reshape(HEAD_DIM, 1), kg, kg.reshape(HEAD_DIM, 1), og)


def _conv_post(y, lg, lb, og):
    mu = jnp.mean(y, axis=-1, keepdims=True)
    yc = y - mu
    rstd = lax.rsqrt(jnp.mean(yc * yc, axis=-1, keepdims=True) + EPS)
    yn = yc * rstd
    ln = yn * lg + lb
    sg = _sig(ln)
    c = ln * sg
    ch, r = _rms(c)
    return yn, rstd, ln, sg, ch, r


CONV_CHUNK = 64


SLAB_ROWS = CONV_CHUNK + 8 * ((CONV_K - 1) // 8)


def _shifted_taps(buf, slab, r0, base):
    for b in range(8):
        taps = range(b, CONV_K, 8)
        span = CONV_CHUNK + 8 * (len(taps) - 1)
        slab[0:span, :] = buf[r0 + base + b:r0 + base + b + span, :]
        for a, j in enumerate(taps):
            yield j, slab[8 * a:8 * a + CONV_CHUNK, :]


def _fold8(x):
    return jnp.sum(x.reshape(x.shape[0] // 8, 8, x.shape[1]), axis=0)


def _conv_fwd(zu, zg, cw, cb, lg, lb, og):
    T = zu.shape[0]
    tt = 512
    halo_spec = pl.BlockSpec((HALO, CONV_W), lambda i: (jnp.maximum(i * (tt // HALO) - 1, 0), 0))

    def body(u_ref, g_ref, uh_ref, gh_ref, cw_ref, cb_ref, lg_ref, lb_ref, og_ref, y_ref, mix_ref, buf, slab):
        i = pl.program_id(0)
        hal = uh_ref[...] * _sig(gh_ref[...])
        buf[0:HALO, :] = jnp.where(i > 0, hal, 0.0)
        buf[HALO:HALO + tt, :] = u_ref[...] * _sig(g_ref[...])
        cbv, lgv, lbv, ogv = cb_ref[...], lg_ref[...], lb_ref[...], og_ref[...]
        for r0 in range(0, tt, CONV_CHUNK):
            rs = slice(r0, r0 + CONV_CHUNK)
            acc = jnp.zeros((CONV_CHUNK, CONV_W), F32)
            for j, win in _shifted_taps(buf, slab, r0, HALO - (CONV_K - 1)):
                acc = acc + cw_ref[j:j + 1, :] * win
            y = acc + cbv
            y_ref[rs, :] = y
            mix_ref[rs, :] = (_conv_post(y, lgv, lbv, ogv)[4] * ogv).astype(BF16)

    return _call(
        body, name="conv_fwd", grid=(T // tt,),
        in_specs=[_rows(tt, CONV_W), _rows(tt, CONV_W), halo_spec, halo_spec] + [RESIDENT] * 5,
        out_specs=[_rows(tt, CONV_W), _rows(tt, CONV_W)],
        out_shape=[jax.ShapeDtypeStruct((T, CONV_W), F32), jax.ShapeDtypeStruct((T, CONV_W), BF16)],
        scratch_shapes=[pltpu.VMEM((HALO + tt, CONV_W), F32), pltpu.VMEM((SLAB_ROWS, CONV_W), F32)],
        compiler_params=_params("parallel"),
    )(zu, zg, zu, zg, cw, cb, lg, lb, og)


def _fwd_out(x, mix_a, mix_c, w_out):
    T = x.shape[0]
    tm = 512

    def body(x_ref, a_ref, c_ref, w_ref, o_ref):
        o_ref[...] = (x_ref[...] + _dot(a_ref[...], w_ref[0:ATTN_W, :])
                      + _dot(c_ref[...], w_ref[ATTN_W:, :]))

    return _call(
        body, name="fwd_out", grid=(T // tm,),
        in_specs=[_rows(tm, D_MODEL), _rows(tm, ATTN_W), _rows(tm, CONV_W), RESIDENT],
        out_specs=_rows(tm, D_MODEL),
        out_shape=jax.ShapeDtypeStruct((T, D_MODEL), F32),
        compiler_params=_params("parallel"),
    )(x, mix_a, mix_c, w_out)


def _chunked(tm):
    return pl.BlockSpec((N_FF, tm, FF_CHUNK), lambda i: (0, i, 0))


def _mlp_fwd(x, g, wup_t, wdown):
    T = x.shape[0]
    tm = 512

    def body(x_ref, g_ref, wu_ref, wd_ref, o_ref, up_ref):
        xv = x_ref[...]
        xh, _ = _rms(xv)
        h = (xh * g_ref[...]).astype(BF16)
        acc = xv
        for c in range(N_FF):
            rows = slice(c * FF_CHUNK, (c + 1) * FF_CHUNK)
            up = _dot_nt(h, wu_ref[rows, :])
            up_ref[c] = up.astype(BF16)
            act = jnp.square(jnp.maximum(up, 0.0))
            acc = acc + _dot(act.astype(BF16), wd_ref[rows, :])
        o_ref[...] = acc

    return _call(
        body, name="mlp_fwd", grid=(T // tm,),
        in_specs=[_rows(tm, D_MODEL), RESIDENT, RESIDENT, RESIDENT],
        out_specs=[_rows(tm, D_MODEL), _chunked(tm)],
        out_shape=[jax.ShapeDtypeStruct((T, D_MODEL), F32), jax.ShapeDtypeStruct((N_FF, T, FF_CHUNK), BF16)],
        compiler_params=_params("parallel"),
    )(x, g, wup_t, wdown)


def _loss_head(y, target):
    T = y.shape[0]
    tm = 512

    def body(y_ref, t_ref, l_ref, d_ref):
        @pl.when(pl.program_id(0) == 0)
        def _():
            l_ref[...] = jnp.zeros_like(l_ref)
        e = y_ref[...] - t_ref[...]
        d_ref[...] = e / D_MODEL
        l_ref[...] += 0.5 * jnp.sum(jnp.mean(e * e, axis=-1, keepdims=True))

    return _call(
        body, name="loss_head", grid=(T // tm,),
        in_specs=[_rows(tm, D_MODEL), _rows(tm, D_MODEL)],
        out_specs=[_acc_rows(128, 8), _rows(tm, D_MODEL)],
        out_shape=[jax.ShapeDtypeStruct((8, 128), F32), jax.ShapeDtypeStruct((T, D_MODEL), F32)],
        compiler_params=_params("arbitrary"),
    )(y, target)


def _bias_table(rel_bias, bucket):
    def body(rb_ref, bk_ref, o_ref):
        bk = bk_ref[...]
        for h in range(N_HEADS):
            acc = jnp.zeros((BLOCK, BLOCK), F32)
            for b in range(N_BUCKETS):
                acc = jnp.where(bk == b, rb_ref[b, h], acc)
            o_ref[h // GQA, :, (h % GQA) * BLOCK:(h % GQA + 1) * BLOCK] = acc

    return _call(
        body, name="bias_table", in_specs=[IN_SMEM, RESIDENT], out_specs=RESIDENT,
        out_shape=jax.ShapeDtypeStruct((N_KV, BLOCK, GQA * BLOCK), F32),
    )(rel_bias, bucket)


def _mlp_bwd(dx2, x1, up, g, wup_t, wdown):
    T = x1.shape[0]
    tm = 512

    def body(d_ref, x_ref, up_ref, g_ref, wu_ref, wd_ref, dx_ref, dup_ref, h_ref, dg_ref, db_ref):
        @pl.when(pl.program_id(0) == 0)
        def _():
            dg_ref[...] = jnp.zeros_like(dg_ref)
        d2 = d_ref[...]
        d2b = d2.astype(BF16)
        db_ref[...] = d2b
        xh, r = _rms(x_ref[...])
        gv = g_ref[...]
        h_ref[...] = (xh * gv).astype(BF16)
        dh = jnp.zeros((tm, D_MODEL), F32)
        for c in range(N_FF):
            rows = slice(c * FF_CHUNK, (c + 1) * FF_CHUNK)
            dact = _dot_nt(d2b, wd_ref[rows, :])
            dup = (dact * (2.0 * jnp.maximum(up_ref[c].astype(F32), 0.0))).astype(BF16)
            dup_ref[c] = dup
            dh = dh + _dot(dup, wu_ref[rows, :])
        dg_ref[...] += _colsum(dh * xh)
        dx_ref[...] = d2 + _rms_bwd(dh * gv, xh, r)

    return _call(
        body, name="mlp_bwd", grid=(T // tm,),
        in_specs=[_rows(tm, D_MODEL), _rows(tm, D_MODEL), _chunked(tm), RESIDENT, RESIDENT, RESIDENT],
        out_specs=[_rows(tm, D_MODEL), _chunked(tm), _rows(tm, D_MODEL), _acc_rows(D_MODEL), _rows(tm, D_MODEL)],
        out_shape=[jax.ShapeDtypeStruct((T, D_MODEL), F32), jax.ShapeDtypeStruct((N_FF, T, FF_CHUNK), BF16),
                   jax.ShapeDtypeStruct((T, D_MODEL), BF16), jax.ShapeDtypeStruct((1, D_MODEL), F32),
                   jax.ShapeDtypeStruct((T, D_MODEL), BF16)],
        compiler_params=_params("arbitrary"),
    )(dx2, x1, up, g, wup_t, wdown)


def _wgrad(a, b, name, chunked=False, square_relu=False, tr=FF_CHUNK):
    if chunked:
        nr, T, tr = a.shape
    else:
        T, R = a.shape
        tr = min(R, tr)
        nr = R // tr
    tk = min(T, 4096)
    nk = T // tk

    def body(a_ref, b_ref, o_ref, acc):
        k = pl.program_id(1)

        @pl.when(k == 0)
        def _():
            acc[...] = jnp.zeros_like(acc)
        av = a_ref[...]
        if square_relu:
            av = jnp.square(jnp.maximum(av.astype(F32), 0.0))
        acc[...] += _dot_tn(av.astype(BF16), b_ref[...].astype(BF16))

        @pl.when(k == nk - 1)
        def _():
            o_ref[...] = acc[...].astype(BF16)

    a_spec = (pl.BlockSpec((None, tk, tr), lambda r, k: (r, k, 0)) if chunked
              else pl.BlockSpec((tk, tr), lambda r, k: (k, r)))
    return _call(
        body, name=name, grid=(nr, nk),
        in_specs=[a_spec, pl.BlockSpec((tk, D_MODEL), lambda r, k: (k, 0))],
        out_specs=pl.BlockSpec((tr, D_MODEL), lambda r, k: (r, 0)),
        out_shape=jax.ShapeDtypeStruct((nr * tr, D_MODEL), BF16),
        scratch_shapes=[pltpu.VMEM((tr, D_MODEL), F32)],
        compiler_params=_params("parallel", "arbitrary"),
    )(a, b)


def _bwd_out(dx1, w_out):
    T = dx1.shape[0]
    tm = 512

    def body(d_ref, w_ref, da_ref, dc_ref, db_ref):
        db = d_ref[...].astype(BF16)
        db_ref[...] = db
        dm = _dot_nt(db, w_ref[...])
        da_ref[...] = dm[:, :ATTN_W]
        dc_ref[...] = dm[:, ATTN_W:]

    return _call(
        body, name="bwd_out", grid=(T // tm,),
        in_specs=[_rows(tm, D_MODEL), RESIDENT],
        out_specs=[_rows(tm, ATTN_W), _rows(tm, CONV_W), _rows(tm, D_MODEL)],
        out_shape=[jax.ShapeDtypeStruct((T, ATTN_W), F32), jax.ShapeDtypeStruct((T, CONV_W), F32),
                   jax.ShapeDtypeStruct((T, D_MODEL), BF16)],
        compiler_params=_params("parallel"),
    )(dx1, w_out)


def _conv_bwd(dmix, y, zu, zg, cw, lg, lb, og):
    T = y.shape[0]
    tt = 512
    nt = T // tt
    per = tt // HALO
    prev_spec = pl.BlockSpec((HALO, CONV_W), lambda i: (jnp.maximum(i * per - 1, 0), 0))
    next_spec = pl.BlockSpec((HALO, CONV_W), lambda i: (jnp.minimum((i + 1) * per, nt * per - 1), 0))

    def body(dm_ref, dmn_ref, y_ref, yn_ref, u_ref, g_ref, uh_ref, gh_ref, cw_ref, lg_ref, lb_ref, og_ref,
             du_ref, dg_ref, pg_ref, hbuf, dybuf, dwacc, pacc, slab):
        i = pl.program_id(0)

        @pl.when(i == 0)
        def _():
            dwacc[...] = jnp.zeros_like(dwacc)
            pacc[...] = jnp.zeros_like(pacc)
        lgv, lbv, ogv = lg_ref[...], lb_ref[...], og_ref[...]

        def chain(yv, dm):
            yn, rstd, ln, sg, ch, r = _conv_post(yv, lgv, lbv, ogv)
            dc = _rms_bwd(dm * ogv, ch, r)
            dln = dc * sg * (1.0 + ln * (1.0 - sg))
            dyn = dln * lgv
            dy = rstd * (dyn - jnp.mean(dyn, axis=-1, keepdims=True)
                         - yn * jnp.mean(dyn * yn, axis=-1, keepdims=True))
            return dy, dm * ch, dln * yn, dln

        hbuf[0:HALO, :] = jnp.where(i > 0, uh_ref[...] * _sig(gh_ref[...]), 0.0)
        for r0 in range(0, tt, CONV_CHUNK):
            rs = slice(r0, r0 + CONV_CHUNK)
            hbuf[HALO + r0:HALO + r0 + CONV_CHUNK, :] = u_ref[rs, :] * _sig(g_ref[rs, :])
            dy, p_og, p_lg, p_lb = chain(y_ref[rs, :], dm_ref[rs, :])
            dybuf[rs, :] = dy
            for k, part in enumerate((dy, p_lg, p_lb, p_og)):
                pacc[8 * k:8 * k + 8, :] += _fold8(part)
        dyh, _, _, _ = chain(yn_ref[...], dmn_ref[...])
        dybuf[tt:tt + HALO, :] = jnp.where(i < nt - 1, dyh, 0.0)
        for r0 in range(0, tt, CONV_CHUNK):
            rs = slice(r0, r0 + CONV_CHUNK)
            dy = dybuf[rs, :]
            dh = jnp.zeros((CONV_CHUNK, CONV_W), F32)
            for j, win in _shifted_taps(dybuf, slab, r0, 0):
                dh = dh + cw_ref[CONV_K - 1 - j:CONV_K - j, :] * win
            for j, win in _shifted_taps(hbuf, slab, r0, HALO - (CONV_K - 1)):
                dwacc[8 * j:8 * j + 8, :] += _fold8(dy * win)
            sgt = _sig(g_ref[rs, :])
            du_ref[rs, :] = (dh * sgt).astype(BF16)
            dg_ref[rs, :] = (dh * u_ref[rs, :] * sgt * (1.0 - sgt)).astype(BF16)

        @pl.when(i == nt - 1)
        def _():
            pg_ref[...] = jnp.zeros_like(pg_ref)
            for j in range(CONV_K):
                pg_ref[j:j + 1, :] = _colsum(dwacc[8 * j:8 * j + 8, :])
            for k in range(4):
                pg_ref[32 + k:33 + k, :] = _colsum(pacc[8 * k:8 * k + 8, :])

    return _call(
        body, name="conv_bwd", grid=(nt,),
        in_specs=[_rows(tt, CONV_W), next_spec, _rows(tt, CONV_W), next_spec, _rows(tt, CONV_W), _rows(tt, CONV_W),
                  prev_spec, prev_spec] + [RESIDENT] * 4,
        out_specs=[_rows(tt, CONV_W), _rows(tt, CONV_W), _acc_rows(CONV_W, 40)],
        out_shape=[jax.ShapeDtypeStruct((T, CONV_W), BF16), jax.ShapeDtypeStruct((T, CONV_W), BF16),
                   jax.ShapeDtypeStruct((40, CONV_W), F32)],
        scratch_shapes=[pltpu.VMEM((HALO + tt, CONV_W), F32), pltpu.VMEM((tt + HALO, CONV_W), F32),
                        pltpu.VMEM((8 * HALO, CONV_W), F32), pltpu.VMEM((32, CONV_W), F32),
                        pltpu.VMEM((SLAB_ROWS, CONV_W), F32)],
        compiler_params=_params("arbitrary"),
    )(dmix, dmix, y, y, zu, zg, zu, zg, cw, lg, lb, og)


def _attn_bwd(dmix, a, zq, zkv, bias, sinks, qg, kg, og):
    T = zq.shape[0]
    rows = ATTN_SUB * BLOCK
    ns = T // rows
    nb = T // BLOCK
    cur = lambda w: pl.BlockSpec((rows, w), lambda n: (jnp.minimum(n, ns - 1), 0))
    scale = 1.0 / math.sqrt(HEAD_DIM)
    done = rows - BLOCK

    def body(dm_ref, a_ref, q_ref, kvc_ref, kvp_ref, bias_ref, sinks_ref, qgt_ref, kg_ref, kgt_ref, og_ref,
             dq_ref, dkv_ref, db_ref, dog_ref, dqg_ref, dkg_ref, dsk_ref, carry):
        n = pl.program_id(0)

        @pl.when(n == 0)
        def _():
            for ref in (db_ref, dog_ref, dqg_ref, dkg_ref, dsk_ref, carry):
                ref[...] = jnp.zeros_like(ref)

        @pl.when(n < ns)
        def _():
            tri = _tri_t()
            ogv, qgt, kgv, kgt = og_ref[...], qgt_ref[...], kg_ref[...], kgt_ref[...]
            lane = lax.broadcasted_iota(jnp.int32, (1, 128), 1)
            kvs = [kvp_ref[...]] + [kvc_ref[i * BLOCK:(i + 1) * BLOCK, :] for i in range(ATTN_SUB)]
            keys = [[_kv_block(kv, kh, kgv, kgt) for kh in range(N_KV)] for kv in kvs]
            dkn = [[jnp.zeros((BLOCK, HEAD_DIM), F32)] * N_KV for _ in kvs]
            dv = [[jnp.zeros((BLOCK, HEAD_DIM), F32)] * N_KV for _ in kvs]
            dsk = jnp.zeros((1, 128), F32)
            for i in range(ATTN_SUB):
                blk = slice(i * BLOCK, (i + 1) * BLOCK)
                ah, ra = _rms(a_ref[blk, :])
                dm = dm_ref[blk, :]
                dog_ref[...] += _colsum(dm * ah)
                da_t = _rms_bwd(dm * ogv, ah, ra).T
                zq_t = q_ref[blk, :].T
                has_prev = (n > 0) if i == 0 else None
                dqs = []
                for kh in range(N_KV):
                    kp, kc = keys[i][kh], keys[i + 1][kh]
                    c = _attn_group(zq_t, kp, kc, kh, bias_ref, sinks_ref, qgt, tri, has_prev)
                    dob = _heads_side_by_side(da_t, kh).astype(BF16)
                    pn = c["pn"]
                    p_c, p_p = _split(pn.astype(BF16), tri)
                    dv[i + 1][kh] = dv[i + 1][kh] + _dot_nt(p_c, dob)
                    dv[i][kh] = dv[i][kh] + _dot_nt(p_p, dob)
                    dp = jnp.where(tri, _dot(kc["v"], dob), _dot(kp["v"], dob))
                    dl = jnp.sum(pn * dp, axis=0, keepdims=True)
                    ds = pn * (dp - dl)
                    dsr = -c["psink"] * dl
                    for g in range(GQA):
                        dsk = dsk + jnp.where(lane == kh * GQA + g, jnp.sum(jnp.where(_head_lanes(g), dsr, 0.0)), 0.0)
                    db_ref[kh] += ds
                    ds_c, ds_p = _split((ds * scale).astype(BF16), tri)
                    dqn = _dot(kc["kn_t"], ds_c) + _dot(kp["kn_t"], ds_p)
                    dkn[i + 1][kh] = dkn[i + 1][kh] + _dot_nt(ds_c, c["qn"])
                    dkn[i][kh] = dkn[i][kh] + _dot_nt(ds_p, c["qn"])
                    dqg_ref[...] += jnp.sum(dqn * c["qhat"], axis=1, keepdims=True)
                    dq_t = _rms0_bwd(dqn * qgt, c["qhat"], c["rq"])
                    dqs += [dq_t[:, g * BLOCK:(g + 1) * BLOCK] for g in range(GQA)]
                dq_ref[blk, :] = jnp.concatenate(dqs, axis=0).T.astype(BF16)
            dsk_ref[...] += dsk
            dkv = []
            for j in range(ATTN_SUB + 1):
                dk = []
                for kh in range(N_KV):
                    key = keys[j][kh]
                    dkg_ref[...] += _colsum(dkn[j][kh] * key["khat"])
                    dk.append(_rms_bwd(dkn[j][kh] * kgv, key["khat"], key["rk"]))
                dkv.append(jnp.concatenate(dk + dv[j], axis=-1))
            if done:
                dkv_ref[0:done, :] = carry[0:done, :].astype(BF16)
            dkv_ref[done:rows, :] = (carry[done:rows, :] + dkv[0]).astype(BF16)
            for j in range(1, ATTN_SUB + 1):
                carry[(j - 1) * BLOCK:j * BLOCK, :] = dkv[j]

        @pl.when(n == ns)
        def _():
            dkv_ref[...] = carry[...].astype(BF16)

    small = lambda w: pl.BlockSpec((1, w), lambda n: (0, 0))
    return _call(
        body, name="attn_bwd", grid=(ns + 1,),
        in_specs=[cur(ATTN_W), cur(ATTN_W), cur(ATTN_W), cur(2 * KV_W),
                  pl.BlockSpec((BLOCK, 2 * KV_W), lambda n: (jnp.clip(n * ATTN_SUB - 1, 0, nb - 1), 0)),
                  RESIDENT, IN_SMEM, RESIDENT, RESIDENT, RESIDENT, RESIDENT],
        out_specs=[cur(ATTN_W), pl.BlockSpec((rows, 2 * KV_W), lambda n: (jnp.maximum(n - 1, 0), 0)),
                   pl.BlockSpec((N_KV, BLOCK, GQA * BLOCK), lambda n: (0, 0, 0)),
                   small(ATTN_W), pl.BlockSpec((HEAD_DIM, 1), lambda n: (0, 0)), small(HEAD_DIM), small(128)],
        out_shape=[jax.ShapeDtypeStruct((T, ATTN_W), BF16), jax.ShapeDtypeStruct((T, 2 * KV_W), BF16),
                   jax.ShapeDtypeStruct((N_KV, BLOCK, GQA * BLOCK), F32),
                   jax.ShapeDtypeStruct((1, ATTN_W), F32), jax.ShapeDtypeStruct((HEAD_DIM, 1), F32),
                   jax.ShapeDtypeStruct((1, HEAD_DIM), F32), jax.ShapeDtypeStruct((1, 128), F32)],
        scratch_shapes=[pltpu.VMEM((rows, 2 * KV_W), F32)],
        compiler_params=_params("arbitrary"),
    )(dmix, a, zq, zkv, zkv, bias, sinks, qg.reshape(HEAD_DIM, 1), kg, kg.reshape(HEAD_DIM, 1), og)


def _bwd_in(dx1, x, dq, dkv, du, dgt, g, wt):
    T = x.shape[0]
    tm = 512

    def body(d1_ref, x_ref, dq_ref, dkv_ref, du_ref, dgt_ref, g_ref, w_ref, dx_ref, h_ref, dg_ref, dz_ref):
        @pl.when(pl.program_id(0) == 0)
        def _():
            dg_ref[...] = jnp.zeros_like(dg_ref)
        xh, r = _rms(x_ref[...])
        gv = g_ref[...]
        h_ref[...] = (xh * gv).astype(BF16)
        dz = jnp.concatenate([dq_ref[...], dkv_ref[...], du_ref[...], dgt_ref[...]], axis=-1)
        dz_ref[...] = dz
        dh = _dot(dz, w_ref[...])
        dg_ref[...] += _colsum(dh * xh)
        dx_ref[...] = d1_ref[...] + _rms_bwd(dh * gv, xh, r)

    return _call(
        body, name="bwd_in", grid=(T // tm,),
        in_specs=[_rows(tm, D_MODEL), _rows(tm, D_MODEL), _rows(tm, ATTN_W), _rows(tm, 2 * KV_W),
                  _rows(tm, CONV_W), _rows(tm, CONV_W), RESIDENT, RESIDENT],
        out_specs=[_rows(tm, D_MODEL), _rows(tm, D_MODEL), _acc_rows(D_MODEL), _rows(tm, IN_W)],
        out_shape=[jax.ShapeDtypeStruct((T, D_MODEL), F32), jax.ShapeDtypeStruct((T, D_MODEL), BF16),
                   jax.ShapeDtypeStruct((1, D_MODEL), F32), jax.ShapeDtypeStruct((T, IN_W), BF16)],
        compiler_params=_params("arbitrary"),
    )(dx1, x, dq, dkv, du, dgt, g, wt)


def _bias_grad(db, bucket):
    def body(db_ref, bk_ref, o_ref):
        bk = bk_ref[...]
        lane = lax.broadcasted_iota(jnp.int32, (1, 128), 1)
        for h in range(N_HEADS):
            cols = slice((h % GQA) * BLOCK, (h % GQA + 1) * BLOCK)
            tot = db_ref[0, h // GQA, :, cols]
            for l in range(1, DEPTH):
                tot = tot + db_ref[l, h // GQA, :, cols]
            out = jnp.zeros((1, 128), F32)
            for b in range(N_BUCKETS):
                out = jnp.where(lane == b, jnp.sum(jnp.where(bk == b, tot, 0.0)), out)
            o_ref[h:h + 1, :] = out

    return _call(
        body, name="bias_grad", in_specs=[RESIDENT, RESIDENT], out_specs=RESIDENT,
        out_shape=jax.ShapeDtypeStruct((N_HEADS, 128), F32),
        compiler_params=pltpu.CompilerParams(vmem_limit_bytes=VMEM_LIMIT),
    )(db, bucket)


def _place():
    return lax.axis_index("x"), lax.axis_index("y"), lax.axis_index("c")


def _all_gather(shards):
    na = len(shards)

    def body(*refs):
        ins, outs = refs[:na], refs[na:2 * na]
        send_sems, recv_sems, local_sems = refs[2 * na:]
        x, y, c = _place()
        me, sibling = (x, y, c), (x, y, 1 - c)
        chips = [(1 - x, y), (x, 1 - y), (1 - x, 1 - y)]

        def rows(a, p):
            r = ins[a].shape[1]
            return outs[a].at[:, pl.ds((4 * p[0] + 2 * p[1] + p[2]) * r, r), :]

        def copy(a, k, block, to, src=None):
            return pltpu.make_async_remote_copy(
                src_ref=rows(a, block) if src is None else src, dst_ref=rows(a, block),
                send_sem=send_sems.at[a, k], recv_sem=recv_sems.at[a, k], device_id=to, device_id_type=MESH)

        mine = [pltpu.make_async_copy(ins[a], rows(a, me), local_sems.at[a]) for a in range(na)]
        for cp in mine:
            cp.start()
        first = []
        for a in range(na):
            first.append(copy(a, 0, me, sibling, src=ins[a]))
            first += [copy(a, 1 + j, me, (*chip, c), src=ins[a]) for j, chip in enumerate(chips)]
        for cp in first:
            cp.start()
        passed = []
        for j, chip in enumerate(chips):
            for a in range(na):
                copy(a, 1 + j, (*chip, c), me).wait_recv()
                cp = copy(a, 4 + j, (*chip, c), sibling)
                cp.start()
                passed.append(cp)
        for a in range(na):
            copy(a, 0, sibling, me).wait_recv()
            for j, chip in enumerate(chips):
                copy(a, 4 + j, (*chip, 1 - c), me).wait_recv()
        for cp in first + passed:
            cp.wait_send()
        for cp in mine:
            cp.wait()

    return _call(
        body, name="all_gather",
        in_specs=[ANY] * na, out_specs=[ANY] * na,
        out_shape=[jax.ShapeDtypeStruct((s.shape[0], N_DEV * s.shape[1], s.shape[2]), s.dtype) for s in shards],
        scratch_shapes=[pltpu.SemaphoreType.DMA((na, 7)), pltpu.SemaphoreType.DMA((na, 7)),
                        pltpu.SemaphoreType.DMA((na,))],
    )(*shards)


IN_HBM = pl.BlockSpec(memory_space=pltpu.HBM)
IN_SEM = pl.BlockSpec(memory_space=pltpu.SEMAPHORE)
DATAFLOW = pltpu.SideEffectType.DATAFLOW_SIDE_EFFECTING


def _exchange_copies(srcs, lands, send_sems, recv_sems, scatter):
    x, y, c = _place()
    me = 4 * x + 2 * y + c
    copies = []
    for k in range(1, N_DEV):
        p = (x ^ (k >> 2), y ^ ((k >> 1) & 1), c ^ (k & 1))
        for a, (src, land) in enumerate(zip(srcs, lands)):
            r = land.shape[0] // N_DEV
            if scatter:
                src = src.at[pl.ds((4 * p[0] + 2 * p[1] + p[2]) * r, r), :]
            copies.append(pltpu.make_async_remote_copy(
                src_ref=src, dst_ref=land.at[pl.ds(me * r, r), :], send_sem=send_sems.at[a * (N_DEV - 1) + k - 1],
                recv_sem=recv_sems.at[a * (N_DEV - 1) + k - 1], device_id=p, device_id_type=MESH))
    return copies


def _exchange_start(srcs, name, scatter):
    na = len(srcs)
    lands = [lax.empty((s.shape[0] * (1 if scatter else N_DEV), s.shape[1]), s.dtype) for s in srcs]

    def body(*refs):
        ins, lnd = refs[:na], refs[na:2 * na]
        send_sems, recv_sems = refs[2 * na], refs[2 * na + 1]
        token = refs[-1]
        for cp in _exchange_copies(ins, lnd, send_sems, recv_sems, scatter):
            cp.start()
        token[...] = jnp.zeros_like(token)

    hbm = lambda a: pltpu.with_memory_space_constraint(a, pltpu.HBM)
    out = _call(
        body, name=name,
        out_shape=(pltpu.SemaphoreType.DMA((na * (N_DEV - 1),)), pltpu.SemaphoreType.DMA((na * (N_DEV - 1),)),
                   *[pltpu.HBM(a.shape, a.dtype) for a in (*srcs, *lands)], jax.ShapeDtypeStruct((8, 128), F32)),
        in_specs=[IN_HBM] * (2 * na),
        out_specs=(IN_SEM, IN_SEM, *[IN_HBM] * (2 * na), RESIDENT),
        input_output_aliases={i: 2 + i for i in range(2 * na)},
        compiler_params=pltpu.CompilerParams(has_side_effects=DATAFLOW),
    )(*[hbm(a) for a in (*srcs, *lands)])
    return (out[0], out[1], out[2:2 + na], out[2 + na:2 + 2 * na]), out[-1]


def _exchange_wait(state, after, name, scatter):
    send_sems, recv_sems, srcs, lands = state
    na = len(srcs)

    def body(*refs):
        ins, lnd = refs[:na], refs[na:2 * na]
        for cp in _exchange_copies(ins, lnd, refs[2 * na], refs[2 * na + 1], scatter):
            cp.wait_send()
            cp.wait_recv()
        refs[-1][...] = jnp.zeros_like(refs[-1])

    out = _call(
        body, name=name,
        out_shape=(*[pltpu.HBM(a.shape, a.dtype) for a in (*srcs, *lands)], jax.ShapeDtypeStruct((8, 128), F32)),
        in_specs=[IN_HBM] * (2 * na) + [IN_SEM, IN_SEM, ANY],
        out_specs=(*[IN_HBM] * (2 * na), RESIDENT),
        input_output_aliases={i: i for i in range(2 * na)},
        compiler_params=pltpu.CompilerParams(has_side_effects=DATAFLOW),
    )(*srcs, *lands, send_sems, recv_sems, after)
    return out[:na], out[na:2 * na], out[-1]


def _adam_math(w, g, m, v):
    m = ADAM_B1 * m + (1.0 - ADAM_B1) * g
    v = ADAM_B2 * v + (1.0 - ADAM_B2) * jnp.square(g)
    m_hat = m / (1.0 - ADAM_B1 ** ADAM_STEP)
    v_hat = v / (1.0 - ADAM_B2 ** ADAM_STEP)
    delta = -ADAM_LR * (m_hat / (jnp.sqrt(v_hat) + ADAM_EPS) + ADAM_WD * w)
    return delta, m, v


def _adam_big(parts, w, m, v):
    nl, r, cdim = w.shape
    tr = r if r <= 256 else 256
    spec = pl.BlockSpec((None, tr, cdim), lambda l, i: (l, i, 0))

    def body(p_ref, w_ref, m_ref, v_ref, g_ref, d_ref, nm_ref, nv_ref):
        g = p_ref[0].astype(F32)
        for s in range(1, N_DEV):
            g = g + p_ref[s].astype(F32)
        d, nm, nv = _adam_math(w_ref[...], g, m_ref[...], v_ref[...])
        g_ref[...] = g
        d_ref[...] = d
        nm_ref[...] = nm
        nv_ref[...] = nv

    return _call(
        body, name="adam_big", grid=(nl, r // tr),
        in_specs=[pl.BlockSpec((None, N_DEV, tr, cdim), lambda l, i: (l, 0, i, 0)), spec, spec, spec],
        out_specs=[spec] * 4,
        out_shape=[jax.ShapeDtypeStruct(w.shape, F32)] * 4,
        compiler_params=_params("parallel", "parallel"),
    )(parts, w, m, v)


def _sum_slots(parts):
    def body(p_ref, o_ref):
        g = p_ref[0]
        for s in range(1, N_DEV):
            g = g + p_ref[s]
        o_ref[...] = g

    return _call(body, name="sum_slots", in_specs=[RESIDENT], out_specs=RESIDENT,
                 out_shape=jax.ShapeDtypeStruct(parts.shape[1:], F32))(parts)


def _adam_small(w, g, m, v):
    def body(w_ref, g_ref, m_ref, v_ref, d_ref, nm_ref, nv_ref):
        d, nm, nv = _adam_math(w_ref[...], g_ref[...], m_ref[...], v_ref[...])
        d_ref[...] = d
        nm_ref[...] = nm
        nv_ref[...] = nv

    return _call(body, name="adam_small", in_specs=[RESIDENT] * 4, out_specs=[RESIDENT] * 3,
                 out_shape=[jax.ShapeDtypeStruct(w.shape, F32)] * 3)(w, g, m, v)


def _pack(arrays):
    parts = []
    for a in arrays:
        flat = a.reshape(-1)
        n = flat.shape[0]
        padded = -(-n // 1024) * 1024
        parts.append(jnp.pad(flat, (0, padded - n)).reshape(padded // 128, 128))
    return jnp.concatenate(parts, axis=0)


def _unpack(packed, shapes):
    out, row = [], 0
    for shp in shapes:
        n = int(np.prod(shp))
        rows = -(-n // 1024) * 8
        out.append(packed[row:row + rows].reshape(-1)[:n].reshape(shp))
        row += rows
    return out


def _t5_bucket_table():
    kj = np.arange(BLOCK)[:, None]
    qi = np.arange(BLOCK)[None, :]
    n = (qi - kj) % BLOCK
    max_exact = N_BUCKETS // 2
    large = max_exact + (np.log(np.maximum(n, 1) / max_exact) / np.log(MAX_DIST / max_exact)
                         * (N_BUCKETS - max_exact)).astype(np.int32)
    large = np.minimum(large, N_BUCKETS - 1)
    return np.where(n < max_exact, n, large).astype(np.int32)


SMALL_NAMES = ("rel_bias", "norm_mix_g", "q_norm_g", "k_norm_g", "sinks", "conv_b", "conv_ln_g", "conv_ln_b",
               "attn_out_g", "conv_out_g", "norm_mlp_g")


def _local_step(x, target, small, conv_w, get_w, put_g, token):
    bucket = jnp.asarray(_t5_bucket_table())
    bias = _bias_table(small["rel_bias"], bucket)
    row = lambda a, l: a[l][None, :]
    cw_pad = jnp.pad(conv_w, ((0, 0), (0, HALO - CONV_K), (0, 0)))
    saved, weights = [], []
    for l in range(DEPTH):
        wt_in, w_out, tok = get_w(l, 0, x)
        zq, zkv, zu, zg = _fwd_in(x, row(small["norm_mix_g"], l) + tok + (token if l == 0 else 0.0), wt_in)
        a, mix_a = _attn_fwd(zq, zkv, bias, small["sinks"][l], row(small["q_norm_g"], l), row(small["k_norm_g"], l),
                             row(small["attn_out_g"], l))
        y, mix_c = _conv_fwd(zu, zg, cw_pad[l], row(small["conv_b"], l), row(small["conv_ln_g"], l),
                             row(small["conv_ln_b"], l), row(small["conv_out_g"], l))
        x1 = _fwd_out(x, mix_a, mix_c, w_out)
        wt_up, w_down, tok = get_w(l, 1, x1)
        weights.append((wt_in, w_out, wt_up, w_down))
        x2, up = _mlp_fwd(x1, row(small["norm_mlp_g"], l) + tok, wt_up, w_down)
        saved.append((x, zq, zkv, zu, zg, a, mix_a, y, mix_c, x1, up))
        x = x2
    loss_part, dx = _loss_head(x, target)

    gs = {n: [None] * DEPTH for n in SMALL_NAMES if n != "rel_bias"}
    g_conv_w, dbias = [None] * DEPTH, [None] * DEPTH
    token = 0.0
    for l in reversed(range(DEPTH)):
        x0, zq, zkv, zu, zg, a, mix_a, y, mix_c, x1, up = saved[l]
        wt_in, w_out, wt_up, w_down = weights[l]
        dx1, dup, h2, gs["norm_mlp_g"][l], dxb = _mlp_bwd(dx, x1, up, row(small["norm_mlp_g"], l) + token, wt_up, w_down)
        g_up = _wgrad(dup, h2, "wgrad_up", chunked=True)
        g_down = _wgrad(up, dxb, "wgrad_down", chunked=True, square_relu=True)
        token = put_g(l, 1, (g_up, g_down))
        dma, dmc, dx1b = _bwd_out(dx1, w_out)
        g_out = jnp.concatenate([_wgrad(mix_a, dx1b, "wgrad_out_a"), _wgrad(mix_c, dx1b, "wgrad_out_c")], axis=0)
        du, dgt, pg = _conv_bwd(dmc, y, zu, zg, cw_pad[l], row(small["conv_ln_g"], l) + token,
                                row(small["conv_ln_b"], l), row(small["conv_out_g"], l))
        g_conv_w[l] = pg[:CONV_K]
        gs["conv_b"][l], gs["conv_ln_g"][l], gs["conv_ln_b"][l], gs["conv_out_g"][l] = pg[32], pg[33], pg[34], pg[35]
        dq, dkv, dbias[l], dog, dqg, dkg, dsk = _attn_bwd(
            dma, a, zq, zkv, bias, small["sinks"][l], row(small["q_norm_g"], l), row(small["k_norm_g"], l),
            row(small["attn_out_g"], l))
        gs["attn_out_g"][l], gs["q_norm_g"][l], gs["k_norm_g"][l], gs["sinks"][l] = dog, dqg, dkg, dsk[0, :N_HEADS]
        dx, h, gs["norm_mix_g"][l], dz = _bwd_in(dx1, x0, dq, dkv, du, dgt, row(small["norm_mix_g"], l), wt_in)
        g_in = _wgrad(dz, h, "wgrad_in", tr=IN_W // 2)
        token = put_g(l, 0, (g_in, g_out))
    small_grads = {n: jnp.stack([jnp.reshape(v, (-1,)) for v in vals]) for n, vals in gs.items()}
    small_grads["rel_bias"] = _bias_grad(jnp.stack(dbias), bucket)[:, :N_BUCKETS].T
    return loss_part, dx, small_grads, jnp.stack(g_conv_w)


def kernel(x, rel_bias, norm_mix_g, w_in, q_norm_g, k_norm_g, sinks, conv_w, conv_b, conv_ln_g, conv_ln_b, attn_out_g, conv_out_g, w_out, norm_mlp_g, w_mlp_up, w_mlp_down, loss_target, m_rel_bias, m_norm_mix_g, m_w_in, m_q_norm_g, m_k_norm_g, m_sinks, m_conv_w, m_conv_b, m_conv_ln_g, m_conv_ln_b, m_attn_out_g, m_conv_out_g, m_w_out, m_norm_mlp_g, m_w_mlp_up, m_w_mlp_down, v_rel_bias, v_norm_mix_g, v_w_in, v_q_norm_g, v_k_norm_g, v_sinks, v_conv_w, v_conv_b, v_conv_ln_g, v_conv_ln_b, v_attn_out_g, v_conv_out_g, v_w_out, v_norm_mlp_g, v_w_mlp_up, v_w_mlp_down):
    args = dict(locals())
    small = {n: args[n] for n in SMALL_NAMES}
    tr = lambda a: jnp.swapaxes(a, 1, 2)
    me = 4 * lax.axis_index("x") + 2 * lax.axis_index("y") + lax.axis_index("c")

    shards = (tr(w_in).astype(BF16), w_out.astype(BF16), tr(w_mlp_up).astype(BF16), w_mlp_down.astype(BF16))
    cw_sh = jnp.pad(tr(conv_w), ((0, 0), (0, 0), (0, HALO - CONV_K))).reshape(DEPTH * (CONV_W // N_DEV), HALO)
    gathers = {}

    def start_gather(l, half, dep):
        arrays = [s[l] for s in shards[2 * half:2 * half + 2]]
        if (l, half) == (0, 0):
            arrays = [cw_sh] + arrays
        if dep is not None:
            arrays = [a + dep[0, 0].astype(a.dtype) for a in arrays]
        gathers[l, half], tok = _exchange_start(arrays, f"gather_{l}{'ab'[half]}_start", scatter=False)
        return tok

    def own_rows(land, block):
        return lax.dynamic_update_slice(land, block, (me * block.shape[0], 0))

    landed = {}

    def get_w(l, half, after):
        if (l, half) not in landed:
            srcs, lands, tok = _exchange_wait(gathers[l, half], after, f"gather_{l}{'ab'[half]}_wait", scatter=False)
            if l + 1 < DEPTH:
                tok = tok + start_gather(l + 1, half, tok)
            landed[l, half] = (*[own_rows(land, src) for land, src in zip(lands, srcs)], tok[0, 0])
        return landed[l, half]

    token = start_gather(0, 0, None)
    token = token + start_gather(0, 1, token)
    cw_all, *first = get_w(0, 0, token)
    landed[0, 0] = tuple(first)
    conv_w_full = jnp.transpose(cw_all.reshape(N_DEV, DEPTH, CONV_W // N_DEV, HALO), (1, 3, 0, 2))
    conv_w_full = conv_w_full.reshape(DEPTH, HALO, CONV_W)[:, :CONV_K, :]

    scatters = {}

    def put_g(l, half, grads):
        scatters[l, half], tok = _exchange_start(list(grads), f"scatter_{l}{'ab'[half]}_start", scatter=True)
        return tok[0, 0]

    loss_part, dx, small_grads, g_conv_w = _local_step(
        x[0], loss_target[0], small, conv_w_full, get_w, put_g, token[0, 0])

    big_w = (tr(w_in), w_out, tr(w_mlp_up), w_mlp_down)
    big_m = (tr(m_w_in), m_w_out, tr(m_w_mlp_up), m_w_mlp_down)
    big_v = (tr(v_w_in), v_w_out, tr(v_w_mlp_up), v_w_mlp_down)
    big_out, after = [None] * 4, dx
    for half in (1, 0):
        parts = [[], []]
        for l in reversed(range(DEPTH)):
            srcs, lands, _ = _exchange_wait(scatters[l, half], after, f"scatter_{l}{'ab'[half]}_wait", scatter=True)
            for k, (land, src) in enumerate(zip(lands, srcs)):
                r = land.shape[0] // N_DEV
                mine = lax.dynamic_slice(src, (me * r, 0), (r, land.shape[1]))
                parts[k].insert(0, own_rows(land, mine).reshape(N_DEV, r, land.shape[1]))
        for k in range(2):
            big_out[2 * half + k] = _adam_big(jnp.stack(parts[k]), big_w[2 * half + k], big_m[2 * half + k],
                                              big_v[2 * half + k])
        after = big_out[2 * half + 1][0]
    for k in (0, 2):
        big_out[k] = [tr(o) for o in big_out[k]]

    order = [n for n in SMALL_NAMES]
    packed = _pack([small_grads[n] for n in order] + [g_conv_w])
    slots, = _all_gather((packed[None],))
    summed = _sum_slots(slots[0].reshape(N_DEV, packed.shape[0], 128))
    shapes = [small[n].shape for n in order] + [(DEPTH, CONV_K, CONV_W)]
    sg = _unpack(summed, shapes)
    g_small = dict(zip(order, sg[:-1]))
    g_small["conv_w"] = lax.dynamic_slice_in_dim(sg[-1], me * (CONV_W // N_DEV), CONV_W // N_DEV, axis=2)
    names = order + ["conv_w"]
    shapes = [args[n].shape for n in names]
    d_p, m_p, v_p = _adam_small(_pack([args[n] for n in names]), _pack([g_small[n] for n in names]),
                                _pack([args["m_" + n] for n in names]), _pack([args["v_" + n] for n in names]))
    res = {"grad": g_small, "delta": dict(zip(names, _unpack(d_p, shapes))),
           "new_m": dict(zip(names, _unpack(m_p, shapes))), "new_v": dict(zip(names, _unpack(v_p, shapes)))}
    for k, n in enumerate(("w_in", "w_out", "w_mlp_up", "w_mlp_down")):
        for kind, val in zip(("grad", "delta", "new_m", "new_v"), big_out[k]):
            res[kind][n] = val

    loss = lax.psum(loss_part[0, 0], ("x", "y", "c"))
    weights = ("rel_bias", "norm_mix_g", "w_in", "q_norm_g", "k_norm_g", "sinks", "conv_w", "conv_b", "conv_ln_g",
               "conv_ln_b", "attn_out_g", "conv_out_g", "w_out", "norm_mlp_g", "w_mlp_up", "w_mlp_down")
    return (loss, dx[None], *[res[kind][n] for kind in ("grad", "delta", "new_m", "new_v") for n in weights])
```

```python
import math

import numpy as np
import jax
import jax.numpy as jnp
from jax import lax
from jax.experimental import pallas as pl
from jax.experimental.pallas import tpu as pltpu

F32, BF16 = jnp.float32, jnp.bfloat16
D_MODEL = 1024
DEPTH = 4
HEAD_DIM = 64
N_HEADS = 8
N_KV = 2
GQA = N_HEADS // N_KV
ATTN_W = N_HEADS * HEAD_DIM
KV_W = N_KV * HEAD_DIM
CONV_W = D_MODEL - ATTN_W
IN_W = ATTN_W + 2 * KV_W + 2 * CONV_W
BLOCK = 128
CONV_K = 31
HALO = 32
N_BUCKETS = 32
MAX_DIST = 128
D_FF = 4 * D_MODEL
FF_CHUNK = 512
N_FF = D_FF // FF_CHUNK
EPS = 1e-6
NEG = -1e30
N_DEV = 8
ADAM_LR, ADAM_B1, ADAM_B2, ADAM_EPS, ADAM_WD, ADAM_STEP = 0.001, 0.9, 0.999, 1e-08, 0.01, 10
VMEM_LIMIT = 56 * 1024 * 1024
MESH = pl.DeviceIdType.MESH

RESIDENT = pl.BlockSpec(memory_space=pltpu.VMEM)
IN_SMEM = pl.BlockSpec(memory_space=pltpu.SMEM)
ANY = pl.BlockSpec(memory_space=pl.ANY)


def _call(body, **kw):
    return pl.pallas_call(body, **kw)


def _params(*sem):
    return pltpu.CompilerParams(dimension_semantics=sem, vmem_limit_bytes=VMEM_LIMIT)


def _dot(a, b):
    return lax.dot_general(a, b, (((1,), (0,)), ((), ())), preferred_element_type=F32)


def _dot_nt(a, b):
    return lax.dot_general(a, b, (((1,), (1,)), ((), ())), preferred_element_type=F32)


def _dot_tn(a, b):
    return lax.dot_general(a, b, (((0,), (0,)), ((), ())), preferred_element_type=F32)


def _sig(x):
    return 1.0 / (1.0 + jnp.exp(-x))


def _rms(x):
    r = lax.rsqrt(jnp.mean(x * x, axis=-1, keepdims=True) + EPS)
    return x * r, r


def _rms_bwd(dy_g, xh, r):
    return r * (dy_g - xh * jnp.mean(dy_g * xh, axis=-1, keepdims=True))


def _rows(tm, w):
    return pl.BlockSpec((tm, w), lambda i: (i, 0))


def _acc_rows(w, rows=1):
    return pl.BlockSpec((rows, w), lambda i: (0, 0))


def _colsum(x):
    return jnp.sum(x, axis=0, keepdims=True)


def _fwd_in(x, g, wt):
    T = x.shape[0]
    tm = 512

    def body(x_ref, g_ref, w_ref, q_ref, kv_ref, u_ref, gt_ref):
        xh, _ = _rms(x_ref[...])
        h = (xh * g_ref[...]).astype(BF16)
        z = _dot_nt(h, w_ref[...])
        q_ref[...] = z[:, :ATTN_W]
        kv_ref[...] = z[:, ATTN_W:ATTN_W + 2 * KV_W]
        u_ref[...] = z[:, ATTN_W + 2 * KV_W:ATTN_W + 2 * KV_W + CONV_W]
        gt_ref[...] = z[:, ATTN_W + 2 * KV_W + CONV_W:]

    widths = (ATTN_W, 2 * KV_W, CONV_W, CONV_W)
    return _call(
        body, name="fwd_in", grid=(T // tm,),
        in_specs=[_rows(tm, D_MODEL), RESIDENT, RESIDENT],
        out_specs=[_rows(tm, w) for w in widths],
        out_shape=[jax.ShapeDtypeStruct((T, w), F32) for w in widths],
        compiler_params=_params("parallel"),
    )(x, g, wt)


ATTN_SUB = 4


def _rms0(x):
    r = lax.rsqrt(jnp.mean(x * x, axis=0, keepdims=True) + EPS)
    return x * r, r


def _rms0_bwd(dy_g, xh, r):
    return r * (dy_g - xh * jnp.mean(dy_g * xh, axis=0, keepdims=True))


def _tri_t():
    kj = lax.broadcasted_iota(jnp.int32, (BLOCK, BLOCK), 0)
    qi = lax.broadcasted_iota(jnp.int32, (BLOCK, BLOCK), 1)
    return jnp.concatenate([kj <= qi] * GQA, axis=1)


def _head_lanes(g):
    lane = lax.broadcasted_iota(jnp.int32, (1, GQA * BLOCK), 1)
    return (lane >= g * BLOCK) & (lane < (g + 1) * BLOCK)


def _heads_side_by_side(xt, kh):
    return jnp.concatenate(
        [xt[(kh * GQA + g) * HEAD_DIM:(kh * GQA + g + 1) * HEAD_DIM, :] for g in range(GQA)], axis=1)


def _kv_block(kv, kh, kg, kg_t):
    kvt = kv.T
    khat, rk = _rms(kv[:, kh * HEAD_DIM:(kh + 1) * HEAD_DIM])
    khat_t, _ = _rms0(kvt[kh * HEAD_DIM:(kh + 1) * HEAD_DIM, :])
    return dict(khat=khat, rk=rk, kn=(khat * kg).astype(BF16), kn_t=(khat_t * kg_t).astype(BF16),
                v=kv[:, KV_W + kh * HEAD_DIM:KV_W + (kh + 1) * HEAD_DIM].astype(BF16),
                v_t=kvt[KV_W + kh * HEAD_DIM:KV_W + (kh + 1) * HEAD_DIM, :].astype(BF16))


def _attn_group(zq_t, kp, kc, kh, bias_ref, sinks_ref, qg_t, tri, has_prev):
    scale = 1.0 / math.sqrt(HEAD_DIM)
    qhat, rq = _rms0(_heads_side_by_side(zq_t, kh))
    qf = qhat * qg_t
    qn = qf.astype(BF16)
    qs = (qf * scale).astype(BF16)
    s = jnp.where(tri, _dot(kc["kn"], qs), _dot(kp["kn"], qs)) + bias_ref[kh]
    if has_prev is not None:
        s = jnp.where(tri | has_prev, s, NEG)
    sink = jnp.zeros((1, GQA * BLOCK), F32)
    for g in range(GQA):
        sink = jnp.where(_head_lanes(g), sinks_ref[kh * GQA + g], sink)
    m = jnp.maximum(jnp.max(s, axis=0, keepdims=True), sink)
    p = jnp.exp(s - m)
    es = jnp.exp(sink - m)
    inv = 1.0 / (jnp.sum(p, axis=0, keepdims=True) + es)
    return dict(qhat=qhat, rq=rq, qn=qn, pn=p * inv, psink=es * inv)


def _split(pb, tri):
    zero = jnp.zeros_like(pb)
    return jnp.where(tri, pb, zero), jnp.where(tri, zero, pb)


def _attn_fwd(zq, zkv, bias, sinks, qg, kg, og):
    T = zq.shape[0]
    rows = ATTN_SUB * BLOCK

    def body(q_ref, kvc_ref, kvp_ref, bias_ref, sinks_ref, qgt_ref, kg_ref, kgt_ref, og_ref, a_ref, mix_ref):
        n = pl.program_id(0)
        tri = _tri_t()
        qgt, kgv, kgt = qgt_ref[...], kg_ref[...], kgt_ref[...]
        kvs = [kvp_ref[...]] + [kvc_ref[i * BLOCK:(i + 1) * BLOCK, :] for i in range(ATTN_SUB)]
        keys = [[_kv_block(kv, kh, kgv, kgt) for kh in range(N_KV)] for kv in kvs]
        for i in range(ATTN_SUB):
            zq_t = q_ref[i * BLOCK:(i + 1) * BLOCK, :].T
            has_prev = (n > 0) if i == 0 else None
            outs = []
            for kh in range(N_KV):
                kp, kc = keys[i][kh], keys[i + 1][kh]
                c = _attn_group(zq_t, kp, kc, kh, bias_ref, sinks_ref, qgt, tri, has_prev)
                p_c, p_p = _split(c["pn"].astype(BF16), tri)
                o_t = _dot(kc["v_t"], p_c) + _dot(kp["v_t"], p_p)
                outs += [o_t[:, g * BLOCK:(g + 1) * BLOCK] for g in range(GQA)]
            a = jnp.concatenate(outs, axis=0).T
            a_ref[i * BLOCK:(i + 1) * BLOCK, :] = a
            ah, _ = _rms(a)
            mix_ref[i * BLOCK:(i + 1) * BLOCK, :] = (ah * og_ref[...]).astype(BF16)

    return _call(
        body, name="attn_fwd", grid=(T // rows,),
        in_specs=[_rows(rows, ATTN_W), _rows(rows, 2 * KV_W),
                  pl.BlockSpec((BLOCK, 2 * KV_W), lambda n: (jnp.maximum(n * ATTN_SUB - 1, 0), 0)),
                  RESIDENT, IN_SMEM, RESIDENT, RESIDENT, RESIDENT, RESIDENT],
        out_specs=[_rows(rows, ATTN_W), _rows(rows, ATTN_W)],
        out_shape=[jax.ShapeDtypeStruct((T, ATTN_W), F32), jax.ShapeDtypeStruct((T, ATTN_W), BF16)],
        compiler_params=_params("parallel"),
    )(zq, zkv, zkv, bias, sinks, qg.reshape(HEAD_DIM, 1), kg, kg.reshape(HEAD_DIM, 1), og)


def _conv_post(y, lg, lb, og):
    mu = jnp.mean(y, axis=-1, keepdims=True)
    yc = y - mu
    rstd = lax.rsqrt(jnp.mean(yc * yc, axis=-1, keepdims=True) + EPS)
    yn = yc * rstd
    ln = yn * lg + lb
    sg = _sig(ln)
    c = ln * sg
    ch, r = _rms(c)
    return yn, rstd, ln, sg, ch, r


CONV_CHUNK = 64


SLAB_ROWS = CONV_CHUNK + 8 * ((CONV_K - 1) // 8)


def _shifted_taps(buf, slab, r0, base):
    for b in range(8):
        taps = range(b, CONV_K, 8)
        span = CONV_CHUNK + 8 * (len(taps) - 1)
        slab[0:span, :] = buf[r0 + base + b:r0 + base + b + span, :]
        for a, j in enumerate(taps):
            yield j, slab[8 * a:8 * a + CONV_CHUNK, :]


def _fold8(x):
    return jnp.sum(x.reshape(x.shape[0] // 8, 8, x.shape[1]), axis=0)


def _conv_fwd(zu, zg, cw, cb, lg, lb, og):
    T = zu.shape[0]
    tt = 512
    halo_spec = pl.BlockSpec((HALO, CONV_W), lambda i: (jnp.maximum(i * (tt // HALO) - 1, 0), 0))

    def body(u_ref, g_ref, uh_ref, gh_ref, cw_ref, cb_ref, lg_ref, lb_ref, og_ref, y_ref, mix_ref, buf, slab):
        i = pl.program_id(0)
        hal = uh_ref[...] * _sig(gh_ref[...])
        buf[0:HALO, :] = jnp.where(i > 0, hal, 0.0)
        buf[HALO:HALO + tt, :] = u_ref[...] * _sig(g_ref[...])
        cbv, lgv, lbv, ogv = cb_ref[...], lg_ref[...], lb_ref[...], og_ref[...]
        for r0 in range(0, tt, CONV_CHUNK):
            rs = slice(r0, r0 + CONV_CHUNK)
            acc = jnp.zeros((CONV_CHUNK, CONV_W), F32)
            for j, win in _shifted_taps(buf, slab, r0, HALO - (CONV_K - 1)):
                acc = acc + cw_ref[j:j + 1, :] * win
            y = acc + cbv
            y_ref[rs, :] = y
            mix_ref[rs, :] = (_conv_post(y, lgv, lbv, ogv)[4] * ogv).astype(BF16)

    return _call(
        body, name="conv_fwd", grid=(T // tt,),
        in_specs=[_rows(tt, CONV_W), _rows(tt, CONV_W), halo_spec, halo_spec] + [RESIDENT] * 5,
        out_specs=[_rows(tt, CONV_W), _rows(tt, CONV_W)],
        out_shape=[jax.ShapeDtypeStruct((T, CONV_W), F32), jax.ShapeDtypeStruct((T, CONV_W), BF16)],
        scratch_shapes=[pltpu.VMEM((HALO + tt, CONV_W), F32), pltpu.VMEM((SLAB_ROWS, CONV_W), F32)],
        compiler_params=_params("parallel"),
    )(zu, zg, zu, zg, cw, cb, lg, lb, og)


def _fwd_out(x, mix_a, mix_c, w_out):
    T = x.shape[0]
    tm = 512

    def body(x_ref, a_ref, c_ref, w_ref, o_ref):
        o_ref[...] = (x_ref[...] + _dot(a_ref[...], w_ref[0:ATTN_W, :])
                      + _dot(c_ref[...], w_ref[ATTN_W:, :]))

    return _call(
        body, name="fwd_out", grid=(T // tm,),
        in_specs=[_rows(tm, D_MODEL), _rows(tm, ATTN_W), _rows(tm, CONV_W), RESIDENT],
        out_specs=_rows(tm, D_MODEL),
        out_shape=jax.ShapeDtypeStruct((T, D_MODEL), F32),
        compiler_params=_params("parallel"),
    )(x, mix_a, mix_c, w_out)


def _chunked(tm):
    return pl.BlockSpec((N_FF, tm, FF_CHUNK), lambda i: (0, i, 0))


def _mlp_fwd(x, g, wup_t, wdown):
    T = x.shape[0]
    tm = 512

    def body(x_ref, g_ref, wu_ref, wd_ref, o_ref, up_ref):
        xv = x_ref[...]
        xh, _ = _rms(xv)
        h = (xh * g_ref[...]).astype(BF16)
        acc = xv
        for c in range(N_FF):
            rows = slice(c * FF_CHUNK, (c + 1) * FF_CHUNK)
            up = _dot_nt(h, wu_ref[rows, :])
            up_ref[c] = up.astype(BF16)
            act = jnp.square(jnp.maximum(up, 0.0))
            acc = acc + _dot(act.astype(BF16), wd_ref[rows, :])
        o_ref[...] = acc

    return _call(
        body, name="mlp_fwd", grid=(T // tm,),
        in_specs=[_rows(tm, D_MODEL), RESIDENT, RESIDENT, RESIDENT],
        out_specs=[_rows(tm, D_MODEL), _chunked(tm)],
        out_shape=[jax.ShapeDtypeStruct((T, D_MODEL), F32), jax.ShapeDtypeStruct((N_FF, T, FF_CHUNK), BF16)],
        compiler_params=_params("parallel"),
    )(x, g, wup_t, wdown)


def _loss_head(y, target):
    T = y.shape[0]
    tm = 512

    def body(y_ref, t_ref, l_ref, d_ref):
        @pl.when(pl.program_id(0) == 0)
        def _():
            l_ref[...] = jnp.zeros_like(l_ref)
        e = y_ref[...] - t_ref[...]
        d_ref[...] = e / D_MODEL
        l_ref[...] += 0.5 * jnp.sum(jnp.mean(e * e, axis=-1, keepdims=True))

    return _call(
        body, name="loss_head", grid=(T // tm,),
        in_specs=[_rows(tm, D_MODEL), _rows(tm, D_MODEL)],
        out_specs=[_acc_rows(128, 8), _rows(tm, D_MODEL)],
        out_shape=[jax.ShapeDtypeStruct((8, 128), F32), jax.ShapeDtypeStruct((T, D_MODEL), F32)],
        compiler_params=_params("arbitrary"),
    )(y, target)


def _bias_table(rel_bias, bucket):
    def body(rb_ref, bk_ref, o_ref):
        bk = bk_ref[...]
        for h in range(N_HEADS):
            acc = jnp.zeros((BLOCK, BLOCK), F32)
            for b in range(N_BUCKETS):
                acc = jnp.where(bk == b, rb_ref[b, h], acc)
            o_ref[h // GQA, :, (h % GQA) * BLOCK:(h % GQA + 1) * BLOCK] = acc

    return _call(
        body, name="bias_table", in_specs=[IN_SMEM, RESIDENT], out_specs=RESIDENT,
        out_shape=jax.ShapeDtypeStruct((N_KV, BLOCK, GQA * BLOCK), F32),
    )(rel_bias, bucket)


def _mlp_bwd(dx2, x1, up, g, wup_t, wdown):
    T = x1.shape[0]
    tm = 512

    def body(d_ref, x_ref, up_ref, g_ref, wu_ref, wd_ref, dx_ref, dup_ref, h_ref, dg_ref, db_ref):
        @pl.when(pl.program_id(0) == 0)
        def _():
            dg_ref[...] = jnp.zeros_like(dg_ref)
        d2 = d_ref[...]
        d2b = d2.astype(BF16)
        db_ref[...] = d2b
        xh, r = _rms(x_ref[...])
        gv = g_ref[...]
        h_ref[...] = (xh * gv).astype(BF16)
        dh = jnp.zeros((tm, D_MODEL), F32)
        for c in range(N_FF):
            rows = slice(c * FF_CHUNK, (c + 1) * FF_CHUNK)
            dact = _dot_nt(d2b, wd_ref[rows, :])
            dup = (dact * (2.0 * jnp.maximum(up_ref[c].astype(F32), 0.0))).astype(BF16)
            dup_ref[c] = dup
            dh = dh + _dot(dup, wu_ref[rows, :])
        dg_ref[...] += _colsum(dh * xh)
        dx_ref[...] = d2 + _rms_bwd(dh * gv, xh, r)

    return _call(
        body, name="mlp_bwd", grid=(T // tm,),
        in_specs=[_rows(tm, D_MODEL), _rows(tm, D_MODEL), _chunked(tm), RESIDENT, RESIDENT, RESIDENT],
        out_specs=[_rows(tm, D_MODEL), _chunked(tm), _rows(tm, D_MODEL), _acc_rows(D_MODEL), _rows(tm, D_MODEL)],
        out_shape=[jax.ShapeDtypeStruct((T, D_MODEL), F32), jax.ShapeDtypeStruct((N_FF, T, FF_CHUNK), BF16),
                   jax.ShapeDtypeStruct((T, D_MODEL), BF16), jax.ShapeDtypeStruct((1, D_MODEL), F32),
                   jax.ShapeDtypeStruct((T, D_MODEL), BF16)],
        compiler_params=_params("arbitrary"),
    )(dx2, x1, up, g, wup_t, wdown)


def _wgrad(a, b, name, chunked=False, square_relu=False, tr=FF_CHUNK):
    if chunked:
        nr, T, tr = a.shape
    else:
        T, R = a.shape
        tr = min(R, tr)
        nr = R // tr
    tk = min(T, 4096)
    nk = T // tk

    def body(a_ref, b_ref, o_ref, acc):
        k = pl.program_id(1)

        @pl.when(k == 0)
        def _():
            acc[...] = jnp.zeros_like(acc)
        av = a_ref[...]
        if square_relu:
            av = jnp.square(jnp.maximum(av.astype(F32), 0.0))
        acc[...] += _dot_tn(av.astype(BF16), b_ref[...].astype(BF16))

        @pl.when(k == nk - 1)
        def _():
            o_ref[...] = acc[...].astype(BF16)

    a_spec = (pl.BlockSpec((None, tk, tr), lambda r, k: (r, k, 0)) if chunked
              else pl.BlockSpec((tk, tr), lambda r, k: (k, r)))
    return _call(
        body, name=name, grid=(nr, nk),
        in_specs=[a_spec, pl.BlockSpec((tk, D_MODEL), lambda r, k: (k, 0))],
        out_specs=pl.BlockSpec((tr, D_MODEL), lambda r, k: (r, 0)),
        out_shape=jax.ShapeDtypeStruct((nr * tr, D_MODEL), BF16),
        scratch_shapes=[pltpu.VMEM((tr, D_MODEL), F32)],
        compiler_params=_params("parallel", "arbitrary"),
    )(a, b)


def _bwd_out(dx1, w_out):
    T = dx1.shape[0]
    tm = 512

    def body(d_ref, w_ref, da_ref, dc_ref, db_ref):
        db = d_ref[...].astype(BF16)
        db_ref[...] = db
        dm = _dot_nt(db, w_ref[...])
        da_ref[...] = dm[:, :ATTN_W]
        dc_ref[...] = dm[:, ATTN_W:]

    return _call(
        body, name="bwd_out", grid=(T // tm,),
        in_specs=[_rows(tm, D_MODEL), RESIDENT],
        out_specs=[_rows(tm, ATTN_W), _rows(tm, CONV_W), _rows(tm, D_MODEL)],
        out_shape=[jax.ShapeDtypeStruct((T, ATTN_W), F32), jax.ShapeDtypeStruct((T, CONV_W), F32),
                   jax.ShapeDtypeStruct((T, D_MODEL), BF16)],
        compiler_params=_params("parallel"),
    )(dx1, w_out)


def _conv_bwd(dmix, y, zu, zg, cw, lg, lb, og):
    T = y.shape[0]
    tt = 512
    nt = T // tt
    per = tt // HALO
    prev_spec = pl.BlockSpec((HALO, CONV_W), lambda i: (jnp.maximum(i * per - 1, 0), 0))
    next_spec = pl.BlockSpec((HALO, CONV_W), lambda i: (jnp.minimum((i + 1) * per, nt * per - 1), 0))

    def body(dm_ref, dmn_ref, y_ref, yn_ref, u_ref, g_ref, uh_ref, gh_ref, cw_ref, lg_ref, lb_ref, og_ref,
             du_ref, dg_ref, pg_ref, hbuf, dybuf, dwacc, pacc, slab):
        i = pl.program_id(0)

        @pl.when(i == 0)
        def _():
            dwacc[...] = jnp.zeros_like(dwacc)
            pacc[...] = jnp.zeros_like(pacc)
        lgv, lbv, ogv = lg_ref[...], lb_ref[...], og_ref[...]

        def chain(yv, dm):
            yn, rstd, ln, sg, ch, r = _conv_post(yv, lgv, lbv, ogv)
            dc = _rms_bwd(dm * ogv, ch, r)
            dln = dc * sg * (1.0 + ln * (1.0 - sg))
            dyn = dln * lgv
            dy = rstd * (dyn - jnp.mean(dyn, axis=-1, keepdims=True)
                         - yn * jnp.mean(dyn * yn, axis=-1, keepdims=True))
            return dy, dm * ch, dln * yn, dln

        hbuf[0:HALO, :] = jnp.where(i > 0, uh_ref[...] * _sig(gh_ref[...]), 0.0)
        for r0 in range(0, tt, CONV_CHUNK):
            rs = slice(r0, r0 + CONV_CHUNK)
            hbuf[HALO + r0:HALO + r0 + CONV_CHUNK, :] = u_ref[rs, :] * _sig(g_ref[rs, :])
            dy, p_og, p_lg, p_lb = chain(y_ref[rs, :], dm_ref[rs, :])
            dybuf[rs, :] = dy
            for k, part in enumerate((dy, p_lg, p_lb, p_og)):
                pacc[8 * k:8 * k + 8, :] += _fold8(part)
        dyh, _, _, _ = chain(yn_ref[...], dmn_ref[...])
        dybuf[tt:tt + HALO, :] = jnp.where(i < nt - 1, dyh, 0.0)
        for r0 in range(0, tt, CONV_CHUNK):
            rs = slice(r0, r0 + CONV_CHUNK)
            dy = dybuf[rs, :]
            dh = jnp.zeros((CONV_CHUNK, CONV_W), F32)
            for j, win in _shifted_taps(dybuf, slab, r0, 0):
                dh = dh + cw_ref[CONV_K - 1 - j:CONV_K - j, :] * win
            for j, win in _shifted_taps(hbuf, slab, r0, HALO - (CONV_K - 1)):
                dwacc[8 * j:8 * j + 8, :] += _fold8(dy * win)
            sgt = _sig(g_ref[rs, :])
            du_ref[rs, :] = (dh * sgt).astype(BF16)
            dg_ref[rs, :] = (dh * u_ref[rs, :] * sgt * (1.0 - sgt)).astype(BF16)

        @pl.when(i == nt - 1)
        def _():
            pg_ref[...] = jnp.zeros_like(pg_ref)
            for j in range(CONV_K):
                pg_ref[j:j + 1, :] = _colsum(dwacc[8 * j:8 * j + 8, :])
            for k in range(4):
                pg_ref[32 + k:33 + k, :] = _colsum(pacc[8 * k:8 * k + 8, :])

    return _call(
        body, name="conv_bwd", grid=(nt,),
        in_specs=[_rows(tt, CONV_W), next_spec, _rows(tt, CONV_W), next_spec, _rows(tt, CONV_W), _rows(tt, CONV_W),
                  prev_spec, prev_spec] + [RESIDENT] * 4,
        out_specs=[_rows(tt, CONV_W), _rows(tt, CONV_W), _acc_rows(CONV_W, 40)],
        out_shape=[jax.ShapeDtypeStruct((T, CONV_W), BF16), jax.ShapeDtypeStruct((T, CONV_W), BF16),
                   jax.ShapeDtypeStruct((40, CONV_W), F32)],
        scratch_shapes=[pltpu.VMEM((HALO + tt, CONV_W), F32), pltpu.VMEM((tt + HALO, CONV_W), F32),
                        pltpu.VMEM((8 * HALO, CONV_W), F32), pltpu.VMEM((32, CONV_W), F32),
                        pltpu.VMEM((SLAB_ROWS, CONV_W), F32)],
        compiler_params=_params("arbitrary"),
    )(dmix, dmix, y, y, zu, zg, zu, zg, cw, lg, lb, og)


def _attn_bwd(dmix, a, zq, zkv, bias, sinks, qg, kg, og):
    T = zq.shape[0]
    rows = ATTN_SUB * BLOCK
    ns = T // rows
    nb = T // BLOCK
    cur = lambda w: pl.BlockSpec((rows, w), lambda n: (jnp.minimum(n, ns - 1), 0))
    scale = 1.0 / math.sqrt(HEAD_DIM)
    done = rows - BLOCK

    def body(dm_ref, a_ref, q_ref, kvc_ref, kvp_ref, bias_ref, sinks_ref, qgt_ref, kg_ref, kgt_ref, og_ref,
             dq_ref, dkv_ref, db_ref, dog_ref, dqg_ref, dkg_ref, dsk_ref, carry):
        n = pl.program_id(0)

        @pl.when(n == 0)
        def _():
            for ref in (db_ref, dog_ref, dqg_ref, dkg_ref, dsk_ref, carry):
                ref[...] = jnp.zeros_like(ref)

        @pl.when(n < ns)
        def _():
            tri = _tri_t()
            ogv, qgt, kgv, kgt = og_ref[...], qgt_ref[...], kg_ref[...], kgt_ref[...]
            lane = lax.broadcasted_iota(jnp.int32, (1, 128), 1)
            kvs = [kvp_ref[...]] + [kvc_ref[i * BLOCK:(i + 1) * BLOCK, :] for i in range(ATTN_SUB)]
            keys = [[_kv_block(kv, kh, kgv, kgt) for kh in range(N_KV)] for kv in kvs]
            dkn = [[jnp.zeros((BLOCK, HEAD_DIM), F32)] * N_KV for _ in kvs]
            dv = [[jnp.zeros((BLOCK, HEAD_DIM), F32)] * N_KV for _ in kvs]
            dsk = jnp.zeros((1, 128), F32)
            for i in range(ATTN_SUB):
                blk = slice(i * BLOCK, (i + 1) * BLOCK)
                ah, ra = _rms(a_ref[blk, :])
                dm = dm_ref[blk, :]
                dog_ref[...] += _colsum(dm * ah)
                da_t = _rms_bwd(dm * ogv, ah, ra).T
                zq_t = q_ref[blk, :].T
                has_prev = (n > 0) if i == 0 else None
                dqs = []
                for kh in range(N_KV):
                    kp, kc = keys[i][kh], keys[i + 1][kh]
                    c = _attn_group(zq_t, kp, kc, kh, bias_ref, sinks_ref, qgt, tri, has_prev)
                    dob = _heads_side_by_side(da_t, kh).astype(BF16)
                    pn = c["pn"]
                    p_c, p_p = _split(pn.astype(BF16), tri)
                    dv[i + 1][kh] = dv[i + 1][kh] + _dot_nt(p_c, dob)
                    dv[i][kh] = dv[i][kh] + _dot_nt(p_p, dob)
                    dp = jnp.where(tri, _dot(kc["v"], dob), _dot(kp["v"], dob))
                    dl = jnp.sum(pn * dp, axis=0, keepdims=True)
                    ds = pn * (dp - dl)
                    dsr = -c["psink"] * dl
                    for g in range(GQA):
                        dsk = dsk + jnp.where(lane == kh * GQA + g, jnp.sum(jnp.where(_head_lanes(g), dsr, 0.0)), 0.0)
                    db_ref[kh] += ds
                    ds_c, ds_p = _split((ds * scale).astype(BF16), tri)
                    dqn = _dot(kc["kn_t"], ds_c) + _dot(kp["kn_t"], ds_p)
                    dkn[i + 1][kh] = dkn[i + 1][kh] + _dot_nt(ds_c, c["qn"])
                    dkn[i][kh] = dkn[i][kh] + _dot_nt(ds_p, c["qn"])
                    dqg_ref[...] += jnp.sum(dqn * c["qhat"], axis=1, keepdims=True)
                    dq_t = _rms0_bwd(dqn * qgt, c["qhat"], c["rq"])
                    dqs += [dq_t[:, g * BLOCK:(g + 1) * BLOCK] for g in range(GQA)]
                dq_ref[blk, :] = jnp.concatenate(dqs, axis=0).T.astype(BF16)
            dsk_ref[...] += dsk
            dkv = []
            for j in range(ATTN_SUB + 1):
                dk = []
                for kh in range(N_KV):
                    key = keys[j][kh]
                    dkg_ref[...] += _colsum(dkn[j][kh] * key["khat"])
                    dk.append(_rms_bwd(dkn[j][kh] * kgv, key["khat"], key["rk"]))
                dkv.append(jnp.concatenate(dk + dv[j], axis=-1))
            if done:
                dkv_ref[0:done, :] = carry[0:done, :].astype(BF16)
            dkv_ref[done:rows, :] = (carry[done:rows, :] + dkv[0]).astype(BF16)
            for j in range(1, ATTN_SUB + 1):
                carry[(j - 1) * BLOCK:j * BLOCK, :] = dkv[j]

        @pl.when(n == ns)
        def _():
            dkv_ref[...] = carry[...].astype(BF16)

    small = lambda w: pl.BlockSpec((1, w), lambda n: (0, 0))
    return _call(
        body, name="attn_bwd", grid=(ns + 1,),
        in_specs=[cur(ATTN_W), cur(ATTN_W), cur(ATTN_W), cur(2 * KV_W),
                  pl.BlockSpec((BLOCK, 2 * KV_W), lambda n: (jnp.clip(n * ATTN_SUB - 1, 0, nb - 1), 0)),
                  RESIDENT, IN_SMEM, RESIDENT, RESIDENT, RESIDENT, RESIDENT],
        out_specs=[cur(ATTN_W), pl.BlockSpec((rows, 2 * KV_W), lambda n: (jnp.maximum(n - 1, 0), 0)),
                   pl.BlockSpec((N_KV, BLOCK, GQA * BLOCK), lambda n: (0, 0, 0)),
                   small(ATTN_W), pl.BlockSpec((HEAD_DIM, 1), lambda n: (0, 0)), small(HEAD_DIM), small(128)],
        out_shape=[jax.ShapeDtypeStruct((T, ATTN_W), BF16), jax.ShapeDtypeStruct((T, 2 * KV_W), BF16),
                   jax.ShapeDtypeStruct((N_KV, BLOCK, GQA * BLOCK), F32),
                   jax.ShapeDtypeStruct((1, ATTN_W), F32), jax.ShapeDtypeStruct((HEAD_DIM, 1), F32),
                   jax.ShapeDtypeStruct((1, HEAD_DIM), F32), jax.ShapeDtypeStruct((1, 128), F32)],
        scratch_shapes=[pltpu.VMEM((rows, 2 * KV_W), F32)],
        compiler_params=_params("arbitrary"),
    )(dmix, a, zq, zkv, zkv, bias, sinks, qg.reshape(HEAD_DIM, 1), kg, kg.reshape(HEAD_DIM, 1), og)


def _bwd_in(dx1, x, dq, dkv, du, dgt, g, wt):
    T = x.shape[0]
    tm = 512

    def body(d1_ref, x_ref, dq_ref, dkv_ref, du_ref, dgt_ref, g_ref, w_ref, dx_ref, h_ref, dg_ref, dz_ref):
        @pl.when(pl.program_id(0) == 0)
        def _():
            dg_ref[...] = jnp.zeros_like(dg_ref)
        xh, r = _rms(x_ref[...])
        gv = g_ref[...]
        h_ref[...] = (xh * gv).astype(BF16)
        dz = jnp.concatenate([dq_ref[...], dkv_ref[...], du_ref[...], dgt_ref[...]], axis=-1)
        dz_ref[...] = dz
        dh = _dot(dz, w_ref[...])
        dg_ref[...] += _colsum(dh * xh)
        dx_ref[...] = d1_ref[...] + _rms_bwd(dh * gv, xh, r)

    return _call(
        body, name="bwd_in", grid=(T // tm,),
        in_specs=[_rows(tm, D_MODEL), _rows(tm, D_MODEL), _rows(tm, ATTN_W), _rows(tm, 2 * KV_W),
                  _rows(tm, CONV_W), _rows(tm, CONV_W), RESIDENT, RESIDENT],
        out_specs=[_rows(tm, D_MODEL), _rows(tm, D_MODEL), _acc_rows(D_MODEL), _rows(tm, IN_W)],
        out_shape=[jax.ShapeDtypeStruct((T, D_MODEL), F32), jax.ShapeDtypeStruct((T, D_MODEL), BF16),
                   jax.ShapeDtypeStruct((1, D_MODEL), F32), jax.ShapeDtypeStruct((T, IN_W), BF16)],
        compiler_params=_params("arbitrary"),
    )(dx1, x, dq, dkv, du, dgt, g, wt)


def _bias_grad(db, bucket):
    def body(db_ref, bk_ref, o_ref):
        bk = bk_ref[...]
        lane = lax.broadcasted_iota(jnp.int32, (1, 128), 1)
        for h in range(N_HEADS):
            cols = slice((h % GQA) * BLOCK, (h % GQA + 1) * BLOCK)
            tot = db_ref[0, h // GQA, :, cols]
            for l in range(1, DEPTH):
                tot = tot + db_ref[l, h // GQA, :, cols]
            out = jnp.zeros((1, 128), F32)
            for b in range(N_BUCKETS):
                out = jnp.where(lane == b, jnp.sum(jnp.where(bk == b, tot, 0.0)), out)
            o_ref[h:h + 1, :] = out

    return _call(
        body, name="bias_grad", in_specs=[RESIDENT, RESIDENT], out_specs=RESIDENT,
        out_shape=jax.ShapeDtypeStruct((N_HEADS, 128), F32),
        compiler_params=pltpu.CompilerParams(vmem_limit_bytes=VMEM_LIMIT),
    )(db, bucket)


def _place():
    return lax.axis_index("x"), lax.axis_index("y"), lax.axis_index("c")


def _all_gather(shards):
    na = len(shards)

    def body(*refs):
        ins, outs = refs[:na], refs[na:2 * na]
        send_sems, recv_sems, local_sems = refs[2 * na:]
        x, y, c = _place()
        me, sibling = (x, y, c), (x, y, 1 - c)
        chips = [(1 - x, y), (x, 1 - y), (1 - x, 1 - y)]

        def rows(a, p):
            r = ins[a].shape[1]
            return outs[a].at[:, pl.ds((4 * p[0] + 2 * p[1] + p[2]) * r, r), :]

        def copy(a, k, block, to, src=None):
            return pltpu.make_async_remote_copy(
                src_ref=rows(a, block) if src is None else src, dst_ref=rows(a, block),
                send_sem=send_sems.at[a, k], recv_sem=recv_sems.at[a, k], device_id=to, device_id_type=MESH)

        mine = [pltpu.make_async_copy(ins[a], rows(a, me), local_sems.at[a]) for a in range(na)]
        for cp in mine:
            cp.start()
        first = []
        for a in range(na):
            first.append(copy(a, 0, me, sibling, src=ins[a]))
            first += [copy(a, 1 + j, me, (*chip, c), src=ins[a]) for j, chip in enumerate(chips)]
        for cp in first:
            cp.start()
        passed = []
        for j, chip in enumerate(chips):
            for a in range(na):
                copy(a, 1 + j, (*chip, c), me).wait_recv()
                cp = copy(a, 4 + j, (*chip, c), sibling)
                cp.start()
                passed.append(cp)
        for a in range(na):
            copy(a, 0, sibling, me).wait_recv()
            for j, chip in enumerate(chips):
                copy(a, 4 + j, (*chip, 1 - c), me).wait_recv()
        for cp in first + passed:
            cp.wait_send()
        for cp in mine:
            cp.wait()

    return _call(
        body, name="all_gather",
        in_specs=[ANY] * na, out_specs=[ANY] * na,
        out_shape=[jax.ShapeDtypeStruct((s.shape[0], N_DEV * s.shape[1], s.shape[2]), s.dtype) for s in shards],
        scratch_shapes=[pltpu.SemaphoreType.DMA((na, 7)), pltpu.SemaphoreType.DMA((na, 7)),
                        pltpu.SemaphoreType.DMA((na,))],
    )(*shards)


IN_HBM = pl.BlockSpec(memory_space=pltpu.HBM)
IN_SEM = pl.BlockSpec(memory_space=pltpu.SEMAPHORE)
DATAFLOW = pltpu.SideEffectType.DATAFLOW_SIDE_EFFECTING


def _exchange_copies(srcs, lands, send_sems, recv_sems, scatter):
    x, y, c = _place()
    me = 4 * x + 2 * y + c
    copies = []
    for k in range(1, N_DEV):
        p = (x ^ (k >> 2), y ^ ((k >> 1) & 1), c ^ (k & 1))
        for a, (src, land) in enumerate(zip(srcs, lands)):
            r = land.shape[0] // N_DEV
            if scatter:
                src = src.at[pl.ds((4 * p[0] + 2 * p[1] + p[2]) * r, r), :]
            copies.append(pltpu.make_async_remote_copy(
                src_ref=src, dst_ref=land.at[pl.ds(me * r, r), :], send_sem=send_sems.at[a * (N_DEV - 1) + k - 1],
                recv_sem=recv_sems.at[a * (N_DEV - 1) + k - 1], device_id=p, device_id_type=MESH))
    return copies


def _own_copies(srcs, lands, send_sems, scatter):
    x, y, c = _place()
    me = 4 * x + 2 * y + c
    copies = []
    for a, (src, land) in enumerate(zip(srcs, lands)):
        r = land.shape[0] // N_DEV
        if scatter:
            src = src.at[pl.ds(me * r, r), :]
        copies.append(pltpu.make_async_copy(src, land.at[pl.ds(me * r, r), :],
                                            send_sems.at[len(srcs) * (N_DEV - 1) + a]))
    return copies


def _exchange_start(srcs, name, scatter):
    na = len(srcs)
    lands = [lax.empty((s.shape[0] * (1 if scatter else N_DEV), s.shape[1]), s.dtype) for s in srcs]

    def body(*refs):
        ins, lnd = refs[:na], refs[na:2 * na]
        send_sems, recv_sems = refs[2 * na], refs[2 * na + 1]
        token = refs[-1]
        for cp in _exchange_copies(ins, lnd, send_sems, recv_sems, scatter) + _own_copies(ins, lnd, send_sems, scatter):
            cp.start()
        token[...] = jnp.zeros_like(token)

    hbm = lambda a: pltpu.with_memory_space_constraint(a, pltpu.HBM)
    out = _call(
        body, name=name,
        out_shape=(pltpu.SemaphoreType.DMA((na * N_DEV,)), pltpu.SemaphoreType.DMA((na * (N_DEV - 1),)),
                   *[pltpu.HBM(a.shape, a.dtype) for a in (*srcs, *lands)], jax.ShapeDtypeStruct((8, 128), F32)),
        in_specs=[IN_HBM] * (2 * na),
        out_specs=(IN_SEM, IN_SEM, *[IN_HBM] * (2 * na), RESIDENT),
        input_output_aliases={i: 2 + i for i in range(2 * na)},
        compiler_params=pltpu.CompilerParams(has_side_effects=DATAFLOW),
    )(*[hbm(a) for a in (*srcs, *lands)])
    return (out[0], out[1], out[2:2 + na], out[2 + na:2 + 2 * na]), out[-1]


def _exchange_wait(state, after, name, scatter):
    send_sems, recv_sems, srcs, lands = state
    na = len(srcs)

    def body(*refs):
        ins, lnd = refs[:na], refs[na:2 * na]
        for cp in _exchange_copies(ins, lnd, refs[2 * na], refs[2 * na + 1], scatter):
            cp.wait_send()
            cp.wait_recv()
        for cp in _own_copies(ins, lnd, refs[2 * na], scatter):
            cp.wait()
        refs[-1][...] = jnp.zeros_like(refs[-1])

    out = _call(
        body, name=name,
        out_shape=(*[pltpu.HBM(a.shape, a.dtype) for a in (*srcs, *lands)], jax.ShapeDtypeStruct((8, 128), F32)),
        in_specs=[IN_HBM] * (2 * na) + [IN_SEM, IN_SEM, ANY],
        out_specs=(*[IN_HBM] * (2 * na), RESIDENT),
        input_output_aliases={i: i for i in range(2 * na)},
        compiler_params=pltpu.CompilerParams(has_side_effects=DATAFLOW),
    )(*srcs, *lands, send_sems, recv_sems, after)
    return out[:na], out[na:2 * na], out[-1]


def _adam_math(w, g, m, v):
    m = ADAM_B1 * m + (1.0 - ADAM_B1) * g
    v = ADAM_B2 * v + (1.0 - ADAM_B2) * jnp.square(g)
    m_hat = m / (1.0 - ADAM_B1 ** ADAM_STEP)
    v_hat = v / (1.0 - ADAM_B2 ** ADAM_STEP)
    delta = -ADAM_LR * (m_hat / (jnp.sqrt(v_hat) + ADAM_EPS) + ADAM_WD * w)
    return delta, m, v


def _adam_big(parts, w, m, v):
    nl, r, cdim = w.shape
    tr = r if r <= 256 else 256
    ni = r // tr
    spec = pl.BlockSpec((None, tr, cdim), lambda l, i: (l, i, 0))

    def part_spec(layer):
        return pl.BlockSpec((N_DEV, tr, cdim),
                            lambda l, i: (0, jnp.where(l == layer, i, jnp.where(l < layer, 0, ni - 1)), 0))

    def body(*refs):
        p_refs = refs[:nl]
        w_ref, m_ref, v_ref, g_ref, d_ref, nm_ref, nv_ref = refs[nl:]
        for layer in range(nl):
            @pl.when(pl.program_id(0) == layer)
            def _(p_ref=p_refs[layer]):
                g = p_ref[0].astype(F32)
                for s in range(1, N_DEV):
                    g = g + p_ref[s].astype(F32)
                d, nm, nv = _adam_math(w_ref[...], g, m_ref[...], v_ref[...])
                g_ref[...] = g
                d_ref[...] = d
                nm_ref[...] = nm
                nv_ref[...] = nv

    return _call(
        body, name="adam_big", grid=(nl, ni),
        in_specs=[part_spec(layer) for layer in range(nl)] + [spec, spec, spec],
        out_specs=[spec] * 4,
        out_shape=[jax.ShapeDtypeStruct(w.shape, F32)] * 4,
        compiler_params=_params("arbitrary", "arbitrary"),
    )(*parts, w, m, v)


def _sum_slots(parts):
    def body(p_ref, o_ref):
        g = p_ref[0]
        for s in range(1, N_DEV):
            g = g + p_ref[s]
        o_ref[...] = g

    return _call(body, name="sum_slots", in_specs=[RESIDENT], out_specs=RESIDENT,
                 out_shape=jax.ShapeDtypeStruct(parts.shape[1:], F32))(parts)


def _adam_small(ws, gs, ms, vs):
    n = len(ws)

    def body(*refs):
        ins, outs = refs[:4 * n], refs[4 * n:]
        for k in range(n):
            d, nm, nv = _adam_math(ins[k][...], ins[n + k][...], ins[2 * n + k][...], ins[3 * n + k][...])
            outs[k][...] = d
            outs[n + k][...] = nm
            outs[2 * n + k][...] = nv

    out = _call(body, name="adam_small", in_specs=[RESIDENT] * (4 * n), out_specs=[RESIDENT] * (3 * n),
                out_shape=[jax.ShapeDtypeStruct(w.shape, F32) for w in ws] * 3)(*ws, *gs, *ms, *vs)
    return out[:n], out[n:2 * n], out[2 * n:]


def _pack(arrays):
    parts = []
    for a in arrays:
        flat = a.reshape(-1)
        n = flat.shape[0]
        padded = -(-n // 1024) * 1024
        parts.append(jnp.pad(flat, (0, padded - n)).reshape(padded // 128, 128))
    return jnp.concatenate(parts, axis=0)


def _unpack(packed, shapes):
    out, row = [], 0
    for shp in shapes:
        n = int(np.prod(shp))
        rows = -(-n // 1024) * 8
        out.append(packed[row:row + rows].reshape(-1)[:n].reshape(shp))
        row += rows
    return out


def _t5_bucket_table():
    kj = np.arange(BLOCK)[:, None]
    qi = np.arange(BLOCK)[None, :]
    n = (qi - kj) % BLOCK
    max_exact = N_BUCKETS // 2
    large = max_exact + (np.log(np.maximum(n, 1) / max_exact) / np.log(MAX_DIST / max_exact)
                         * (N_BUCKETS - max_exact)).astype(np.int32)
    large = np.minimum(large, N_BUCKETS - 1)
    return np.where(n < max_exact, n, large).astype(np.int32)


SMALL_NAMES = ("rel_bias", "norm_mix_g", "q_norm_g", "k_norm_g", "sinks", "conv_b", "conv_ln_g", "conv_ln_b",
               "attn_out_g", "conv_out_g", "norm_mlp_g")


def _local_step(x, target, small, conv_w, get_w, put_g, token):
    bucket = jnp.asarray(_t5_bucket_table())
    bias = _bias_table(small["rel_bias"], bucket)
    row = lambda a, l: a[l][None, :]
    cw_pad = jnp.pad(conv_w, ((0, 0), (0, HALO - CONV_K), (0, 0)))
    saved, weights = [], []
    for l in range(DEPTH):
        wt_in, w_out, tok = get_w(l, 0, x)
        zq, zkv, zu, zg = _fwd_in(x, row(small["norm_mix_g"], l) + tok + (token if l == 0 else 0.0), wt_in)
        a, mix_a = _attn_fwd(zq, zkv, bias, small["sinks"][l], row(small["q_norm_g"], l), row(small["k_norm_g"], l),
                             row(small["attn_out_g"], l))
        y, mix_c = _conv_fwd(zu, zg, cw_pad[l], row(small["conv_b"], l), row(small["conv_ln_g"], l),
                             row(small["conv_ln_b"], l), row(small["conv_out_g"], l))
        x1 = _fwd_out(x, mix_a, mix_c, w_out)
        wt_up, w_down, tok = get_w(l, 1, x1)
        weights.append((wt_in, w_out, wt_up, w_down))
        x2, up = _mlp_fwd(x1, row(small["norm_mlp_g"], l) + tok, wt_up, w_down)
        saved.append((x, zq, zkv, zu, zg, a, mix_a, y, mix_c, x1, up))
        x = x2
    loss_part, dx = _loss_head(x, target)

    gs = {n: [None] * DEPTH for n in SMALL_NAMES if n != "rel_bias"}
    g_conv_w, dbias = [None] * DEPTH, [None] * DEPTH
    token = 0.0
    for l in reversed(range(DEPTH)):
        x0, zq, zkv, zu, zg, a, mix_a, y, mix_c, x1, up = saved[l]
        wt_in, w_out, wt_up, w_down = weights[l]
        dx1, dup, h2, gs["norm_mlp_g"][l], dxb = _mlp_bwd(dx, x1, up, row(small["norm_mlp_g"], l) + token, wt_up, w_down)
        g_up = _wgrad(dup, h2, "wgrad_up", chunked=True)
        g_down = _wgrad(up, dxb, "wgrad_down", chunked=True, square_relu=True)
        token = put_g(l, 1, (g_up, g_down))
        dma, dmc, dx1b = _bwd_out(dx1, w_out)
        g_out = jnp.concatenate([_wgrad(mix_a, dx1b, "wgrad_out_a"), _wgrad(mix_c, dx1b, "wgrad_out_c")], axis=0)
        du, dgt, pg = _conv_bwd(dmc, y, zu, zg, cw_pad[l], row(small["conv_ln_g"], l) + token,
                                row(small["conv_ln_b"], l), row(small["conv_out_g"], l))
        g_conv_w[l] = pg[:CONV_K]
        gs["conv_b"][l], gs["conv_ln_g"][l], gs["conv_ln_b"][l], gs["conv_out_g"][l] = pg[32], pg[33], pg[34], pg[35]
        dq, dkv, dbias[l], dog, dqg, dkg, dsk = _attn_bwd(
            dma, a, zq, zkv, bias, small["sinks"][l], row(small["q_norm_g"], l), row(small["k_norm_g"], l),
            row(small["attn_out_g"], l))
        gs["attn_out_g"][l], gs["q_norm_g"][l], gs["k_norm_g"][l], gs["sinks"][l] = dog, dqg, dkg, dsk[0, :N_HEADS]
        dx, h, gs["norm_mix_g"][l], dz = _bwd_in(dx1, x0, dq, dkv, du, dgt, row(small["norm_mix_g"], l), wt_in)
        g_in = _wgrad(dz, h, "wgrad_in", tr=IN_W // 2)
        token = put_g(l, 0, (g_in, g_out))
    small_grads = {n: jnp.stack([jnp.reshape(v, (-1,)) for v in vals]) for n, vals in gs.items()}
    small_grads["rel_bias"] = _bias_grad(jnp.stack(dbias), bucket)[:, :N_BUCKETS].T
    return loss_part, dx, small_grads, jnp.stack(g_conv_w)


def kernel(x, rel_bias, norm_mix_g, w_in, q_norm_g, k_norm_g, sinks, conv_w, conv_b, conv_ln_g, conv_ln_b, attn_out_g, conv_out_g, w_out, norm_mlp_g, w_mlp_up, w_mlp_down, loss_target, m_rel_bias, m_norm_mix_g, m_w_in, m_q_norm_g, m_k_norm_g, m_sinks, m_conv_w, m_conv_b, m_conv_ln_g, m_conv_ln_b, m_attn_out_g, m_conv_out_g, m_w_out, m_norm_mlp_g, m_w_mlp_up, m_w_mlp_down, v_rel_bias, v_norm_mix_g, v_w_in, v_q_norm_g, v_k_norm_g, v_sinks, v_conv_w, v_conv_b, v_conv_ln_g, v_conv_ln_b, v_attn_out_g, v_conv_out_g, v_w_out, v_norm_mlp_g, v_w_mlp_up, v_w_mlp_down):
    args = dict(locals())
    small = {n: args[n] for n in SMALL_NAMES}
    tr = lambda a: jnp.swapaxes(a, 1, 2)
    me = 4 * lax.axis_index("x") + 2 * lax.axis_index("y") + lax.axis_index("c")

    shards = (tr(w_in).astype(BF16), w_out.astype(BF16), tr(w_mlp_up).astype(BF16), w_mlp_down.astype(BF16))
    cw_sh = jnp.pad(tr(conv_w), ((0, 0), (0, 0), (0, HALO - CONV_K))).reshape(DEPTH * (CONV_W // N_DEV), HALO)
    gathers = {}

    def start_gather(l, half, dep):
        arrays = [s[l] for s in shards[2 * half:2 * half + 2]]
        if (l, half) == (0, 0):
            arrays = [cw_sh] + arrays
        if dep is not None:
            arrays = [a + dep[0, 0].astype(a.dtype) for a in arrays]
        gathers[l, half], tok = _exchange_start(arrays, f"gather_{l}{'ab'[half]}_start", scatter=False)
        return tok

    landed = {}

    def get_w(l, half, after):
        if (l, half) not in landed:
            _, lands, tok = _exchange_wait(gathers[l, half], after, f"gather_{l}{'ab'[half]}_wait", scatter=False)
            if l + 1 < DEPTH:
                tok = tok + start_gather(l + 1, half, tok)
            landed[l, half] = (*lands, tok[0, 0])
        return landed[l, half]

    token = start_gather(0, 0, None)
    token = token + start_gather(0, 1, token)
    cw_all, *first = get_w(0, 0, token)
    landed[0, 0] = tuple(first)
    conv_w_full = jnp.transpose(cw_all.reshape(N_DEV, DEPTH, CONV_W // N_DEV, HALO), (1, 3, 0, 2))
    conv_w_full = conv_w_full.reshape(DEPTH, HALO, CONV_W)[:, :CONV_K, :]

    scatters = {}

    last = {}

    def put_g(l, half, grads):
        scatters[l, half], last["token"] = _exchange_start(list(grads), f"scatter_{l}{'ab'[half]}_start", scatter=True)
        return last["token"][0, 0]

    loss_part, dx, small_grads, g_conv_w = _local_step(
        x[0], loss_target[0], small, conv_w_full, get_w, put_g, token[0, 0])

    big_w = (tr(w_in), w_out, tr(w_mlp_up), w_mlp_down)
    big_m = (tr(m_w_in), m_w_out, tr(m_w_mlp_up), m_w_mlp_down)
    big_v = (tr(v_w_in), v_w_out, tr(v_w_mlp_up), v_w_mlp_down)
    big_out, after = [None] * 4, last["token"]
    for half in (1, 0):
        parts = [[None] * DEPTH, [None] * DEPTH]
        for l in reversed(range(DEPTH)):
            _, lands, _ = _exchange_wait(scatters[l, half], after, f"scatter_{l}{'ab'[half]}_wait", scatter=True)
            for k, land in enumerate(lands):
                parts[k][l] = land.reshape(N_DEV, land.shape[0] // N_DEV, land.shape[1])
        for k in range(2):
            big_out[2 * half + k] = _adam_big(parts[k], big_w[2 * half + k], big_m[2 * half + k], big_v[2 * half + k])
        after = big_out[2 * half + 1][0]
    for k in (0, 2):
        big_out[k] = [tr(o) for o in big_out[k]]

    order = [n for n in SMALL_NAMES]
    packed = _pack([small_grads[n] for n in order] + [g_conv_w])
    slots, = _all_gather((packed[None],))
    summed = _sum_slots(slots[0].reshape(N_DEV, packed.shape[0], 128))
    shapes = [small[n].shape for n in order] + [(DEPTH, CONV_K, CONV_W)]
    sg = _unpack(summed, shapes)
    g_small = dict(zip(order, sg[:-1]))
    g_small["conv_w"] = lax.dynamic_slice_in_dim(sg[-1], me * (CONV_W // N_DEV), CONV_W // N_DEV, axis=2)
    names = order + ["conv_w"]
    deltas, new_m, new_v = _adam_small([args[n] for n in names], [g_small[n] for n in names],
                                       [args["m_" + n] for n in names], [args["v_" + n] for n in names])
    res = {"grad": g_small, "delta": dict(zip(names, deltas)), "new_m": dict(zip(names, new_m)),
           "new_v": dict(zip(names, new_v))}
    for k, n in enumerate(("w_in", "w_out", "w_mlp_up", "w_mlp_down")):
        for kind, val in zip(("grad", "delta", "new_m", "new_v"), big_out[k]):
            res[kind][n] = val

    loss = lax.psum(loss_part[0, 0], ("x", "y", "c"))
    weights = ("rel_bias", "norm_mix_g", "w_in", "q_norm_g", "k_norm_g", "sinks", "conv_w", "conv_b", "conv_ln_g",
               "conv_ln_b", "attn_out_g", "conv_out_g", "w_out", "norm_mlp_g", "w_mlp_up", "w_mlp_down")
    return (loss, dx[None], *[res[kind][n] for kind in ("grad", "delta", "new_m", "new_v") for n in weights])
```

```python
import math

import numpy as np
import jax
import jax.numpy as jnp
from jax import lax
from jax.experimental import pallas as pl
from jax.experimental.pallas import tpu as pltpu

F32, BF16 = jnp.float32, jnp.bfloat16
D_MODEL = 1024
DEPTH = 4
HEAD_DIM = 64
N_HEADS = 8
N_KV = 2
GQA = N_HEADS // N_KV
ATTN_W = N_HEADS * HEAD_DIM
KV_W = N_KV * HEAD_DIM
CONV_W = D_MODEL - ATTN_W
IN_W = ATTN_W + 2 * KV_W + 2 * CONV_W
BLOCK = 128
CONV_K = 31
HALO = 32
N_BUCKETS = 32
MAX_DIST = 128
D_FF = 4 * D_MODEL
FF_CHUNK = 512
N_FF = D_FF // FF_CHUNK
EPS = 1e-6
NEG = -1e30
N_DEV = 8
ADAM_LR, ADAM_B1, ADAM_B2, ADAM_EPS, ADAM_WD, ADAM_STEP = 0.001, 0.9, 0.999, 1e-08, 0.01, 10
VMEM_LIMIT = 56 * 1024 * 1024
MESH = pl.DeviceIdType.MESH

RESIDENT = pl.BlockSpec(memory_space=pltpu.VMEM)
IN_SMEM = pl.BlockSpec(memory_space=pltpu.SMEM)
ANY = pl.BlockSpec(memory_space=pl.ANY)


def _call(body, **kw):
    return pl.pallas_call(body, **kw)


def _params(*sem):
    return pltpu.CompilerParams(dimension_semantics=sem, vmem_limit_bytes=VMEM_LIMIT)


def _dot(a, b):
    return lax.dot_general(a, b, (((1,), (0,)), ((), ())), preferred_element_type=F32)


def _dot_nt(a, b):
    return lax.dot_general(a, b, (((1,), (1,)), ((), ())), preferred_element_type=F32)


def _dot_tn(a, b):
    return lax.dot_general(a, b, (((0,), (0,)), ((), ())), preferred_element_type=F32)


def _sig(x):
    return 1.0 / (1.0 + jnp.exp(-x))


def _rms(x):
    r = lax.rsqrt(jnp.mean(x * x, axis=-1, keepdims=True) + EPS)
    return x * r, r


def _rms_bwd(dy_g, xh, r):
    return r * (dy_g - xh * jnp.mean(dy_g * xh, axis=-1, keepdims=True))


def _rows(tm, w):
    return pl.BlockSpec((tm, w), lambda i: (i, 0))


def _acc_rows(w, rows=1):
    return pl.BlockSpec((rows, w), lambda i: (0, 0))


def _colsum(x):
    return jnp.sum(x, axis=0, keepdims=True)


def _fwd_in(x, g, wt):
    T = x.shape[0]
    tm = 512

    def body(x_ref, g_ref, w_ref, q_ref, kv_ref, u_ref, gt_ref):
        xh, _ = _rms(x_ref[...])
        h = (xh * g_ref[...]).astype(BF16)
        z = _dot_nt(h, w_ref[...])
        q_ref[...] = z[:, :ATTN_W]
        kv_ref[...] = z[:, ATTN_W:ATTN_W + 2 * KV_W]
        u_ref[...] = z[:, ATTN_W + 2 * KV_W:ATTN_W + 2 * KV_W + CONV_W]
        gt_ref[...] = z[:, ATTN_W + 2 * KV_W + CONV_W:]

    widths = (ATTN_W, 2 * KV_W, CONV_W, CONV_W)
    return _call(
        body, name="fwd_in", grid=(T // tm,),
        in_specs=[_rows(tm, D_MODEL), RESIDENT, RESIDENT],
        out_specs=[_rows(tm, w) for w in widths],
        out_shape=[jax.ShapeDtypeStruct((T, w), F32) for w in widths],
        compiler_params=_params("parallel"),
    )(x, g, wt)


ATTN_SUB = 8


def _rms0(x):
    r = lax.rsqrt(jnp.mean(x * x, axis=0, keepdims=True) + EPS)
    return x * r, r


def _rms0_bwd(dy_g, xh, r):
    return r * (dy_g - xh * jnp.mean(dy_g * xh, axis=0, keepdims=True))


def _tri_t():
    kj = lax.broadcasted_iota(jnp.int32, (BLOCK, BLOCK), 0)
    qi = lax.broadcasted_iota(jnp.int32, (BLOCK, BLOCK), 1)
    return jnp.concatenate([kj <= qi] * GQA, axis=1)


def _head_lanes(g):
    lane = lax.broadcasted_iota(jnp.int32, (1, GQA * BLOCK), 1)
    return (lane >= g * BLOCK) & (lane < (g + 1) * BLOCK)


def _heads_side_by_side(xt, kh):
    return jnp.concatenate(
        [xt[(kh * GQA + g) * HEAD_DIM:(kh * GQA + g + 1) * HEAD_DIM, :] for g in range(GQA)], axis=1)


def _kv_block(kv, kh, kg, kg_t):
    kvt = kv.T
    khat, rk = _rms(kv[:, kh * HEAD_DIM:(kh + 1) * HEAD_DIM])
    khat_t, _ = _rms0(kvt[kh * HEAD_DIM:(kh + 1) * HEAD_DIM, :])
    return dict(khat=khat, rk=rk, kn=(khat * kg).astype(BF16), kn_t=(khat_t * kg_t).astype(BF16),
                v=kv[:, KV_W + kh * HEAD_DIM:KV_W + (kh + 1) * HEAD_DIM].astype(BF16),
                v_t=kvt[KV_W + kh * HEAD_DIM:KV_W + (kh + 1) * HEAD_DIM, :].astype(BF16))


def _attn_group(zq_t, kp, kc, kh, bias_ref, sinks_ref, qg_t, tri, has_prev):
    scale = 1.0 / math.sqrt(HEAD_DIM)
    qhat, rq = _rms0(_heads_side_by_side(zq_t, kh))
    qf = qhat * qg_t
    qn = qf.astype(BF16)
    qs = (qf * scale).astype(BF16)
    s = jnp.where(tri, _dot(kc["kn"], qs), _dot(kp["kn"], qs)) + bias_ref[kh]
    if has_prev is not None:
        s = jnp.where(tri | has_prev, s, NEG)
    sink = jnp.zeros((1, GQA * BLOCK), F32)
    for g in range(GQA):
        sink = jnp.where(_head_lanes(g), sinks_ref[kh * GQA + g], sink)
    m = jnp.maximum(jnp.max(s, axis=0, keepdims=True), sink)
    p = jnp.exp(s - m)
    es = jnp.exp(sink - m)
    inv = 1.0 / (jnp.sum(p, axis=0, keepdims=True) + es)
    return dict(qhat=qhat, rq=rq, qn=qn, pn=p * inv, psink=es * inv)


def _split(pb, tri):
    zero = jnp.zeros_like(pb)
    return jnp.where(tri, pb, zero), jnp.where(tri, zero, pb)


def _attn_fwd(zq, zkv, bias, sinks, qg, kg, og):
    T = zq.shape[0]
    rows = ATTN_SUB * BLOCK

    def body(q_ref, kvc_ref, kvp_ref, bias_ref, sinks_ref, qgt_ref, kg_ref, kgt_ref, og_ref, a_ref, mix_ref):
        n = pl.program_id(0)
        tri = _tri_t()
        qgt, kgv, kgt = qgt_ref[...], kg_ref[...], kgt_ref[...]
        kvs = [kvp_ref[...]] + [kvc_ref[i * BLOCK:(i + 1) * BLOCK, :] for i in range(ATTN_SUB)]
        keys = [[_kv_block(kv, kh, kgv, kgt) for kh in range(N_KV)] for kv in kvs]
        for i in range(ATTN_SUB):
            zq_t = q_ref[i * BLOCK:(i + 1) * BLOCK, :].T
            has_prev = (n > 0) if i == 0 else None
            outs = []
            for kh in range(N_KV):
                kp, kc = keys[i][kh], keys[i + 1][kh]
                c = _attn_group(zq_t, kp, kc, kh, bias_ref, sinks_ref, qgt, tri, has_prev)
                p_c, p_p = _split(c["pn"].astype(BF16), tri)
                o_t = _dot(kc["v_t"], p_c) + _dot(kp["v_t"], p_p)
                outs += [o_t[:, g * BLOCK:(g + 1) * BLOCK] for g in range(GQA)]
            a = jnp.concatenate(outs, axis=0).T
            a_ref[i * BLOCK:(i + 1) * BLOCK, :] = a
            ah, _ = _rms(a)
            mix_ref[i * BLOCK:(i + 1) * BLOCK, :] = (ah * og_ref[...]).astype(BF16)

    return _call(
        body, name="attn_fwd", grid=(T // rows,),
        in_specs=[_rows(rows, ATTN_W), _rows(rows, 2 * KV_W),
                  pl.BlockSpec((BLOCK, 2 * KV_W), lambda n: (jnp.maximum(n * ATTN_SUB - 1, 0), 0)),
                  RESIDENT, IN_SMEM, RESIDENT, RESIDENT, RESIDENT, RESIDENT],
        out_specs=[_rows(rows, ATTN_W), _rows(rows, ATTN_W)],
        out_shape=[jax.ShapeDtypeStruct((T, ATTN_W), F32), jax.ShapeDtypeStruct((T, ATTN_W), BF16)],
        compiler_params=_params("parallel"),
    )(zq, zkv, zkv, bias, sinks, qg.reshape(HEAD_DIM, 1), kg, kg.reshape(HEAD_DIM, 1), og)


def _conv_post(y, lg, lb, og):
    mu = jnp.mean(y, axis=-1, keepdims=True)
    yc = y - mu
    rstd = lax.rsqrt(jnp.mean(yc * yc, axis=-1, keepdims=True) + EPS)
    yn = yc * rstd
    ln = yn * lg + lb
    sg = _sig(ln)
    c = ln * sg
    ch, r = _rms(c)
    return yn, rstd, ln, sg, ch, r


CONV_CHUNK = 64


SLAB_ROWS = CONV_CHUNK + 8 * ((CONV_K - 1) // 8)


def _shifted_taps(buf, slab, r0, base):
    for b in range(8):
        taps = range(b, CONV_K, 8)
        span = CONV_CHUNK + 8 * (len(taps) - 1)
        slab[0:span, :] = buf[r0 + base + b:r0 + base + b + span, :]
        for a, j in enumerate(taps):
            yield j, slab[8 * a:8 * a + CONV_CHUNK, :]


def _fold8(x):
    return jnp.sum(x.reshape(x.shape[0] // 8, 8, x.shape[1]), axis=0)


def _conv_fwd(zu, zg, cw, cb, lg, lb, og):
    T = zu.shape[0]
    tt = 512
    halo_spec = pl.BlockSpec((HALO, CONV_W), lambda i: (jnp.maximum(i * (tt // HALO) - 1, 0), 0))

    def body(u_ref, g_ref, uh_ref, gh_ref, cw_ref, cb_ref, lg_ref, lb_ref, og_ref, y_ref, mix_ref, buf, slab):
        i = pl.program_id(0)
        hal = uh_ref[...] * _sig(gh_ref[...])
        buf[0:HALO, :] = jnp.where(i > 0, hal, 0.0)
        buf[HALO:HALO + tt, :] = u_ref[...] * _sig(g_ref[...])
        cbv, lgv, lbv, ogv = cb_ref[...], lg_ref[...], lb_ref[...], og_ref[...]
        for r0 in range(0, tt, CONV_CHUNK):
            rs = slice(r0, r0 + CONV_CHUNK)
            acc = jnp.zeros((CONV_CHUNK, CONV_W), F32)
            for j, win in _shifted_taps(buf, slab, r0, HALO - (CONV_K - 1)):
                acc = acc + cw_ref[j:j + 1, :] * win
            y = acc + cbv
            y_ref[rs, :] = y
            mix_ref[rs, :] = (_conv_post(y, lgv, lbv, ogv)[4] * ogv).astype(BF16)

    return _call(
        body, name="conv_fwd", grid=(T // tt,),
        in_specs=[_rows(tt, CONV_W), _rows(tt, CONV_W), halo_spec, halo_spec] + [RESIDENT] * 5,
        out_specs=[_rows(tt, CONV_W), _rows(tt, CONV_W)],
        out_shape=[jax.ShapeDtypeStruct((T, CONV_W), F32), jax.ShapeDtypeStruct((T, CONV_W), BF16)],
        scratch_shapes=[pltpu.VMEM((HALO + tt, CONV_W), F32), pltpu.VMEM((SLAB_ROWS, CONV_W), F32)],
        compiler_params=_params("parallel"),
    )(zu, zg, zu, zg, cw, cb, lg, lb, og)


def _chunked(tm):
    return pl.BlockSpec((N_FF, tm, FF_CHUNK), lambda i: (0, i, 0))


def _fwd_out(x, mix_a, mix_c, w_out):
    T = x.shape[0]
    tm = 512

    def body(x_ref, a_ref, c_ref, w_ref, o_ref):
        o_ref[...] = (x_ref[...] + _dot(a_ref[...], w_ref[0:ATTN_W, :])
                      + _dot(c_ref[...], w_ref[ATTN_W:, :]))

    return _call(
        body, name="fwd_out", grid=(T // tm,),
        in_specs=[_rows(tm, D_MODEL), _rows(tm, ATTN_W), _rows(tm, CONV_W), RESIDENT],
        out_specs=_rows(tm, D_MODEL),
        out_shape=jax.ShapeDtypeStruct((T, D_MODEL), F32),
        compiler_params=_params("parallel"),
    )(x, mix_a, mix_c, w_out)


def _mlp_fwd(x, g, wup_t, wdown):
    T = x.shape[0]
    tm = 512

    def body(x_ref, g_ref, wu_ref, wd_ref, o_ref, up_ref):
        xv = x_ref[...]
        xh, _ = _rms(xv)
        h = (xh * g_ref[...]).astype(BF16)
        acc = xv
        for c in range(N_FF):
            rows = slice(c * FF_CHUNK, (c + 1) * FF_CHUNK)
            up = _dot_nt(h, wu_ref[rows, :])
            up_ref[c] = up.astype(BF16)
            act = jnp.square(jnp.maximum(up, 0.0))
            acc = acc + _dot(act.astype(BF16), wd_ref[rows, :])
        o_ref[...] = acc

    return _call(
        body, name="mlp_fwd", grid=(T // tm,),
        in_specs=[_rows(tm, D_MODEL), RESIDENT, RESIDENT, RESIDENT],
        out_specs=[_rows(tm, D_MODEL), _chunked(tm)],
        out_shape=[jax.ShapeDtypeStruct((T, D_MODEL), F32), jax.ShapeDtypeStruct((N_FF, T, FF_CHUNK), BF16)],
        compiler_params=_params("parallel"),
    )(x, g, wup_t, wdown)


def _loss_head(y, target):
    T = y.shape[0]
    tm = 512

    def body(y_ref, t_ref, l_ref, d_ref):
        @pl.when(pl.program_id(0) == 0)
        def _():
            l_ref[...] = jnp.zeros_like(l_ref)
        e = y_ref[...] - t_ref[...]
        d_ref[...] = e / D_MODEL
        l_ref[...] += 0.5 * jnp.sum(jnp.mean(e * e, axis=-1, keepdims=True))

    return _call(
        body, name="loss_head", grid=(T // tm,),
        in_specs=[_rows(tm, D_MODEL), _rows(tm, D_MODEL)],
        out_specs=[_acc_rows(128, 8), _rows(tm, D_MODEL)],
        out_shape=[jax.ShapeDtypeStruct((8, 128), F32), jax.ShapeDtypeStruct((T, D_MODEL), F32)],
        compiler_params=_params("arbitrary"),
    )(y, target)


def _bias_table(rel_bias, bucket):
    def body(rb_ref, bk_ref, o_ref):
        bk = bk_ref[...]
        for h in range(N_HEADS):
            acc = jnp.zeros((BLOCK, BLOCK), F32)
            for b in range(N_BUCKETS):
                acc = jnp.where(bk == b, rb_ref[b, h], acc)
            o_ref[h // GQA, :, (h % GQA) * BLOCK:(h % GQA + 1) * BLOCK] = acc

    return _call(
        body, name="bias_table", in_specs=[IN_SMEM, RESIDENT], out_specs=RESIDENT,
        out_shape=jax.ShapeDtypeStruct((N_KV, BLOCK, GQA * BLOCK), F32),
    )(rel_bias, bucket)


def _mlp_bwd(dx2, x1, up, g, wup_t, wdown):
    T = x1.shape[0]
    tm = 512

    def body(d_ref, x_ref, up_ref, g_ref, wu_ref, wd_ref, dx_ref, dup_ref, h_ref, dg_ref, db_ref):
        @pl.when(pl.program_id(0) == 0)
        def _():
            dg_ref[...] = jnp.zeros_like(dg_ref)
        d2 = d_ref[...]
        d2b = d2.astype(BF16)
        db_ref[...] = d2b
        xh, r = _rms(x_ref[...])
        gv = g_ref[...]
        h_ref[...] = (xh * gv).astype(BF16)
        dh = jnp.zeros((tm, D_MODEL), F32)
        for c in range(N_FF):
            rows = slice(c * FF_CHUNK, (c + 1) * FF_CHUNK)
            dact = _dot_nt(d2b, wd_ref[rows, :])
            dup = (dact * (2.0 * jnp.maximum(up_ref[c].astype(F32), 0.0))).astype(BF16)
            dup_ref[c] = dup
            dh = dh + _dot(dup, wu_ref[rows, :])
        dg_ref[...] += _colsum(dh * xh)
        dx_ref[...] = d2 + _rms_bwd(dh * gv, xh, r)

    return _call(
        body, name="mlp_bwd", grid=(T // tm,),
        in_specs=[_rows(tm, D_MODEL), _rows(tm, D_MODEL), _chunked(tm), RESIDENT, RESIDENT, RESIDENT],
        out_specs=[_rows(tm, D_MODEL), _chunked(tm), _rows(tm, D_MODEL), _acc_rows(D_MODEL), _rows(tm, D_MODEL)],
        out_shape=[jax.ShapeDtypeStruct((T, D_MODEL), F32), jax.ShapeDtypeStruct((N_FF, T, FF_CHUNK), BF16),
                   jax.ShapeDtypeStruct((T, D_MODEL), BF16), jax.ShapeDtypeStruct((1, D_MODEL), F32),
                   jax.ShapeDtypeStruct((T, D_MODEL), BF16)],
        compiler_params=_params("arbitrary"),
    )(dx2, x1, up, g, wup_t, wdown)


def _bwd_out(dx1, w_out):
    T = dx1.shape[0]
    tm = 512

    def body(d_ref, w_ref, da_ref, dc_ref, db_ref):
        db = d_ref[...].astype(BF16)
        db_ref[...] = db
        dm = _dot_nt(db, w_ref[...])
        da_ref[...] = dm[:, :ATTN_W]
        dc_ref[...] = dm[:, ATTN_W:]

    return _call(
        body, name="bwd_out", grid=(T // tm,),
        in_specs=[_rows(tm, D_MODEL), RESIDENT],
        out_specs=[_rows(tm, ATTN_W), _rows(tm, CONV_W), _rows(tm, D_MODEL)],
        out_shape=[jax.ShapeDtypeStruct((T, ATTN_W), F32), jax.ShapeDtypeStruct((T, CONV_W), F32),
                   jax.ShapeDtypeStruct((T, D_MODEL), BF16)],
        compiler_params=_params("parallel"),
    )(dx1, w_out)


def _wgrad(a, b, name, chunked=False, square_relu=False, tr=FF_CHUNK):
    if chunked:
        nr, T, tr = a.shape
    else:
        T, R = a.shape
        tr = min(R, tr)
        nr = R // tr
    tk = min(T, 4096)
    nk = T // tk

    def body(a_ref, b_ref, o_ref, acc):
        k = pl.program_id(1)

        @pl.when(k == 0)
        def _():
            acc[...] = jnp.zeros_like(acc)
        av = a_ref[...]
        if square_relu:
            av = jnp.square(jnp.maximum(av.astype(F32), 0.0))
        acc[...] += _dot_tn(av.astype(BF16), b_ref[...].astype(BF16))

        @pl.when(k == nk - 1)
        def _():
            o_ref[...] = acc[...].astype(BF16)

    a_spec = (pl.BlockSpec((None, tk, tr), lambda r, k: (r, k, 0)) if chunked
              else pl.BlockSpec((tk, tr), lambda r, k: (k, r)))
    return _call(
        body, name=name, grid=(nr, nk),
        in_specs=[a_spec, pl.BlockSpec((tk, D_MODEL), lambda r, k: (k, 0))],
        out_specs=pl.BlockSpec((tr, D_MODEL), lambda r, k: (r, 0)),
        out_shape=jax.ShapeDtypeStruct((nr * tr, D_MODEL), BF16),
        scratch_shapes=[pltpu.VMEM((tr, D_MODEL), F32)],
        compiler_params=_params("parallel", "arbitrary"),
    )(a, b)


def _conv_bwd(dmix, y, zu, zg, cw, lg, lb, og):
    T = y.shape[0]
    tt = 512
    nt = T // tt
    per = tt // HALO
    prev_spec = pl.BlockSpec((HALO, CONV_W), lambda i: (jnp.maximum(i * per - 1, 0), 0))
    next_spec = pl.BlockSpec((HALO, CONV_W), lambda i: (jnp.minimum((i + 1) * per, nt * per - 1), 0))

    def body(dm_ref, dmn_ref, y_ref, yn_ref, u_ref, g_ref, uh_ref, gh_ref, cw_ref, lg_ref, lb_ref, og_ref,
             du_ref, dg_ref, pg_ref, hbuf, dybuf, dwacc, pacc, slab):
        i = pl.program_id(0)

        @pl.when(i == 0)
        def _():
            dwacc[...] = jnp.zeros_like(dwacc)
            pacc[...] = jnp.zeros_like(pacc)
        lgv, lbv, ogv = lg_ref[...], lb_ref[...], og_ref[...]

        def chain(yv, dm):
            yn, rstd, ln, sg, ch, r = _conv_post(yv, lgv, lbv, ogv)
            dc = _rms_bwd(dm * ogv, ch, r)
            dln = dc * sg * (1.0 + ln * (1.0 - sg))
            dyn = dln * lgv
            dy = rstd * (dyn - jnp.mean(dyn, axis=-1, keepdims=True)
                         - yn * jnp.mean(dyn * yn, axis=-1, keepdims=True))
            return dy, dm * ch, dln * yn, dln

        hbuf[0:HALO, :] = jnp.where(i > 0, uh_ref[...] * _sig(gh_ref[...]), 0.0)
        for r0 in range(0, tt, CONV_CHUNK):
            rs = slice(r0, r0 + CONV_CHUNK)
            hbuf[HALO + r0:HALO + r0 + CONV_CHUNK, :] = u_ref[rs, :] * _sig(g_ref[rs, :])
            dy, p_og, p_lg, p_lb = chain(y_ref[rs, :], dm_ref[rs, :])
            dybuf[rs, :] = dy
            for k, part in enumerate((dy, p_lg, p_lb, p_og)):
                pacc[8 * k:8 * k + 8, :] += _fold8(part)
        dyh, _, _, _ = chain(yn_ref[...], dmn_ref[...])
        dybuf[tt:tt + HALO, :] = jnp.where(i < nt - 1, dyh, 0.0)
        for r0 in range(0, tt, CONV_CHUNK):
            rs = slice(r0, r0 + CONV_CHUNK)
            dy = dybuf[rs, :]
            dh = jnp.zeros((CONV_CHUNK, CONV_W), F32)
            for j, win in _shifted_taps(dybuf, slab, r0, 0):
                dh = dh + cw_ref[CONV_K - 1 - j:CONV_K - j, :] * win
            for j, win in _shifted_taps(hbuf, slab, r0, HALO - (CONV_K - 1)):
                dwacc[8 * j:8 * j + 8, :] += _fold8(dy * win)
            sgt = _sig(g_ref[rs, :])
            du_ref[rs, :] = (dh * sgt).astype(BF16)
            dg_ref[rs, :] = (dh * u_ref[rs, :] * sgt * (1.0 - sgt)).astype(BF16)

        @pl.when(i == nt - 1)
        def _():
            pg_ref[...] = jnp.zeros_like(pg_ref)
            for j in range(CONV_K):
                pg_ref[j:j + 1, :] = _colsum(dwacc[8 * j:8 * j + 8, :])
            for k in range(4):
                pg_ref[32 + k:33 + k, :] = _colsum(pacc[8 * k:8 * k + 8, :])

    return _call(
        body, name="conv_bwd", grid=(nt,),
        in_specs=[_rows(tt, CONV_W), next_spec, _rows(tt, CONV_W), next_spec, _rows(tt, CONV_W), _rows(tt, CONV_W),
                  prev_spec, prev_spec] + [RESIDENT] * 4,
        out_specs=[_rows(tt, CONV_W), _rows(tt, CONV_W), _acc_rows(CONV_W, 40)],
        out_shape=[jax.ShapeDtypeStruct((T, CONV_W), BF16), jax.ShapeDtypeStruct((T, CONV_W), BF16),
                   jax.ShapeDtypeStruct((40, CONV_W), F32)],
        scratch_shapes=[pltpu.VMEM((HALO + tt, CONV_W), F32), pltpu.VMEM((tt + HALO, CONV_W), F32),
                        pltpu.VMEM((8 * HALO, CONV_W), F32), pltpu.VMEM((32, CONV_W), F32),
                        pltpu.VMEM((SLAB_ROWS, CONV_W), F32)],
        compiler_params=_params("arbitrary"),
    )(dmix, dmix, y, y, zu, zg, zu, zg, cw, lg, lb, og)


def _attn_bwd(dmix, a, zq, zkv, bias, sinks, qg, kg, og):
    T = zq.shape[0]
    rows = ATTN_SUB * BLOCK
    ns = T // rows
    nb = T // BLOCK
    cur = lambda w: pl.BlockSpec((rows, w), lambda n: (jnp.minimum(n, ns - 1), 0))
    scale = 1.0 / math.sqrt(HEAD_DIM)
    done = rows - BLOCK

    def body(dm_ref, a_ref, q_ref, kvc_ref, kvp_ref, bias_ref, sinks_ref, qgt_ref, kg_ref, kgt_ref, og_ref,
             dq_ref, dkv_ref, db_ref, dog_ref, dqg_ref, dkg_ref, dsk_ref, carry):
        n = pl.program_id(0)

        @pl.when(n == 0)
        def _():
            for ref in (db_ref, dog_ref, dqg_ref, dkg_ref, dsk_ref, carry):
                ref[...] = jnp.zeros_like(ref)

        @pl.when(n < ns)
        def _():
            tri = _tri_t()
            ogv, qgt, kgv, kgt = og_ref[...], qgt_ref[...], kg_ref[...], kgt_ref[...]
            lane = lax.broadcasted_iota(jnp.int32, (1, 128), 1)
            kvs = [kvp_ref[...]] + [kvc_ref[i * BLOCK:(i + 1) * BLOCK, :] for i in range(ATTN_SUB)]
            keys = [[_kv_block(kv, kh, kgv, kgt) for kh in range(N_KV)] for kv in kvs]
            dkn = [[jnp.zeros((BLOCK, HEAD_DIM), F32)] * N_KV for _ in kvs]
            dv = [[jnp.zeros((BLOCK, HEAD_DIM), F32)] * N_KV for _ in kvs]
            dsk = jnp.zeros((1, 128), F32)
            for i in range(ATTN_SUB):
                blk = slice(i * BLOCK, (i + 1) * BLOCK)
                ah, ra = _rms(a_ref[blk, :])
                dm = dm_ref[blk, :]
                dog_ref[...] += _colsum(dm * ah)
                da_t = _rms_bwd(dm * ogv, ah, ra).T
                zq_t = q_ref[blk, :].T
                has_prev = (n > 0) if i == 0 else None
                dqs = []
                for kh in range(N_KV):
                    kp, kc = keys[i][kh], keys[i + 1][kh]
                    c = _attn_group(zq_t, kp, kc, kh, bias_ref, sinks_ref, qgt, tri, has_prev)
                    dob = _heads_side_by_side(da_t, kh).astype(BF16)
                    pn = c["pn"]
                    p_c, p_p = _split(pn.astype(BF16), tri)
                    dv[i + 1][kh] = dv[i + 1][kh] + _dot_nt(p_c, dob)
                    dv[i][kh] = dv[i][kh] + _dot_nt(p_p, dob)
                    dp = jnp.where(tri, _dot(kc["v"], dob), _dot(kp["v"], dob))
                    dl = jnp.sum(pn * dp, axis=0, keepdims=True)
                    ds = pn * (dp - dl)
                    dsr = -c["psink"] * dl
                    for g in range(GQA):
                        dsk = dsk + jnp.where(lane == kh * GQA + g, jnp.sum(jnp.where(_head_lanes(g), dsr, 0.0)), 0.0)
                    db_ref[kh] += ds
                    ds_c, ds_p = _split((ds * scale).astype(BF16), tri)
                    dqn = _dot(kc["kn_t"], ds_c) + _dot(kp["kn_t"], ds_p)
                    dkn[i + 1][kh] = dkn[i + 1][kh] + _dot_nt(ds_c, c["qn"])
                    dkn[i][kh] = dkn[i][kh] + _dot_nt(ds_p, c["qn"])
                    dqg_ref[...] += jnp.sum(dqn * c["qhat"], axis=1, keepdims=True)
                    dq_t = _rms0_bwd(dqn * qgt, c["qhat"], c["rq"])
                    dqs += [dq_t[:, g * BLOCK:(g + 1) * BLOCK] for g in range(GQA)]
                dq_ref[blk, :] = jnp.concatenate(dqs, axis=0).T.astype(BF16)
            dsk_ref[...] += dsk
            dkv = []
            for j in range(ATTN_SUB + 1):
                dk = []
                for kh in range(N_KV):
                    key = keys[j][kh]
                    dkg_ref[...] += _colsum(dkn[j][kh] * key["khat"])
                    dk.append(_rms_bwd(dkn[j][kh] * kgv, key["khat"], key["rk"]))
                dkv.append(jnp.concatenate(dk + dv[j], axis=-1))
            if done:
                dkv_ref[0:done, :] = carry[0:done, :].astype(BF16)
            dkv_ref[done:rows, :] = (carry[done:rows, :] + dkv[0]).astype(BF16)
            for j in range(1, ATTN_SUB + 1):
                carry[(j - 1) * BLOCK:j * BLOCK, :] = dkv[j]

        @pl.when(n == ns)
        def _():
            dkv_ref[...] = carry[...].astype(BF16)

    small = lambda w: pl.BlockSpec((1, w), lambda n: (0, 0))
    return _call(
        body, name="attn_bwd", grid=(ns + 1,),
        in_specs=[cur(ATTN_W), cur(ATTN_W), cur(ATTN_W), cur(2 * KV_W),
                  pl.BlockSpec((BLOCK, 2 * KV_W), lambda n: (jnp.clip(n * ATTN_SUB - 1, 0, nb - 1), 0)),
                  RESIDENT, IN_SMEM, RESIDENT, RESIDENT, RESIDENT, RESIDENT],
        out_specs=[cur(ATTN_W), pl.BlockSpec((rows, 2 * KV_W), lambda n: (jnp.maximum(n - 1, 0), 0)),
                   pl.BlockSpec((N_KV, BLOCK, GQA * BLOCK), lambda n: (0, 0, 0)),
                   small(ATTN_W), pl.BlockSpec((HEAD_DIM, 1), lambda n: (0, 0)), small(HEAD_DIM), small(128)],
        out_shape=[jax.ShapeDtypeStruct((T, ATTN_W), BF16), jax.ShapeDtypeStruct((T, 2 * KV_W), BF16),
                   jax.ShapeDtypeStruct((N_KV, BLOCK, GQA * BLOCK), F32),
                   jax.ShapeDtypeStruct((1, ATTN_W), F32), jax.ShapeDtypeStruct((HEAD_DIM, 1), F32),
                   jax.ShapeDtypeStruct((1, HEAD_DIM), F32), jax.ShapeDtypeStruct((1, 128), F32)],
        scratch_shapes=[pltpu.VMEM((rows, 2 * KV_W), F32)],
        compiler_params=_params("arbitrary"),
    )(dmix, a, zq, zkv, zkv, bias, sinks, qg.reshape(HEAD_DIM, 1), kg, kg.reshape(HEAD_DIM, 1), og)


def _bwd_in(dx1, x, dq, dkv, du, dgt, g, wt):
    T = x.shape[0]
    tm = 512

    def body(d1_ref, x_ref, dq_ref, dkv_ref, du_ref, dgt_ref, g_ref, w_ref, dx_ref, h_ref, dg_ref, dz_ref):
        @pl.when(pl.program_id(0) == 0)
        def _():
            dg_ref[...] = jnp.zeros_like(dg_ref)
        xh, r = _rms(x_ref[...])
        gv = g_ref[...]
        h_ref[...] = (xh * gv).astype(BF16)
        dz = jnp.concatenate([dq_ref[...], dkv_ref[...], du_ref[...], dgt_ref[...]], axis=-1)
        dz_ref[...] = dz
        dh = _dot(dz, w_ref[...])
        dg_ref[...] += _colsum(dh * xh)
        dx_ref[...] = d1_ref[...] + _rms_bwd(dh * gv, xh, r)

    return _call(
        body, name="bwd_in", grid=(T // tm,),
        in_specs=[_rows(tm, D_MODEL), _rows(tm, D_MODEL), _rows(tm, ATTN_W), _rows(tm, 2 * KV_W),
                  _rows(tm, CONV_W), _rows(tm, CONV_W), RESIDENT, RESIDENT],
        out_specs=[_rows(tm, D_MODEL), _rows(tm, D_MODEL), _acc_rows(D_MODEL), _rows(tm, IN_W)],
        out_shape=[jax.ShapeDtypeStruct((T, D_MODEL), F32), jax.ShapeDtypeStruct((T, D_MODEL), BF16),
                   jax.ShapeDtypeStruct((1, D_MODEL), F32), jax.ShapeDtypeStruct((T, IN_W), BF16)],
        compiler_params=_params("arbitrary"),
    )(dx1, x, dq, dkv, du, dgt, g, wt)


def _bias_grad(db, bucket):
    def body(db_ref, bk_ref, o_ref):
        bk = bk_ref[...]
        lane = lax.broadcasted_iota(jnp.int32, (1, 128), 1)
        for h in range(N_HEADS):
            cols = slice((h % GQA) * BLOCK, (h % GQA + 1) * BLOCK)
            tot = db_ref[0, h // GQA, :, cols]
            for l in range(1, DEPTH):
                tot = tot + db_ref[l, h // GQA, :, cols]
            out = jnp.zeros((1, 128), F32)
            for b in range(N_BUCKETS):
                out = jnp.where(lane == b, jnp.sum(jnp.where(bk == b, tot, 0.0)), out)
            o_ref[h:h + 1, :] = out

    return _call(
        body, name="bias_grad", in_specs=[RESIDENT, RESIDENT], out_specs=RESIDENT,
        out_shape=jax.ShapeDtypeStruct((N_HEADS, 128), F32),
        compiler_params=pltpu.CompilerParams(vmem_limit_bytes=VMEM_LIMIT),
    )(db, bucket)


def _place():
    return lax.axis_index("x"), lax.axis_index("y"), lax.axis_index("c")


def _all_gather(shards):
    na = len(shards)

    def body(*refs):
        ins, outs = refs[:na], refs[na:2 * na]
        send_sems, recv_sems, local_sems = refs[2 * na:]
        x, y, c = _place()
        me, sibling = (x, y, c), (x, y, 1 - c)
        chips = [(1 - x, y), (x, 1 - y), (1 - x, 1 - y)]

        def rows(a, p):
            r = ins[a].shape[1]
            return outs[a].at[:, pl.ds((4 * p[0] + 2 * p[1] + p[2]) * r, r), :]

        def copy(a, k, block, to, src=None):
            return pltpu.make_async_remote_copy(
                src_ref=rows(a, block) if src is None else src, dst_ref=rows(a, block),
                send_sem=send_sems.at[a, k], recv_sem=recv_sems.at[a, k], device_id=to, device_id_type=MESH)

        mine = [pltpu.make_async_copy(ins[a], rows(a, me), local_sems.at[a]) for a in range(na)]
        for cp in mine:
            cp.start()
        first = []
        for a in range(na):
            first.append(copy(a, 0, me, sibling, src=ins[a]))
            first += [copy(a, 1 + j, me, (*chip, c), src=ins[a]) for j, chip in enumerate(chips)]
        for cp in first:
            cp.start()
        passed = []
        for j, chip in enumerate(chips):
            for a in range(na):
                copy(a, 1 + j, (*chip, c), me).wait_recv()
                cp = copy(a, 4 + j, (*chip, c), sibling)
                cp.start()
                passed.append(cp)
        for a in range(na):
            copy(a, 0, sibling, me).wait_recv()
            for j, chip in enumerate(chips):
                copy(a, 4 + j, (*chip, 1 - c), me).wait_recv()
        for cp in first + passed:
            cp.wait_send()
        for cp in mine:
            cp.wait()

    return _call(
        body, name="all_gather",
        in_specs=[ANY] * na, out_specs=[ANY] * na,
        out_shape=[jax.ShapeDtypeStruct((s.shape[0], N_DEV * s.shape[1], s.shape[2]), s.dtype) for s in shards],
        scratch_shapes=[pltpu.SemaphoreType.DMA((na, 7)), pltpu.SemaphoreType.DMA((na, 7)),
                        pltpu.SemaphoreType.DMA((na,))],
    )(*shards)


IN_HBM = pl.BlockSpec(memory_space=pltpu.HBM)
IN_SEM = pl.BlockSpec(memory_space=pltpu.SEMAPHORE)
DATAFLOW = pltpu.SideEffectType.DATAFLOW_SIDE_EFFECTING


def _exchange_copies(srcs, lands, send_sems, recv_sems, scatter):
    x, y, c = _place()
    me = 4 * x + 2 * y + c
    copies = []
    for k in range(1, N_DEV):
        p = (x ^ (k >> 2), y ^ ((k >> 1) & 1), c ^ (k & 1))
        for a, (src, land) in enumerate(zip(srcs, lands)):
            r = land.shape[0] // N_DEV
            if scatter:
                src = src.at[pl.ds((4 * p[0] + 2 * p[1] + p[2]) * r, r), :]
            copies.append(pltpu.make_async_remote_copy(
                src_ref=src, dst_ref=land.at[pl.ds(me * r, r), :], send_sem=send_sems.at[a * (N_DEV - 1) + k - 1],
                recv_sem=recv_sems.at[a * (N_DEV - 1) + k - 1], device_id=p, device_id_type=MESH))
    return copies


def _own_copies(srcs, lands, send_sems, scatter):
    x, y, c = _place()
    me = 4 * x + 2 * y + c
    copies = []
    for a, (src, land) in enumerate(zip(srcs, lands)):
        r = land.shape[0] // N_DEV
        if scatter:
            src = src.at[pl.ds(me * r, r), :]
        copies.append(pltpu.make_async_copy(src, land.at[pl.ds(me * r, r), :],
                                            send_sems.at[len(srcs) * (N_DEV - 1) + a]))
    return copies


def _exchange_start(srcs, name, scatter):
    na = len(srcs)
    lands = [lax.empty((s.shape[0] * (1 if scatter else N_DEV), s.shape[1]), s.dtype) for s in srcs]

    def body(*refs):
        ins, lnd = refs[:na], refs[na:2 * na]
        send_sems, recv_sems = refs[2 * na], refs[2 * na + 1]
        token = refs[-1]
        for cp in _exchange_copies(ins, lnd, send_sems, recv_sems, scatter) + _own_copies(ins, lnd, send_sems, scatter):
            cp.start()
        token[...] = jnp.zeros_like(token)

    hbm = lambda a: pltpu.with_memory_space_constraint(a, pltpu.HBM)
    out = _call(
        body, name=name,
        out_shape=(pltpu.SemaphoreType.DMA((na * N_DEV,)), pltpu.SemaphoreType.DMA((na * (N_DEV - 1),)),
                   *[pltpu.HBM(a.shape, a.dtype) for a in (*srcs, *lands)], jax.ShapeDtypeStruct((8, 128), F32)),
        in_specs=[IN_HBM] * (2 * na),
        out_specs=(IN_SEM, IN_SEM, *[IN_HBM] * (2 * na), RESIDENT),
        input_output_aliases={i: 2 + i for i in range(2 * na)},
        compiler_params=pltpu.CompilerParams(has_side_effects=DATAFLOW),
    )(*[hbm(a) for a in (*srcs, *lands)])
    return (out[0], out[1], out[2:2 + na], out[2 + na:2 + 2 * na]), out[-1]


def _exchange_wait(state, after, name, scatter):
    send_sems, recv_sems, srcs, lands = state
    na = len(srcs)

    def body(*refs):
        ins, lnd = refs[:na], refs[na:2 * na]
        for cp in _exchange_copies(ins, lnd, refs[2 * na], refs[2 * na + 1], scatter):
            cp.wait_send()
            cp.wait_recv()
        for cp in _own_copies(ins, lnd, refs[2 * na], scatter):
            cp.wait()
        refs[-1][...] = jnp.zeros_like(refs[-1])

    out = _call(
        body, name=name,
        out_shape=(*[pltpu.HBM(a.shape, a.dtype) for a in (*srcs, *lands)], jax.ShapeDtypeStruct((8, 128), F32)),
        in_specs=[IN_HBM] * (2 * na) + [IN_SEM, IN_SEM, ANY],
        out_specs=(*[IN_HBM] * (2 * na), RESIDENT),
        input_output_aliases={i: i for i in range(2 * na)},
        compiler_params=pltpu.CompilerParams(has_side_effects=DATAFLOW),
    )(*srcs, *lands, send_sems, recv_sems, after)
    return out[:na], out[na:2 * na], out[-1]


def _adam_math(w, g, m, v):
    m = ADAM_B1 * m + (1.0 - ADAM_B1) * g
    v = ADAM_B2 * v + (1.0 - ADAM_B2) * jnp.square(g)
    m_hat = m / (1.0 - ADAM_B1 ** ADAM_STEP)
    v_hat = v / (1.0 - ADAM_B2 ** ADAM_STEP)
    delta = -ADAM_LR * (m_hat / (jnp.sqrt(v_hat) + ADAM_EPS) + ADAM_WD * w)
    return delta, m, v


def _adam_big(parts, w, m, v):
    nl, r, cdim = w.shape
    tr = r if r <= 256 else 256
    ni = r // tr
    spec = pl.BlockSpec((None, tr, cdim), lambda l, i: (l, i, 0))

    def part_spec(layer):
        return pl.BlockSpec((N_DEV, tr, cdim),
                            lambda l, i: (0, jnp.where(l == layer, i, jnp.where(l < layer, 0, ni - 1)), 0))

    def body(*refs):
        p_refs = refs[:nl]
        w_ref, m_ref, v_ref, g_ref, d_ref, nm_ref, nv_ref = refs[nl:]
        for layer in range(nl):
            @pl.when(pl.program_id(0) == layer)
            def _(p_ref=p_refs[layer]):
                g = p_ref[0].astype(F32)
                for s in range(1, N_DEV):
                    g = g + p_ref[s].astype(F32)
                d, nm, nv = _adam_math(w_ref[...], g, m_ref[...], v_ref[...])
                g_ref[...] = g
                d_ref[...] = d
                nm_ref[...] = nm
                nv_ref[...] = nv

    return _call(
        body, name="adam_big", grid=(nl, ni),
        in_specs=[part_spec(layer) for layer in range(nl)] + [spec, spec, spec],
        out_specs=[spec] * 4,
        out_shape=[jax.ShapeDtypeStruct(w.shape, F32)] * 4,
        compiler_params=_params("arbitrary", "arbitrary"),
    )(*parts, w, m, v)


def _sum_slots(parts):
    def body(p_ref, o_ref):
        g = p_ref[0]
        for s in range(1, N_DEV):
            g = g + p_ref[s]
        o_ref[...] = g

    return _call(body, name="sum_slots", in_specs=[RESIDENT], out_specs=RESIDENT,
                 out_shape=jax.ShapeDtypeStruct(parts.shape[1:], F32))(parts)


def _adam_small(ws, gs, ms, vs):
    n = len(ws)

    def body(*refs):
        ins, outs = refs[:4 * n], refs[4 * n:]
        for k in range(n):
            d, nm, nv = _adam_math(ins[k][...], ins[n + k][...], ins[2 * n + k][...], ins[3 * n + k][...])
            outs[k][...] = d
            outs[n + k][...] = nm
            outs[2 * n + k][...] = nv

    out = _call(body, name="adam_small", in_specs=[RESIDENT] * (4 * n), out_specs=[RESIDENT] * (3 * n),
                out_shape=[jax.ShapeDtypeStruct(w.shape, F32) for w in ws] * 3)(*ws, *gs, *ms, *vs)
    return out[:n], out[n:2 * n], out[2 * n:]


def _pack(arrays):
    parts = []
    for a in arrays:
        flat = a.reshape(-1)
        n = flat.shape[0]
        padded = -(-n // 1024) * 1024
        parts.append(jnp.pad(flat, (0, padded - n)).reshape(padded // 128, 128))
    return jnp.concatenate(parts, axis=0)


def _unpack(packed, shapes):
    out, row = [], 0
    for shp in shapes:
        n = int(np.prod(shp))
        rows = -(-n // 1024) * 8
        out.append(packed[row:row + rows].reshape(-1)[:n].reshape(shp))
        row += rows
    return out


def _t5_bucket_table():
    kj = np.arange(BLOCK)[:, None]
    qi = np.arange(BLOCK)[None, :]
    n = (qi - kj) % BLOCK
    max_exact = N_BUCKETS // 2
    large = max_exact + (np.log(np.maximum(n, 1) / max_exact) / np.log(MAX_DIST / max_exact)
                         * (N_BUCKETS - max_exact)).astype(np.int32)
    large = np.minimum(large, N_BUCKETS - 1)
    return np.where(n < max_exact, n, large).astype(np.int32)


SMALL_NAMES = ("rel_bias", "norm_mix_g", "q_norm_g", "k_norm_g", "sinks", "conv_b", "conv_ln_g", "conv_ln_b",
               "attn_out_g", "conv_out_g", "norm_mlp_g")


def _local_step(x, target, small, conv_w, get_w, put_g, token):
    bucket = jnp.asarray(_t5_bucket_table())
    bias = _bias_table(small["rel_bias"], bucket)
    row = lambda a, l: a[l][None, :]
    cw_pad = jnp.pad(conv_w, ((0, 0), (0, HALO - CONV_K), (0, 0)))
    saved, weights = [], []
    for l in range(DEPTH):
        wt_in, w_out, tok = get_w(l, 0, x)
        zq, zkv, zu, zg = _fwd_in(x, row(small["norm_mix_g"], l) + tok + (token if l == 0 else 0.0), wt_in)
        a, mix_a = _attn_fwd(zq, zkv, bias, small["sinks"][l], row(small["q_norm_g"], l), row(small["k_norm_g"], l),
                             row(small["attn_out_g"], l))
        y, mix_c = _conv_fwd(zu, zg, cw_pad[l], row(small["conv_b"], l), row(small["conv_ln_g"], l),
                             row(small["conv_ln_b"], l), row(small["conv_out_g"], l))
        x1 = _fwd_out(x, mix_a, mix_c, w_out)
        wt_up, w_down, tok = get_w(l, 1, x1)
        weights.append((wt_in, w_out, wt_up, w_down))
        x2, up = _mlp_fwd(x1, row(small["norm_mlp_g"], l) + tok, wt_up, w_down)
        saved.append((x, zq, zkv, zu, zg, a, mix_a, y, mix_c, x1, up))
        x = x2
    loss_part, dx = _loss_head(x, target)

    gs = {n: [None] * DEPTH for n in SMALL_NAMES if n != "rel_bias"}
    g_conv_w, dbias = [None] * DEPTH, [None] * DEPTH
    token = 0.0
    for l in reversed(range(DEPTH)):
        x0, zq, zkv, zu, zg, a, mix_a, y, mix_c, x1, up = saved[l]
        wt_in, w_out, wt_up, w_down = weights[l]
        dx1, dup, h2, gs["norm_mlp_g"][l], dxb = _mlp_bwd(dx, x1, up, row(small["norm_mlp_g"], l) + token, wt_up, w_down)
        g_up = _wgrad(dup, h2, "wgrad_up", chunked=True)
        g_down = _wgrad(up, dxb, "wgrad_down", chunked=True, square_relu=True)
        token = put_g(l, 1, (g_up, g_down))
        dma, dmc, dx1b = _bwd_out(dx1, w_out)
        g_out = jnp.concatenate([_wgrad(mix_a, dx1b, "wgrad_out_a"), _wgrad(mix_c, dx1b, "wgrad_out_c")], axis=0)
        du, dgt, pg = _conv_bwd(dmc, y, zu, zg, cw_pad[l], row(small["conv_ln_g"], l) + token,
                                row(small["conv_ln_b"], l), row(small["conv_out_g"], l))
        g_conv_w[l] = pg[:CONV_K]
        gs["conv_b"][l], gs["conv_ln_g"][l], gs["conv_ln_b"][l], gs["conv_out_g"][l] = pg[32], pg[33], pg[34], pg[35]
        dq, dkv, dbias[l], dog, dqg, dkg, dsk = _attn_bwd(
            dma, a, zq, zkv, bias, small["sinks"][l], row(small["q_norm_g"], l), row(small["k_norm_g"], l),
            row(small["attn_out_g"], l))
        gs["attn_out_g"][l], gs["q_norm_g"][l], gs["k_norm_g"][l], gs["sinks"][l] = dog, dqg, dkg, dsk[0, :N_HEADS]
        dx, h, gs["norm_mix_g"][l], dz = _bwd_in(dx1, x0, dq, dkv, du, dgt, row(small["norm_mix_g"], l), wt_in)
        g_in = _wgrad(dz, h, "wgrad_in", tr=IN_W // 2)
        token = put_g(l, 0, (g_in, g_out))
    small_grads = {n: jnp.stack([jnp.reshape(v, (-1,)) for v in vals]) for n, vals in gs.items()}
    small_grads["rel_bias"] = _bias_grad(jnp.stack(dbias), bucket)[:, :N_BUCKETS].T
    return loss_part, dx, small_grads, jnp.stack(g_conv_w)


def kernel(x, rel_bias, norm_mix_g, w_in, q_norm_g, k_norm_g, sinks, conv_w, conv_b, conv_ln_g, conv_ln_b, attn_out_g, conv_out_g, w_out, norm_mlp_g, w_mlp_up, w_mlp_down, loss_target, m_rel_bias, m_norm_mix_g, m_w_in, m_q_norm_g, m_k_norm_g, m_sinks, m_conv_w, m_conv_b, m_conv_ln_g, m_conv_ln_b, m_attn_out_g, m_conv_out_g, m_w_out, m_norm_mlp_g, m_w_mlp_up, m_w_mlp_down, v_rel_bias, v_norm_mix_g, v_w_in, v_q_norm_g, v_k_norm_g, v_sinks, v_conv_w, v_conv_b, v_conv_ln_g, v_conv_ln_b, v_attn_out_g, v_conv_out_g, v_w_out, v_norm_mlp_g, v_w_mlp_up, v_w_mlp_down):
    args = dict(locals())
    small = {n: args[n] for n in SMALL_NAMES}
    tr = lambda a: jnp.swapaxes(a, 1, 2)
    me = 4 * lax.axis_index("x") + 2 * lax.axis_index("y") + lax.axis_index("c")

    shards = (tr(w_in).astype(BF16), w_out.astype(BF16), tr(w_mlp_up).astype(BF16), w_mlp_down.astype(BF16))
    cw_sh = jnp.pad(tr(conv_w), ((0, 0), (0, 0), (0, HALO - CONV_K))).reshape(DEPTH * (CONV_W // N_DEV), HALO)
    gathers = {}

    def start_gather(l, half, dep):
        arrays = [s[l] for s in shards[2 * half:2 * half + 2]]
        if (l, half) == (0, 0):
            arrays = [cw_sh] + arrays
        if dep is not None:
            arrays = [a + dep[0, 0].astype(a.dtype) for a in arrays]
        gathers[l, half], tok = _exchange_start(arrays, f"gather_{l}{'ab'[half]}_start", scatter=False)
        return tok

    landed = {}

    def get_w(l, half, after):
        if (l, half) not in landed:
            _, lands, tok = _exchange_wait(gathers[l, half], after, f"gather_{l}{'ab'[half]}_wait", scatter=False)
            if l + 1 < DEPTH:
                tok = tok + start_gather(l + 1, half, tok)
            landed[l, half] = (*lands, tok[0, 0])
        return landed[l, half]

    token = start_gather(0, 0, None)
    token = token + start_gather(0, 1, token)
    cw_all, *first = get_w(0, 0, token)
    landed[0, 0] = tuple(first)
    conv_w_full = jnp.transpose(cw_all.reshape(N_DEV, DEPTH, CONV_W // N_DEV, HALO), (1, 3, 0, 2))
    conv_w_full = conv_w_full.reshape(DEPTH, HALO, CONV_W)[:, :CONV_K, :]

    scatters = {}

    last = {}

    def put_g(l, half, grads):
        scatters[l, half], last["token"] = _exchange_start(list(grads), f"scatter_{l}{'ab'[half]}_start", scatter=True)
        return last["token"][0, 0]

    loss_part, dx, small_grads, g_conv_w = _local_step(
        x[0], loss_target[0], small, conv_w_full, get_w, put_g, token[0, 0])

    big_w = (tr(w_in), w_out, tr(w_mlp_up), w_mlp_down)
    big_m = (tr(m_w_in), m_w_out, tr(m_w_mlp_up), m_w_mlp_down)
    big_v = (tr(v_w_in), v_w_out, tr(v_w_mlp_up), v_w_mlp_down)
    big_out, after = [None] * 4, last["token"]
    for half in (1, 0):
        parts = [[None] * DEPTH, [None] * DEPTH]
        for l in reversed(range(DEPTH)):
            _, lands, _ = _exchange_wait(scatters[l, half], after, f"scatter_{l}{'ab'[half]}_wait", scatter=True)
            for k, land in enumerate(lands):
                parts[k][l] = land.reshape(N_DEV, land.shape[0] // N_DEV, land.shape[1])
        for k in range(2):
            big_out[2 * half + k] = _adam_big(parts[k], big_w[2 * half + k], big_m[2 * half + k], big_v[2 * half + k])
        after = big_out[2 * half + 1][0]
    for k in (0, 2):
        big_out[k] = [tr(o) for o in big_out[k]]

    order = [n for n in SMALL_NAMES]
    packed = _pack([small_grads[n] for n in order] + [g_conv_w])
    slots, = _all_gather((packed[None],))
    summed = _sum_slots(slots[0].reshape(N_DEV, packed.shape[0], 128))
    shapes = [small[n].shape for n in order] + [(DEPTH, CONV_K, CONV_W)]
    sg = _unpack(summed, shapes)
    g_small = dict(zip(order, sg[:-1]))
    g_small["conv_w"] = lax.dynamic_slice_in_dim(sg[-1], me * (CONV_W // N_DEV), CONV_W // N_DEV, axis=2)
    names = order + ["conv_w"]
    deltas, new_m, new_v = _adam_small([args[n] for n in names], [g_small[n] for n in names],
                                       [args["m_" + n] for n in names], [args["v_" + n] for n in names])
    res = {"grad": g_small, "delta": dict(zip(names, deltas)), "new_m": dict(zip(names, new_m)),
           "new_v": dict(zip(names, new_v))}
    for k, n in enumerate(("w_in", "w_out", "w_mlp_up", "w_mlp_down")):
        for kind, val in zip(("grad", "delta", "new_m", "new_v"), big_out[k]):
            res[kind][n] = val

    loss = lax.psum(loss_part[0, 0], ("x", "y", "c"))
    weights = ("rel_bias", "norm_mix_g", "w_in", "q_norm_g", "k_norm_g", "sinks", "conv_w", "conv_b", "conv_ln_g",
               "conv_ln_b", "attn_out_g", "conv_out_g", "w_out", "norm_mlp_g", "w_mlp_up", "w_mlp_down")
    return (loss, dx[None], *[res[kind][n] for kind in ("grad", "delta", "new_m", "new_v") for n in weights])
```

```python
import math

import numpy as np
import jax
import jax.numpy as jnp
from jax import lax
from jax.experimental import pallas as pl
from jax.experimental.pallas import tpu as pltpu

F32, BF16 = jnp.float32, jnp.bfloat16
D_MODEL = 1024
DEPTH = 4
HEAD_DIM = 64
N_HEADS = 8
N_KV = 2
GQA = N_HEADS // N_KV
ATTN_W = N_HEADS * HEAD_DIM
KV_W = N_KV * HEAD_DIM
CONV_W = D_MODEL - ATTN_W
IN_W = ATTN_W + 2 * KV_W + 2 * CONV_W
BLOCK = 128
CONV_K = 31
HALO = 32
N_BUCKETS = 32
MAX_DIST = 128
D_FF = 4 * D_MODEL
FF_CHUNK = 512
N_FF = D_FF // FF_CHUNK
EPS = 1e-6
NEG = -1e30
N_DEV = 8
ADAM_LR, ADAM_B1, ADAM_B2, ADAM_EPS, ADAM_WD, ADAM_STEP = 0.001, 0.9, 0.999, 1e-08, 0.01, 10
VMEM_LIMIT = 56 * 1024 * 1024
MESH = pl.DeviceIdType.MESH

RESIDENT = pl.BlockSpec(memory_space=pltpu.VMEM)
IN_SMEM = pl.BlockSpec(memory_space=pltpu.SMEM)
ANY = pl.BlockSpec(memory_space=pl.ANY)


def _call(body, **kw):
    return pl.pallas_call(body, **kw)


def _params(*sem):
    return pltpu.CompilerParams(dimension_semantics=sem, vmem_limit_bytes=VMEM_LIMIT)


def _dot(a, b):
    return lax.dot_general(a, b, (((1,), (0,)), ((), ())), preferred_element_type=F32)


def _dot_nt(a, b):
    return lax.dot_general(a, b, (((1,), (1,)), ((), ())), preferred_element_type=F32)


def _dot_tn(a, b):
    return lax.dot_general(a, b, (((0,), (0,)), ((), ())), preferred_element_type=F32)


def _sig(x):
    return 1.0 / (1.0 + jnp.exp(-x))


def _rms(x):
    r = lax.rsqrt(jnp.mean(x * x, axis=-1, keepdims=True) + EPS)
    return x * r, r


def _rms_bwd(dy_g, xh, r):
    return r * (dy_g - xh * jnp.mean(dy_g * xh, axis=-1, keepdims=True))


def _rows(tm, w):
    return pl.BlockSpec((tm, w), lambda i: (i, 0))


def _acc_rows(w, rows=1):
    return pl.BlockSpec((rows, w), lambda i: (0, 0))


def _colsum(x):
    return jnp.sum(x, axis=0, keepdims=True)


def _fwd_in(x, g, wt):
    T = x.shape[0]
    tm = 512

    def body(x_ref, g_ref, w_ref, q_ref, kv_ref, u_ref, gt_ref):
        xh, _ = _rms(x_ref[...])
        h = (xh * g_ref[...]).astype(BF16)
        z = _dot_nt(h, w_ref[...])
        q_ref[...] = z[:, :ATTN_W]
        kv_ref[...] = z[:, ATTN_W:ATTN_W + 2 * KV_W]
        u_ref[...] = z[:, ATTN_W + 2 * KV_W:ATTN_W + 2 * KV_W + CONV_W]
        gt_ref[...] = z[:, ATTN_W + 2 * KV_W + CONV_W:]

    widths = (ATTN_W, 2 * KV_W, CONV_W, CONV_W)
    return _call(
        body, name="fwd_in", grid=(T // tm,),
        in_specs=[_rows(tm, D_MODEL), RESIDENT, RESIDENT],
        out_specs=[_rows(tm, w) for w in widths],
        out_shape=[jax.ShapeDtypeStruct((T, w), F32) for w in widths],
        compiler_params=_params("parallel"),
    )(x, g, wt)


ATTN_SUB = 8


def _rms0(x):
    r = lax.rsqrt(jnp.mean(x * x, axis=0, keepdims=True) + EPS)
    return x * r, r


def _rms0_bwd(dy_g, xh, r):
    return r * (dy_g - xh * jnp.mean(dy_g * xh, axis=0, keepdims=True))


def _tri_t():
    kj = lax.broadcasted_iota(jnp.int32, (BLOCK, BLOCK), 0)
    qi = lax.broadcasted_iota(jnp.int32, (BLOCK, BLOCK), 1)
    return jnp.concatenate([kj <= qi] * GQA, axis=1)


def _head_lanes(g):
    lane = lax.broadcasted_iota(jnp.int32, (1, GQA * BLOCK), 1)
    return (lane >= g * BLOCK) & (lane < (g + 1) * BLOCK)


def _heads_side_by_side(xt, kh):
    return jnp.concatenate(
        [xt[(kh * GQA + g) * HEAD_DIM:(kh * GQA + g + 1) * HEAD_DIM, :] for g in range(GQA)], axis=1)


def _kv_block(kv, kh, kg, kg_t):
    kvt = kv.T
    khat, rk = _rms(kv[:, kh * HEAD_DIM:(kh + 1) * HEAD_DIM])
    khat_t, _ = _rms0(kvt[kh * HEAD_DIM:(kh + 1) * HEAD_DIM, :])
    return dict(khat=khat, rk=rk, kn=(khat * kg).astype(BF16), kn_t=(khat_t * kg_t).astype(BF16),
                v=kv[:, KV_W + kh * HEAD_DIM:KV_W + (kh + 1) * HEAD_DIM].astype(BF16),
                v_t=kvt[KV_W + kh * HEAD_DIM:KV_W + (kh + 1) * HEAD_DIM, :].astype(BF16))


def _attn_group(zq_t, kp, kc, kh, bias_ref, sinks_ref, qg_t, tri, has_prev):
    scale = 1.0 / math.sqrt(HEAD_DIM)
    qhat, rq = _rms0(_heads_side_by_side(zq_t, kh))
    qf = qhat * qg_t
    qn = qf.astype(BF16)
    qs = (qf * scale).astype(BF16)
    s = jnp.where(tri, _dot(kc["kn"], qs), _dot(kp["kn"], qs)) + bias_ref[kh]
    if has_prev is not None:
        s = jnp.where(tri | has_prev, s, NEG)
    sink = jnp.zeros((1, GQA * BLOCK), F32)
    for g in range(GQA):
        sink = jnp.where(_head_lanes(g), sinks_ref[kh * GQA + g], sink)
    m = jnp.maximum(jnp.max(s, axis=0, keepdims=True), sink)
    p = jnp.exp(s - m)
    es = jnp.exp(sink - m)
    inv = 1.0 / (jnp.sum(p, axis=0, keepdims=True) + es)
    return dict(qhat=qhat, rq=rq, qn=qn, pn=p * inv, psink=es * inv)


def _split(pb, tri):
    zero = jnp.zeros_like(pb)
    return jnp.where(tri, pb, zero), jnp.where(tri, zero, pb)


def _attn_fwd(zq, zkv, bias, sinks, qg, kg, og):
    T = zq.shape[0]
    rows = ATTN_SUB * BLOCK

    def body(q_ref, kvc_ref, kvp_ref, bias_ref, sinks_ref, qgt_ref, kg_ref, kgt_ref, og_ref, a_ref, mix_ref):
        n = pl.program_id(0)
        tri = _tri_t()
        qgt, kgv, kgt = qgt_ref[...], kg_ref[...], kgt_ref[...]
        kvs = [kvp_ref[...]] + [kvc_ref[i * BLOCK:(i + 1) * BLOCK, :] for i in range(ATTN_SUB)]
        keys = [[_kv_block(kv, kh, kgv, kgt) for kh in range(N_KV)] for kv in kvs]
        for i in range(ATTN_SUB):
            zq_t = q_ref[i * BLOCK:(i + 1) * BLOCK, :].T
            has_prev = (n > 0) if i == 0 else None
            outs = []
            for kh in range(N_KV):
                kp, kc = keys[i][kh], keys[i + 1][kh]
                c = _attn_group(zq_t, kp, kc, kh, bias_ref, sinks_ref, qgt, tri, has_prev)
                p_c, p_p = _split(c["pn"].astype(BF16), tri)
                o_t = _dot(kc["v_t"], p_c) + _dot(kp["v_t"], p_p)
                outs += [o_t[:, g * BLOCK:(g + 1) * BLOCK] for g in range(GQA)]
            a = jnp.concatenate(outs, axis=0).T
            a_ref[i * BLOCK:(i + 1) * BLOCK, :] = a
            ah, _ = _rms(a)
            mix_ref[i * BLOCK:(i + 1) * BLOCK, :] = (ah * og_ref[...]).astype(BF16)

    return _call(
        body, name="attn_fwd", grid=(T // rows,),
        in_specs=[_rows(rows, ATTN_W), _rows(rows, 2 * KV_W),
                  pl.BlockSpec((BLOCK, 2 * KV_W), lambda n: (jnp.maximum(n * ATTN_SUB - 1, 0), 0)),
                  RESIDENT, IN_SMEM, RESIDENT, RESIDENT, RESIDENT, RESIDENT],
        out_specs=[_rows(rows, ATTN_W), _rows(rows, ATTN_W)],
        out_shape=[jax.ShapeDtypeStruct((T, ATTN_W), F32), jax.ShapeDtypeStruct((T, ATTN_W), BF16)],
        compiler_params=_params("parallel"),
    )(zq, zkv, zkv, bias, sinks, qg.reshape(HEAD_DIM, 1), kg, kg.reshape(HEAD_DIM, 1), og)


def _conv_post(y, lg, lb, og):
    mu = jnp.mean(y, axis=-1, keepdims=True)
    yc = y - mu
    rstd = lax.rsqrt(jnp.mean(yc * yc, axis=-1, keepdims=True) + EPS)
    yn = yc * rstd
    ln = yn * lg + lb
    sg = _sig(ln)
    c = ln * sg
    ch, r = _rms(c)
    return yn, rstd, ln, sg, ch, r


CONV_CHUNK = 64


SLAB_ROWS = CONV_CHUNK + 8 * ((CONV_K - 1) // 8)


def _shifted_taps(buf, slab, r0, base):
    for b in range(8):
        taps = range(b, CONV_K, 8)
        span = CONV_CHUNK + 8 * (len(taps) - 1)
        slab[0:span, :] = buf[r0 + base + b:r0 + base + b + span, :]
        for a, j in enumerate(taps):
            yield j, slab[8 * a:8 * a + CONV_CHUNK, :]


def _fold8(x):
    return jnp.sum(x.reshape(x.shape[0] // 8, 8, x.shape[1]), axis=0)


def _conv_fwd(zu, zg, cw, cb, lg, lb, og):
    T = zu.shape[0]
    tt = 512
    halo_spec = pl.BlockSpec((HALO, CONV_W), lambda i: (jnp.maximum(i * (tt // HALO) - 1, 0), 0))

    def body(u_ref, g_ref, uh_ref, gh_ref, cw_ref, cb_ref, lg_ref, lb_ref, og_ref, y_ref, mix_ref, buf, slab):
        i = pl.program_id(0)
        hal = uh_ref[...] * _sig(gh_ref[...])
        buf[0:HALO, :] = jnp.where(i > 0, hal, 0.0)
        buf[HALO:HALO + tt, :] = u_ref[...] * _sig(g_ref[...])
        cbv, lgv, lbv, ogv = cb_ref[...], lg_ref[...], lb_ref[...], og_ref[...]
        for r0 in range(0, tt, CONV_CHUNK):
            rs = slice(r0, r0 + CONV_CHUNK)
            acc = jnp.zeros((CONV_CHUNK, CONV_W), F32)
            for j, win in _shifted_taps(buf, slab, r0, HALO - (CONV_K - 1)):
                acc = acc + cw_ref[j:j + 1, :] * win
            y = acc + cbv
            y_ref[rs, :] = y
            mix_ref[rs, :] = (_conv_post(y, lgv, lbv, ogv)[4] * ogv).astype(BF16)

    return _call(
        body, name="conv_fwd", grid=(T // tt,),
        in_specs=[_rows(tt, CONV_W), _rows(tt, CONV_W), halo_spec, halo_spec] + [RESIDENT] * 5,
        out_specs=[_rows(tt, CONV_W), _rows(tt, CONV_W)],
        out_shape=[jax.ShapeDtypeStruct((T, CONV_W), F32), jax.ShapeDtypeStruct((T, CONV_W), BF16)],
        scratch_shapes=[pltpu.VMEM((HALO + tt, CONV_W), F32), pltpu.VMEM((SLAB_ROWS, CONV_W), F32)],
        compiler_params=_params("parallel"),
    )(zu, zg, zu, zg, cw, cb, lg, lb, og)


def _chunked(tm):
    return pl.BlockSpec((N_FF, tm, FF_CHUNK), lambda i: (0, i, 0))


def _fwd_out(x, mix_a, mix_c, w_out):
    T = x.shape[0]
    tm = 512

    def body(x_ref, a_ref, c_ref, w_ref, o_ref):
        o_ref[...] = (x_ref[...] + _dot(a_ref[...], w_ref[0:ATTN_W, :])
                      + _dot(c_ref[...], w_ref[ATTN_W:, :]))

    return _call(
        body, name="fwd_out", grid=(T // tm,),
        in_specs=[_rows(tm, D_MODEL), _rows(tm, ATTN_W), _rows(tm, CONV_W), RESIDENT],
        out_specs=_rows(tm, D_MODEL),
        out_shape=jax.ShapeDtypeStruct((T, D_MODEL), F32),
        compiler_params=_params("parallel"),
    )(x, mix_a, mix_c, w_out)


def _mlp_fwd(x, g, wup_t, wdown):
    T = x.shape[0]
    tm = 512

    def body(x_ref, g_ref, wu_ref, wd_ref, o_ref, up_ref):
        xv = x_ref[...]
        xh, _ = _rms(xv)
        h = (xh * g_ref[...]).astype(BF16)
        acc = xv
        for c in range(N_FF):
            rows = slice(c * FF_CHUNK, (c + 1) * FF_CHUNK)
            up = _dot_nt(h, wu_ref[rows, :])
            up_ref[c] = up.astype(BF16)
            act = jnp.square(jnp.maximum(up, 0.0))
            acc = acc + _dot(act.astype(BF16), wd_ref[rows, :])
        o_ref[...] = acc

    return _call(
        body, name="mlp_fwd", grid=(T // tm,),
        in_specs=[_rows(tm, D_MODEL), RESIDENT, RESIDENT, RESIDENT],
        out_specs=[_rows(tm, D_MODEL), _chunked(tm)],
        out_shape=[jax.ShapeDtypeStruct((T, D_MODEL), F32), jax.ShapeDtypeStruct((N_FF, T, FF_CHUNK), BF16)],
        compiler_params=_params("parallel"),
    )(x, g, wup_t, wdown)


def _loss_head(y, target):
    T = y.shape[0]
    tm = 512

    def body(y_ref, t_ref, l_ref, d_ref):
        @pl.when(pl.program_id(0) == 0)
        def _():
            l_ref[...] = jnp.zeros_like(l_ref)
        e = y_ref[...] - t_ref[...]
        d_ref[...] = e / D_MODEL
        l_ref[...] += 0.5 * jnp.sum(jnp.mean(e * e, axis=-1, keepdims=True))

    return _call(
        body, name="loss_head", grid=(T // tm,),
        in_specs=[_rows(tm, D_MODEL), _rows(tm, D_MODEL)],
        out_specs=[_acc_rows(128, 8), _rows(tm, D_MODEL)],
        out_shape=[jax.ShapeDtypeStruct((8, 128), F32), jax.ShapeDtypeStruct((T, D_MODEL), F32)],
        compiler_params=_params("arbitrary"),
    )(y, target)


def _bias_table(rel_bias, bucket):
    def body(rb_ref, bk_ref, o_ref):
        bk = bk_ref[...]
        for h in range(N_HEADS):
            acc = jnp.zeros((BLOCK, BLOCK), F32)
            for b in range(N_BUCKETS):
                acc = jnp.where(bk == b, rb_ref[b, h], acc)
            o_ref[h // GQA, :, (h % GQA) * BLOCK:(h % GQA + 1) * BLOCK] = acc

    return _call(
        body, name="bias_table", in_specs=[IN_SMEM, RESIDENT], out_specs=RESIDENT,
        out_shape=jax.ShapeDtypeStruct((N_KV, BLOCK, GQA * BLOCK), F32),
    )(rel_bias, bucket)


def _mlp_bwd(dx2, x1, up, g, wup_t, wdown):
    T = x1.shape[0]
    tm = 512

    def body(d_ref, x_ref, up_ref, g_ref, wu_ref, wd_ref, dx_ref, dup_ref, h_ref, dg_ref, db_ref):
        @pl.when(pl.program_id(0) == 0)
        def _():
            dg_ref[...] = jnp.zeros_like(dg_ref)
        d2 = d_ref[...]
        d2b = d2.astype(BF16)
        db_ref[...] = d2b
        xh, r = _rms(x_ref[...])
        gv = g_ref[...]
        h_ref[...] = (xh * gv).astype(BF16)
        dh = jnp.zeros((tm, D_MODEL), F32)
        for c in range(N_FF):
            rows = slice(c * FF_CHUNK, (c + 1) * FF_CHUNK)
            dact = _dot_nt(d2b, wd_ref[rows, :])
            dup = (dact * (2.0 * jnp.maximum(up_ref[c].astype(F32), 0.0))).astype(BF16)
            dup_ref[c] = dup
            dh = dh + _dot(dup, wu_ref[rows, :])
        dg_ref[...] += _colsum(dh * xh)
        dx_ref[...] = d2 + _rms_bwd(dh * gv, xh, r)

    return _call(
        body, name="mlp_bwd", grid=(T // tm,),
        in_specs=[_rows(tm, D_MODEL), _rows(tm, D_MODEL), _chunked(tm), RESIDENT, RESIDENT, RESIDENT],
        out_specs=[_rows(tm, D_MODEL), _chunked(tm), _rows(tm, D_MODEL), _acc_rows(D_MODEL), _rows(tm, D_MODEL)],
        out_shape=[jax.ShapeDtypeStruct((T, D_MODEL), F32), jax.ShapeDtypeStruct((N_FF, T, FF_CHUNK), BF16),
                   jax.ShapeDtypeStruct((T, D_MODEL), BF16), jax.ShapeDtypeStruct((1, D_MODEL), F32),
                   jax.ShapeDtypeStruct((T, D_MODEL), BF16)],
        compiler_params=_params("arbitrary"),
    )(dx2, x1, up, g, wup_t, wdown)


def _bwd_out(dx1, w_out):
    T = dx1.shape[0]
    tm = 512

    def body(d_ref, w_ref, da_ref, dc_ref, db_ref):
        db = d_ref[...].astype(BF16)
        db_ref[...] = db
        dm = _dot_nt(db, w_ref[...])
        da_ref[...] = dm[:, :ATTN_W]
        dc_ref[...] = dm[:, ATTN_W:]

    return _call(
        body, name="bwd_out", grid=(T // tm,),
        in_specs=[_rows(tm, D_MODEL), RESIDENT],
        out_specs=[_rows(tm, ATTN_W), _rows(tm, CONV_W), _rows(tm, D_MODEL)],
        out_shape=[jax.ShapeDtypeStruct((T, ATTN_W), F32), jax.ShapeDtypeStruct((T, CONV_W), F32),
                   jax.ShapeDtypeStruct((T, D_MODEL), BF16)],
        compiler_params=_params("parallel"),
    )(dx1, w_out)


def _wgrad(a, b, name, chunked=False, square_relu=False, tr=FF_CHUNK):
    if chunked:
        nr, T, tr = a.shape
    else:
        T, R = a.shape
        tr = min(R, tr)
        nr = R // tr
    piece = min(T, 2048)

    def body(a_ref, b_ref, o_ref):
        if square_relu:
            tot = jnp.zeros((tr, D_MODEL), F32)
            for k0 in range(0, T, piece):
                av = jnp.square(jnp.maximum(a_ref[k0:k0 + piece, :].astype(F32), 0.0)).astype(BF16)
                tot = tot + _dot_tn(av, b_ref[k0:k0 + piece, :])
        else:
            tot = _dot_tn(a_ref[...], b_ref[...])
        o_ref[...] = tot.astype(BF16)

    a_spec = (pl.BlockSpec((None, T, tr), lambda r: (r, 0, 0)) if chunked else pl.BlockSpec((T, tr), lambda r: (0, r)))
    return _call(
        body, name=name, grid=(nr,),
        in_specs=[a_spec, RESIDENT],
        out_specs=pl.BlockSpec((tr, D_MODEL), lambda r: (r, 0)),
        out_shape=jax.ShapeDtypeStruct((nr * tr, D_MODEL), BF16),
        compiler_params=_params("parallel"),
    )(a, b)


def _conv_bwd(dmix, y, zu, zg, cw, lg, lb, og):
    T = y.shape[0]
    tt = 512
    nt = T // tt
    per = tt // HALO
    prev_spec = pl.BlockSpec((HALO, CONV_W), lambda i: (jnp.maximum(i * per - 1, 0), 0))
    next_spec = pl.BlockSpec((HALO, CONV_W), lambda i: (jnp.minimum((i + 1) * per, nt * per - 1), 0))

    def body(dm_ref, dmn_ref, y_ref, yn_ref, u_ref, g_ref, uh_ref, gh_ref, cw_ref, lg_ref, lb_ref, og_ref,
             du_ref, dg_ref, pg_ref, hbuf, dybuf, dwacc, pacc, slab):
        i = pl.program_id(0)

        @pl.when(i == 0)
        def _():
            dwacc[...] = jnp.zeros_like(dwacc)
            pacc[...] = jnp.zeros_like(pacc)
        lgv, lbv, ogv = lg_ref[...], lb_ref[...], og_ref[...]

        def chain(yv, dm):
            yn, rstd, ln, sg, ch, r = _conv_post(yv, lgv, lbv, ogv)
            dc = _rms_bwd(dm * ogv, ch, r)
            dln = dc * sg * (1.0 + ln * (1.0 - sg))
            dyn = dln * lgv
            dy = rstd * (dyn - jnp.mean(dyn, axis=-1, keepdims=True)
                         - yn * jnp.mean(dyn * yn, axis=-1, keepdims=True))
            return dy, dm * ch, dln * yn, dln

        hbuf[0:HALO, :] = jnp.where(i > 0, uh_ref[...] * _sig(gh_ref[...]), 0.0)
        for r0 in range(0, tt, CONV_CHUNK):
            rs = slice(r0, r0 + CONV_CHUNK)
            hbuf[HALO + r0:HALO + r0 + CONV_CHUNK, :] = u_ref[rs, :] * _sig(g_ref[rs, :])
            dy, p_og, p_lg, p_lb = chain(y_ref[rs, :], dm_ref[rs, :])
            dybuf[rs, :] = dy
            for k, part in enumerate((dy, p_lg, p_lb, p_og)):
                pacc[8 * k:8 * k + 8, :] += _fold8(part)
        dyh, _, _, _ = chain(yn_ref[...], dmn_ref[...])
        dybuf[tt:tt + HALO, :] = jnp.where(i < nt - 1, dyh, 0.0)
        for r0 in range(0, tt, CONV_CHUNK):
            rs = slice(r0, r0 + CONV_CHUNK)
            dy = dybuf[rs, :]
            dh = jnp.zeros((CONV_CHUNK, CONV_W), F32)
            for j, win in _shifted_taps(dybuf, slab, r0, 0):
                dh = dh + cw_ref[CONV_K - 1 - j:CONV_K - j, :] * win
            for j, win in _shifted_taps(hbuf, slab, r0, HALO - (CONV_K - 1)):
                dwacc[8 * j:8 * j + 8, :] += _fold8(dy * win)
            sgt = _sig(g_ref[rs, :])
            du_ref[rs, :] = (dh * sgt).astype(BF16)
            dg_ref[rs, :] = (dh * u_ref[rs, :] * sgt * (1.0 - sgt)).astype(BF16)

        @pl.when(i == nt - 1)
        def _():
            pg_ref[...] = jnp.zeros_like(pg_ref)
            for j in range(CONV_K):
                pg_ref[j:j + 1, :] = _colsum(dwacc[8 * j:8 * j + 8, :])
            for k in range(4):
                pg_ref[32 + k:33 + k, :] = _colsum(pacc[8 * k:8 * k + 8, :])

    return _call(
        body, name="conv_bwd", grid=(nt,),
        in_specs=[_rows(tt, CONV_W), next_spec, _rows(tt, CONV_W), next_spec, _rows(tt, CONV_W), _rows(tt, CONV_W),
                  prev_spec, prev_spec] + [RESIDENT] * 4,
        out_specs=[_rows(tt, CONV_W), _rows(tt, CONV_W), _acc_rows(CONV_W, 40)],
        out_shape=[jax.ShapeDtypeStruct((T, CONV_W), BF16), jax.ShapeDtypeStruct((T, CONV_W), BF16),
                   jax.ShapeDtypeStruct((40, CONV_W), F32)],
        scratch_shapes=[pltpu.VMEM((HALO + tt, CONV_W), F32), pltpu.VMEM((tt + HALO, CONV_W), F32),
                        pltpu.VMEM((8 * HALO, CONV_W), F32), pltpu.VMEM((32, CONV_W), F32),
                        pltpu.VMEM((SLAB_ROWS, CONV_W), F32)],
        compiler_params=_params("arbitrary"),
    )(dmix, dmix, y, y, zu, zg, zu, zg, cw, lg, lb, og)


def _attn_bwd(dmix, a, zq, zkv, bias, sinks, qg, kg, og):
    T = zq.shape[0]
    rows = ATTN_SUB * BLOCK
    ns = T // rows
    nb = T // BLOCK
    cur = lambda w: pl.BlockSpec((rows, w), lambda n: (jnp.minimum(n, ns - 1), 0))
    scale = 1.0 / math.sqrt(HEAD_DIM)
    done = rows - BLOCK

    def body(dm_ref, a_ref, q_ref, kvc_ref, kvp_ref, bias_ref, sinks_ref, qgt_ref, kg_ref, kgt_ref, og_ref,
             dq_ref, dkv_ref, db_ref, dog_ref, dqg_ref, dkg_ref, dsk_ref, carry):
        n = pl.program_id(0)

        @pl.when(n == 0)
        def _():
            for ref in (db_ref, dog_ref, dqg_ref, dkg_ref, dsk_ref, carry):
                ref[...] = jnp.zeros_like(ref)

        @pl.when(n < ns)
        def _():
            tri = _tri_t()
            ogv, qgt, kgv, kgt = og_ref[...], qgt_ref[...], kg_ref[...], kgt_ref[...]
            lane = lax.broadcasted_iota(jnp.int32, (1, 128), 1)
            kvs = [kvp_ref[...]] + [kvc_ref[i * BLOCK:(i + 1) * BLOCK, :] for i in range(ATTN_SUB)]
            keys = [[_kv_block(kv, kh, kgv, kgt) for kh in range(N_KV)] for kv in kvs]
            dkn = [[jnp.zeros((BLOCK, HEAD_DIM), F32)] * N_KV for _ in kvs]
            dv = [[jnp.zeros((BLOCK, HEAD_DIM), F32)] * N_KV for _ in kvs]
            dsk = jnp.zeros((1, 128), F32)
            for i in range(ATTN_SUB):
                blk = slice(i * BLOCK, (i + 1) * BLOCK)
                ah, ra = _rms(a_ref[blk, :])
                dm = dm_ref[blk, :]
                dog_ref[...] += _colsum(dm * ah)
                da_t = _rms_bwd(dm * ogv, ah, ra).T
                zq_t = q_ref[blk, :].T
                has_prev = (n > 0) if i == 0 else None
                dqs = []
                for kh in range(N_KV):
                    kp, kc = keys[i][kh], keys[i + 1][kh]
                    c = _attn_group(zq_t, kp, kc, kh, bias_ref, sinks_ref, qgt, tri, has_prev)
                    dob = _heads_side_by_side(da_t, kh).astype(BF16)
                    pn = c["pn"]
                    p_c, p_p = _split(pn.astype(BF16), tri)
                    dv[i + 1][kh] = dv[i + 1][kh] + _dot_nt(p_c, dob)
                    dv[i][kh] = dv[i][kh] + _dot_nt(p_p, dob)
                    dp = jnp.where(tri, _dot(kc["v"], dob), _dot(kp["v"], dob))
                    dl = jnp.sum(pn * dp, axis=0, keepdims=True)
                    ds = pn * (dp - dl)
                    dsr = -c["psink"] * dl
                    for g in range(GQA):
                        dsk = dsk + jnp.where(lane == kh * GQA + g, jnp.sum(jnp.where(_head_lanes(g), dsr, 0.0)), 0.0)
                    db_ref[kh] += ds
                    ds_c, ds_p = _split((ds * scale).astype(BF16), tri)
                    dqn = _dot(kc["kn_t"], ds_c) + _dot(kp["kn_t"], ds_p)
                    dkn[i + 1][kh] = dkn[i + 1][kh] + _dot_nt(ds_c, c["qn"])
                    dkn[i][kh] = dkn[i][kh] + _dot_nt(ds_p, c["qn"])
                    dqg_ref[...] += jnp.sum(dqn * c["qhat"], axis=1, keepdims=True)
                    dq_t = _rms0_bwd(dqn * qgt, c["qhat"], c["rq"])
                    dqs += [dq_t[:, g * BLOCK:(g + 1) * BLOCK] for g in range(GQA)]
                dq_ref[blk, :] = jnp.concatenate(dqs, axis=0).T.astype(BF16)
            dsk_ref[...] += dsk
            dkv = []
            for j in range(ATTN_SUB + 1):
                dk = []
                for kh in range(N_KV):
                    key = keys[j][kh]
                    dkg_ref[...] += _colsum(dkn[j][kh] * key["khat"])
                    dk.append(_rms_bwd(dkn[j][kh] * kgv, key["khat"], key["rk"]))
                dkv.append(jnp.concatenate(dk + dv[j], axis=-1))
            if done:
                dkv_ref[0:done, :] = carry[0:done, :].astype(BF16)
            dkv_ref[done:rows, :] = (carry[done:rows, :] + dkv[0]).astype(BF16)
            for j in range(1, ATTN_SUB + 1):
                carry[(j - 1) * BLOCK:j * BLOCK, :] = dkv[j]

        @pl.when(n == ns)
        def _():
            dkv_ref[...] = carry[...].astype(BF16)

    small = lambda w: pl.BlockSpec((1, w), lambda n: (0, 0))
    return _call(
        body, name="attn_bwd", grid=(ns + 1,),
        in_specs=[cur(ATTN_W), cur(ATTN_W), cur(ATTN_W), cur(2 * KV_W),
                  pl.BlockSpec((BLOCK, 2 * KV_W), lambda n: (jnp.clip(n * ATTN_SUB - 1, 0, nb - 1), 0)),
                  RESIDENT, IN_SMEM, RESIDENT, RESIDENT, RESIDENT, RESIDENT],
        out_specs=[cur(ATTN_W), pl.BlockSpec((rows, 2 * KV_W), lambda n: (jnp.maximum(n - 1, 0), 0)),
                   pl.BlockSpec((N_KV, BLOCK, GQA * BLOCK), lambda n: (0, 0, 0)),
                   small(ATTN_W), pl.BlockSpec((HEAD_DIM, 1), lambda n: (0, 0)), small(HEAD_DIM), small(128)],
        out_shape=[jax.ShapeDtypeStruct((T, ATTN_W), BF16), jax.ShapeDtypeStruct((T, 2 * KV_W), BF16),
                   jax.ShapeDtypeStruct((N_KV, BLOCK, GQA * BLOCK), F32),
                   jax.ShapeDtypeStruct((1, ATTN_W), F32), jax.ShapeDtypeStruct((HEAD_DIM, 1), F32),
                   jax.ShapeDtypeStruct((1, HEAD_DIM), F32), jax.ShapeDtypeStruct((1, 128), F32)],
        scratch_shapes=[pltpu.VMEM((rows, 2 * KV_W), F32)],
        compiler_params=_params("arbitrary"),
    )(dmix, a, zq, zkv, zkv, bias, sinks, qg.reshape(HEAD_DIM, 1), kg, kg.reshape(HEAD_DIM, 1), og)


def _bwd_in(dx1, x, dq, dkv, du, dgt, g, wt):
    T = x.shape[0]
    tm = 512

    def body(d1_ref, x_ref, dq_ref, dkv_ref, du_ref, dgt_ref, g_ref, w_ref, dx_ref, h_ref, dg_ref, dz_ref):
        @pl.when(pl.program_id(0) == 0)
        def _():
            dg_ref[...] = jnp.zeros_like(dg_ref)
        xh, r = _rms(x_ref[...])
        gv = g_ref[...]
        h_ref[...] = (xh * gv).astype(BF16)
        dz = jnp.concatenate([dq_ref[...], dkv_ref[...], du_ref[...], dgt_ref[...]], axis=-1)
        dz_ref[...] = dz
        dh = _dot(dz, w_ref[...])
        dg_ref[...] += _colsum(dh * xh)
        dx_ref[...] = d1_ref[...] + _rms_bwd(dh * gv, xh, r)

    return _call(
        body, name="bwd_in", grid=(T // tm,),
        in_specs=[_rows(tm, D_MODEL), _rows(tm, D_MODEL), _rows(tm, ATTN_W), _rows(tm, 2 * KV_W),
                  _rows(tm, CONV_W), _rows(tm, CONV_W), RESIDENT, RESIDENT],
        out_specs=[_rows(tm, D_MODEL), _rows(tm, D_MODEL), _acc_rows(D_MODEL), _rows(tm, IN_W)],
        out_shape=[jax.ShapeDtypeStruct((T, D_MODEL), F32), jax.ShapeDtypeStruct((T, D_MODEL), BF16),
                   jax.ShapeDtypeStruct((1, D_MODEL), F32), jax.ShapeDtypeStruct((T, IN_W), BF16)],
        compiler_params=_params("arbitrary"),
    )(dx1, x, dq, dkv, du, dgt, g, wt)


def _bias_grad(db, bucket):
    def body(db_ref, bk_ref, o_ref):
        bk = bk_ref[...]
        lane = lax.broadcasted_iota(jnp.int32, (1, 128), 1)
        for h in range(N_HEADS):
            cols = slice((h % GQA) * BLOCK, (h % GQA + 1) * BLOCK)
            tot = db_ref[0, h // GQA, :, cols]
            for l in range(1, DEPTH):
                tot = tot + db_ref[l, h // GQA, :, cols]
            out = jnp.zeros((1, 128), F32)
            for b in range(N_BUCKETS):
                out = jnp.where(lane == b, jnp.sum(jnp.where(bk == b, tot, 0.0)), out)
            o_ref[h:h + 1, :] = out

    return _call(
        body, name="bias_grad", in_specs=[RESIDENT, RESIDENT], out_specs=RESIDENT,
        out_shape=jax.ShapeDtypeStruct((N_HEADS, 128), F32),
        compiler_params=pltpu.CompilerParams(vmem_limit_bytes=VMEM_LIMIT),
    )(db, bucket)


def _place():
    return lax.axis_index("x"), lax.axis_index("y"), lax.axis_index("c")


def _all_gather(shards):
    na = len(shards)

    def body(*refs):
        ins, outs = refs[:na], refs[na:2 * na]
        send_sems, recv_sems, local_sems = refs[2 * na:]
        x, y, c = _place()
        me, sibling = (x, y, c), (x, y, 1 - c)
        chips = [(1 - x, y), (x, 1 - y), (1 - x, 1 - y)]

        def rows(a, p):
            r = ins[a].shape[1]
            return outs[a].at[:, pl.ds((4 * p[0] + 2 * p[1] + p[2]) * r, r), :]

        def copy(a, k, block, to, src=None):
            return pltpu.make_async_remote_copy(
                src_ref=rows(a, block) if src is None else src, dst_ref=rows(a, block),
                send_sem=send_sems.at[a, k], recv_sem=recv_sems.at[a, k], device_id=to, device_id_type=MESH)

        mine = [pltpu.make_async_copy(ins[a], rows(a, me), local_sems.at[a]) for a in range(na)]
        for cp in mine:
            cp.start()
        first = []
        for a in range(na):
            first.append(copy(a, 0, me, sibling, src=ins[a]))
            first += [copy(a, 1 + j, me, (*chip, c), src=ins[a]) for j, chip in enumerate(chips)]
        for cp in first:
            cp.start()
        passed = []
        for j, chip in enumerate(chips):
            for a in range(na):
                copy(a, 1 + j, (*chip, c), me).wait_recv()
                cp = copy(a, 4 + j, (*chip, c), sibling)
                cp.start()
                passed.append(cp)
        for a in range(na):
            copy(a, 0, sibling, me).wait_recv()
            for j, chip in enumerate(chips):
                copy(a, 4 + j, (*chip, 1 - c), me).wait_recv()
        for cp in first + passed:
            cp.wait_send()
        for cp in mine:
            cp.wait()

    return _call(
        body, name="all_gather",
        in_specs=[ANY] * na, out_specs=[ANY] * na,
        out_shape=[jax.ShapeDtypeStruct((s.shape[0], N_DEV * s.shape[1], s.shape[2]), s.dtype) for s in shards],
        scratch_shapes=[pltpu.SemaphoreType.DMA((na, 7)), pltpu.SemaphoreType.DMA((na, 7)),
                        pltpu.SemaphoreType.DMA((na,))],
    )(*shards)


IN_HBM = pl.BlockSpec(memory_space=pltpu.HBM)
IN_SEM = pl.BlockSpec(memory_space=pltpu.SEMAPHORE)
DATAFLOW = pltpu.SideEffectType.DATAFLOW_SIDE_EFFECTING


def _exchange_copies(srcs, lands, send_sems, recv_sems, scatter):
    x, y, c = _place()
    me = 4 * x + 2 * y + c
    copies = []
    for k in range(1, N_DEV):
        p = (x ^ (k >> 2), y ^ ((k >> 1) & 1), c ^ (k & 1))
        for a, (src, land) in enumerate(zip(srcs, lands)):
            r = land.shape[0] // N_DEV
            if scatter:
                src = src.at[pl.ds((4 * p[0] + 2 * p[1] + p[2]) * r, r), :]
            copies.append(pltpu.make_async_remote_copy(
                src_ref=src, dst_ref=land.at[pl.ds(me * r, r), :], send_sem=send_sems.at[a * (N_DEV - 1) + k - 1],
                recv_sem=recv_sems.at[a * (N_DEV - 1) + k - 1], device_id=p, device_id_type=MESH))
    return copies


def _own_copies(srcs, lands, send_sems, scatter):
    x, y, c = _place()
    me = 4 * x + 2 * y + c
    copies = []
    for a, (src, land) in enumerate(zip(srcs, lands)):
        r = land.shape[0] // N_DEV
        if scatter:
            src = src.at[pl.ds(me * r, r), :]
        copies.append(pltpu.make_async_copy(src, land.at[pl.ds(me * r, r), :],
                                            send_sems.at[len(srcs) * (N_DEV - 1) + a]))
    return copies


def _exchange_start(srcs, name, scatter):
    na = len(srcs)
    lands = [lax.empty((s.shape[0] * (1 if scatter else N_DEV), s.shape[1]), s.dtype) for s in srcs]

    def body(*refs):
        ins, lnd = refs[:na], refs[na:2 * na]
        send_sems, recv_sems = refs[2 * na], refs[2 * na + 1]
        token = refs[-1]
        for cp in _exchange_copies(ins, lnd, send_sems, recv_sems, scatter) + _own_copies(ins, lnd, send_sems, scatter):
            cp.start()
        token[...] = jnp.zeros_like(token)

    hbm = lambda a: pltpu.with_memory_space_constraint(a, pltpu.HBM)
    out = _call(
        body, name=name,
        out_shape=(pltpu.SemaphoreType.DMA((na * N_DEV,)), pltpu.SemaphoreType.DMA((na * (N_DEV - 1),)),
                   *[pltpu.HBM(a.shape, a.dtype) for a in (*srcs, *lands)], jax.ShapeDtypeStruct((8, 128), F32)),
        in_specs=[IN_HBM] * (2 * na),
        out_specs=(IN_SEM, IN_SEM, *[IN_HBM] * (2 * na), RESIDENT),
        input_output_aliases={i: 2 + i for i in range(2 * na)},
        compiler_params=pltpu.CompilerParams(has_side_effects=DATAFLOW),
    )(*[hbm(a) for a in (*srcs, *lands)])
    return (out[0], out[1], out[2:2 + na], out[2 + na:2 + 2 * na]), out[-1]


def _exchange_wait(state, after, name, scatter):
    send_sems, recv_sems, srcs, lands = state
    na = len(srcs)

    def body(*refs):
        ins, lnd = refs[:na], refs[na:2 * na]
        for cp in _exchange_copies(ins, lnd, refs[2 * na], refs[2 * na + 1], scatter):
            cp.wait_send()
            cp.wait_recv()
        for cp in _own_copies(ins, lnd, refs[2 * na], scatter):
            cp.wait()
        refs[-1][...] = jnp.zeros_like(refs[-1])

    out = _call(
        body, name=name,
        out_shape=(*[pltpu.HBM(a.shape, a.dtype) for a in (*srcs, *lands)], jax.ShapeDtypeStruct((8, 128), F32)),
        in_specs=[IN_HBM] * (2 * na) + [IN_SEM, IN_SEM, ANY],
        out_specs=(*[IN_HBM] * (2 * na), RESIDENT),
        input_output_aliases={i: i for i in range(2 * na)},
        compiler_params=pltpu.CompilerParams(has_side_effects=DATAFLOW),
    )(*srcs, *lands, send_sems, recv_sems, after)
    return out[:na], out[na:2 * na], out[-1]


def _adam_math(w, g, m, v):
    m = ADAM_B1 * m + (1.0 - ADAM_B1) * g
    v = ADAM_B2 * v + (1.0 - ADAM_B2) * jnp.square(g)
    m_hat = m / (1.0 - ADAM_B1 ** ADAM_STEP)
    v_hat = v / (1.0 - ADAM_B2 ** ADAM_STEP)
    delta = -ADAM_LR * (m_hat / (jnp.sqrt(v_hat) + ADAM_EPS) + ADAM_WD * w)
    return delta, m, v


def _adam_big(parts, w, m, v):
    nl, r, cdim = w.shape
    tr = r if r <= 256 else 256
    ni = r // tr
    spec = pl.BlockSpec((None, tr, cdim), lambda l, i: (l, i, 0))

    def part_spec(layer):
        return pl.BlockSpec((N_DEV, tr, cdim),
                            lambda l, i: (0, jnp.where(l == layer, i, jnp.where(l < layer, 0, ni - 1)), 0))

    def body(*refs):
        p_refs = refs[:nl]
        w_ref, m_ref, v_ref, g_ref, d_ref, nm_ref, nv_ref = refs[nl:]
        for layer in range(nl):
            @pl.when(pl.program_id(0) == layer)
            def _(p_ref=p_refs[layer]):
                g = p_ref[0].astype(F32)
                for s in range(1, N_DEV):
                    g = g + p_ref[s].astype(F32)
                d, nm, nv = _adam_math(w_ref[...], g, m_ref[...], v_ref[...])
                g_ref[...] = g
                d_ref[...] = d
                nm_ref[...] = nm
                nv_ref[...] = nv

    return _call(
        body, name="adam_big", grid=(nl, ni),
        in_specs=[part_spec(layer) for layer in range(nl)] + [spec, spec, spec],
        out_specs=[spec] * 4,
        out_shape=[jax.ShapeDtypeStruct(w.shape, F32)] * 4,
        compiler_params=_params("arbitrary", "arbitrary"),
    )(*parts, w, m, v)


def _sum_slots(parts):
    def body(p_ref, o_ref):
        g = p_ref[0]
        for s in range(1, N_DEV):
            g = g + p_ref[s]
        o_ref[...] = g

    return _call(body, name="sum_slots", in_specs=[RESIDENT], out_specs=RESIDENT,
                 out_shape=jax.ShapeDtypeStruct(parts.shape[1:], F32))(parts)


def _adam_small(ws, gs, ms, vs):
    n = len(ws)

    def body(*refs):
        ins, outs = refs[:4 * n], refs[4 * n:]
        for k in range(n):
            d, nm, nv = _adam_math(ins[k][...], ins[n + k][...], ins[2 * n + k][...], ins[3 * n + k][...])
            outs[k][...] = d
            outs[n + k][...] = nm
            outs[2 * n + k][...] = nv

    out = _call(body, name="adam_small", in_specs=[RESIDENT] * (4 * n), out_specs=[RESIDENT] * (3 * n),
                out_shape=[jax.ShapeDtypeStruct(w.shape, F32) for w in ws] * 3)(*ws, *gs, *ms, *vs)
    return out[:n], out[n:2 * n], out[2 * n:]


def _pack(arrays):
    parts = []
    for a in arrays:
        flat = a.reshape(-1)
        n = flat.shape[0]
        padded = -(-n // 1024) * 1024
        parts.append(jnp.pad(flat, (0, padded - n)).reshape(padded // 128, 128))
    return jnp.concatenate(parts, axis=0)


def _unpack(packed, shapes):
    out, row = [], 0
    for shp in shapes:
        n = int(np.prod(shp))
        rows = -(-n // 1024) * 8
        out.append(packed[row:row + rows].reshape(-1)[:n].reshape(shp))
        row += rows
    return out


def _t5_bucket_table():
    kj = np.arange(BLOCK)[:, None]
    qi = np.arange(BLOCK)[None, :]
    n = (qi - kj) % BLOCK
    max_exact = N_BUCKETS // 2
    large = max_exact + (np.log(np.maximum(n, 1) / max_exact) / np.log(MAX_DIST / max_exact)
                         * (N_BUCKETS - max_exact)).astype(np.int32)
    large = np.minimum(large, N_BUCKETS - 1)
    return np.where(n < max_exact, n, large).astype(np.int32)


SMALL_NAMES = ("rel_bias", "norm_mix_g", "q_norm_g", "k_norm_g", "sinks", "conv_b", "conv_ln_g", "conv_ln_b",
               "attn_out_g", "conv_out_g", "norm_mlp_g")


def _local_step(x, target, small, conv_w, get_w, put_g, token):
    bucket = jnp.asarray(_t5_bucket_table())
    bias = _bias_table(small["rel_bias"], bucket)
    row = lambda a, l: a[l][None, :]
    cw_pad = jnp.pad(conv_w, ((0, 0), (0, HALO - CONV_K), (0, 0)))
    saved, weights = [], []
    for l in range(DEPTH):
        wt_in, w_out, tok = get_w(l, 0, x)
        zq, zkv, zu, zg = _fwd_in(x, row(small["norm_mix_g"], l) + tok + (token if l == 0 else 0.0), wt_in)
        a, mix_a = _attn_fwd(zq, zkv, bias, small["sinks"][l], row(small["q_norm_g"], l), row(small["k_norm_g"], l),
                             row(small["attn_out_g"], l))
        y, mix_c = _conv_fwd(zu, zg, cw_pad[l], row(small["conv_b"], l), row(small["conv_ln_g"], l),
                             row(small["conv_ln_b"], l), row(small["conv_out_g"], l))
        x1 = _fwd_out(x, mix_a, mix_c, w_out)
        wt_up, w_down, tok = get_w(l, 1, x1)
        weights.append((wt_in, w_out, wt_up, w_down))
        x2, up = _mlp_fwd(x1, row(small["norm_mlp_g"], l) + tok, wt_up, w_down)
        saved.append((x, zq, zkv, zu, zg, a, mix_a, y, mix_c, x1, up))
        x = x2
    loss_part, dx = _loss_head(x, target)

    gs = {n: [None] * DEPTH for n in SMALL_NAMES if n != "rel_bias"}
    g_conv_w, dbias = [None] * DEPTH, [None] * DEPTH
    token = 0.0
    for l in reversed(range(DEPTH)):
        x0, zq, zkv, zu, zg, a, mix_a, y, mix_c, x1, up = saved[l]
        wt_in, w_out, wt_up, w_down = weights[l]
        dx1, dup, h2, gs["norm_mlp_g"][l], dxb = _mlp_bwd(dx, x1, up, row(small["norm_mlp_g"], l) + token, wt_up, w_down)
        g_up = _wgrad(dup, h2, "wgrad_up", chunked=True)
        g_down = _wgrad(up, dxb, "wgrad_down", chunked=True, square_relu=True)
        token = put_g(l, 1, (g_up, g_down))
        dma, dmc, dx1b = _bwd_out(dx1, w_out)
        g_out = jnp.concatenate([_wgrad(mix_a, dx1b, "wgrad_out_a"), _wgrad(mix_c, dx1b, "wgrad_out_c")], axis=0)
        du, dgt, pg = _conv_bwd(dmc, y, zu, zg, cw_pad[l], row(small["conv_ln_g"], l) + token,
                                row(small["conv_ln_b"], l), row(small["conv_out_g"], l))
        g_conv_w[l] = pg[:CONV_K]
        gs["conv_b"][l], gs["conv_ln_g"][l], gs["conv_ln_b"][l], gs["conv_out_g"][l] = pg[32], pg[33], pg[34], pg[35]
        dq, dkv, dbias[l], dog, dqg, dkg, dsk = _attn_bwd(
            dma, a, zq, zkv, bias, small["sinks"][l], row(small["q_norm_g"], l), row(small["k_norm_g"], l),
            row(small["attn_out_g"], l))
        gs["attn_out_g"][l], gs["q_norm_g"][l], gs["k_norm_g"][l], gs["sinks"][l] = dog, dqg, dkg, dsk[0, :N_HEADS]
        dx, h, gs["norm_mix_g"][l], dz = _bwd_in(dx1, x0, dq, dkv, du, dgt, row(small["norm_mix_g"], l), wt_in)
        g_in = _wgrad(dz, h, "wgrad_in", tr=256)
        token = put_g(l, 0, (g_in, g_out))
    small_grads = {n: jnp.stack([jnp.reshape(v, (-1,)) for v in vals]) for n, vals in gs.items()}
    small_grads["rel_bias"] = _bias_grad(jnp.stack(dbias), bucket)[:, :N_BUCKETS].T
    return loss_part, dx, small_grads, jnp.stack(g_conv_w)


def kernel(x, rel_bias, norm_mix_g, w_in, q_norm_g, k_norm_g, sinks, conv_w, conv_b, conv_ln_g, conv_ln_b, attn_out_g, conv_out_g, w_out, norm_mlp_g, w_mlp_up, w_mlp_down, loss_target, m_rel_bias, m_norm_mix_g, m_w_in, m_q_norm_g, m_k_norm_g, m_sinks, m_conv_w, m_conv_b, m_conv_ln_g, m_conv_ln_b, m_attn_out_g, m_conv_out_g, m_w_out, m_norm_mlp_g, m_w_mlp_up, m_w_mlp_down, v_rel_bias, v_norm_mix_g, v_w_in, v_q_norm_g, v_k_norm_g, v_sinks, v_conv_w, v_conv_b, v_conv_ln_g, v_conv_ln_b, v_attn_out_g, v_conv_out_g, v_w_out, v_norm_mlp_g, v_w_mlp_up, v_w_mlp_down):
    args = dict(locals())
    small = {n: args[n] for n in SMALL_NAMES}
    tr = lambda a: jnp.swapaxes(a, 1, 2)
    me = 4 * lax.axis_index("x") + 2 * lax.axis_index("y") + lax.axis_index("c")

    shards = (tr(w_in).astype(BF16), w_out.astype(BF16), tr(w_mlp_up).astype(BF16), w_mlp_down.astype(BF16))
    cw_sh = jnp.pad(tr(conv_w), ((0, 0), (0, 0), (0, HALO - CONV_K))).reshape(DEPTH * (CONV_W // N_DEV), HALO)
    gathers = {}

    def start_gather(l, half, dep):
        arrays = [s[l] for s in shards[2 * half:2 * half + 2]]
        if (l, half) == (0, 0):
            arrays = [cw_sh] + arrays
        if dep is not None:
            arrays = [a + dep[0, 0].astype(a.dtype) for a in arrays]
        gathers[l, half], tok = _exchange_start(arrays, f"gather_{l}{'ab'[half]}_start", scatter=False)
        return tok

    landed = {}

    def get_w(l, half, after):
        if (l, half) not in landed:
            _, lands, tok = _exchange_wait(gathers[l, half], after, f"gather_{l}{'ab'[half]}_wait", scatter=False)
            if l + 1 < DEPTH:
                tok = tok + start_gather(l + 1, half, tok)
            landed[l, half] = (*lands, tok[0, 0])
        return landed[l, half]

    token = start_gather(0, 0, None)
    token = token + start_gather(0, 1, token)
    cw_all, *first = get_w(0, 0, token)
    landed[0, 0] = tuple(first)
    conv_w_full = jnp.transpose(cw_all.reshape(N_DEV, DEPTH, CONV_W // N_DEV, HALO), (1, 3, 0, 2))
    conv_w_full = conv_w_full.reshape(DEPTH, HALO, CONV_W)[:, :CONV_K, :]

    scatters = {}

    last = {}

    def put_g(l, half, grads):
        scatters[l, half], last["token"] = _exchange_start(list(grads), f"scatter_{l}{'ab'[half]}_start", scatter=True)
        return last["token"][0, 0]

    loss_part, dx, small_grads, g_conv_w = _local_step(
        x[0], loss_target[0], small, conv_w_full, get_w, put_g, token[0, 0])

    big_w = (tr(w_in), w_out, tr(w_mlp_up), w_mlp_down)
    big_m = (tr(m_w_in), m_w_out, tr(m_w_mlp_up), m_w_mlp_down)
    big_v = (tr(v_w_in), v_w_out, tr(v_w_mlp_up), v_w_mlp_down)
    big_out, after = [None] * 4, last["token"]
    for half in (1, 0):
        parts = [[None] * DEPTH, [None] * DEPTH]
        for l in reversed(range(DEPTH)):
            _, lands, _ = _exchange_wait(scatters[l, half], after, f"scatter_{l}{'ab'[half]}_wait", scatter=True)
            for k, land in enumerate(lands):
                parts[k][l] = land.reshape(N_DEV, land.shape[0] // N_DEV, land.shape[1])
        for k in range(2):
            big_out[2 * half + k] = _adam_big(parts[k], big_w[2 * half + k], big_m[2 * half + k], big_v[2 * half + k])
        after = big_out[2 * half + 1][0]
    for k in (0, 2):
        big_out[k] = [tr(o) for o in big_out[k]]

    order = [n for n in SMALL_NAMES]
    packed = _pack([small_grads[n] for n in order] + [g_conv_w])
    slots, = _all_gather((packed[None],))
    summed = _sum_slots(slots[0].reshape(N_DEV, packed.shape[0], 128))
    shapes = [small[n].shape for n in order] + [(DEPTH, CONV_K, CONV_W)]
    sg = _unpack(summed, shapes)
    g_small = dict(zip(order, sg[:-1]))
    g_small["conv_w"] = lax.dynamic_slice_in_dim(sg[-1], me * (CONV_W // N_DEV), CONV_W // N_DEV, axis=2)
    names = order + ["conv_w"]
    deltas, new_m, new_v = _adam_small([args[n] for n in names], [g_small[n] for n in names],
                                       [args["m_" + n] for n in names], [args["v_" + n] for n in names])
    res = {"grad": g_small, "delta": dict(zip(names, deltas)), "new_m": dict(zip(names, new_m)),
           "new_v": dict(zip(names, new_v))}
    for k, n in enumerate(("w_in", "w_out", "w_mlp_up", "w_mlp_down")):
        for kind, val in zip(("grad", "delta", "new_m", "new_v"), big_out[k]):
            res[kind][n] = val

    loss = lax.psum(loss_part[0, 0], ("x", "y", "c"))
    weights = ("rel_bias", "norm_mix_g", "w_in", "q_norm_g", "k_norm_g", "sinks", "conv_w", "conv_b", "conv_ln_g",
               "conv_ln_b", "attn_out_g", "conv_out_g", "w_out", "norm_mlp_g", "w_mlp_up", "w_mlp_down")
    return (loss, dx[None], *[res[kind][n] for kind in ("grad", "delta", "new_m", "new_v") for n in weights])
```

```python
import math

import numpy as np
import jax
import jax.numpy as jnp
from jax import lax
from jax.experimental import pallas as pl
from jax.experimental.pallas import tpu as pltpu

F32, BF16 = jnp.float32, jnp.bfloat16
D_MODEL = 1024
DEPTH = 4
HEAD_DIM = 64
N_HEADS = 8
N_KV = 2
GQA = N_HEADS // N_KV
ATTN_W = N_HEADS * HEAD_DIM
KV_W = N_KV * HEAD_DIM
CONV_W = D_MODEL - ATTN_W
IN_W = ATTN_W + 2 * KV_W + 2 * CONV_W
BLOCK = 128
CONV_K = 31
HALO = 32
N_BUCKETS = 32
MAX_DIST = 128
D_FF = 4 * D_MODEL
FF_CHUNK = 512
N_FF = D_FF // FF_CHUNK
EPS = 1e-6
NEG = -1e30
N_DEV = 8
ADAM_LR, ADAM_B1, ADAM_B2, ADAM_EPS, ADAM_WD, ADAM_STEP = 0.001, 0.9, 0.999, 1e-08, 0.01, 10
VMEM_LIMIT = 56 * 1024 * 1024
MESH = pl.DeviceIdType.MESH

RESIDENT = pl.BlockSpec(memory_space=pltpu.VMEM)
IN_SMEM = pl.BlockSpec(memory_space=pltpu.SMEM)
ANY = pl.BlockSpec(memory_space=pl.ANY)


def _call(body, **kw):
    return pl.pallas_call(body, **kw)


def _params(*sem):
    return pltpu.CompilerParams(dimension_semantics=sem, vmem_limit_bytes=VMEM_LIMIT)


def _dot(a, b):
    return lax.dot_general(a, b, (((1,), (0,)), ((), ())), preferred_element_type=F32)


def _dot_nt(a, b):
    return lax.dot_general(a, b, (((1,), (1,)), ((), ())), preferred_element_type=F32)


def _dot_tn(a, b):
    return lax.dot_general(a, b, (((0,), (0,)), ((), ())), preferred_element_type=F32)


def _sig(x):
    return 1.0 / (1.0 + jnp.exp(-x))


def _rms(x):
    r = lax.rsqrt(jnp.mean(x * x, axis=-1, keepdims=True) + EPS)
    return x * r, r


def _rms_bwd(dy_g, xh, r):
    return r * (dy_g - xh * jnp.mean(dy_g * xh, axis=-1, keepdims=True))


def _rows(tm, w):
    return pl.BlockSpec((tm, w), lambda i: (i, 0))


def _acc_rows(w, rows=1):
    return pl.BlockSpec((rows, w), lambda i: (0, 0))


def _colsum(x):
    return jnp.sum(x, axis=0, keepdims=True)


def _fwd_in(x, g, wt):
    T = x.shape[0]
    tm = 512

    def body(x_ref, g_ref, w_ref, q_ref, kv_ref, u_ref, gt_ref):
        xh, _ = _rms(x_ref[...])
        h = (xh * g_ref[...]).astype(BF16)
        z = _dot_nt(h, w_ref[...])
        q_ref[...] = z[:, :ATTN_W]
        kv_ref[...] = z[:, ATTN_W:ATTN_W + 2 * KV_W]
        u_ref[...] = z[:, ATTN_W + 2 * KV_W:ATTN_W + 2 * KV_W + CONV_W]
        gt_ref[...] = z[:, ATTN_W + 2 * KV_W + CONV_W:]

    widths = (ATTN_W, 2 * KV_W, CONV_W, CONV_W)
    return _call(
        body, name="fwd_in", grid=(T // tm,),
        in_specs=[_rows(tm, D_MODEL), RESIDENT, RESIDENT],
        out_specs=[_rows(tm, w) for w in widths],
        out_shape=[jax.ShapeDtypeStruct((T, w), F32) for w in widths],
        compiler_params=_params("parallel"),
    )(x, g, wt)


ATTN_SUB = 16


def _rms0(x):
    r = lax.rsqrt(jnp.mean(x * x, axis=0, keepdims=True) + EPS)
    return x * r, r


def _rms0_bwd(dy_g, xh, r):
    return r * (dy_g - xh * jnp.mean(dy_g * xh, axis=0, keepdims=True))


def _tri_t():
    kj = lax.broadcasted_iota(jnp.int32, (BLOCK, BLOCK), 0)
    qi = lax.broadcasted_iota(jnp.int32, (BLOCK, BLOCK), 1)
    return jnp.concatenate([kj <= qi] * GQA, axis=1)


def _head_lanes(g):
    lane = lax.broadcasted_iota(jnp.int32, (1, GQA * BLOCK), 1)
    return (lane >= g * BLOCK) & (lane < (g + 1) * BLOCK)


def _heads_side_by_side(xt, kh):
    return jnp.concatenate(
        [xt[(kh * GQA + g) * HEAD_DIM:(kh * GQA + g + 1) * HEAD_DIM, :] for g in range(GQA)], axis=1)


def _kv_block(kv, kh, kg, kg_t):
    kvt = kv.T
    khat, rk = _rms(kv[:, kh * HEAD_DIM:(kh + 1) * HEAD_DIM])
    khat_t, _ = _rms0(kvt[kh * HEAD_DIM:(kh + 1) * HEAD_DIM, :])
    return dict(khat=khat, rk=rk, kn=(khat * kg).astype(BF16), kn_t=(khat_t * kg_t).astype(BF16),
                v=kv[:, KV_W + kh * HEAD_DIM:KV_W + (kh + 1) * HEAD_DIM].astype(BF16),
                v_t=kvt[KV_W + kh * HEAD_DIM:KV_W + (kh + 1) * HEAD_DIM, :].astype(BF16))


def _attn_group(zq_t, kp, kc, kh, bias_ref, sinks_ref, qg_t, tri, has_prev):
    scale = 1.0 / math.sqrt(HEAD_DIM)
    qhat, rq = _rms0(_heads_side_by_side(zq_t, kh))
    qf = qhat * qg_t
    qn = qf.astype(BF16)
    qs = (qf * scale).astype(BF16)
    s = jnp.where(tri, _dot(kc["kn"], qs), _dot(kp["kn"], qs)) + bias_ref[kh]
    if has_prev is not None:
        s = jnp.where(tri | has_prev, s, NEG)
    sink = jnp.zeros((1, GQA * BLOCK), F32)
    for g in range(GQA):
        sink = jnp.where(_head_lanes(g), sinks_ref[kh * GQA + g], sink)
    m = jnp.maximum(jnp.max(s, axis=0, keepdims=True), sink)
    p = jnp.exp(s - m)
    es = jnp.exp(sink - m)
    inv = 1.0 / (jnp.sum(p, axis=0, keepdims=True) + es)
    return dict(qhat=qhat, rq=rq, qn=qn, pn=p * inv, psink=es * inv)


def _split(pb, tri):
    zero = jnp.zeros_like(pb)
    return jnp.where(tri, pb, zero), jnp.where(tri, zero, pb)


def _attn_fwd(zq, zkv, bias, sinks, qg, kg, og):
    T = zq.shape[0]
    rows = ATTN_SUB * BLOCK

    def body(q_ref, kvc_ref, kvp_ref, bias_ref, sinks_ref, qgt_ref, kg_ref, kgt_ref, og_ref, a_ref, mix_ref):
        n = pl.program_id(0)
        tri = _tri_t()
        qgt, kgv, kgt = qgt_ref[...], kg_ref[...], kgt_ref[...]
        kvs = [kvp_ref[...]] + [kvc_ref[i * BLOCK:(i + 1) * BLOCK, :] for i in range(ATTN_SUB)]
        keys = [[_kv_block(kv, kh, kgv, kgt) for kh in range(N_KV)] for kv in kvs]
        for i in range(ATTN_SUB):
            zq_t = q_ref[i * BLOCK:(i + 1) * BLOCK, :].T
            has_prev = (n > 0) if i == 0 else None
            outs = []
            for kh in range(N_KV):
                kp, kc = keys[i][kh], keys[i + 1][kh]
                c = _attn_group(zq_t, kp, kc, kh, bias_ref, sinks_ref, qgt, tri, has_prev)
                p_c, p_p = _split(c["pn"].astype(BF16), tri)
                o_t = _dot(kc["v_t"], p_c) + _dot(kp["v_t"], p_p)
                outs += [o_t[:, g * BLOCK:(g + 1) * BLOCK] for g in range(GQA)]
            a = jnp.concatenate(outs, axis=0).T
            a_ref[i * BLOCK:(i + 1) * BLOCK, :] = a
            ah, _ = _rms(a)
            mix_ref[i * BLOCK:(i + 1) * BLOCK, :] = (ah * og_ref[...]).astype(BF16)

    return _call(
        body, name="attn_fwd", grid=(T // rows,),
        in_specs=[_rows(rows, ATTN_W), _rows(rows, 2 * KV_W),
                  pl.BlockSpec((BLOCK, 2 * KV_W), lambda n: (jnp.maximum(n * ATTN_SUB - 1, 0), 0)),
                  RESIDENT, IN_SMEM, RESIDENT, RESIDENT, RESIDENT, RESIDENT],
        out_specs=[_rows(rows, ATTN_W), _rows(rows, ATTN_W)],
        out_shape=[jax.ShapeDtypeStruct((T, ATTN_W), F32), jax.ShapeDtypeStruct((T, ATTN_W), BF16)],
        compiler_params=_params("parallel"),
    )(zq, zkv, zkv, bias, sinks, qg.reshape(HEAD_DIM, 1), kg, kg.reshape(HEAD_DIM, 1), og)


def _conv_post(y, lg, lb, og):
    mu = jnp.mean(y, axis=-1, keepdims=True)
    yc = y - mu
    rstd = lax.rsqrt(jnp.mean(yc * yc, axis=-1, keepdims=True) + EPS)
    yn = yc * rstd
    ln = yn * lg + lb
    sg = _sig(ln)
    c = ln * sg
    ch, r = _rms(c)
    return yn, rstd, ln, sg, ch, r


CONV_CHUNK = 64


SLAB_ROWS = CONV_CHUNK + 8 * ((CONV_K - 1) // 8)


def _shifted_taps(buf, slab, r0, base):
    for b in range(8):
        taps = range(b, CONV_K, 8)
        span = CONV_CHUNK + 8 * (len(taps) - 1)
        slab[0:span, :] = buf[r0 + base + b:r0 + base + b + span, :]
        for a, j in enumerate(taps):
            yield j, slab[8 * a:8 * a + CONV_CHUNK, :]


def _fold8(x):
    return jnp.sum(x.reshape(x.shape[0] // 8, 8, x.shape[1]), axis=0)


def _conv_fwd(zu, zg, cw, cb, lg, lb, og):
    T = zu.shape[0]
    tt = 512
    halo_spec = pl.BlockSpec((HALO, CONV_W), lambda i: (jnp.maximum(i * (tt // HALO) - 1, 0), 0))

    def body(u_ref, g_ref, uh_ref, gh_ref, cw_ref, cb_ref, lg_ref, lb_ref, og_ref, y_ref, mix_ref, buf, slab):
        i = pl.program_id(0)
        hal = uh_ref[...] * _sig(gh_ref[...])
        buf[0:HALO, :] = jnp.where(i > 0, hal, 0.0)
        buf[HALO:HALO + tt, :] = u_ref[...] * _sig(g_ref[...])
        cbv, lgv, lbv, ogv = cb_ref[...], lg_ref[...], lb_ref[...], og_ref[...]
        for r0 in range(0, tt, CONV_CHUNK):
            rs = slice(r0, r0 + CONV_CHUNK)
            acc = jnp.zeros((CONV_CHUNK, CONV_W), F32)
            for j, win in _shifted_taps(buf, slab, r0, HALO - (CONV_K - 1)):
                acc = acc + cw_ref[j:j + 1, :] * win
            y = acc + cbv
            y_ref[rs, :] = y
            mix_ref[rs, :] = (_conv_post(y, lgv, lbv, ogv)[4] * ogv).astype(BF16)

    return _call(
        body, name="conv_fwd", grid=(T // tt,),
        in_specs=[_rows(tt, CONV_W), _rows(tt, CONV_W), halo_spec, halo_spec] + [RESIDENT] * 5,
        out_specs=[_rows(tt, CONV_W), _rows(tt, CONV_W)],
        out_shape=[jax.ShapeDtypeStruct((T, CONV_W), F32), jax.ShapeDtypeStruct((T, CONV_W), BF16)],
        scratch_shapes=[pltpu.VMEM((HALO + tt, CONV_W), F32), pltpu.VMEM((SLAB_ROWS, CONV_W), F32)],
        compiler_params=_params("parallel"),
    )(zu, zg, zu, zg, cw, cb, lg, lb, og)


def _chunked(tm):
    return pl.BlockSpec((N_FF, tm, FF_CHUNK), lambda i: (0, i, 0))


def _fwd_out(x, mix_a, mix_c, w_out):
    T = x.shape[0]
    tm = 512

    def body(x_ref, a_ref, c_ref, w_ref, o_ref):
        o_ref[...] = (x_ref[...] + _dot(a_ref[...], w_ref[0:ATTN_W, :])
                      + _dot(c_ref[...], w_ref[ATTN_W:, :]))

    return _call(
        body, name="fwd_out", grid=(T // tm,),
        in_specs=[_rows(tm, D_MODEL), _rows(tm, ATTN_W), _rows(tm, CONV_W), RESIDENT],
        out_specs=_rows(tm, D_MODEL),
        out_shape=jax.ShapeDtypeStruct((T, D_MODEL), F32),
        compiler_params=_params("parallel"),
    )(x, mix_a, mix_c, w_out)


def _mlp_fwd(x, g, wup_t, wdown):
    T = x.shape[0]
    tm = 512

    def body(x_ref, g_ref, wu_ref, wd_ref, o_ref, up_ref):
        xv = x_ref[...]
        xh, _ = _rms(xv)
        h = (xh * g_ref[...]).astype(BF16)
        acc = xv
        for c in range(N_FF):
            rows = slice(c * FF_CHUNK, (c + 1) * FF_CHUNK)
            up = _dot_nt(h, wu_ref[rows, :])
            up_ref[c] = up.astype(BF16)
            act = jnp.square(jnp.maximum(up, 0.0))
            acc = acc + _dot(act.astype(BF16), wd_ref[rows, :])
        o_ref[...] = acc

    return _call(
        body, name="mlp_fwd", grid=(T // tm,),
        in_specs=[_rows(tm, D_MODEL), RESIDENT, RESIDENT, RESIDENT],
        out_specs=[_rows(tm, D_MODEL), _chunked(tm)],
        out_shape=[jax.ShapeDtypeStruct((T, D_MODEL), F32), jax.ShapeDtypeStruct((N_FF, T, FF_CHUNK), BF16)],
        compiler_params=_params("parallel"),
    )(x, g, wup_t, wdown)


def _loss_head(y, target):
    T = y.shape[0]
    tm = 512

    def body(y_ref, t_ref, l_ref, d_ref):
        @pl.when(pl.program_id(0) == 0)
        def _():
            l_ref[...] = jnp.zeros_like(l_ref)
        e = y_ref[...] - t_ref[...]
        d_ref[...] = e / D_MODEL
        l_ref[...] += 0.5 * jnp.sum(jnp.mean(e * e, axis=-1, keepdims=True))

    return _call(
        body, name="loss_head", grid=(T // tm,),
        in_specs=[_rows(tm, D_MODEL), _rows(tm, D_MODEL)],
        out_specs=[_acc_rows(128, 8), _rows(tm, D_MODEL)],
        out_shape=[jax.ShapeDtypeStruct((8, 128), F32), jax.ShapeDtypeStruct((T, D_MODEL), F32)],
        compiler_params=_params("arbitrary"),
    )(y, target)


def _bias_table(rel_bias, bucket):
    def body(rb_ref, bk_ref, o_ref):
        bk = bk_ref[...]
        for h in range(N_HEADS):
            acc = jnp.zeros((BLOCK, BLOCK), F32)
            for b in range(N_BUCKETS):
                acc = jnp.where(bk == b, rb_ref[b, h], acc)
            o_ref[h // GQA, :, (h % GQA) * BLOCK:(h % GQA + 1) * BLOCK] = acc

    return _call(
        body, name="bias_table", in_specs=[IN_SMEM, RESIDENT], out_specs=RESIDENT,
        out_shape=jax.ShapeDtypeStruct((N_KV, BLOCK, GQA * BLOCK), F32),
    )(rel_bias, bucket)


def _mlp_bwd(dx2, x1, up, g, wup_t, wdown):
    T = x1.shape[0]
    tm = 512

    def body(d_ref, x_ref, up_ref, g_ref, wu_ref, wd_ref, dx_ref, dup_ref, h_ref, dg_ref, db_ref):
        @pl.when(pl.program_id(0) == 0)
        def _():
            dg_ref[...] = jnp.zeros_like(dg_ref)
        d2 = d_ref[...]
        d2b = d2.astype(BF16)
        db_ref[...] = d2b
        xh, r = _rms(x_ref[...])
        gv = g_ref[...]
        h_ref[...] = (xh * gv).astype(BF16)
        dh = jnp.zeros((tm, D_MODEL), F32)
        for c in range(N_FF):
            rows = slice(c * FF_CHUNK, (c + 1) * FF_CHUNK)
            dact = _dot_nt(d2b, wd_ref[rows, :])
            dup = (dact * (2.0 * jnp.maximum(up_ref[c].astype(F32), 0.0))).astype(BF16)
            dup_ref[c] = dup
            dh = dh + _dot(dup, wu_ref[rows, :])
        dg_ref[...] += _colsum(dh * xh)
        dx_ref[...] = d2 + _rms_bwd(dh * gv, xh, r)

    return _call(
        body, name="mlp_bwd", grid=(T // tm,),
        in_specs=[_rows(tm, D_MODEL), _rows(tm, D_MODEL), _chunked(tm), RESIDENT, RESIDENT, RESIDENT],
        out_specs=[_rows(tm, D_MODEL), _chunked(tm), _rows(tm, D_MODEL), _acc_rows(D_MODEL), _rows(tm, D_MODEL)],
        out_shape=[jax.ShapeDtypeStruct((T, D_MODEL), F32), jax.ShapeDtypeStruct((N_FF, T, FF_CHUNK), BF16),
                   jax.ShapeDtypeStruct((T, D_MODEL), BF16), jax.ShapeDtypeStruct((1, D_MODEL), F32),
                   jax.ShapeDtypeStruct((T, D_MODEL), BF16)],
        compiler_params=_params("arbitrary"),
    )(dx2, x1, up, g, wup_t, wdown)


def _bwd_out(dx1, w_out):
    T = dx1.shape[0]
    tm = 512

    def body(d_ref, w_ref, da_ref, dc_ref, db_ref):
        db = d_ref[...].astype(BF16)
        db_ref[...] = db
        dm = _dot_nt(db, w_ref[...])
        da_ref[...] = dm[:, :ATTN_W]
        dc_ref[...] = dm[:, ATTN_W:]

    return _call(
        body, name="bwd_out", grid=(T // tm,),
        in_specs=[_rows(tm, D_MODEL), RESIDENT],
        out_specs=[_rows(tm, ATTN_W), _rows(tm, CONV_W), _rows(tm, D_MODEL)],
        out_shape=[jax.ShapeDtypeStruct((T, ATTN_W), F32), jax.ShapeDtypeStruct((T, CONV_W), F32),
                   jax.ShapeDtypeStruct((T, D_MODEL), BF16)],
        compiler_params=_params("parallel"),
    )(dx1, w_out)


def _wgrad_one_block(a, b, name):
    T, tr = a.shape
    tk = min(T, 4096)
    nk = T // tk

    def body(a_ref, b_ref, o_ref, acc):
        k = pl.program_id(0)

        @pl.when(k == 0)
        def _():
            acc[...] = jnp.zeros_like(acc)
        acc[...] += _dot_tn(a_ref[...], b_ref[...])

        @pl.when(k == nk - 1)
        def _():
            o_ref[...] = acc[...].astype(BF16)

    return _call(
        body, name=name, grid=(nk,),
        in_specs=[_rows(tk, tr), _rows(tk, D_MODEL)],
        out_specs=pl.BlockSpec((tr, D_MODEL), lambda k: (0, 0)),
        out_shape=jax.ShapeDtypeStruct((tr, D_MODEL), BF16),
        scratch_shapes=[pltpu.VMEM((tr, D_MODEL), F32)],
        compiler_params=_params("arbitrary"),
    )(a, b)


def _wgrad(a, b, name, chunked=False, square_relu=False, tr=FF_CHUNK):
    if chunked:
        nr, T, tr = a.shape
    else:
        T, R = a.shape
        tr = min(R, tr)
        nr = R // tr
    piece = min(T, 2048)
    if nr == 1:
        return _wgrad_one_block(a, b, name)

    def body(a_ref, b_ref, o_ref):
        if square_relu:
            tot = jnp.zeros((tr, D_MODEL), F32)
            for k0 in range(0, T, piece):
                av = jnp.square(jnp.maximum(a_ref[k0:k0 + piece, :].astype(F32), 0.0)).astype(BF16)
                tot = tot + _dot_tn(av, b_ref[k0:k0 + piece, :])
        else:
            tot = _dot_tn(a_ref[...], b_ref[...])
        o_ref[...] = tot.astype(BF16)

    a_spec = (pl.BlockSpec((None, T, tr), lambda r: (r, 0, 0)) if chunked else pl.BlockSpec((T, tr), lambda r: (0, r)))
    return _call(
        body, name=name, grid=(nr,),
        in_specs=[a_spec, RESIDENT],
        out_specs=pl.BlockSpec((tr, D_MODEL), lambda r: (r, 0)),
        out_shape=jax.ShapeDtypeStruct((nr * tr, D_MODEL), BF16),
        compiler_params=_params("parallel"),
    )(a, b)


def _conv_bwd(dmix, y, zu, zg, cw, lg, lb, og):
    T = y.shape[0]
    tt = 512
    nt = T // tt
    per = tt // HALO
    prev_spec = pl.BlockSpec((HALO, CONV_W), lambda i: (jnp.maximum(i * per - 1, 0), 0))
    next_spec = pl.BlockSpec((HALO, CONV_W), lambda i: (jnp.minimum((i + 1) * per, nt * per - 1), 0))

    def body(dm_ref, dmn_ref, y_ref, yn_ref, u_ref, g_ref, uh_ref, gh_ref, cw_ref, lg_ref, lb_ref, og_ref,
             du_ref, dg_ref, pg_ref, hbuf, dybuf, dwacc, pacc, slab):
        i = pl.program_id(0)

        @pl.when(i == 0)
        def _():
            dwacc[...] = jnp.zeros_like(dwacc)
            pacc[...] = jnp.zeros_like(pacc)
        lgv, lbv, ogv = lg_ref[...], lb_ref[...], og_ref[...]

        def chain(yv, dm):
            yn, rstd, ln, sg, ch, r = _conv_post(yv, lgv, lbv, ogv)
            dc = _rms_bwd(dm * ogv, ch, r)
            dln = dc * sg * (1.0 + ln * (1.0 - sg))
            dyn = dln * lgv
            dy = rstd * (dyn - jnp.mean(dyn, axis=-1, keepdims=True)
                         - yn * jnp.mean(dyn * yn, axis=-1, keepdims=True))
            return dy, dm * ch, dln * yn, dln

        hbuf[0:HALO, :] = jnp.where(i > 0, uh_ref[...] * _sig(gh_ref[...]), 0.0)
        for r0 in range(0, tt, CONV_CHUNK):
            rs = slice(r0, r0 + CONV_CHUNK)
            hbuf[HALO + r0:HALO + r0 + CONV_CHUNK, :] = u_ref[rs, :] * _sig(g_ref[rs, :])
            dy, p_og, p_lg, p_lb = chain(y_ref[rs, :], dm_ref[rs, :])
            dybuf[rs, :] = dy
            for k, part in enumerate((dy, p_lg, p_lb, p_og)):
                pacc[8 * k:8 * k + 8, :] += _fold8(part)
        dyh, _, _, _ = chain(yn_ref[...], dmn_ref[...])
        dybuf[tt:tt + HALO, :] = jnp.where(i < nt - 1, dyh, 0.0)
        for r0 in range(0, tt, CONV_CHUNK):
            rs = slice(r0, r0 + CONV_CHUNK)
            dy = dybuf[rs, :]
            dh = jnp.zeros((CONV_CHUNK, CONV_W), F32)
            for j, win in _shifted_taps(dybuf, slab, r0, 0):
                dh = dh + cw_ref[CONV_K - 1 - j:CONV_K - j, :] * win
            for j, win in _shifted_taps(hbuf, slab, r0, HALO - (CONV_K - 1)):
                dwacc[8 * j:8 * j + 8, :] += _fold8(dy * win)
            sgt = _sig(g_ref[rs, :])
            du_ref[rs, :] = (dh * sgt).astype(BF16)
            dg_ref[rs, :] = (dh * u_ref[rs, :] * sgt * (1.0 - sgt)).astype(BF16)

        @pl.when(i == nt - 1)
        def _():
            pg_ref[...] = jnp.zeros_like(pg_ref)
            for j in range(CONV_K):
                pg_ref[j:j + 1, :] = _colsum(dwacc[8 * j:8 * j + 8, :])
            for k in range(4):
                pg_ref[32 + k:33 + k, :] = _colsum(pacc[8 * k:8 * k + 8, :])

    return _call(
        body, name="conv_bwd", grid=(nt,),
        in_specs=[_rows(tt, CONV_W), next_spec, _rows(tt, CONV_W), next_spec, _rows(tt, CONV_W), _rows(tt, CONV_W),
                  prev_spec, prev_spec] + [RESIDENT] * 4,
        out_specs=[_rows(tt, CONV_W), _rows(tt, CONV_W), _acc_rows(CONV_W, 40)],
        out_shape=[jax.ShapeDtypeStruct((T, CONV_W), BF16), jax.ShapeDtypeStruct((T, CONV_W), BF16),
                   jax.ShapeDtypeStruct((40, CONV_W), F32)],
        scratch_shapes=[pltpu.VMEM((HALO + tt, CONV_W), F32), pltpu.VMEM((tt + HALO, CONV_W), F32),
                        pltpu.VMEM((8 * HALO, CONV_W), F32), pltpu.VMEM((32, CONV_W), F32),
                        pltpu.VMEM((SLAB_ROWS, CONV_W), F32)],
        compiler_params=_params("arbitrary"),
    )(dmix, dmix, y, y, zu, zg, zu, zg, cw, lg, lb, og)


def _attn_bwd(dmix, a, zq, zkv, bias, sinks, qg, kg, og):
    T = zq.shape[0]
    rows = ATTN_SUB * BLOCK
    ns = T // rows
    nb = T // BLOCK
    cur = lambda w: pl.BlockSpec((rows, w), lambda n: (jnp.minimum(n, ns - 1), 0))
    scale = 1.0 / math.sqrt(HEAD_DIM)
    done = rows - BLOCK

    def body(dm_ref, a_ref, q_ref, kvc_ref, kvp_ref, bias_ref, sinks_ref, qgt_ref, kg_ref, kgt_ref, og_ref,
             dq_ref, dkv_ref, db_ref, dog_ref, dqg_ref, dkg_ref, dsk_ref, carry):
        n = pl.program_id(0)

        @pl.when(n == 0)
        def _():
            for ref in (db_ref, dog_ref, dqg_ref, dkg_ref, dsk_ref, carry):
                ref[...] = jnp.zeros_like(ref)

        @pl.when(n < ns)
        def _():
            tri = _tri_t()
            ogv, qgt, kgv, kgt = og_ref[...], qgt_ref[...], kg_ref[...], kgt_ref[...]
            lane = lax.broadcasted_iota(jnp.int32, (1, 128), 1)
            kvs = [kvp_ref[...]] + [kvc_ref[i * BLOCK:(i + 1) * BLOCK, :] for i in range(ATTN_SUB)]
            keys = [[_kv_block(kv, kh, kgv, kgt) for kh in range(N_KV)] for kv in kvs]
            dkn = [[jnp.zeros((BLOCK, HEAD_DIM), F32)] * N_KV for _ in kvs]
            dv = [[jnp.zeros((BLOCK, HEAD_DIM), F32)] * N_KV for _ in kvs]
            dsk = jnp.zeros((1, 128), F32)
            for i in range(ATTN_SUB):
                blk = slice(i * BLOCK, (i + 1) * BLOCK)
                ah, ra = _rms(a_ref[blk, :])
                dm = dm_ref[blk, :]
                dog_ref[...] += _colsum(dm * ah)
                da_t = _rms_bwd(dm * ogv, ah, ra).T
                zq_t = q_ref[blk, :].T
                has_prev = (n > 0) if i == 0 else None
                dqs = []
                for kh in range(N_KV):
                    kp, kc = keys[i][kh], keys[i + 1][kh]
                    c = _attn_group(zq_t, kp, kc, kh, bias_ref, sinks_ref, qgt, tri, has_prev)
                    dob = _heads_side_by_side(da_t, kh).astype(BF16)
                    pn = c["pn"]
                    p_c, p_p = _split(pn.astype(BF16), tri)
                    dv[i + 1][kh] = dv[i + 1][kh] + _dot_nt(p_c, dob)
                    dv[i][kh] = dv[i][kh] + _dot_nt(p_p, dob)
                    dp = jnp.where(tri, _dot(kc["v"], dob), _dot(kp["v"], dob))
                    dl = jnp.sum(pn * dp, axis=0, keepdims=True)
                    ds = pn * (dp - dl)
                    dsr = -c["psink"] * dl
                    for g in range(GQA):
                        dsk = dsk + jnp.where(lane == kh * GQA + g, jnp.sum(jnp.where(_head_lanes(g), dsr, 0.0)), 0.0)
                    db_ref[kh] += ds
                    ds_c, ds_p = _split((ds * scale).astype(BF16), tri)
                    dqn = _dot(kc["kn_t"], ds_c) + _dot(kp["kn_t"], ds_p)
                    dkn[i + 1][kh] = dkn[i + 1][kh] + _dot_nt(ds_c, c["qn"])
                    dkn[i][kh] = dkn[i][kh] + _dot_nt(ds_p, c["qn"])
                    dqg_ref[...] += jnp.sum(dqn * c["qhat"], axis=1, keepdims=True)
                    dq_t = _rms0_bwd(dqn * qgt, c["qhat"], c["rq"])
                    dqs += [dq_t[:, g * BLOCK:(g + 1) * BLOCK] for g in range(GQA)]
                dq_ref[blk, :] = jnp.concatenate(dqs, axis=0).T.astype(BF16)
            dsk_ref[...] += dsk
            dkv = []
            for j in range(ATTN_SUB + 1):
                dk = []
                for kh in range(N_KV):
                    key = keys[j][kh]
                    dkg_ref[...] += _colsum(dkn[j][kh] * key["khat"])
                    dk.append(_rms_bwd(dkn[j][kh] * kgv, key["khat"], key["rk"]))
                dkv.append(jnp.concatenate(dk + dv[j], axis=-1))
            if done:
                dkv_ref[0:done, :] = carry[0:done, :].astype(BF16)
            dkv_ref[done:rows, :] = (carry[done:rows, :] + dkv[0]).astype(BF16)
            for j in range(1, ATTN_SUB + 1):
                carry[(j - 1) * BLOCK:j * BLOCK, :] = dkv[j]

        @pl.when(n == ns)
        def _():
            dkv_ref[...] = carry[...].astype(BF16)

    small = lambda w: pl.BlockSpec((1, w), lambda n: (0, 0))
    return _call(
        body, name="attn_bwd", grid=(ns + 1,),
        in_specs=[cur(ATTN_W), cur(ATTN_W), cur(ATTN_W), cur(2 * KV_W),
                  pl.BlockSpec((BLOCK, 2 * KV_W), lambda n: (jnp.clip(n * ATTN_SUB - 1, 0, nb - 1), 0)),
                  RESIDENT, IN_SMEM, RESIDENT, RESIDENT, RESIDENT, RESIDENT],
        out_specs=[cur(ATTN_W), pl.BlockSpec((rows, 2 * KV_W), lambda n: (jnp.maximum(n - 1, 0), 0)),
                   pl.BlockSpec((N_KV, BLOCK, GQA * BLOCK), lambda n: (0, 0, 0)),
                   small(ATTN_W), pl.BlockSpec((HEAD_DIM, 1), lambda n: (0, 0)), small(HEAD_DIM), small(128)],
        out_shape=[jax.ShapeDtypeStruct((T, ATTN_W), BF16), jax.ShapeDtypeStruct((T, 2 * KV_W), BF16),
                   jax.ShapeDtypeStruct((N_KV, BLOCK, GQA * BLOCK), F32),
                   jax.ShapeDtypeStruct((1, ATTN_W), F32), jax.ShapeDtypeStruct((HEAD_DIM, 1), F32),
                   jax.ShapeDtypeStruct((1, HEAD_DIM), F32), jax.ShapeDtypeStruct((1, 128), F32)],
        scratch_shapes=[pltpu.VMEM((rows, 2 * KV_W), F32)],
        compiler_params=_params("arbitrary"),
    )(dmix, a, zq, zkv, zkv, bias, sinks, qg.reshape(HEAD_DIM, 1), kg, kg.reshape(HEAD_DIM, 1), og)


def _bwd_in(dx1, x, dq, dkv, du, dgt, g, wt):
    T = x.shape[0]
    tm = 512

    def body(d1_ref, x_ref, dq_ref, dkv_ref, du_ref, dgt_ref, g_ref, w_ref, dx_ref, h_ref, dg_ref, dz_ref):
        @pl.when(pl.program_id(0) == 0)
        def _():
            dg_ref[...] = jnp.zeros_like(dg_ref)
        xh, r = _rms(x_ref[...])
        gv = g_ref[...]
        h_ref[...] = (xh * gv).astype(BF16)
        dz = jnp.concatenate([dq_ref[...], dkv_ref[...], du_ref[...], dgt_ref[...]], axis=-1)
        dz_ref[...] = dz
        dh = _dot(dz, w_ref[...])
        dg_ref[...] += _colsum(dh * xh)
        dx_ref[...] = d1_ref[...] + _rms_bwd(dh * gv, xh, r)

    return _call(
        body, name="bwd_in", grid=(T // tm,),
        in_specs=[_rows(tm, D_MODEL), _rows(tm, D_MODEL), _rows(tm, ATTN_W), _rows(tm, 2 * KV_W),
                  _rows(tm, CONV_W), _rows(tm, CONV_W), RESIDENT, RESIDENT],
        out_specs=[_rows(tm, D_MODEL), _rows(tm, D_MODEL), _acc_rows(D_MODEL), _rows(tm, IN_W)],
        out_shape=[jax.ShapeDtypeStruct((T, D_MODEL), F32), jax.ShapeDtypeStruct((T, D_MODEL), BF16),
                   jax.ShapeDtypeStruct((1, D_MODEL), F32), jax.ShapeDtypeStruct((T, IN_W), BF16)],
        compiler_params=_params("arbitrary"),
    )(dx1, x, dq, dkv, du, dgt, g, wt)


def _bias_grad(db, bucket):
    def body(db_ref, bk_ref, o_ref):
        bk = bk_ref[...]
        lane = lax.broadcasted_iota(jnp.int32, (1, 128), 1)
        for h in range(N_HEADS):
            cols = slice((h % GQA) * BLOCK, (h % GQA + 1) * BLOCK)
            tot = db_ref[0, h // GQA, :, cols]
            for l in range(1, DEPTH):
                tot = tot + db_ref[l, h // GQA, :, cols]
            out = jnp.zeros((1, 128), F32)
            for b in range(N_BUCKETS):
                out = jnp.where(lane == b, jnp.sum(jnp.where(bk == b, tot, 0.0)), out)
            o_ref[h:h + 1, :] = out

    return _call(
        body, name="bias_grad", in_specs=[RESIDENT, RESIDENT], out_specs=RESIDENT,
        out_shape=jax.ShapeDtypeStruct((N_HEADS, 128), F32),
        compiler_params=pltpu.CompilerParams(vmem_limit_bytes=VMEM_LIMIT),
    )(db, bucket)


def _place():
    return lax.axis_index("x"), lax.axis_index("y"), lax.axis_index("c")


def _all_gather(shards):
    na = len(shards)

    def body(*refs):
        ins, outs = refs[:na], refs[na:2 * na]
        send_sems, recv_sems, local_sems = refs[2 * na:]
        x, y, c = _place()
        me, sibling = (x, y, c), (x, y, 1 - c)
        chips = [(1 - x, y), (x, 1 - y), (1 - x, 1 - y)]

        def rows(a, p):
            r = ins[a].shape[1]
            return outs[a].at[:, pl.ds((4 * p[0] + 2 * p[1] + p[2]) * r, r), :]

        def copy(a, k, block, to, src=None):
            return pltpu.make_async_remote_copy(
                src_ref=rows(a, block) if src is None else src, dst_ref=rows(a, block),
                send_sem=send_sems.at[a, k], recv_sem=recv_sems.at[a, k], device_id=to, device_id_type=MESH)

        mine = [pltpu.make_async_copy(ins[a], rows(a, me), local_sems.at[a]) for a in range(na)]
        for cp in mine:
            cp.start()
        first = []
        for a in range(na):
            first.append(copy(a, 0, me, sibling, src=ins[a]))
            first += [copy(a, 1 + j, me, (*chip, c), src=ins[a]) for j, chip in enumerate(chips)]
        for cp in first:
            cp.start()
        passed = []
        for j, chip in enumerate(chips):
            for a in range(na):
                copy(a, 1 + j, (*chip, c), me).wait_recv()
                cp = copy(a, 4 + j, (*chip, c), sibling)
                cp.start()
                passed.append(cp)
        for a in range(na):
            copy(a, 0, sibling, me).wait_recv()
            for j, chip in enumerate(chips):
                copy(a, 4 + j, (*chip, 1 - c), me).wait_recv()
        for cp in first + passed:
            cp.wait_send()
        for cp in mine:
            cp.wait()

    return _call(
        body, name="all_gather",
        in_specs=[ANY] * na, out_specs=[ANY] * na,
        out_shape=[jax.ShapeDtypeStruct((s.shape[0], N_DEV * s.shape[1], s.shape[2]), s.dtype) for s in shards],
        scratch_shapes=[pltpu.SemaphoreType.DMA((na, 7)), pltpu.SemaphoreType.DMA((na, 7)),
                        pltpu.SemaphoreType.DMA((na,))],
    )(*shards)


IN_HBM = pl.BlockSpec(memory_space=pltpu.HBM)
IN_SEM = pl.BlockSpec(memory_space=pltpu.SEMAPHORE)
DATAFLOW = pltpu.SideEffectType.DATAFLOW_SIDE_EFFECTING


def _exchange_copies(srcs, lands, send_sems, recv_sems, scatter):
    x, y, c = _place()
    me = 4 * x + 2 * y + c
    copies = []
    for k in range(1, N_DEV):
        p = (x ^ (k >> 2), y ^ ((k >> 1) & 1), c ^ (k & 1))
        for a, (src, land) in enumerate(zip(srcs, lands)):
            r = land.shape[0] // N_DEV
            if scatter:
                src = src.at[pl.ds((4 * p[0] + 2 * p[1] + p[2]) * r, r), :]
            copies.append(pltpu.make_async_remote_copy(
                src_ref=src, dst_ref=land.at[pl.ds(me * r, r), :], send_sem=send_sems.at[a * (N_DEV - 1) + k - 1],
                recv_sem=recv_sems.at[a * (N_DEV - 1) + k - 1], device_id=p, device_id_type=MESH))
    return copies


def _own_copies(srcs, lands, send_sems, scatter):
    x, y, c = _place()
    me = 4 * x + 2 * y + c
    copies = []
    for a, (src, land) in enumerate(zip(srcs, lands)):
        r = land.shape[0] // N_DEV
        if scatter:
            src = src.at[pl.ds(me * r, r), :]
        copies.append(pltpu.make_async_copy(src, land.at[pl.ds(me * r, r), :],
                                            send_sems.at[len(srcs) * (N_DEV - 1) + a]))
    return copies


def _exchange_start(srcs, name, scatter):
    na = len(srcs)
    lands = [lax.empty((s.shape[0] * (1 if scatter else N_DEV), s.shape[1]), s.dtype) for s in srcs]

    def body(*refs):
        ins, lnd = refs[:na], refs[na:2 * na]
        send_sems, recv_sems = refs[2 * na], refs[2 * na + 1]
        token = refs[-1]
        for cp in _exchange_copies(ins, lnd, send_sems, recv_sems, scatter) + _own_copies(ins, lnd, send_sems, scatter):
            cp.start()
        token[...] = jnp.zeros_like(token)

    hbm = lambda a: pltpu.with_memory_space_constraint(a, pltpu.HBM)
    out = _call(
        body, name=name,
        out_shape=(pltpu.SemaphoreType.DMA((na * N_DEV,)), pltpu.SemaphoreType.DMA((na * (N_DEV - 1),)),
                   *[pltpu.HBM(a.shape, a.dtype) for a in (*srcs, *lands)], jax.ShapeDtypeStruct((8, 128), F32)),
        in_specs=[IN_HBM] * (2 * na),
        out_specs=(IN_SEM, IN_SEM, *[IN_HBM] * (2 * na), RESIDENT),
        input_output_aliases={i: 2 + i for i in range(2 * na)},
        compiler_params=pltpu.CompilerParams(has_side_effects=DATAFLOW),
    )(*[hbm(a) for a in (*srcs, *lands)])
    return (out[0], out[1], out[2:2 + na], out[2 + na:2 + 2 * na]), out[-1]


def _exchange_wait(state, after, name, scatter):
    send_sems, recv_sems, srcs, lands = state
    na = len(srcs)

    def body(*refs):
        ins, lnd = refs[:na], refs[na:2 * na]
        for cp in _exchange_copies(ins, lnd, refs[2 * na], refs[2 * na + 1], scatter):
            cp.wait_send()
            cp.wait_recv()
        for cp in _own_copies(ins, lnd, refs[2 * na], scatter):
            cp.wait()
        refs[-1][...] = jnp.zeros_like(refs[-1])

    out = _call(
        body, name=name,
        out_shape=(*[pltpu.HBM(a.shape, a.dtype) for a in (*srcs, *lands)], jax.ShapeDtypeStruct((8, 128), F32)),
        in_specs=[IN_HBM] * (2 * na) + [IN_SEM, IN_SEM, ANY],
        out_specs=(*[IN_HBM] * (2 * na), RESIDENT),
        input_output_aliases={i: i for i in range(2 * na)},
        compiler_params=pltpu.CompilerParams(has_side_effects=DATAFLOW),
    )(*srcs, *lands, send_sems, recv_sems, after)
    return out[:na], out[na:2 * na], out[-1]


def _adam_math(w, g, m, v):
    m = ADAM_B1 * m + (1.0 - ADAM_B1) * g
    v = ADAM_B2 * v + (1.0 - ADAM_B2) * jnp.square(g)
    m_hat = m / (1.0 - ADAM_B1 ** ADAM_STEP)
    v_hat = v / (1.0 - ADAM_B2 ** ADAM_STEP)
    delta = -ADAM_LR * (m_hat / (jnp.sqrt(v_hat) + ADAM_EPS) + ADAM_WD * w)
    return delta, m, v


def _adam_big(parts, w, m, v):
    nl, r, cdim = w.shape
    tr = r if r <= 256 else 256
    ni = r // tr
    spec = pl.BlockSpec((None, tr, cdim), lambda l, i: (l, i, 0))

    def part_spec(layer):
        return pl.BlockSpec((N_DEV, tr, cdim),
                            lambda l, i: (0, jnp.where(l == layer, i, jnp.where(l < layer, 0, ni - 1)), 0))

    def body(*refs):
        p_refs = refs[:nl]
        w_ref, m_ref, v_ref, g_ref, d_ref, nm_ref, nv_ref = refs[nl:]
        for layer in range(nl):
            @pl.when(pl.program_id(0) == layer)
            def _(p_ref=p_refs[layer]):
                g = p_ref[0].astype(F32)
                for s in range(1, N_DEV):
                    g = g + p_ref[s].astype(F32)
                d, nm, nv = _adam_math(w_ref[...], g, m_ref[...], v_ref[...])
                g_ref[...] = g
                d_ref[...] = d
                nm_ref[...] = nm
                nv_ref[...] = nv

    return _call(
        body, name="adam_big", grid=(nl, ni),
        in_specs=[part_spec(layer) for layer in range(nl)] + [spec, spec, spec],
        out_specs=[spec] * 4,
        out_shape=[jax.ShapeDtypeStruct(w.shape, F32)] * 4,
        compiler_params=_params("arbitrary", "arbitrary"),
    )(*parts, w, m, v)


def _sum_slots(parts):
    def body(p_ref, o_ref):
        g = p_ref[0]
        for s in range(1, N_DEV):
            g = g + p_ref[s]
        o_ref[...] = g

    return _call(body, name="sum_slots", in_specs=[RESIDENT], out_specs=RESIDENT,
                 out_shape=jax.ShapeDtypeStruct(parts.shape[1:], F32))(parts)


def _adam_small(ws, gs, ms, vs):
    n = len(ws)

    def body(*refs):
        ins, outs = refs[:4 * n], refs[4 * n:]
        for k in range(n):
            d, nm, nv = _adam_math(ins[k][...], ins[n + k][...], ins[2 * n + k][...], ins[3 * n + k][...])
            outs[k][...] = d
            outs[n + k][...] = nm
            outs[2 * n + k][...] = nv

    out = _call(body, name="adam_small", in_specs=[RESIDENT] * (4 * n), out_specs=[RESIDENT] * (3 * n),
                out_shape=[jax.ShapeDtypeStruct(w.shape, F32) for w in ws] * 3)(*ws, *gs, *ms, *vs)
    return out[:n], out[n:2 * n], out[2 * n:]


def _pack(arrays):
    parts = []
    for a in arrays:
        flat = a.reshape(-1)
        n = flat.shape[0]
        padded = -(-n // 1024) * 1024
        parts.append(jnp.pad(flat, (0, padded - n)).reshape(padded // 128, 128))
    return jnp.concatenate(parts, axis=0)


def _unpack(packed, shapes):
    out, row = [], 0
    for shp in shapes:
        n = int(np.prod(shp))
        rows = -(-n // 1024) * 8
        out.append(packed[row:row + rows].reshape(-1)[:n].reshape(shp))
        row += rows
    return out


def _t5_bucket_table():
    kj = np.arange(BLOCK)[:, None]
    qi = np.arange(BLOCK)[None, :]
    n = (qi - kj) % BLOCK
    max_exact = N_BUCKETS // 2
    large = max_exact + (np.log(np.maximum(n, 1) / max_exact) / np.log(MAX_DIST / max_exact)
                         * (N_BUCKETS - max_exact)).astype(np.int32)
    large = np.minimum(large, N_BUCKETS - 1)
    return np.where(n < max_exact, n, large).astype(np.int32)


SMALL_NAMES = ("rel_bias", "norm_mix_g", "q_norm_g", "k_norm_g", "sinks", "conv_b", "conv_ln_g", "conv_ln_b",
               "attn_out_g", "conv_out_g", "norm_mlp_g")


def _local_step(x, target, small, conv_w, get_w, put_g, token):
    bucket = jnp.asarray(_t5_bucket_table())
    bias = _bias_table(small["rel_bias"], bucket)
    row = lambda a, l: a[l][None, :]
    cw_pad = jnp.pad(conv_w, ((0, 0), (0, HALO - CONV_K), (0, 0)))
    saved, weights = [], []
    for l in range(DEPTH):
        wt_in, w_out, tok = get_w(l, 0, x)
        zq, zkv, zu, zg = _fwd_in(x, row(small["norm_mix_g"], l) + tok + (token if l == 0 else 0.0), wt_in)
        a, mix_a = _attn_fwd(zq, zkv, bias, small["sinks"][l], row(small["q_norm_g"], l), row(small["k_norm_g"], l),
                             row(small["attn_out_g"], l))
        y, mix_c = _conv_fwd(zu, zg, cw_pad[l], row(small["conv_b"], l), row(small["conv_ln_g"], l),
                             row(small["conv_ln_b"], l), row(small["conv_out_g"], l))
        x1 = _fwd_out(x, mix_a, mix_c, w_out)
        wt_up, w_down, tok = get_w(l, 1, x1)
        weights.append((wt_in, w_out, wt_up, w_down))
        x2, up = _mlp_fwd(x1, row(small["norm_mlp_g"], l) + tok, wt_up, w_down)
        saved.append((x, zq, zkv, zu, zg, a, mix_a, y, mix_c, x1, up))
        x = x2
    loss_part, dx = _loss_head(x, target)

    gs = {n: [None] * DEPTH for n in SMALL_NAMES if n != "rel_bias"}
    g_conv_w, dbias = [None] * DEPTH, [None] * DEPTH
    token = 0.0
    for l in reversed(range(DEPTH)):
        x0, zq, zkv, zu, zg, a, mix_a, y, mix_c, x1, up = saved[l]
        wt_in, w_out, wt_up, w_down = weights[l]
        dx1, dup, h2, gs["norm_mlp_g"][l], dxb = _mlp_bwd(dx, x1, up, row(small["norm_mlp_g"], l) + token, wt_up, w_down)
        g_up = _wgrad(dup, h2, "wgrad_up", chunked=True)
        g_down = _wgrad(up, dxb, "wgrad_down", chunked=True, square_relu=True)
        token = put_g(l, 1, (g_up, g_down))
        dma, dmc, dx1b = _bwd_out(dx1, w_out)
        g_out = jnp.concatenate([_wgrad(mix_a, dx1b, "wgrad_out_a"), _wgrad(mix_c, dx1b, "wgrad_out_c")], axis=0)
        du, dgt, pg = _conv_bwd(dmc, y, zu, zg, cw_pad[l], row(small["conv_ln_g"], l) + token,
                                row(small["conv_ln_b"], l), row(small["conv_out_g"], l))
        g_conv_w[l] = pg[:CONV_K]
        gs["conv_b"][l], gs["conv_ln_g"][l], gs["conv_ln_b"][l], gs["conv_out_g"][l] = pg[32], pg[33], pg[34], pg[35]
        dq, dkv, dbias[l], dog, dqg, dkg, dsk = _attn_bwd(
            dma, a, zq, zkv, bias, small["sinks"][l], row(small["q_norm_g"], l), row(small["k_norm_g"], l),
            row(small["attn_out_g"], l))
        gs["attn_out_g"][l], gs["q_norm_g"][l], gs["k_norm_g"][l], gs["sinks"][l] = dog, dqg, dkg, dsk[0, :N_HEADS]
        dx, h, gs["norm_mix_g"][l], dz = _bwd_in(dx1, x0, dq, dkv, du, dgt, row(small["norm_mix_g"], l), wt_in)
        g_in = _wgrad(dz, h, "wgrad_in", tr=256)
        token = put_g(l, 0, (g_in, g_out))
    small_grads = {n: jnp.stack([jnp.reshape(v, (-1,)) for v in vals]) for n, vals in gs.items()}
    small_grads["rel_bias"] = _bias_grad(jnp.stack(dbias), bucket)[:, :N_BUCKETS].T
    return loss_part, dx, small_grads, jnp.stack(g_conv_w)


def kernel(x, rel_bias, norm_mix_g, w_in, q_norm_g, k_norm_g, sinks, conv_w, conv_b, conv_ln_g, conv_ln_b, attn_out_g, conv_out_g, w_out, norm_mlp_g, w_mlp_up, w_mlp_down, loss_target, m_rel_bias, m_norm_mix_g, m_w_in, m_q_norm_g, m_k_norm_g, m_sinks, m_conv_w, m_conv_b, m_conv_ln_g, m_conv_ln_b, m_attn_out_g, m_conv_out_g, m_w_out, m_norm_mlp_g, m_w_mlp_up, m_w_mlp_down, v_rel_bias, v_norm_mix_g, v_w_in, v_q_norm_g, v_k_norm_g, v_sinks, v_conv_w, v_conv_b, v_conv_ln_g, v_conv_ln_b, v_attn_out_g, v_conv_out_g, v_w_out, v_norm_mlp_g, v_w_mlp_up, v_w_mlp_down):
    args = dict(locals())
    small = {n: args[n] for n in SMALL_NAMES}
    tr = lambda a: jnp.swapaxes(a, 1, 2)
    me = 4 * lax.axis_index("x") + 2 * lax.axis_index("y") + lax.axis_index("c")

    shards = (tr(w_in).astype(BF16), w_out.astype(BF16), tr(w_mlp_up).astype(BF16), w_mlp_down.astype(BF16))
    cw_sh = jnp.pad(tr(conv_w), ((0, 0), (0, 0), (0, HALO - CONV_K))).reshape(DEPTH * (CONV_W // N_DEV), HALO)
    gathers = {}

    def start_gather(l, half, dep):
        arrays = [s[l] for s in shards[2 * half:2 * half + 2]]
        if (l, half) == (0, 0):
            arrays = [cw_sh] + arrays
        if dep is not None:
            arrays = [a + dep[0, 0].astype(a.dtype) for a in arrays]
        gathers[l, half], tok = _exchange_start(arrays, f"gather_{l}{'ab'[half]}_start", scatter=False)
        return tok

    landed = {}

    def get_w(l, half, after):
        if (l, half) not in landed:
            _, lands, tok = _exchange_wait(gathers[l, half], after, f"gather_{l}{'ab'[half]}_wait", scatter=False)
            if l + 1 < DEPTH:
                tok = tok + start_gather(l + 1, half, tok)
            landed[l, half] = (*lands, tok[0, 0])
        return landed[l, half]

    token = start_gather(0, 0, None)
    token = token + start_gather(0, 1, token)
    cw_all, *first = get_w(0, 0, token)
    landed[0, 0] = tuple(first)
    conv_w_full = jnp.transpose(cw_all.reshape(N_DEV, DEPTH, CONV_W // N_DEV, HALO), (1, 3, 0, 2))
    conv_w_full = conv_w_full.reshape(DEPTH, HALO, CONV_W)[:, :CONV_K, :]

    scatters = {}

    last = {}

    def put_g(l, half, grads):
        scatters[l, half], last["token"] = _exchange_start(list(grads), f"scatter_{l}{'ab'[half]}_start", scatter=True)
        return last["token"][0, 0]

    loss_part, dx, small_grads, g_conv_w = _local_step(
        x[0], loss_target[0], small, conv_w_full, get_w, put_g, token[0, 0])

    big_w = (tr(w_in), w_out, tr(w_mlp_up), w_mlp_down)
    big_m = (tr(m_w_in), m_w_out, tr(m_w_mlp_up), m_w_mlp_down)
    big_v = (tr(v_w_in), v_w_out, tr(v_w_mlp_up), v_w_mlp_down)
    big_out, after = [None] * 4, last["token"]
    for half in (1, 0):
        parts = [[None] * DEPTH, [None] * DEPTH]
        for l in reversed(range(DEPTH)):
            _, lands, _ = _exchange_wait(scatters[l, half], after, f"scatter_{l}{'ab'[half]}_wait", scatter=True)
            for k, land in enumerate(lands):
                parts[k][l] = land.reshape(N_DEV, land.shape[0] // N_DEV, land.shape[1])
        for k in range(2):
            big_out[2 * half + k] = _adam_big(parts[k], big_w[2 * half + k], big_m[2 * half + k], big_v[2 * half + k])
        after = big_out[2 * half + 1][0]
    for k in (0, 2):
        big_out[k] = [tr(o) for o in big_out[k]]

    order = [n for n in SMALL_NAMES]
    packed = _pack([small_grads[n] for n in order] + [g_conv_w])
    slots, = _all_gather((packed[None],))
    summed = _sum_slots(slots[0].reshape(N_DEV, packed.shape[0], 128))
    shapes = [small[n].shape for n in order] + [(DEPTH, CONV_K, CONV_W)]
    sg = _unpack(summed, shapes)
    g_small = dict(zip(order, sg[:-1]))
    g_small["conv_w"] = lax.dynamic_slice_in_dim(sg[-1], me * (CONV_W // N_DEV), CONV_W // N_DEV, axis=2)
    names = order + ["conv_w"]
    deltas, new_m, new_v = _adam_small([args[n] for n in names], [g_small[n] for n in names],
                                       [args["m_" + n] for n in names], [args["v_" + n] for n in names])
    res = {"grad": g_small, "delta": dict(zip(names, deltas)), "new_m": dict(zip(names, new_m)),
           "new_v": dict(zip(names, new_v))}
    for k, n in enumerate(("w_in", "w_out", "w_mlp_up", "w_mlp_down")):
        for kind, val in zip(("grad", "delta", "new_m", "new_v"), big_out[k]):
            res[kind][n] = val

    loss = lax.psum(loss_part[0, 0], ("x", "y", "c"))
    weights = ("rel_bias", "norm_mix_g", "w_in", "q_norm_g", "k_norm_g", "sinks", "conv_w", "conv_b", "conv_ln_g",
               "conv_ln_b", "attn_out_g", "conv_out_g", "w_out", "norm_mlp_g", "w_mlp_up", "w_mlp_down")
    return (loss, dx[None], *[res[kind][n] for kind in ("grad", "delta", "new_m", "new_v") for n in weights])
```

```python
import math

import numpy as np
import jax
import jax.numpy as jnp
from jax import lax
from jax.experimental import pallas as pl
from jax.experimental.pallas import tpu as pltpu

F32, BF16 = jnp.float32, jnp.bfloat16
D_MODEL = 1024
DEPTH = 4
HEAD_DIM = 64
N_HEADS = 8
N_KV = 2
GQA = N_HEADS // N_KV
ATTN_W = N_HEADS * HEAD_DIM
KV_W = N_KV * HEAD_DIM
CONV_W = D_MODEL - ATTN_W
IN_W = ATTN_W + 2 * KV_W + 2 * CONV_W
BLOCK = 128
CONV_K = 31
HALO = 32
N_BUCKETS = 32
MAX_DIST = 128
D_FF = 4 * D_MODEL
FF_CHUNK = 512
N_FF = D_FF // FF_CHUNK
EPS = 1e-6
NEG = -1e30
N_DEV = 8
ADAM_LR, ADAM_B1, ADAM_B2, ADAM_EPS, ADAM_WD, ADAM_STEP = 0.001, 0.9, 0.999, 1e-08, 0.01, 10
VMEM_LIMIT = 56 * 1024 * 1024
MESH = pl.DeviceIdType.MESH

RESIDENT = pl.BlockSpec(memory_space=pltpu.VMEM)
IN_SMEM = pl.BlockSpec(memory_space=pltpu.SMEM)
ANY = pl.BlockSpec(memory_space=pl.ANY)


def _call(body, **kw):
    return pl.pallas_call(body, **kw)


def _params(*sem):
    return pltpu.CompilerParams(dimension_semantics=sem, vmem_limit_bytes=VMEM_LIMIT)


def _dot(a, b):
    return lax.dot_general(a, b, (((1,), (0,)), ((), ())), preferred_element_type=F32)


def _dot_nt(a, b):
    return lax.dot_general(a, b, (((1,), (1,)), ((), ())), preferred_element_type=F32)


def _dot_tn(a, b):
    return lax.dot_general(a, b, (((0,), (0,)), ((), ())), preferred_element_type=F32)


def _sig(x):
    return 1.0 / (1.0 + jnp.exp(-x))


def _rms(x):
    r = lax.rsqrt(jnp.mean(x * x, axis=-1, keepdims=True) + EPS)
    return x * r, r


def _rms_bwd(dy_g, xh, r):
    return r * (dy_g - xh * jnp.mean(dy_g * xh, axis=-1, keepdims=True))


def _rows(tm, w):
    return pl.BlockSpec((tm, w), lambda i: (i, 0))


def _acc_rows(w, rows=1):
    return pl.BlockSpec((rows, w), lambda i: (0, 0))


def _colsum(x):
    return jnp.sum(x, axis=0, keepdims=True)


def _fwd_in(x, g, wt):
    T = x.shape[0]
    tm = 512

    def body(x_ref, g_ref, w_ref, q_ref, kv_ref, u_ref, gt_ref):
        xh, _ = _rms(x_ref[...])
        h = (xh * g_ref[...]).astype(BF16)
        z = _dot_nt(h, w_ref[...])
        q_ref[...] = z[:, :ATTN_W]
        kv_ref[...] = z[:, ATTN_W:ATTN_W + 2 * KV_W]
        u_ref[...] = z[:, ATTN_W + 2 * KV_W:ATTN_W + 2 * KV_W + CONV_W]
        gt_ref[...] = z[:, ATTN_W + 2 * KV_W + CONV_W:]

    widths = (ATTN_W, 2 * KV_W, CONV_W, CONV_W)
    return _call(
        body, name="fwd_in", grid=(T // tm,),
        in_specs=[_rows(tm, D_MODEL), RESIDENT, RESIDENT],
        out_specs=[_rows(tm, w) for w in widths],
        out_shape=[jax.ShapeDtypeStruct((T, w), F32) for w in widths],
        compiler_params=_params("parallel"),
    )(x, g, wt)


ATTN_SUB = 16


def _rms0(x):
    r = lax.rsqrt(jnp.mean(x * x, axis=0, keepdims=True) + EPS)
    return x * r, r


def _rms0_bwd(dy_g, xh, r):
    return r * (dy_g - xh * jnp.mean(dy_g * xh, axis=0, keepdims=True))


def _tri_t():
    kj = lax.broadcasted_iota(jnp.int32, (BLOCK, BLOCK), 0)
    qi = lax.broadcasted_iota(jnp.int32, (BLOCK, BLOCK), 1)
    return jnp.concatenate([kj <= qi] * GQA, axis=1)


def _head_lanes(g):
    lane = lax.broadcasted_iota(jnp.int32, (1, GQA * BLOCK), 1)
    return (lane >= g * BLOCK) & (lane < (g + 1) * BLOCK)


def _heads_side_by_side(xt, kh):
    return jnp.concatenate(
        [xt[(kh * GQA + g) * HEAD_DIM:(kh * GQA + g + 1) * HEAD_DIM, :] for g in range(GQA)], axis=1)


def _kv_block(kv, kh, kg, kg_t):
    kvt = kv.T
    khat, rk = _rms(kv[:, kh * HEAD_DIM:(kh + 1) * HEAD_DIM])
    khat_t, _ = _rms0(kvt[kh * HEAD_DIM:(kh + 1) * HEAD_DIM, :])
    return dict(khat=khat, rk=rk, kn=(khat * kg).astype(BF16), kn_t=(khat_t * kg_t).astype(BF16),
                v=kv[:, KV_W + kh * HEAD_DIM:KV_W + (kh + 1) * HEAD_DIM].astype(BF16),
                v_t=kvt[KV_W + kh * HEAD_DIM:KV_W + (kh + 1) * HEAD_DIM, :].astype(BF16))


def _attn_group(zq_t, kp, kc, kh, bias_ref, sinks_ref, qg_t, tri, has_prev):
    scale = 1.0 / math.sqrt(HEAD_DIM)
    qhat, rq = _rms0(_heads_side_by_side(zq_t, kh))
    qf = qhat * qg_t
    qn = qf.astype(BF16)
    qs = (qf * scale).astype(BF16)
    s = jnp.where(tri, _dot(kc["kn"], qs), _dot(kp["kn"], qs)) + bias_ref[kh]
    if has_prev is not None:
        s = jnp.where(tri | has_prev, s, NEG)
    sink = jnp.zeros((1, GQA * BLOCK), F32)
    for g in range(GQA):
        sink = jnp.where(_head_lanes(g), sinks_ref[kh * GQA + g], sink)
    m = jnp.maximum(jnp.max(s, axis=0, keepdims=True), sink)
    p = jnp.exp(s - m)
    es = jnp.exp(sink - m)
    inv = 1.0 / (jnp.sum(p, axis=0, keepdims=True) + es)
    return dict(qhat=qhat, rq=rq, qn=qn, pn=p * inv, psink=es * inv)


def _split(pb, tri):
    zero = jnp.zeros_like(pb)
    return jnp.where(tri, pb, zero), jnp.where(tri, zero, pb)


def _attn_fwd(zq, zkv, bias, sinks, qg, kg, og):
    T = zq.shape[0]
    rows = ATTN_SUB * BLOCK

    def body(q_ref, kvc_ref, kvp_ref, bias_ref, sinks_ref, qgt_ref, kg_ref, kgt_ref, og_ref, a_ref, mix_ref):
        n = pl.program_id(0)
        tri = _tri_t()
        qgt, kgv, kgt = qgt_ref[...], kg_ref[...], kgt_ref[...]
        kvs = [kvp_ref[...]] + [kvc_ref[i * BLOCK:(i + 1) * BLOCK, :] for i in range(ATTN_SUB)]
        keys = [[_kv_block(kv, kh, kgv, kgt) for kh in range(N_KV)] for kv in kvs]
        for i in range(ATTN_SUB):
            zq_t = q_ref[i * BLOCK:(i + 1) * BLOCK, :].T
            has_prev = (n > 0) if i == 0 else None
            outs = []
            for kh in range(N_KV):
                kp, kc = keys[i][kh], keys[i + 1][kh]
                c = _attn_group(zq_t, kp, kc, kh, bias_ref, sinks_ref, qgt, tri, has_prev)
                p_c, p_p = _split(c["pn"].astype(BF16), tri)
                o_t = _dot(kc["v_t"], p_c) + _dot(kp["v_t"], p_p)
                outs += [o_t[:, g * BLOCK:(g + 1) * BLOCK] for g in range(GQA)]
            a = jnp.concatenate(outs, axis=0).T
            a_ref[i * BLOCK:(i + 1) * BLOCK, :] = a
            ah, _ = _rms(a)
            mix_ref[i * BLOCK:(i + 1) * BLOCK, :] = (ah * og_ref[...]).astype(BF16)

    return _call(
        body, name="attn_fwd", grid=(T // rows,),
        in_specs=[_rows(rows, ATTN_W), _rows(rows, 2 * KV_W),
                  pl.BlockSpec((BLOCK, 2 * KV_W), lambda n: (jnp.maximum(n * ATTN_SUB - 1, 0), 0)),
                  RESIDENT, IN_SMEM, RESIDENT, RESIDENT, RESIDENT, RESIDENT],
        out_specs=[_rows(rows, ATTN_W), _rows(rows, ATTN_W)],
        out_shape=[jax.ShapeDtypeStruct((T, ATTN_W), F32), jax.ShapeDtypeStruct((T, ATTN_W), BF16)],
        compiler_params=_params("parallel"),
    )(zq, zkv, zkv, bias, sinks, qg.reshape(HEAD_DIM, 1), kg, kg.reshape(HEAD_DIM, 1), og)


def _conv_post(y, lg, lb, og):
    mu = jnp.mean(y, axis=-1, keepdims=True)
    yc = y - mu
    rstd = lax.rsqrt(jnp.mean(yc * yc, axis=-1, keepdims=True) + EPS)
    yn = yc * rstd
    ln = yn * lg + lb
    sg = _sig(ln)
    c = ln * sg
    ch, r = _rms(c)
    return yn, rstd, ln, sg, ch, r


CONV_CHUNK = 64


SLAB_ROWS = CONV_CHUNK + 8 * ((CONV_K - 1) // 8)


def _shifted_taps(buf, slab, r0, base):
    for b in range(8):
        taps = range(b, CONV_K, 8)
        span = CONV_CHUNK + 8 * (len(taps) - 1)
        slab[0:span, :] = buf[r0 + base + b:r0 + base + b + span, :]
        for a, j in enumerate(taps):
            yield j, slab[8 * a:8 * a + CONV_CHUNK, :]


def _fold8(x):
    return jnp.sum(x.reshape(x.shape[0] // 8, 8, x.shape[1]), axis=0)


def _conv_fwd(zu, zg, cw, cb, lg, lb, og):
    T = zu.shape[0]
    tt = 512
    halo_spec = pl.BlockSpec((HALO, CONV_W), lambda i: (jnp.maximum(i * (tt // HALO) - 1, 0), 0))

    def body(u_ref, g_ref, uh_ref, gh_ref, cw_ref, cb_ref, lg_ref, lb_ref, og_ref, y_ref, mix_ref, buf, slab):
        i = pl.program_id(0)
        hal = uh_ref[...] * _sig(gh_ref[...])
        buf[0:HALO, :] = jnp.where(i > 0, hal, 0.0)
        buf[HALO:HALO + tt, :] = u_ref[...] * _sig(g_ref[...])
        cbv, lgv, lbv, ogv = cb_ref[...], lg_ref[...], lb_ref[...], og_ref[...]
        for r0 in range(0, tt, CONV_CHUNK):
            rs = slice(r0, r0 + CONV_CHUNK)
            acc = jnp.zeros((CONV_CHUNK, CONV_W), F32)
            for j, win in _shifted_taps(buf, slab, r0, HALO - (CONV_K - 1)):
                acc = acc + cw_ref[j:j + 1, :] * win
            y = acc + cbv
            y_ref[rs, :] = y
            mix_ref[rs, :] = (_conv_post(y, lgv, lbv, ogv)[4] * ogv).astype(BF16)

    return _call(
        body, name="conv_fwd", grid=(T // tt,),
        in_specs=[_rows(tt, CONV_W), _rows(tt, CONV_W), halo_spec, halo_spec] + [RESIDENT] * 5,
        out_specs=[_rows(tt, CONV_W), _rows(tt, CONV_W)],
        out_shape=[jax.ShapeDtypeStruct((T, CONV_W), F32), jax.ShapeDtypeStruct((T, CONV_W), BF16)],
        scratch_shapes=[pltpu.VMEM((HALO + tt, CONV_W), F32), pltpu.VMEM((SLAB_ROWS, CONV_W), F32)],
        compiler_params=_params("parallel"),
    )(zu, zg, zu, zg, cw, cb, lg, lb, og)


def _chunked(tm):
    return pl.BlockSpec((N_FF, tm, FF_CHUNK), lambda i: (0, i, 0))


def _fwd_out(x, mix_a, mix_c, w_out):
    T = x.shape[0]
    tm = 512

    def body(x_ref, a_ref, c_ref, w_ref, o_ref):
        o_ref[...] = (x_ref[...] + _dot(a_ref[...], w_ref[0:ATTN_W, :])
                      + _dot(c_ref[...], w_ref[ATTN_W:, :]))

    return _call(
        body, name="fwd_out", grid=(T // tm,),
        in_specs=[_rows(tm, D_MODEL), _rows(tm, ATTN_W), _rows(tm, CONV_W), RESIDENT],
        out_specs=_rows(tm, D_MODEL),
        out_shape=jax.ShapeDtypeStruct((T, D_MODEL), F32),
        compiler_params=_params("parallel"),
    )(x, mix_a, mix_c, w_out)


def _mlp_fwd(x, g, wup_t, wdown):
    T = x.shape[0]
    tm = 512

    def body(x_ref, g_ref, wu_ref, wd_ref, o_ref, up_ref):
        xv = x_ref[...]
        xh, _ = _rms(xv)
        h = (xh * g_ref[...]).astype(BF16)
        acc = xv
        for c in range(N_FF):
            rows = slice(c * FF_CHUNK, (c + 1) * FF_CHUNK)
            up = _dot_nt(h, wu_ref[rows, :])
            up_ref[c] = up.astype(BF16)
            act = jnp.square(jnp.maximum(up, 0.0))
            acc = acc + _dot(act.astype(BF16), wd_ref[rows, :])
        o_ref[...] = acc

    return _call(
        body, name="mlp_fwd", grid=(T // tm,),
        in_specs=[_rows(tm, D_MODEL), RESIDENT, RESIDENT, RESIDENT],
        out_specs=[_rows(tm, D_MODEL), _chunked(tm)],
        out_shape=[jax.ShapeDtypeStruct((T, D_MODEL), F32), jax.ShapeDtypeStruct((N_FF, T, FF_CHUNK), BF16)],
        compiler_params=_params("parallel"),
    )(x, g, wup_t, wdown)


def _loss_head(y, target):
    T = y.shape[0]
    tm = 512

    def body(y_ref, t_ref, l_ref, d_ref):
        @pl.when(pl.program_id(0) == 0)
        def _():
            l_ref[...] = jnp.zeros_like(l_ref)
        e = y_ref[...] - t_ref[...]
        d_ref[...] = e / D_MODEL
        l_ref[...] += 0.5 * jnp.sum(jnp.mean(e * e, axis=-1, keepdims=True))

    return _call(
        body, name="loss_head", grid=(T // tm,),
        in_specs=[_rows(tm, D_MODEL), _rows(tm, D_MODEL)],
        out_specs=[_acc_rows(128, 8), _rows(tm, D_MODEL)],
        out_shape=[jax.ShapeDtypeStruct((8, 128), F32), jax.ShapeDtypeStruct((T, D_MODEL), F32)],
        compiler_params=_params("arbitrary"),
    )(y, target)


def _bias_table(rel_bias, bucket):
    def body(rb_ref, bk_ref, o_ref):
        bk = bk_ref[...]
        for h in range(N_HEADS):
            acc = jnp.zeros((BLOCK, BLOCK), F32)
            for b in range(N_BUCKETS):
                acc = jnp.where(bk == b, rb_ref[b, h], acc)
            o_ref[h // GQA, :, (h % GQA) * BLOCK:(h % GQA + 1) * BLOCK] = acc

    return _call(
        body, name="bias_table", in_specs=[IN_SMEM, RESIDENT], out_specs=RESIDENT,
        out_shape=jax.ShapeDtypeStruct((N_KV, BLOCK, GQA * BLOCK), F32),
    )(rel_bias, bucket)


def _mlp_bwd(dx2, x1, up, g, wup_t, wdown):
    T = x1.shape[0]
    tm = 512

    def body(d_ref, x_ref, up_ref, g_ref, wu_ref, wd_ref, dx_ref, dup_ref, h_ref, dg_ref, db_ref):
        @pl.when(pl.program_id(0) == 0)
        def _():
            dg_ref[...] = jnp.zeros_like(dg_ref)
        d2 = d_ref[...]
        d2b = d2.astype(BF16)
        db_ref[...] = d2b
        xh, r = _rms(x_ref[...])
        gv = g_ref[...]
        h_ref[...] = (xh * gv).astype(BF16)
        dh = jnp.zeros((tm, D_MODEL), F32)
        for c in range(N_FF):
            rows = slice(c * FF_CHUNK, (c + 1) * FF_CHUNK)
            dact = _dot_nt(d2b, wd_ref[rows, :])
            dup = (dact * (2.0 * jnp.maximum(up_ref[c].astype(F32), 0.0))).astype(BF16)
            dup_ref[c] = dup
            dh = dh + _dot(dup, wu_ref[rows, :])
        dg_ref[...] += _colsum(dh * xh)
        dx_ref[...] = d2 + _rms_bwd(dh * gv, xh, r)

    return _call(
        body, name="mlp_bwd", grid=(T // tm,),
        in_specs=[_rows(tm, D_MODEL), _rows(tm, D_MODEL), _chunked(tm), RESIDENT, RESIDENT, RESIDENT],
        out_specs=[_rows(tm, D_MODEL), _chunked(tm), _rows(tm, D_MODEL), _acc_rows(D_MODEL), _rows(tm, D_MODEL)],
        out_shape=[jax.ShapeDtypeStruct((T, D_MODEL), F32), jax.ShapeDtypeStruct((N_FF, T, FF_CHUNK), BF16),
                   jax.ShapeDtypeStruct((T, D_MODEL), BF16), jax.ShapeDtypeStruct((1, D_MODEL), F32),
                   jax.ShapeDtypeStruct((T, D_MODEL), BF16)],
        compiler_params=_params("arbitrary"),
    )(dx2, x1, up, g, wup_t, wdown)


def _bwd_out(dx1, w_out):
    T = dx1.shape[0]
    tm = 512

    def body(d_ref, w_ref, da_ref, dc_ref, db_ref):
        db = d_ref[...].astype(BF16)
        db_ref[...] = db
        dm = _dot_nt(db, w_ref[...])
        da_ref[...] = dm[:, :ATTN_W]
        dc_ref[...] = dm[:, ATTN_W:]

    return _call(
        body, name="bwd_out", grid=(T // tm,),
        in_specs=[_rows(tm, D_MODEL), RESIDENT],
        out_specs=[_rows(tm, ATTN_W), _rows(tm, CONV_W), _rows(tm, D_MODEL)],
        out_shape=[jax.ShapeDtypeStruct((T, ATTN_W), F32), jax.ShapeDtypeStruct((T, CONV_W), F32),
                   jax.ShapeDtypeStruct((T, D_MODEL), BF16)],
        compiler_params=_params("parallel"),
    )(dx1, w_out)


def _wgrad_one_block(a, b, name):
    T, tr = a.shape
    tk = min(T, 4096)
    nk = T // tk

    def body(a_ref, b_ref, o_ref, acc):
        k = pl.program_id(0)

        @pl.when(k == 0)
        def _():
            acc[...] = jnp.zeros_like(acc)
        acc[...] += _dot_tn(a_ref[...], b_ref[...])

        @pl.when(k == nk - 1)
        def _():
            o_ref[...] = acc[...].astype(BF16)

    return _call(
        body, name=name, grid=(nk,),
        in_specs=[_rows(tk, tr), _rows(tk, D_MODEL)],
        out_specs=pl.BlockSpec((tr, D_MODEL), lambda k: (0, 0)),
        out_shape=jax.ShapeDtypeStruct((tr, D_MODEL), BF16),
        scratch_shapes=[pltpu.VMEM((tr, D_MODEL), F32)],
        compiler_params=_params("arbitrary"),
    )(a, b)


def _wgrad(a, b, name, chunked=False, square_relu=False, tr=FF_CHUNK):
    if chunked:
        nr, T, tr = a.shape
    else:
        T, R = a.shape
        tr = min(R, tr)
        nr = R // tr
    piece = min(T, 2048)
    if nr == 1:
        return _wgrad_one_block(a, b, name)

    def body(a_ref, b_ref, o_ref):
        if square_relu:
            tot = jnp.zeros((tr, D_MODEL), F32)
            for k0 in range(0, T, piece):
                av = jnp.square(jnp.maximum(a_ref[k0:k0 + piece, :].astype(F32), 0.0)).astype(BF16)
                tot = tot + _dot_tn(av, b_ref[k0:k0 + piece, :])
        else:
            tot = _dot_tn(a_ref[...], b_ref[...])
        o_ref[...] = tot.astype(BF16)

    a_spec = (pl.BlockSpec((None, T, tr), lambda r: (r, 0, 0)) if chunked else pl.BlockSpec((T, tr), lambda r: (0, r)))
    return _call(
        body, name=name, grid=(nr,),
        in_specs=[a_spec, RESIDENT],
        out_specs=pl.BlockSpec((tr, D_MODEL), lambda r: (r, 0)),
        out_shape=jax.ShapeDtypeStruct((nr * tr, D_MODEL), BF16),
        compiler_params=_params("parallel"),
    )(a, b)


def _conv_bwd(dmix, y, zu, zg, cw, lg, lb, og):
    T = y.shape[0]
    tt = 512
    nt = T // tt
    per = tt // HALO
    prev_spec = pl.BlockSpec((HALO, CONV_W), lambda i: (jnp.maximum(i * per - 1, 0), 0))
    next_spec = pl.BlockSpec((HALO, CONV_W), lambda i: (jnp.minimum((i + 1) * per, nt * per - 1), 0))

    def body(dm_ref, dmn_ref, y_ref, yn_ref, u_ref, g_ref, uh_ref, gh_ref, cw_ref, lg_ref, lb_ref, og_ref,
             du_ref, dg_ref, pg_ref, hbuf, dybuf, dwacc, pacc, slab):
        i = pl.program_id(0)

        @pl.when(i == 0)
        def _():
            dwacc[...] = jnp.zeros_like(dwacc)
            pacc[...] = jnp.zeros_like(pacc)
        lgv, lbv, ogv = lg_ref[...], lb_ref[...], og_ref[...]

        def chain(yv, dm):
            yn, rstd, ln, sg, ch, r = _conv_post(yv, lgv, lbv, ogv)
            dc = _rms_bwd(dm * ogv, ch, r)
            dln = dc * sg * (1.0 + ln * (1.0 - sg))
            dyn = dln * lgv
            dy = rstd * (dyn - jnp.mean(dyn, axis=-1, keepdims=True)
                         - yn * jnp.mean(dyn * yn, axis=-1, keepdims=True))
            return dy, dm * ch, dln * yn, dln

        hbuf[0:HALO, :] = jnp.where(i > 0, uh_ref[...] * _sig(gh_ref[...]), 0.0)
        for r0 in range(0, tt, CONV_CHUNK):
            rs = slice(r0, r0 + CONV_CHUNK)
            hbuf[HALO + r0:HALO + r0 + CONV_CHUNK, :] = u_ref[rs, :] * _sig(g_ref[rs, :])
            dy, p_og, p_lg, p_lb = chain(y_ref[rs, :], dm_ref[rs, :])
            dybuf[rs, :] = dy
            for k, part in enumerate((dy, p_lg, p_lb, p_og)):
                pacc[8 * k:8 * k + 8, :] += _fold8(part)
        dyh, _, _, _ = chain(yn_ref[...], dmn_ref[...])
        dybuf[tt:tt + HALO, :] = jnp.where(i < nt - 1, dyh, 0.0)
        for r0 in range(0, tt, CONV_CHUNK):
            rs = slice(r0, r0 + CONV_CHUNK)
            dy = dybuf[rs, :]
            dh = jnp.zeros((CONV_CHUNK, CONV_W), F32)
            for j, win in _shifted_taps(dybuf, slab, r0, 0):
                dh = dh + cw_ref[CONV_K - 1 - j:CONV_K - j, :] * win
            for j, win in _shifted_taps(hbuf, slab, r0, HALO - (CONV_K - 1)):
                dwacc[8 * j:8 * j + 8, :] += _fold8(dy * win)
            sgt = _sig(g_ref[rs, :])
            du_ref[rs, :] = (dh * sgt).astype(BF16)
            dg_ref[rs, :] = (dh * u_ref[rs, :] * sgt * (1.0 - sgt)).astype(BF16)

        @pl.when(i == nt - 1)
        def _():
            pg_ref[...] = jnp.zeros_like(pg_ref)
            for j in range(CONV_K):
                pg_ref[j:j + 1, :] = _colsum(dwacc[8 * j:8 * j + 8, :])
            for k in range(4):
                pg_ref[32 + k:33 + k, :] = _colsum(pacc[8 * k:8 * k + 8, :])

    return _call(
        body, name="conv_bwd", grid=(nt,),
        in_specs=[_rows(tt, CONV_W), next_spec, _rows(tt, CONV_W), next_spec, _rows(tt, CONV_W), _rows(tt, CONV_W),
                  prev_spec, prev_spec] + [RESIDENT] * 4,
        out_specs=[_rows(tt, CONV_W), _rows(tt, CONV_W), _acc_rows(CONV_W, 40)],
        out_shape=[jax.ShapeDtypeStruct((T, CONV_W), BF16), jax.ShapeDtypeStruct((T, CONV_W), BF16),
                   jax.ShapeDtypeStruct((40, CONV_W), F32)],
        scratch_shapes=[pltpu.VMEM((HALO + tt, CONV_W), F32), pltpu.VMEM((tt + HALO, CONV_W), F32),
                        pltpu.VMEM((8 * HALO, CONV_W), F32), pltpu.VMEM((32, CONV_W), F32),
                        pltpu.VMEM((SLAB_ROWS, CONV_W), F32)],
        compiler_params=_params("arbitrary"),
    )(dmix, dmix, y, y, zu, zg, zu, zg, cw, lg, lb, og)


def _attn_bwd(dmix, a, zq, zkv, bias, sinks, qg, kg, og):
    T = zq.shape[0]
    rows = ATTN_SUB * BLOCK
    ns = T // rows
    nb = T // BLOCK
    cur = lambda w: pl.BlockSpec((rows, w), lambda n: (jnp.minimum(n, ns - 1), 0))
    scale = 1.0 / math.sqrt(HEAD_DIM)
    done = rows - BLOCK

    def body(dm_ref, a_ref, q_ref, kvc_ref, kvp_ref, bias_ref, sinks_ref, qgt_ref, kg_ref, kgt_ref, og_ref,
             dq_ref, dkv_ref, db_ref, dog_ref, dqg_ref, dkg_ref, dsk_ref, carry):
        n = pl.program_id(0)

        @pl.when(n == 0)
        def _():
            for ref in (db_ref, dog_ref, dqg_ref, dkg_ref, dsk_ref, carry):
                ref[...] = jnp.zeros_like(ref)

        @pl.when(n < ns)
        def _():
            tri = _tri_t()
            ogv, qgt, kgv, kgt = og_ref[...], qgt_ref[...], kg_ref[...], kgt_ref[...]
            lane = lax.broadcasted_iota(jnp.int32, (1, 128), 1)
            kvs = [kvp_ref[...]] + [kvc_ref[i * BLOCK:(i + 1) * BLOCK, :] for i in range(ATTN_SUB)]
            keys = [[_kv_block(kv, kh, kgv, kgt) for kh in range(N_KV)] for kv in kvs]
            dkn = [[jnp.zeros((BLOCK, HEAD_DIM), F32)] * N_KV for _ in kvs]
            dv = [[jnp.zeros((BLOCK, HEAD_DIM), F32)] * N_KV for _ in kvs]
            dsk = jnp.zeros((1, 128), F32)
            for i in range(ATTN_SUB):
                blk = slice(i * BLOCK, (i + 1) * BLOCK)
                ah, ra = _rms(a_ref[blk, :])
                dm = dm_ref[blk, :]
                dog_ref[...] += _colsum(dm * ah)
                da_t = _rms_bwd(dm * ogv, ah, ra).T
                zq_t = q_ref[blk, :].T
                has_prev = (n > 0) if i == 0 else None
                dqs = []
                for kh in range(N_KV):
                    kp, kc = keys[i][kh], keys[i + 1][kh]
                    c = _attn_group(zq_t, kp, kc, kh, bias_ref, sinks_ref, qgt, tri, has_prev)
                    dob = _heads_side_by_side(da_t, kh).astype(BF16)
                    pn = c["pn"]
                    p_c, p_p = _split(pn.astype(BF16), tri)
                    dv[i + 1][kh] = dv[i + 1][kh] + _dot_nt(p_c, dob)
                    dv[i][kh] = dv[i][kh] + _dot_nt(p_p, dob)
                    dp = jnp.where(tri, _dot(kc["v"], dob), _dot(kp["v"], dob))
                    dl = jnp.sum(pn * dp, axis=0, keepdims=True)
                    ds = pn * (dp - dl)
                    dsr = -c["psink"] * dl
                    for g in range(GQA):
                        dsk = dsk + jnp.where(lane == kh * GQA + g, jnp.sum(jnp.where(_head_lanes(g), dsr, 0.0)), 0.0)
                    db_ref[kh] += ds
                    ds_c, ds_p = _split((ds * scale).astype(BF16), tri)
                    dqn = _dot(kc["kn_t"], ds_c) + _dot(kp["kn_t"], ds_p)
                    dkn[i + 1][kh] = dkn[i + 1][kh] + _dot_nt(ds_c, c["qn"])
                    dkn[i][kh] = dkn[i][kh] + _dot_nt(ds_p, c["qn"])
                    dqg_ref[...] += jnp.sum(dqn * c["qhat"], axis=1, keepdims=True)
                    dq_t = _rms0_bwd(dqn * qgt, c["qhat"], c["rq"])
                    dqs += [dq_t[:, g * BLOCK:(g + 1) * BLOCK] for g in range(GQA)]
                dq_ref[blk, :] = jnp.concatenate(dqs, axis=0).T.astype(BF16)
            dsk_ref[...] += dsk
            dkv = []
            for j in range(ATTN_SUB + 1):
                dk = []
                for kh in range(N_KV):
                    key = keys[j][kh]
                    dkg_ref[...] += _colsum(dkn[j][kh] * key["khat"])
                    dk.append(_rms_bwd(dkn[j][kh] * kgv, key["khat"], key["rk"]))
                dkv.append(jnp.concatenate(dk + dv[j], axis=-1))
            if done:
                dkv_ref[0:done, :] = carry[0:done, :].astype(BF16)
            dkv_ref[done:rows, :] = (carry[done:rows, :] + dkv[0]).astype(BF16)
            for j in range(1, ATTN_SUB + 1):
                carry[(j - 1) * BLOCK:j * BLOCK, :] = dkv[j]

        @pl.when(n == ns)
        def _():
            dkv_ref[...] = carry[...].astype(BF16)

    small = lambda w: pl.BlockSpec((1, w), lambda n: (0, 0))
    return _call(
        body, name="attn_bwd", grid=(ns + 1,),
        in_specs=[cur(ATTN_W), cur(ATTN_W), cur(ATTN_W), cur(2 * KV_W),
                  pl.BlockSpec((BLOCK, 2 * KV_W), lambda n: (jnp.clip(n * ATTN_SUB - 1, 0, nb - 1), 0)),
                  RESIDENT, IN_SMEM, RESIDENT, RESIDENT, RESIDENT, RESIDENT],
        out_specs=[cur(ATTN_W), pl.BlockSpec((rows, 2 * KV_W), lambda n: (jnp.maximum(n - 1, 0), 0)),
                   pl.BlockSpec((N_KV, BLOCK, GQA * BLOCK), lambda n: (0, 0, 0)),
                   small(ATTN_W), pl.BlockSpec((HEAD_DIM, 1), lambda n: (0, 0)), small(HEAD_DIM), small(128)],
        out_shape=[jax.ShapeDtypeStruct((T, ATTN_W), BF16), jax.ShapeDtypeStruct((T, 2 * KV_W), BF16),
                   jax.ShapeDtypeStruct((N_KV, BLOCK, GQA * BLOCK), F32),
                   jax.ShapeDtypeStruct((1, ATTN_W), F32), jax.ShapeDtypeStruct((HEAD_DIM, 1), F32),
                   jax.ShapeDtypeStruct((1, HEAD_DIM), F32), jax.ShapeDtypeStruct((1, 128), F32)],
        scratch_shapes=[pltpu.VMEM((rows, 2 * KV_W), F32)],
        compiler_params=_params("arbitrary"),
    )(dmix, a, zq, zkv, zkv, bias, sinks, qg.reshape(HEAD_DIM, 1), kg, kg.reshape(HEAD_DIM, 1), og)


def _bwd_in(dx1, x, dq, dkv, du, dgt, g, wt):
    T = x.shape[0]
    tm = 512

    def body(d1_ref, x_ref, dq_ref, dkv_ref, du_ref, dgt_ref, g_ref, w_ref, dx_ref, h_ref, dg_ref, dz_ref):
        @pl.when(pl.program_id(0) == 0)
        def _():
            dg_ref[...] = jnp.zeros_like(dg_ref)
        xh, r = _rms(x_ref[...])
        gv = g_ref[...]
        h_ref[...] = (xh * gv).astype(BF16)
        dz = jnp.concatenate([dq_ref[...], dkv_ref[...], du_ref[...], dgt_ref[...]], axis=-1)
        dz_ref[...] = dz
        dh = _dot(dz, w_ref[...])
        dg_ref[...] += _colsum(dh * xh)
        dx_ref[...] = d1_ref[...] + _rms_bwd(dh * gv, xh, r)

    return _call(
        body, name="bwd_in", grid=(T // tm,),
        in_specs=[_rows(tm, D_MODEL), _rows(tm, D_MODEL), _rows(tm, ATTN_W), _rows(tm, 2 * KV_W),
                  _rows(tm, CONV_W), _rows(tm, CONV_W), RESIDENT, RESIDENT],
        out_specs=[_rows(tm, D_MODEL), _rows(tm, D_MODEL), _acc_rows(D_MODEL), _rows(tm, IN_W)],
        out_shape=[jax.ShapeDtypeStruct((T, D_MODEL), F32), jax.ShapeDtypeStruct((T, D_MODEL), BF16),
                   jax.ShapeDtypeStruct((1, D_MODEL), F32), jax.ShapeDtypeStruct((T, IN_W), BF16)],
        compiler_params=_params("arbitrary"),
    )(dx1, x, dq, dkv, du, dgt, g, wt)


def _bias_grad(db, bucket):
    def body(db_ref, bk_ref, o_ref):
        bk = bk_ref[...]
        lane = lax.broadcasted_iota(jnp.int32, (1, 128), 1)
        for h in range(N_HEADS):
            cols = slice((h % GQA) * BLOCK, (h % GQA + 1) * BLOCK)
            tot = db_ref[0, h // GQA, :, cols]
            for l in range(1, DEPTH):
                tot = tot + db_ref[l, h // GQA, :, cols]
            out = jnp.zeros((1, 128), F32)
            for b in range(N_BUCKETS):
                out = jnp.where(lane == b, jnp.sum(jnp.where(bk == b, tot, 0.0)), out)
            o_ref[h:h + 1, :] = out

    return _call(
        body, name="bias_grad", in_specs=[RESIDENT, RESIDENT], out_specs=RESIDENT,
        out_shape=jax.ShapeDtypeStruct((N_HEADS, 128), F32),
        compiler_params=pltpu.CompilerParams(vmem_limit_bytes=VMEM_LIMIT),
    )(db, bucket)


def _place():
    return lax.axis_index("x"), lax.axis_index("y"), lax.axis_index("c")


def _all_gather(shards):
    na = len(shards)

    def body(*refs):
        ins, outs = refs[:na], refs[na:2 * na]
        send_sems, recv_sems, local_sems = refs[2 * na:]
        x, y, c = _place()
        me, sibling = (x, y, c), (x, y, 1 - c)
        chips = [(1 - x, y), (x, 1 - y), (1 - x, 1 - y)]

        def rows(a, p):
            r = ins[a].shape[1]
            return outs[a].at[:, pl.ds((4 * p[0] + 2 * p[1] + p[2]) * r, r), :]

        def copy(a, k, block, to, src=None):
            return pltpu.make_async_remote_copy(
                src_ref=rows(a, block) if src is None else src, dst_ref=rows(a, block),
                send_sem=send_sems.at[a, k], recv_sem=recv_sems.at[a, k], device_id=to, device_id_type=MESH)

        mine = [pltpu.make_async_copy(ins[a], rows(a, me), local_sems.at[a]) for a in range(na)]
        for cp in mine:
            cp.start()
        first = []
        for a in range(na):
            first.append(copy(a, 0, me, sibling, src=ins[a]))
            first += [copy(a, 1 + j, me, (*chip, c), src=ins[a]) for j, chip in enumerate(chips)]
        for cp in first:
            cp.start()
        passed = []
        for j, chip in enumerate(chips):
            for a in range(na):
                copy(a, 1 + j, (*chip, c), me).wait_recv()
                cp = copy(a, 4 + j, (*chip, c), sibling)
                cp.start()
                passed.append(cp)
        for a in range(na):
            copy(a, 0, sibling, me).wait_recv()
            for j, chip in enumerate(chips):
                copy(a, 4 + j, (*chip, 1 - c), me).wait_recv()
        for cp in first + passed:
            cp.wait_send()
        for cp in mine:
            cp.wait()

    return _call(
        body, name="all_gather",
        in_specs=[ANY] * na, out_specs=[ANY] * na,
        out_shape=[jax.ShapeDtypeStruct((s.shape[0], N_DEV * s.shape[1], s.shape[2]), s.dtype) for s in shards],
        scratch_shapes=[pltpu.SemaphoreType.DMA((na, 7)), pltpu.SemaphoreType.DMA((na, 7)),
                        pltpu.SemaphoreType.DMA((na,))],
    )(*shards)


IN_HBM = pl.BlockSpec(memory_space=pltpu.HBM)
IN_SEM = pl.BlockSpec(memory_space=pltpu.SEMAPHORE)
DATAFLOW = pltpu.SideEffectType.DATAFLOW_SIDE_EFFECTING


def _exchange_copies(srcs, lands, send_sems, recv_sems, scatter):
    x, y, c = _place()
    me = 4 * x + 2 * y + c
    copies = []
    for k in range(1, N_DEV):
        p = (x ^ (k >> 2), y ^ ((k >> 1) & 1), c ^ (k & 1))
        for a, (src, land) in enumerate(zip(srcs, lands)):
            r = land.shape[0] // N_DEV
            if scatter:
                src = src.at[pl.ds((4 * p[0] + 2 * p[1] + p[2]) * r, r), :]
            copies.append(pltpu.make_async_remote_copy(
                src_ref=src, dst_ref=land.at[pl.ds(me * r, r), :], send_sem=send_sems.at[a * (N_DEV - 1) + k - 1],
                recv_sem=recv_sems.at[a * (N_DEV - 1) + k - 1], device_id=p, device_id_type=MESH))
    return copies


def _own_copies(srcs, lands, send_sems, scatter):
    x, y, c = _place()
    me = 4 * x + 2 * y + c
    copies = []
    for a, (src, land) in enumerate(zip(srcs, lands)):
        r = land.shape[0] // N_DEV
        if scatter:
            src = src.at[pl.ds(me * r, r), :]
        copies.append(pltpu.make_async_copy(src, land.at[pl.ds(me * r, r), :],
                                            send_sems.at[len(srcs) * (N_DEV - 1) + a]))
    return copies


def _exchange_start(srcs, name, scatter):
    na = len(srcs)
    lands = [lax.empty((s.shape[0] * (1 if scatter else N_DEV), s.shape[1]), s.dtype) for s in srcs]

    def body(*refs):
        ins, lnd = refs[:na], refs[na:2 * na]
        send_sems, recv_sems = refs[2 * na], refs[2 * na + 1]
        token = refs[-1]
        for cp in _exchange_copies(ins, lnd, send_sems, recv_sems, scatter) + _own_copies(ins, lnd, send_sems, scatter):
            cp.start()
        token[...] = jnp.zeros_like(token)

    hbm = lambda a: pltpu.with_memory_space_constraint(a, pltpu.HBM)
    out = _call(
        body, name=name,
        out_shape=(pltpu.SemaphoreType.DMA((na * N_DEV,)), pltpu.SemaphoreType.DMA((na * (N_DEV - 1),)),
                   *[pltpu.HBM(a.shape, a.dtype) for a in (*srcs, *lands)], jax.ShapeDtypeStruct((8, 128), F32)),
        in_specs=[IN_HBM] * (2 * na),
        out_specs=(IN_SEM, IN_SEM, *[IN_HBM] * (2 * na), RESIDENT),
        input_output_aliases={i: 2 + i for i in range(2 * na)},
        compiler_params=pltpu.CompilerParams(has_side_effects=DATAFLOW),
    )(*[hbm(a) for a in (*srcs, *lands)])
    return (out[0], out[1], out[2:2 + na], out[2 + na:2 + 2 * na]), out[-1]


def _exchange_wait(state, after, name, scatter):
    send_sems, recv_sems, srcs, lands = state
    na = len(srcs)

    def body(*refs):
        ins, lnd = refs[:na], refs[na:2 * na]
        for cp in _exchange_copies(ins, lnd, refs[2 * na], refs[2 * na + 1], scatter):
            cp.wait_send()
            cp.wait_recv()
        for cp in _own_copies(ins, lnd, refs[2 * na], scatter):
            cp.wait()
        refs[-1][...] = jnp.zeros_like(refs[-1])

    out = _call(
        body, name=name,
        out_shape=(*[pltpu.HBM(a.shape, a.dtype) for a in (*srcs, *lands)], jax.ShapeDtypeStruct((8, 128), F32)),
        in_specs=[IN_HBM] * (2 * na) + [IN_SEM, IN_SEM, ANY],
        out_specs=(*[IN_HBM] * (2 * na), RESIDENT),
        input_output_aliases={i: i for i in range(2 * na)},
        compiler_params=pltpu.CompilerParams(has_side_effects=DATAFLOW),
    )(*srcs, *lands, send_sems, recv_sems, after)
    return out[:na], out[na:2 * na], out[-1]


def _adam_math(w, g, m, v):
    m = ADAM_B1 * m + (1.0 - ADAM_B1) * g
    v = ADAM_B2 * v + (1.0 - ADAM_B2) * jnp.square(g)
    m_hat = m / (1.0 - ADAM_B1 ** ADAM_STEP)
    v_hat = v / (1.0 - ADAM_B2 ** ADAM_STEP)
    delta = -ADAM_LR * (m_hat / (jnp.sqrt(v_hat) + ADAM_EPS) + ADAM_WD * w)
    return delta, m, v


def _adam_big(parts, w, m, v, transposed=False):
    if transposed:
        nl, cdim, r = w.shape
    else:
        nl, r, cdim = w.shape
    tr = r if r <= 256 else 256
    ni = r // tr
    spec = (pl.BlockSpec((None, cdim, tr), lambda l, i: (l, 0, i)) if transposed
            else pl.BlockSpec((None, tr, cdim), lambda l, i: (l, i, 0)))

    def part_spec(layer):
        return pl.BlockSpec((N_DEV, tr, cdim),
                            lambda l, i: (0, jnp.where(l == layer, i, jnp.where(l < layer, 0, ni - 1)), 0))

    def body(*refs):
        p_refs = refs[:nl]
        w_ref, m_ref, v_ref, g_ref, d_ref, nm_ref, nv_ref = refs[nl:]
        for layer in range(nl):
            @pl.when(pl.program_id(0) == layer)
            def _(p_ref=p_refs[layer]):
                g = p_ref[0].astype(F32)
                for s in range(1, N_DEV):
                    g = g + p_ref[s].astype(F32)
                if transposed:
                    g = g.T
                d, nm, nv = _adam_math(w_ref[...], g, m_ref[...], v_ref[...])
                g_ref[...] = g
                d_ref[...] = d
                nm_ref[...] = nm
                nv_ref[...] = nv

    return _call(
        body, name="adam_big", grid=(nl, ni),
        in_specs=[part_spec(layer) for layer in range(nl)] + [spec, spec, spec],
        out_specs=[spec] * 4,
        out_shape=[jax.ShapeDtypeStruct(w.shape, F32)] * 4,
        compiler_params=_params("arbitrary", "arbitrary"),
    )(*parts, w, m, v)


def _sum_slots(parts):
    def body(p_ref, o_ref):
        g = p_ref[0]
        for s in range(1, N_DEV):
            g = g + p_ref[s]
        o_ref[...] = g

    return _call(body, name="sum_slots", in_specs=[RESIDENT], out_specs=RESIDENT,
                 out_shape=jax.ShapeDtypeStruct(parts.shape[1:], F32))(parts)


def _adam_small(ws, gs, ms, vs):
    n = len(ws)

    def body(*refs):
        ins, outs = refs[:4 * n], refs[4 * n:]
        for k in range(n):
            d, nm, nv = _adam_math(ins[k][...], ins[n + k][...], ins[2 * n + k][...], ins[3 * n + k][...])
            outs[k][...] = d
            outs[n + k][...] = nm
            outs[2 * n + k][...] = nv

    out = _call(body, name="adam_small", in_specs=[RESIDENT] * (4 * n), out_specs=[RESIDENT] * (3 * n),
                out_shape=[jax.ShapeDtypeStruct(w.shape, F32) for w in ws] * 3)(*ws, *gs, *ms, *vs)
    return out[:n], out[n:2 * n], out[2 * n:]


def _pack(arrays):
    parts = []
    for a in arrays:
        flat = a.reshape(-1)
        n = flat.shape[0]
        padded = -(-n // 1024) * 1024
        parts.append(jnp.pad(flat, (0, padded - n)).reshape(padded // 128, 128))
    return jnp.concatenate(parts, axis=0)


def _unpack(packed, shapes):
    out, row = [], 0
    for shp in shapes:
        n = int(np.prod(shp))
        rows = -(-n // 1024) * 8
        out.append(packed[row:row + rows].reshape(-1)[:n].reshape(shp))
        row += rows
    return out


def _t5_bucket_table():
    kj = np.arange(BLOCK)[:, None]
    qi = np.arange(BLOCK)[None, :]
    n = (qi - kj) % BLOCK
    max_exact = N_BUCKETS // 2
    large = max_exact + (np.log(np.maximum(n, 1) / max_exact) / np.log(MAX_DIST / max_exact)
                         * (N_BUCKETS - max_exact)).astype(np.int32)
    large = np.minimum(large, N_BUCKETS - 1)
    return np.where(n < max_exact, n, large).astype(np.int32)


SMALL_NAMES = ("rel_bias", "norm_mix_g", "q_norm_g", "k_norm_g", "sinks", "conv_b", "conv_ln_g", "conv_ln_b",
               "attn_out_g", "conv_out_g", "norm_mlp_g")


def _local_step(x, target, small, conv_w, get_w, put_g, token):
    bucket = jnp.asarray(_t5_bucket_table())
    bias = _bias_table(small["rel_bias"], bucket)
    row = lambda a, l: a[l][None, :]
    cw_pad = jnp.pad(conv_w, ((0, 0), (0, HALO - CONV_K), (0, 0)))
    saved, weights = [], []
    for l in range(DEPTH):
        wt_in, w_out, tok = get_w(l, 0, x)
        zq, zkv, zu, zg = _fwd_in(x, row(small["norm_mix_g"], l) + tok + (token if l == 0 else 0.0), wt_in)
        a, mix_a = _attn_fwd(zq, zkv, bias, small["sinks"][l], row(small["q_norm_g"], l), row(small["k_norm_g"], l),
                             row(small["attn_out_g"], l))
        y, mix_c = _conv_fwd(zu, zg, cw_pad[l], row(small["conv_b"], l), row(small["conv_ln_g"], l),
                             row(small["conv_ln_b"], l), row(small["conv_out_g"], l))
        x1 = _fwd_out(x, mix_a, mix_c, w_out)
        wt_up, w_down, tok = get_w(l, 1, x1)
        weights.append((wt_in, w_out, wt_up, w_down))
        x2, up = _mlp_fwd(x1, row(small["norm_mlp_g"], l) + tok, wt_up, w_down)
        saved.append((x, zq, zkv, zu, zg, a, mix_a, y, mix_c, x1, up))
        x = x2
    loss_part, dx = _loss_head(x, target)

    gs = {n: [None] * DEPTH for n in SMALL_NAMES if n != "rel_bias"}
    g_conv_w, dbias = [None] * DEPTH, [None] * DEPTH
    token = 0.0
    for l in reversed(range(DEPTH)):
        x0, zq, zkv, zu, zg, a, mix_a, y, mix_c, x1, up = saved[l]
        wt_in, w_out, wt_up, w_down = weights[l]
        dx1, dup, h2, gs["norm_mlp_g"][l], dxb = _mlp_bwd(dx, x1, up, row(small["norm_mlp_g"], l) + token, wt_up, w_down)
        g_up = _wgrad(dup, h2, "wgrad_up", chunked=True)
        g_down = _wgrad(up, dxb, "wgrad_down", chunked=True, square_relu=True)
        token = put_g(l, 1, (g_up, g_down))
        dma, dmc, dx1b = _bwd_out(dx1, w_out)
        g_out = jnp.concatenate([_wgrad(mix_a, dx1b, "wgrad_out_a"), _wgrad(mix_c, dx1b, "wgrad_out_c")], axis=0)
        du, dgt, pg = _conv_bwd(dmc, y, zu, zg, cw_pad[l], row(small["conv_ln_g"], l) + token,
                                row(small["conv_ln_b"], l), row(small["conv_out_g"], l))
        g_conv_w[l] = pg[:CONV_K]
        gs["conv_b"][l], gs["conv_ln_g"][l], gs["conv_ln_b"][l], gs["conv_out_g"][l] = pg[32], pg[33], pg[34], pg[35]
        dq, dkv, dbias[l], dog, dqg, dkg, dsk = _attn_bwd(
            dma, a, zq, zkv, bias, small["sinks"][l], row(small["q_norm_g"], l), row(small["k_norm_g"], l),
            row(small["attn_out_g"], l))
        gs["attn_out_g"][l], gs["q_norm_g"][l], gs["k_norm_g"][l], gs["sinks"][l] = dog, dqg, dkg, dsk[0, :N_HEADS]
        dx, h, gs["norm_mix_g"][l], dz = _bwd_in(dx1, x0, dq, dkv, du, dgt, row(small["norm_mix_g"], l), wt_in)
        g_in = _wgrad(dz, h, "wgrad_in", tr=256)
        token = put_g(l, 0, (g_in, g_out))
    small_grads = {n: jnp.stack([jnp.reshape(v, (-1,)) for v in vals]) for n, vals in gs.items()}
    small_grads["rel_bias"] = _bias_grad(jnp.stack(dbias), bucket)[:, :N_BUCKETS].T
    return loss_part, dx, small_grads, jnp.stack(g_conv_w)


def kernel(x, rel_bias, norm_mix_g, w_in, q_norm_g, k_norm_g, sinks, conv_w, conv_b, conv_ln_g, conv_ln_b, attn_out_g, conv_out_g, w_out, norm_mlp_g, w_mlp_up, w_mlp_down, loss_target, m_rel_bias, m_norm_mix_g, m_w_in, m_q_norm_g, m_k_norm_g, m_sinks, m_conv_w, m_conv_b, m_conv_ln_g, m_conv_ln_b, m_attn_out_g, m_conv_out_g, m_w_out, m_norm_mlp_g, m_w_mlp_up, m_w_mlp_down, v_rel_bias, v_norm_mix_g, v_w_in, v_q_norm_g, v_k_norm_g, v_sinks, v_conv_w, v_conv_b, v_conv_ln_g, v_conv_ln_b, v_attn_out_g, v_conv_out_g, v_w_out, v_norm_mlp_g, v_w_mlp_up, v_w_mlp_down):
    args = dict(locals())
    small = {n: args[n] for n in SMALL_NAMES}
    tr = lambda a: jnp.swapaxes(a, 1, 2)
    me = 4 * lax.axis_index("x") + 2 * lax.axis_index("y") + lax.axis_index("c")

    shards = (tr(w_in).astype(BF16), w_out.astype(BF16), tr(w_mlp_up).astype(BF16), w_mlp_down.astype(BF16))
    cw_sh = jnp.pad(tr(conv_w), ((0, 0), (0, 0), (0, HALO - CONV_K))).reshape(DEPTH * (CONV_W // N_DEV), HALO)
    gathers = {}

    def start_gather(l, half, dep):
        arrays = [s[l] for s in shards[2 * half:2 * half + 2]]
        if (l, half) == (0, 0):
            arrays = [cw_sh] + arrays
        if dep is not None:
            arrays = [a + dep[0, 0].astype(a.dtype) for a in arrays]
        gathers[l, half], tok = _exchange_start(arrays, f"gather_{l}{'ab'[half]}_start", scatter=False)
        return tok

    landed = {}

    def get_w(l, half, after):
        if (l, half) not in landed:
            _, lands, tok = _exchange_wait(gathers[l, half], after, f"gather_{l}{'ab'[half]}_wait", scatter=False)
            if l + 1 < DEPTH:
                tok = tok + start_gather(l + 1, half, tok)
            landed[l, half] = (*lands, tok[0, 0])
        return landed[l, half]

    token = start_gather(0, 0, None)
    token = token + start_gather(0, 1, token)
    cw_all, *first = get_w(0, 0, token)
    landed[0, 0] = tuple(first)
    conv_w_full = jnp.transpose(cw_all.reshape(N_DEV, DEPTH, CONV_W // N_DEV, HALO), (1, 3, 0, 2))
    conv_w_full = conv_w_full.reshape(DEPTH, HALO, CONV_W)[:, :CONV_K, :]

    scatters = {}

    last = {}

    def put_g(l, half, grads):
        scatters[l, half], last["token"] = _exchange_start(list(grads), f"scatter_{l}{'ab'[half]}_start", scatter=True)
        return last["token"][0, 0]

    loss_part, dx, small_grads, g_conv_w = _local_step(
        x[0], loss_target[0], small, conv_w_full, get_w, put_g, token[0, 0])

    big_w = (tr(w_in), w_out, w_mlp_up, w_mlp_down)
    big_m = (tr(m_w_in), m_w_out, m_w_mlp_up, m_w_mlp_down)
    big_v = (tr(v_w_in), v_w_out, v_w_mlp_up, v_w_mlp_down)
    big_out, after = [None] * 4, last["token"]
    for half in (1, 0):
        parts = [[None] * DEPTH, [None] * DEPTH]
        for l in reversed(range(DEPTH)):
            _, lands, _ = _exchange_wait(scatters[l, half], after, f"scatter_{l}{'ab'[half]}_wait", scatter=True)
            for k, land in enumerate(lands):
                parts[k][l] = land.reshape(N_DEV, land.shape[0] // N_DEV, land.shape[1])
        for k in range(2):
            n = 2 * half + k
            big_out[n] = _adam_big(parts[k], big_w[n], big_m[n], big_v[n], transposed=(n == 2))
        after = big_out[2 * half + 1][0]
    big_out[0] = [tr(o) for o in big_out[0]]

    order = [n for n in SMALL_NAMES]
    packed = _pack([small_grads[n] for n in order] + [g_conv_w])
    slots, = _all_gather((packed[None],))
    summed = _sum_slots(slots[0].reshape(N_DEV, packed.shape[0], 128))
    shapes = [small[n].shape for n in order] + [(DEPTH, CONV_K, CONV_W)]
    sg = _unpack(summed, shapes)
    g_small = dict(zip(order, sg[:-1]))
    g_small["conv_w"] = lax.dynamic_slice_in_dim(sg[-1], me * (CONV_W // N_DEV), CONV_W // N_DEV, axis=2)
    names = order + ["conv_w"]
    deltas, new_m, new_v = _adam_small([args[n] for n in names], [g_small[n] for n in names],
                                       [args["m_" + n] for n in names], [args["v_" + n] for n in names])
    res = {"grad": g_small, "delta": dict(zip(names, deltas)), "new_m": dict(zip(names, new_m)),
           "new_v": dict(zip(names, new_v))}
    for k, n in enumerate(("w_in", "w_out", "w_mlp_up", "w_mlp_down")):
        for kind, val in zip(("grad", "delta", "new_m", "new_v"), big_out[k]):
            res[kind][n] = val

    loss = lax.psum(loss_part[0, 0], ("x", "y", "c"))
    weights = ("rel_bias", "norm_mix_g", "w_in", "q_norm_g", "k_norm_g", "sinks", "conv_w", "conv_b", "conv_ln_g",
               "conv_ln_b", "attn_out_g", "conv_out_g", "w_out", "norm_mlp_g", "w_mlp_up", "w_mlp_down")
    return (loss, dx[None], *[res[kind][n] for kind in ("grad", "delta", "new_m", "new_v") for n in weights])
```

```python
import math

import numpy as np
import jax
import jax.numpy as jnp
from jax import lax
from jax.experimental import pallas as pl
from jax.experimental.pallas import tpu as pltpu

F32, BF16 = jnp.float32, jnp.bfloat16
D_MODEL = 1024
DEPTH = 4
HEAD_DIM = 64
N_HEADS = 8
N_KV = 2
GQA = N_HEADS // N_KV
ATTN_W = N_HEADS * HEAD_DIM
KV_W = N_KV * HEAD_DIM
CONV_W = D_MODEL - ATTN_W
IN_W = ATTN_W + 2 * KV_W + 2 * CONV_W
BLOCK = 128
CONV_K = 31
HALO = 32
N_BUCKETS = 32
MAX_DIST = 128
D_FF = 4 * D_MODEL
FF_CHUNK = 512
N_FF = D_FF // FF_CHUNK
EPS = 1e-6
NEG = -1e30
N_DEV = 8
ADAM_LR, ADAM_B1, ADAM_B2, ADAM_EPS, ADAM_WD, ADAM_STEP = 0.001, 0.9, 0.999, 1e-08, 0.01, 10
VMEM_LIMIT = 56 * 1024 * 1024
MESH = pl.DeviceIdType.MESH

RESIDENT = pl.BlockSpec(memory_space=pltpu.VMEM)
IN_SMEM = pl.BlockSpec(memory_space=pltpu.SMEM)
ANY = pl.BlockSpec(memory_space=pl.ANY)


def _call(body, **kw):
    return pl.pallas_call(body, **kw)


def _params(*sem):
    return pltpu.CompilerParams(dimension_semantics=sem, vmem_limit_bytes=VMEM_LIMIT)


def _dot(a, b):
    return lax.dot_general(a, b, (((1,), (0,)), ((), ())), preferred_element_type=F32)


def _dot_nt(a, b):
    return lax.dot_general(a, b, (((1,), (1,)), ((), ())), preferred_element_type=F32)


def _dot_tn(a, b):
    return lax.dot_general(a, b, (((0,), (0,)), ((), ())), preferred_element_type=F32)


def _sig(x):
    return 1.0 / (1.0 + jnp.exp(-x))


def _rms(x):
    r = lax.rsqrt(jnp.mean(x * x, axis=-1, keepdims=True) + EPS)
    return x * r, r


def _rms_bwd(dy_g, xh, r):
    return r * (dy_g - xh * jnp.mean(dy_g * xh, axis=-1, keepdims=True))


def _rows(tm, w):
    return pl.BlockSpec((tm, w), lambda i: (i, 0))


def _acc_rows(w, rows=1):
    return pl.BlockSpec((rows, w), lambda i: (0, 0))


def _colsum(x):
    return jnp.sum(x, axis=0, keepdims=True)


def _fwd_in(x, g, wt):
    T = x.shape[0]
    tm = 512

    def body(x_ref, g_ref, w_ref, q_ref, kv_ref, u_ref, gt_ref):
        xh, _ = _rms(x_ref[...])
        h = (xh * g_ref[...]).astype(BF16)
        z = _dot_nt(h, w_ref[...])
        q_ref[...] = z[:, :ATTN_W]
        kv_ref[...] = z[:, ATTN_W:ATTN_W + 2 * KV_W]
        u_ref[...] = z[:, ATTN_W + 2 * KV_W:ATTN_W + 2 * KV_W + CONV_W]
        gt_ref[...] = z[:, ATTN_W + 2 * KV_W + CONV_W:]

    widths = (ATTN_W, 2 * KV_W, CONV_W, CONV_W)
    return _call(
        body, name="fwd_in", grid=(T // tm,),
        in_specs=[_rows(tm, D_MODEL), RESIDENT, RESIDENT],
        out_specs=[_rows(tm, w) for w in widths],
        out_shape=[jax.ShapeDtypeStruct((T, w), F32) for w in widths],
        compiler_params=_params("parallel"),
    )(x, g, wt)


ATTN_SUB = 16


def _rms0(x):
    r = lax.rsqrt(jnp.mean(x * x, axis=0, keepdims=True) + EPS)
    return x * r, r


def _rms0_bwd(dy_g, xh, r):
    return r * (dy_g - xh * jnp.mean(dy_g * xh, axis=0, keepdims=True))


def _tri_t():
    kj = lax.broadcasted_iota(jnp.int32, (BLOCK, BLOCK), 0)
    qi = lax.broadcasted_iota(jnp.int32, (BLOCK, BLOCK), 1)
    return jnp.concatenate([kj <= qi] * GQA, axis=1)


def _head_lanes(g):
    lane = lax.broadcasted_iota(jnp.int32, (1, GQA * BLOCK), 1)
    return (lane >= g * BLOCK) & (lane < (g + 1) * BLOCK)


def _heads_side_by_side(xt, kh):
    return jnp.concatenate(
        [xt[(kh * GQA + g) * HEAD_DIM:(kh * GQA + g + 1) * HEAD_DIM, :] for g in range(GQA)], axis=1)


def _kv_block(kv, kh, kg, kg_t):
    kvt = kv.T
    khat, rk = _rms(kv[:, kh * HEAD_DIM:(kh + 1) * HEAD_DIM])
    khat_t, _ = _rms0(kvt[kh * HEAD_DIM:(kh + 1) * HEAD_DIM, :])
    return dict(khat=khat, rk=rk, kn=(khat * kg).astype(BF16), kn_t=(khat_t * kg_t).astype(BF16),
                v=kv[:, KV_W + kh * HEAD_DIM:KV_W + (kh + 1) * HEAD_DIM].astype(BF16),
                v_t=kvt[KV_W + kh * HEAD_DIM:KV_W + (kh + 1) * HEAD_DIM, :].astype(BF16))


def _attn_group(zq_t, kp, kc, kh, bias_ref, sinks_ref, qg_t, tri, has_prev):
    scale = 1.0 / math.sqrt(HEAD_DIM)
    qhat, rq = _rms0(_heads_side_by_side(zq_t, kh))
    qf = qhat * qg_t
    qn = qf.astype(BF16)
    qs = (qf * scale).astype(BF16)
    s = jnp.where(tri, _dot(kc["kn"], qs), _dot(kp["kn"], qs)) + bias_ref[kh]
    if has_prev is not None:
        s = jnp.where(tri | has_prev, s, NEG)
    sink = jnp.zeros((1, GQA * BLOCK), F32)
    for g in range(GQA):
        sink = jnp.where(_head_lanes(g), sinks_ref[kh * GQA + g], sink)
    m = jnp.maximum(jnp.max(s, axis=0, keepdims=True), sink)
    p = jnp.exp(s - m)
    es = jnp.exp(sink - m)
    inv = 1.0 / (jnp.sum(p, axis=0, keepdims=True) + es)
    return dict(qhat=qhat, rq=rq, qn=qn, pn=p * inv, psink=es * inv)


def _split(pb, tri):
    zero = jnp.zeros_like(pb)
    return jnp.where(tri, pb, zero), jnp.where(tri, zero, pb)


def _attn_fwd(zq, zkv, bias, sinks, qg, kg, og):
    T = zq.shape[0]
    rows = ATTN_SUB * BLOCK

    def body(q_ref, kvc_ref, kvp_ref, bias_ref, sinks_ref, qgt_ref, kg_ref, kgt_ref, og_ref, a_ref, mix_ref):
        n = pl.program_id(0)
        tri = _tri_t()
        qgt, kgv, kgt = qgt_ref[...], kg_ref[...], kgt_ref[...]
        kvs = [kvp_ref[...]] + [kvc_ref[i * BLOCK:(i + 1) * BLOCK, :] for i in range(ATTN_SUB)]
        keys = [[_kv_block(kv, kh, kgv, kgt) for kh in range(N_KV)] for kv in kvs]
        for i in range(ATTN_SUB):
            zq_t = q_ref[i * BLOCK:(i + 1) * BLOCK, :].T
            has_prev = (n > 0) if i == 0 else None
            outs = []
            for kh in range(N_KV):
                kp, kc = keys[i][kh], keys[i + 1][kh]
                c = _attn_group(zq_t, kp, kc, kh, bias_ref, sinks_ref, qgt, tri, has_prev)
                p_c, p_p = _split(c["pn"].astype(BF16), tri)
                o_t = _dot(kc["v_t"], p_c) + _dot(kp["v_t"], p_p)
                outs += [o_t[:, g * BLOCK:(g + 1) * BLOCK] for g in range(GQA)]
            a = jnp.concatenate(outs, axis=0).T
            a_ref[i * BLOCK:(i + 1) * BLOCK, :] = a
            ah, _ = _rms(a)
            mix_ref[i * BLOCK:(i + 1) * BLOCK, :] = (ah * og_ref[...]).astype(BF16)

    return _call(
        body, name="attn_fwd", grid=(T // rows,),
        in_specs=[_rows(rows, ATTN_W), _rows(rows, 2 * KV_W),
                  pl.BlockSpec((BLOCK, 2 * KV_W), lambda n: (jnp.maximum(n * ATTN_SUB - 1, 0), 0)),
                  RESIDENT, IN_SMEM, RESIDENT, RESIDENT, RESIDENT, RESIDENT],
        out_specs=[_rows(rows, ATTN_W), _rows(rows, ATTN_W)],
        out_shape=[jax.ShapeDtypeStruct((T, ATTN_W), F32), jax.ShapeDtypeStruct((T, ATTN_W), BF16)],
        compiler_params=_params("parallel"),
    )(zq, zkv, zkv, bias, sinks, qg.reshape(HEAD_DIM, 1), kg, kg.reshape(HEAD_DIM, 1), og)


def _conv_post(y, lg, lb, og):
    mu = jnp.mean(y, axis=-1, keepdims=True)
    yc = y - mu
    rstd = lax.rsqrt(jnp.mean(yc * yc, axis=-1, keepdims=True) + EPS)
    yn = yc * rstd
    ln = yn * lg + lb
    sg = _sig(ln)
    c = ln * sg
    ch, r = _rms(c)
    return yn, rstd, ln, sg, ch, r


CONV_CHUNK = 64


SLAB_ROWS = CONV_CHUNK + 8 * ((CONV_K - 1) // 8)


def _shifted_taps(buf, slab, r0, base):
    for b in range(8):
        taps = range(b, CONV_K, 8)
        span = CONV_CHUNK + 8 * (len(taps) - 1)
        slab[0:span, :] = buf[r0 + base + b:r0 + base + b + span, :]
        for a, j in enumerate(taps):
            yield j, slab[8 * a:8 * a + CONV_CHUNK, :]


def _fold8(x):
    return jnp.sum(x.reshape(x.shape[0] // 8, 8, x.shape[1]), axis=0)


def _conv_fwd(zu, zg, cw, cb, lg, lb, og):
    T = zu.shape[0]
    tt = 512
    halo_spec = pl.BlockSpec((HALO, CONV_W), lambda i: (jnp.maximum(i * (tt // HALO) - 1, 0), 0))

    def body(u_ref, g_ref, uh_ref, gh_ref, cw_ref, cb_ref, lg_ref, lb_ref, og_ref, y_ref, mix_ref, buf, slab):
        i = pl.program_id(0)
        hal = uh_ref[...] * _sig(gh_ref[...])
        buf[0:HALO, :] = jnp.where(i > 0, hal, 0.0)
        buf[HALO:HALO + tt, :] = u_ref[...] * _sig(g_ref[...])
        cbv, lgv, lbv, ogv = cb_ref[...], lg_ref[...], lb_ref[...], og_ref[...]
        for r0 in range(0, tt, CONV_CHUNK):
            rs = slice(r0, r0 + CONV_CHUNK)
            acc = jnp.zeros((CONV_CHUNK, CONV_W), F32)
            for j, win in _shifted_taps(buf, slab, r0, HALO - (CONV_K - 1)):
                acc = acc + cw_ref[j:j + 1, :] * win
            y = acc + cbv
            y_ref[rs, :] = y
            mix_ref[rs, :] = (_conv_post(y, lgv, lbv, ogv)[4] * ogv).astype(BF16)

    return _call(
        body, name="conv_fwd", grid=(T // tt,),
        in_specs=[_rows(tt, CONV_W), _rows(tt, CONV_W), halo_spec, halo_spec] + [RESIDENT] * 5,
        out_specs=[_rows(tt, CONV_W), _rows(tt, CONV_W)],
        out_shape=[jax.ShapeDtypeStruct((T, CONV_W), F32), jax.ShapeDtypeStruct((T, CONV_W), BF16)],
        scratch_shapes=[pltpu.VMEM((HALO + tt, CONV_W), F32), pltpu.VMEM((SLAB_ROWS, CONV_W), F32)],
        compiler_params=_params("parallel"),
    )(zu, zg, zu, zg, cw, cb, lg, lb, og)


def _chunked(tm):
    return pl.BlockSpec((N_FF, tm, FF_CHUNK), lambda i: (0, i, 0))


def _fwd_out(x, mix_a, mix_c, w_out):
    T = x.shape[0]
    tm = 512

    def body(x_ref, a_ref, c_ref, w_ref, o_ref):
        o_ref[...] = (x_ref[...] + _dot(a_ref[...], w_ref[0:ATTN_W, :])
                      + _dot(c_ref[...], w_ref[ATTN_W:, :]))

    return _call(
        body, name="fwd_out", grid=(T // tm,),
        in_specs=[_rows(tm, D_MODEL), _rows(tm, ATTN_W), _rows(tm, CONV_W), RESIDENT],
        out_specs=_rows(tm, D_MODEL),
        out_shape=jax.ShapeDtypeStruct((T, D_MODEL), F32),
        compiler_params=_params("parallel"),
    )(x, mix_a, mix_c, w_out)


def _mlp_fwd(x, g, wup_t, wdown):
    T = x.shape[0]
    tm = 512

    def body(x_ref, g_ref, wu_ref, wd_ref, o_ref, up_ref):
        xv = x_ref[...]
        xh, _ = _rms(xv)
        h = (xh * g_ref[...]).astype(BF16)
        acc = xv
        for c in range(N_FF):
            rows = slice(c * FF_CHUNK, (c + 1) * FF_CHUNK)
            up = _dot_nt(h, wu_ref[rows, :])
            up_ref[c] = up.astype(BF16)
            act = jnp.square(jnp.maximum(up, 0.0))
            acc = acc + _dot(act.astype(BF16), wd_ref[rows, :])
        o_ref[...] = acc

    return _call(
        body, name="mlp_fwd", grid=(T // tm,),
        in_specs=[_rows(tm, D_MODEL), RESIDENT, RESIDENT, RESIDENT],
        out_specs=[_rows(tm, D_MODEL), _chunked(tm)],
        out_shape=[jax.ShapeDtypeStruct((T, D_MODEL), F32), jax.ShapeDtypeStruct((N_FF, T, FF_CHUNK), BF16)],
        compiler_params=_params("parallel"),
    )(x, g, wup_t, wdown)


def _loss_head(y, target):
    T = y.shape[0]
    tm = 512

    def body(y_ref, t_ref, l_ref, d_ref):
        @pl.when(pl.program_id(0) == 0)
        def _():
            l_ref[...] = jnp.zeros_like(l_ref)
        e = y_ref[...] - t_ref[...]
        d_ref[...] = e / D_MODEL
        l_ref[...] += 0.5 * jnp.sum(jnp.mean(e * e, axis=-1, keepdims=True))

    return _call(
        body, name="loss_head", grid=(T // tm,),
        in_specs=[_rows(tm, D_MODEL), _rows(tm, D_MODEL)],
        out_specs=[_acc_rows(128, 8), _rows(tm, D_MODEL)],
        out_shape=[jax.ShapeDtypeStruct((8, 128), F32), jax.ShapeDtypeStruct((T, D_MODEL), F32)],
        compiler_params=_params("arbitrary"),
    )(y, target)


def _bias_table(rel_bias, bucket):
    def body(rb_ref, bk_ref, o_ref):
        bk = bk_ref[...]
        for h in range(N_HEADS):
            acc = jnp.zeros((BLOCK, BLOCK), F32)
            for b in range(N_BUCKETS):
                acc = jnp.where(bk == b, rb_ref[b, h], acc)
            o_ref[h // GQA, :, (h % GQA) * BLOCK:(h % GQA + 1) * BLOCK] = acc

    return _call(
        body, name="bias_table", in_specs=[IN_SMEM, RESIDENT], out_specs=RESIDENT,
        out_shape=jax.ShapeDtypeStruct((N_KV, BLOCK, GQA * BLOCK), F32),
    )(rel_bias, bucket)


def _mlp_bwd(dx2, x1, up, g, wup_t, wdown):
    T = x1.shape[0]
    tm = 512

    def body(d_ref, x_ref, up_ref, g_ref, wu_ref, wd_ref, dx_ref, dup_ref, h_ref, dg_ref, db_ref):
        @pl.when(pl.program_id(0) == 0)
        def _():
            dg_ref[...] = jnp.zeros_like(dg_ref)
        d2 = d_ref[...]
        d2b = d2.astype(BF16)
        db_ref[...] = d2b
        xh, r = _rms(x_ref[...])
        gv = g_ref[...]
        h_ref[...] = (xh * gv).astype(BF16)
        dh = jnp.zeros((tm, D_MODEL), F32)
        for c in range(N_FF):
            rows = slice(c * FF_CHUNK, (c + 1) * FF_CHUNK)
            dact = _dot_nt(d2b, wd_ref[rows, :])
            dup = (dact * (2.0 * jnp.maximum(up_ref[c].astype(F32), 0.0))).astype(BF16)
            dup_ref[c] = dup
            dh = dh + _dot(dup, wu_ref[rows, :])
        dg_ref[...] += _colsum(dh * xh)
        dx_ref[...] = d2 + _rms_bwd(dh * gv, xh, r)

    return _call(
        body, name="mlp_bwd", grid=(T // tm,),
        in_specs=[_rows(tm, D_MODEL), _rows(tm, D_MODEL), _chunked(tm), RESIDENT, RESIDENT, RESIDENT],
        out_specs=[_rows(tm, D_MODEL), _chunked(tm), _rows(tm, D_MODEL), _acc_rows(D_MODEL), _rows(tm, D_MODEL)],
        out_shape=[jax.ShapeDtypeStruct((T, D_MODEL), F32), jax.ShapeDtypeStruct((N_FF, T, FF_CHUNK), BF16),
                   jax.ShapeDtypeStruct((T, D_MODEL), BF16), jax.ShapeDtypeStruct((1, D_MODEL), F32),
                   jax.ShapeDtypeStruct((T, D_MODEL), BF16)],
        compiler_params=_params("arbitrary"),
    )(dx2, x1, up, g, wup_t, wdown)


def _bwd_out(dx1, w_out):
    T = dx1.shape[0]
    tm = 512

    def body(d_ref, w_ref, da_ref, dc_ref, db_ref):
        db = d_ref[...].astype(BF16)
        db_ref[...] = db
        dm = _dot_nt(db, w_ref[...])
        da_ref[...] = dm[:, :ATTN_W]
        dc_ref[...] = dm[:, ATTN_W:]

    return _call(
        body, name="bwd_out", grid=(T // tm,),
        in_specs=[_rows(tm, D_MODEL), RESIDENT],
        out_specs=[_rows(tm, ATTN_W), _rows(tm, CONV_W), _rows(tm, D_MODEL)],
        out_shape=[jax.ShapeDtypeStruct((T, ATTN_W), F32), jax.ShapeDtypeStruct((T, CONV_W), F32),
                   jax.ShapeDtypeStruct((T, D_MODEL), BF16)],
        compiler_params=_params("parallel"),
    )(dx1, w_out)


def _wgrad_one_block(a, b, name):
    T, tr = a.shape
    tk = min(T, 4096)
    nk = T // tk

    def body(a_ref, b_ref, o_ref, acc):
        k = pl.program_id(0)

        @pl.when(k == 0)
        def _():
            acc[...] = jnp.zeros_like(acc)
        acc[...] += _dot_tn(a_ref[...], b_ref[...])

        @pl.when(k == nk - 1)
        def _():
            o_ref[...] = acc[...].astype(BF16)

    return _call(
        body, name=name, grid=(nk,),
        in_specs=[_rows(tk, tr), _rows(tk, D_MODEL)],
        out_specs=pl.BlockSpec((tr, D_MODEL), lambda k: (0, 0)),
        out_shape=jax.ShapeDtypeStruct((tr, D_MODEL), BF16),
        scratch_shapes=[pltpu.VMEM((tr, D_MODEL), F32)],
        compiler_params=_params("arbitrary"),
    )(a, b)


def _wgrad(a, b, name, chunked=False, square_relu=False, tr=FF_CHUNK):
    if chunked:
        nr, T, tr = a.shape
    else:
        T, R = a.shape
        tr = min(R, tr)
        nr = R // tr
    piece = min(T, 2048)
    if nr == 1:
        return _wgrad_one_block(a, b, name)

    def body(a_ref, b_ref, o_ref):
        if square_relu:
            tot = jnp.zeros((tr, D_MODEL), F32)
            for k0 in range(0, T, piece):
                av = jnp.square(jnp.maximum(a_ref[k0:k0 + piece, :].astype(F32), 0.0)).astype(BF16)
                tot = tot + _dot_tn(av, b_ref[k0:k0 + piece, :])
        else:
            tot = _dot_tn(a_ref[...], b_ref[...])
        o_ref[...] = tot.astype(BF16)

    a_spec = (pl.BlockSpec((None, T, tr), lambda r: (r, 0, 0)) if chunked else pl.BlockSpec((T, tr), lambda r: (0, r)))
    return _call(
        body, name=name, grid=(nr,),
        in_specs=[a_spec, RESIDENT],
        out_specs=pl.BlockSpec((tr, D_MODEL), lambda r: (r, 0)),
        out_shape=jax.ShapeDtypeStruct((nr * tr, D_MODEL), BF16),
        compiler_params=_params("parallel"),
    )(a, b)


def _conv_bwd(dmix, y, zu, zg, cw, lg, lb, og):
    T = y.shape[0]
    tt = 512
    nt = T // tt
    per = tt // HALO
    prev_spec = pl.BlockSpec((HALO, CONV_W), lambda i: (jnp.maximum(i * per - 1, 0), 0))
    next_spec = pl.BlockSpec((HALO, CONV_W), lambda i: (jnp.minimum((i + 1) * per, nt * per - 1), 0))

    def body(dm_ref, dmn_ref, y_ref, yn_ref, u_ref, g_ref, uh_ref, gh_ref, cw_ref, lg_ref, lb_ref, og_ref,
             du_ref, dg_ref, pg_ref, hbuf, dybuf, dwacc, pacc, slab):
        i = pl.program_id(0)

        @pl.when(i == 0)
        def _():
            dwacc[...] = jnp.zeros_like(dwacc)
            pacc[...] = jnp.zeros_like(pacc)
        lgv, lbv, ogv = lg_ref[...], lb_ref[...], og_ref[...]

        def chain(yv, dm):
            yn, rstd, ln, sg, ch, r = _conv_post(yv, lgv, lbv, ogv)
            dc = _rms_bwd(dm * ogv, ch, r)
            dln = dc * sg * (1.0 + ln * (1.0 - sg))
            dyn = dln * lgv
            dy = rstd * (dyn - jnp.mean(dyn, axis=-1, keepdims=True)
                         - yn * jnp.mean(dyn * yn, axis=-1, keepdims=True))
            return dy, dm * ch, dln * yn, dln

        hbuf[0:HALO, :] = jnp.where(i > 0, uh_ref[...] * _sig(gh_ref[...]), 0.0)
        for r0 in range(0, tt, CONV_CHUNK):
            rs = slice(r0, r0 + CONV_CHUNK)
            hbuf[HALO + r0:HALO + r0 + CONV_CHUNK, :] = u_ref[rs, :] * _sig(g_ref[rs, :])
            dy, p_og, p_lg, p_lb = chain(y_ref[rs, :], dm_ref[rs, :])
            dybuf[rs, :] = dy
            for k, part in enumerate((dy, p_lg, p_lb, p_og)):
                pacc[8 * k:8 * k + 8, :] += _fold8(part)
        dyh, _, _, _ = chain(yn_ref[...], dmn_ref[...])
        dybuf[tt:tt + HALO, :] = jnp.where(i < nt - 1, dyh, 0.0)
        for r0 in range(0, tt, CONV_CHUNK):
            rs = slice(r0, r0 + CONV_CHUNK)
            dy = dybuf[rs, :]
            dh = jnp.zeros((CONV_CHUNK, CONV_W), F32)
            for j, win in _shifted_taps(dybuf, slab, r0, 0):
                dh = dh + cw_ref[CONV_K - 1 - j:CONV_K - j, :] * win
            for j, win in _shifted_taps(hbuf, slab, r0, HALO - (CONV_K - 1)):
                dwacc[8 * j:8 * j + 8, :] += _fold8(dy * win)
            sgt = _sig(g_ref[rs, :])
            du_ref[rs, :] = (dh * sgt).astype(BF16)
            dg_ref[rs, :] = (dh * u_ref[rs, :] * sgt * (1.0 - sgt)).astype(BF16)

        @pl.when(i == nt - 1)
        def _():
            pg_ref[...] = jnp.zeros_like(pg_ref)
            for j in range(CONV_K):
                pg_ref[j:j + 1, :] = _colsum(dwacc[8 * j:8 * j + 8, :])
            for k in range(4):
                pg_ref[32 + k:33 + k, :] = _colsum(pacc[8 * k:8 * k + 8, :])

    return _call(
        body, name="conv_bwd", grid=(nt,),
        in_specs=[_rows(tt, CONV_W), next_spec, _rows(tt, CONV_W), next_spec, _rows(tt, CONV_W), _rows(tt, CONV_W),
                  prev_spec, prev_spec] + [RESIDENT] * 4,
        out_specs=[_rows(tt, CONV_W), _rows(tt, CONV_W), _acc_rows(CONV_W, 40)],
        out_shape=[jax.ShapeDtypeStruct((T, CONV_W), BF16), jax.ShapeDtypeStruct((T, CONV_W), BF16),
                   jax.ShapeDtypeStruct((40, CONV_W), F32)],
        scratch_shapes=[pltpu.VMEM((HALO + tt, CONV_W), F32), pltpu.VMEM((tt + HALO, CONV_W), F32),
                        pltpu.VMEM((8 * HALO, CONV_W), F32), pltpu.VMEM((32, CONV_W), F32),
                        pltpu.VMEM((SLAB_ROWS, CONV_W), F32)],
        compiler_params=_params("arbitrary"),
    )(dmix, dmix, y, y, zu, zg, zu, zg, cw, lg, lb, og)


def _attn_bwd(dmix, a, zq, zkv, bias, sinks, qg, kg, og):
    T = zq.shape[0]
    rows = ATTN_SUB * BLOCK
    ns = T // rows
    nb = T // BLOCK
    cur = lambda w: pl.BlockSpec((rows, w), lambda n: (jnp.minimum(n, ns - 1), 0))
    scale = 1.0 / math.sqrt(HEAD_DIM)
    done = rows - BLOCK

    def body(dm_ref, a_ref, q_ref, kvc_ref, kvp_ref, bias_ref, sinks_ref, qgt_ref, kg_ref, kgt_ref, og_ref,
             dq_ref, dkv_ref, db_ref, dog_ref, dqg_ref, dkg_ref, dsk_ref, carry):
        n = pl.program_id(0)

        @pl.when(n == 0)
        def _():
            for ref in (db_ref, dog_ref, dqg_ref, dkg_ref, dsk_ref, carry):
                ref[...] = jnp.zeros_like(ref)

        @pl.when(n < ns)
        def _():
            tri = _tri_t()
            ogv, qgt, kgv, kgt = og_ref[...], qgt_ref[...], kg_ref[...], kgt_ref[...]
            lane = lax.broadcasted_iota(jnp.int32, (1, 128), 1)
            kvs = [kvp_ref[...]] + [kvc_ref[i * BLOCK:(i + 1) * BLOCK, :] for i in range(ATTN_SUB)]
            keys = [[_kv_block(kv, kh, kgv, kgt) for kh in range(N_KV)] for kv in kvs]
            dkn = [[jnp.zeros((BLOCK, HEAD_DIM), F32)] * N_KV for _ in kvs]
            dv = [[jnp.zeros((BLOCK, HEAD_DIM), F32)] * N_KV for _ in kvs]
            dsk = jnp.zeros((1, 128), F32)
            for i in range(ATTN_SUB):
                blk = slice(i * BLOCK, (i + 1) * BLOCK)
                ah, ra = _rms(a_ref[blk, :])
                dm = dm_ref[blk, :]
                dog_ref[...] += _colsum(dm * ah)
                da_t = _rms_bwd(dm * ogv, ah, ra).T
                zq_t = q_ref[blk, :].T
                has_prev = (n > 0) if i == 0 else None
                dqs = []
                for kh in range(N_KV):
                    kp, kc = keys[i][kh], keys[i + 1][kh]
                    c = _attn_group(zq_t, kp, kc, kh, bias_ref, sinks_ref, qgt, tri, has_prev)
                    dob = _heads_side_by_side(da_t, kh).astype(BF16)
                    pn = c["pn"]
                    p_c, p_p = _split(pn.astype(BF16), tri)
                    dv[i + 1][kh] = dv[i + 1][kh] + _dot_nt(p_c, dob)
                    dv[i][kh] = dv[i][kh] + _dot_nt(p_p, dob)
                    dp = jnp.where(tri, _dot(kc["v"], dob), _dot(kp["v"], dob))
                    dl = jnp.sum(pn * dp, axis=0, keepdims=True)
                    ds = pn * (dp - dl)
                    dsr = -c["psink"] * dl
                    for g in range(GQA):
                        dsk = dsk + jnp.where(lane == kh * GQA + g, jnp.sum(jnp.where(_head_lanes(g), dsr, 0.0)), 0.0)
                    db_ref[kh] += ds
                    ds_c, ds_p = _split((ds * scale).astype(BF16), tri)
                    dqn = _dot(kc["kn_t"], ds_c) + _dot(kp["kn_t"], ds_p)
                    dkn[i + 1][kh] = dkn[i + 1][kh] + _dot_nt(ds_c, c["qn"])
                    dkn[i][kh] = dkn[i][kh] + _dot_nt(ds_p, c["qn"])
                    dqg_ref[...] += jnp.sum(dqn * c["qhat"], axis=1, keepdims=True)
                    dq_t = _rms0_bwd(dqn * qgt, c["qhat"], c["rq"])
                    dqs += [dq_t[:, g * BLOCK:(g + 1) * BLOCK] for g in range(GQA)]
                dq_ref[blk, :] = jnp.concatenate(dqs, axis=0).T.astype(BF16)
            dsk_ref[...] += dsk
            dkv = []
            for j in range(ATTN_SUB + 1):
                dk = []
                for kh in range(N_KV):
                    key = keys[j][kh]
                    dkg_ref[...] += _colsum(dkn[j][kh] * key["khat"])
                    dk.append(_rms_bwd(dkn[j][kh] * kgv, key["khat"], key["rk"]))
                dkv.append(jnp.concatenate(dk + dv[j], axis=-1))
            if done:
                dkv_ref[0:done, :] = carry[0:done, :].astype(BF16)
            dkv_ref[done:rows, :] = (carry[done:rows, :] + dkv[0]).astype(BF16)
            for j in range(1, ATTN_SUB + 1):
                carry[(j - 1) * BLOCK:j * BLOCK, :] = dkv[j]

        @pl.when(n == ns)
        def _():
            dkv_ref[...] = carry[...].astype(BF16)

    small = lambda w: pl.BlockSpec((1, w), lambda n: (0, 0))
    return _call(
        body, name="attn_bwd", grid=(ns + 1,),
        in_specs=[cur(ATTN_W), cur(ATTN_W), cur(ATTN_W), cur(2 * KV_W),
                  pl.BlockSpec((BLOCK, 2 * KV_W), lambda n: (jnp.clip(n * ATTN_SUB - 1, 0, nb - 1), 0)),
                  RESIDENT, IN_SMEM, RESIDENT, RESIDENT, RESIDENT, RESIDENT],
        out_specs=[cur(ATTN_W), pl.BlockSpec((rows, 2 * KV_W), lambda n: (jnp.maximum(n - 1, 0), 0)),
                   pl.BlockSpec((N_KV, BLOCK, GQA * BLOCK), lambda n: (0, 0, 0)),
                   small(ATTN_W), pl.BlockSpec((HEAD_DIM, 1), lambda n: (0, 0)), small(HEAD_DIM), small(128)],
        out_shape=[jax.ShapeDtypeStruct((T, ATTN_W), BF16), jax.ShapeDtypeStruct((T, 2 * KV_W), BF16),
                   jax.ShapeDtypeStruct((N_KV, BLOCK, GQA * BLOCK), F32),
                   jax.ShapeDtypeStruct((1, ATTN_W), F32), jax.ShapeDtypeStruct((HEAD_DIM, 1), F32),
                   jax.ShapeDtypeStruct((1, HEAD_DIM), F32), jax.ShapeDtypeStruct((1, 128), F32)],
        scratch_shapes=[pltpu.VMEM((rows, 2 * KV_W), F32)],
        compiler_params=_params("arbitrary"),
    )(dmix, a, zq, zkv, zkv, bias, sinks, qg.reshape(HEAD_DIM, 1), kg, kg.reshape(HEAD_DIM, 1), og)


def _bwd_in(dx1, x, dq, dkv, du, dgt, g, wt):
    T = x.shape[0]
    tm = 512

    def body(d1_ref, x_ref, dq_ref, dkv_ref, du_ref, dgt_ref, g_ref, w_ref, dx_ref, h_ref, dg_ref, dz_ref):
        @pl.when(pl.program_id(0) == 0)
        def _():
            dg_ref[...] = jnp.zeros_like(dg_ref)
        xh, r = _rms(x_ref[...])
        gv = g_ref[...]
        h_ref[...] = (xh * gv).astype(BF16)
        dz = jnp.concatenate([dq_ref[...], dkv_ref[...], du_ref[...], dgt_ref[...]], axis=-1)
        dz_ref[...] = dz
        dh = _dot(dz, w_ref[...])
        dg_ref[...] += _colsum(dh * xh)
        dx_ref[...] = d1_ref[...] + _rms_bwd(dh * gv, xh, r)

    return _call(
        body, name="bwd_in", grid=(T // tm,),
        in_specs=[_rows(tm, D_MODEL), _rows(tm, D_MODEL), _rows(tm, ATTN_W), _rows(tm, 2 * KV_W),
                  _rows(tm, CONV_W), _rows(tm, CONV_W), RESIDENT, RESIDENT],
        out_specs=[_rows(tm, D_MODEL), _rows(tm, D_MODEL), _acc_rows(D_MODEL), _rows(tm, IN_W)],
        out_shape=[jax.ShapeDtypeStruct((T, D_MODEL), F32), jax.ShapeDtypeStruct((T, D_MODEL), BF16),
                   jax.ShapeDtypeStruct((1, D_MODEL), F32), jax.ShapeDtypeStruct((T, IN_W), BF16)],
        compiler_params=_params("arbitrary"),
    )(dx1, x, dq, dkv, du, dgt, g, wt)


def _bias_grad(db, bucket):
    def body(db_ref, bk_ref, o_ref):
        bk = bk_ref[...]
        lane = lax.broadcasted_iota(jnp.int32, (1, 128), 1)
        for h in range(N_HEADS):
            cols = slice((h % GQA) * BLOCK, (h % GQA + 1) * BLOCK)
            tot = db_ref[0, h // GQA, :, cols]
            for l in range(1, DEPTH):
                tot = tot + db_ref[l, h // GQA, :, cols]
            out = jnp.zeros((1, 128), F32)
            for b in range(N_BUCKETS):
                out = jnp.where(lane == b, jnp.sum(jnp.where(bk == b, tot, 0.0)), out)
            o_ref[h:h + 1, :] = out

    return _call(
        body, name="bias_grad", in_specs=[RESIDENT, RESIDENT], out_specs=RESIDENT,
        out_shape=jax.ShapeDtypeStruct((N_HEADS, 128), F32),
        compiler_params=pltpu.CompilerParams(vmem_limit_bytes=VMEM_LIMIT),
    )(db, bucket)


def _place():
    return lax.axis_index("x"), lax.axis_index("y"), lax.axis_index("c")


def _all_gather(shards):
    na = len(shards)

    def body(*refs):
        ins, outs = refs[:na], refs[na:2 * na]
        send_sems, recv_sems, local_sems = refs[2 * na:]
        x, y, c = _place()
        me, sibling = (x, y, c), (x, y, 1 - c)
        chips = [(1 - x, y), (x, 1 - y), (1 - x, 1 - y)]

        def rows(a, p):
            r = ins[a].shape[1]
            return outs[a].at[:, pl.ds((4 * p[0] + 2 * p[1] + p[2]) * r, r), :]

        def copy(a, k, block, to, src=None):
            return pltpu.make_async_remote_copy(
                src_ref=rows(a, block) if src is None else src, dst_ref=rows(a, block),
                send_sem=send_sems.at[a, k], recv_sem=recv_sems.at[a, k], device_id=to, device_id_type=MESH)

        mine = [pltpu.make_async_copy(ins[a], rows(a, me), local_sems.at[a]) for a in range(na)]
        for cp in mine:
            cp.start()
        first = []
        for a in range(na):
            first.append(copy(a, 0, me, sibling, src=ins[a]))
            first += [copy(a, 1 + j, me, (*chip, c), src=ins[a]) for j, chip in enumerate(chips)]
        for cp in first:
            cp.start()
        passed = []
        for j, chip in enumerate(chips):
            for a in range(na):
                copy(a, 1 + j, (*chip, c), me).wait_recv()
                cp = copy(a, 4 + j, (*chip, c), sibling)
                cp.start()
                passed.append(cp)
        for a in range(na):
            copy(a, 0, sibling, me).wait_recv()
            for j, chip in enumerate(chips):
                copy(a, 4 + j, (*chip, 1 - c), me).wait_recv()
        for cp in first + passed:
            cp.wait_send()
        for cp in mine:
            cp.wait()

    return _call(
        body, name="all_gather",
        in_specs=[ANY] * na, out_specs=[ANY] * na,
        out_shape=[jax.ShapeDtypeStruct((s.shape[0], N_DEV * s.shape[1], s.shape[2]), s.dtype) for s in shards],
        scratch_shapes=[pltpu.SemaphoreType.DMA((na, 7)), pltpu.SemaphoreType.DMA((na, 7)),
                        pltpu.SemaphoreType.DMA((na,))],
    )(*shards)


IN_HBM = pl.BlockSpec(memory_space=pltpu.HBM)
IN_SEM = pl.BlockSpec(memory_space=pltpu.SEMAPHORE)
DATAFLOW = pltpu.SideEffectType.DATAFLOW_SIDE_EFFECTING


def _exchange_copies(srcs, lands, send_sems, recv_sems, scatter):
    x, y, c = _place()
    me = 4 * x + 2 * y + c
    copies = []
    for k in range(1, N_DEV):
        p = (x ^ (k >> 2), y ^ ((k >> 1) & 1), c ^ (k & 1))
        for a, (src, land) in enumerate(zip(srcs, lands)):
            r = land.shape[0] // N_DEV
            if scatter:
                src = src.at[pl.ds((4 * p[0] + 2 * p[1] + p[2]) * r, r), :]
            copies.append(pltpu.make_async_remote_copy(
                src_ref=src, dst_ref=land.at[pl.ds(me * r, r), :], send_sem=send_sems.at[a * (N_DEV - 1) + k - 1],
                recv_sem=recv_sems.at[a * (N_DEV - 1) + k - 1], device_id=p, device_id_type=MESH))
    return copies


def _own_copies(srcs, lands, send_sems, scatter):
    x, y, c = _place()
    me = 4 * x + 2 * y + c
    copies = []
    for a, (src, land) in enumerate(zip(srcs, lands)):
        r = land.shape[0] // N_DEV
        if scatter:
            src = src.at[pl.ds(me * r, r), :]
        copies.append(pltpu.make_async_copy(src, land.at[pl.ds(me * r, r), :],
                                            send_sems.at[len(srcs) * (N_DEV - 1) + a]))
    return copies


def _exchange_start(srcs, name, scatter):
    na = len(srcs)
    lands = [lax.empty((s.shape[0] * (1 if scatter else N_DEV), s.shape[1]), s.dtype) for s in srcs]

    def body(*refs):
        ins, lnd = refs[:na], refs[na:2 * na]
        send_sems, recv_sems = refs[2 * na], refs[2 * na + 1]
        token = refs[-1]
        for cp in _exchange_copies(ins, lnd, send_sems, recv_sems, scatter) + _own_copies(ins, lnd, send_sems, scatter):
            cp.start()
        token[...] = jnp.zeros_like(token)

    hbm = lambda a: pltpu.with_memory_space_constraint(a, pltpu.HBM)
    out = _call(
        body, name=name,
        out_shape=(pltpu.SemaphoreType.DMA((na * N_DEV,)), pltpu.SemaphoreType.DMA((na * (N_DEV - 1),)),
                   *[pltpu.HBM(a.shape, a.dtype) for a in (*srcs, *lands)], jax.ShapeDtypeStruct((8, 128), F32)),
        in_specs=[IN_HBM] * (2 * na),
        out_specs=(IN_SEM, IN_SEM, *[IN_HBM] * (2 * na), RESIDENT),
        input_output_aliases={i: 2 + i for i in range(2 * na)},
        compiler_params=pltpu.CompilerParams(has_side_effects=DATAFLOW),
    )(*[hbm(a) for a in (*srcs, *lands)])
    return (out[0], out[1], out[2:2 + na], out[2 + na:2 + 2 * na]), out[-1]


def _exchange_wait(state, after, name, scatter):
    send_sems, recv_sems, srcs, lands = state
    na = len(srcs)

    def body(*refs):
        ins, lnd = refs[:na], refs[na:2 * na]
        for cp in _exchange_copies(ins, lnd, refs[2 * na], refs[2 * na + 1], scatter):
            cp.wait_send()
            cp.wait_recv()
        for cp in _own_copies(ins, lnd, refs[2 * na], scatter):
            cp.wait()
        refs[-1][...] = jnp.zeros_like(refs[-1])

    out = _call(
        body, name=name,
        out_shape=(*[pltpu.HBM(a.shape, a.dtype) for a in (*srcs, *lands)], jax.ShapeDtypeStruct((8, 128), F32)),
        in_specs=[IN_HBM] * (2 * na) + [IN_SEM, IN_SEM, ANY],
        out_specs=(*[IN_HBM] * (2 * na), RESIDENT),
        input_output_aliases={i: i for i in range(2 * na)},
        compiler_params=pltpu.CompilerParams(has_side_effects=DATAFLOW),
    )(*srcs, *lands, send_sems, recv_sems, after)
    return out[:na], out[na:2 * na], out[-1]


def _adam_math(w, g, m, v):
    m = ADAM_B1 * m + (1.0 - ADAM_B1) * g
    v = ADAM_B2 * v + (1.0 - ADAM_B2) * jnp.square(g)
    m_hat = m / (1.0 - ADAM_B1 ** ADAM_STEP)
    v_hat = v / (1.0 - ADAM_B2 ** ADAM_STEP)
    delta = -ADAM_LR * (m_hat / (jnp.sqrt(v_hat) + ADAM_EPS) + ADAM_WD * w)
    return delta, m, v


def _adam_big(parts, w, m, v, transposed=False):
    if transposed:
        nl, cdim, r = w.shape
    else:
        nl, r, cdim = w.shape
    tr = r if r <= 256 else 256
    ni = r // tr
    spec = (pl.BlockSpec((None, cdim, tr), lambda l, i: (l, 0, i)) if transposed
            else pl.BlockSpec((None, tr, cdim), lambda l, i: (l, i, 0)))

    def part_spec(layer):
        return pl.BlockSpec((N_DEV, tr, cdim),
                            lambda l, i: (0, jnp.where(l == layer, i, jnp.where(l < layer, 0, ni - 1)), 0))

    def body(*refs):
        p_refs = refs[:nl]
        w_ref, m_ref, v_ref, g_ref, d_ref, nm_ref, nv_ref = refs[nl:]
        for layer in range(nl):
            @pl.when(pl.program_id(0) == layer)
            def _(p_ref=p_refs[layer]):
                g = p_ref[0].astype(F32)
                for s in range(1, N_DEV):
                    g = g + p_ref[s].astype(F32)
                if transposed:
                    g = g.T
                d, nm, nv = _adam_math(w_ref[...], g, m_ref[...], v_ref[...])
                g_ref[...] = g
                d_ref[...] = d
                nm_ref[...] = nm
                nv_ref[...] = nv

    return _call(
        body, name="adam_big", grid=(nl, ni),
        in_specs=[part_spec(layer) for layer in range(nl)] + [spec, spec, spec],
        out_specs=[spec] * 4,
        out_shape=[jax.ShapeDtypeStruct(w.shape, F32)] * 4,
        compiler_params=_params("arbitrary", "arbitrary"),
    )(*parts, w, m, v)


def _sum_slots(parts):
    def body(p_ref, o_ref):
        g = p_ref[0]
        for s in range(1, N_DEV):
            g = g + p_ref[s]
        o_ref[...] = g

    return _call(body, name="sum_slots", in_specs=[RESIDENT], out_specs=RESIDENT,
                 out_shape=jax.ShapeDtypeStruct(parts.shape[1:], F32))(parts)


def _adam_small(ws, gs, ms, vs):
    n = len(ws)

    def body(*refs):
        ins, outs = refs[:4 * n], refs[4 * n:]
        for k in range(n):
            d, nm, nv = _adam_math(ins[k][...], ins[n + k][...], ins[2 * n + k][...], ins[3 * n + k][...])
            outs[k][...] = d
            outs[n + k][...] = nm
            outs[2 * n + k][...] = nv

    out = _call(body, name="adam_small", in_specs=[RESIDENT] * (4 * n), out_specs=[RESIDENT] * (3 * n),
                out_shape=[jax.ShapeDtypeStruct(w.shape, F32) for w in ws] * 3)(*ws, *gs, *ms, *vs)
    return out[:n], out[n:2 * n], out[2 * n:]


def _pack(arrays):
    parts = []
    for a in arrays:
        flat = a.reshape(-1)
        n = flat.shape[0]
        padded = -(-n // 1024) * 1024
        parts.append(jnp.pad(flat, (0, padded - n)).reshape(padded // 128, 128))
    return jnp.concatenate(parts, axis=0)


def _unpack(packed, shapes):
    out, row = [], 0
    for shp in shapes:
        n = int(np.prod(shp))
        rows = -(-n // 1024) * 8
        out.append(packed[row:row + rows].reshape(-1)[:n].reshape(shp))
        row += rows
    return out


def _t5_bucket_table():
    kj = np.arange(BLOCK)[:, None]
    qi = np.arange(BLOCK)[None, :]
    n = (qi - kj) % BLOCK
    max_exact = N_BUCKETS // 2
    large = max_exact + (np.log(np.maximum(n, 1) / max_exact) / np.log(MAX_DIST / max_exact)
                         * (N_BUCKETS - max_exact)).astype(np.int32)
    large = np.minimum(large, N_BUCKETS - 1)
    return np.where(n < max_exact, n, large).astype(np.int32)


SMALL_NAMES = ("rel_bias", "norm_mix_g", "q_norm_g", "k_norm_g", "sinks", "conv_b", "conv_ln_g", "conv_ln_b",
               "attn_out_g", "conv_out_g", "norm_mlp_g")


def _local_step(x, target, small, conv_w, get_w, put_g, token):
    bucket = jnp.asarray(_t5_bucket_table())
    bias = _bias_table(small["rel_bias"], bucket)
    row = lambda a, l: a[l][None, :]
    cw_pad = jnp.pad(conv_w, ((0, 0), (0, HALO - CONV_K), (0, 0)))
    saved, weights = [], []
    for l in range(DEPTH):
        wt_in, w_out, tok = get_w(l, 0, x)
        zq, zkv, zu, zg = _fwd_in(x, row(small["norm_mix_g"], l) + tok + (token if l == 0 else 0.0), wt_in)
        a, mix_a = _attn_fwd(zq, zkv, bias, small["sinks"][l], row(small["q_norm_g"], l), row(small["k_norm_g"], l),
                             row(small["attn_out_g"], l))
        y, mix_c = _conv_fwd(zu, zg, cw_pad[l], row(small["conv_b"], l), row(small["conv_ln_g"], l),
                             row(small["conv_ln_b"], l), row(small["conv_out_g"], l))
        x1 = _fwd_out(x, mix_a, mix_c, w_out)
        wt_up, w_down, tok = get_w(l, 1, x1)
        weights.append((wt_in, w_out, wt_up, w_down))
        x2, up = _mlp_fwd(x1, row(small["norm_mlp_g"], l) + tok, wt_up, w_down)
        saved.append((x, zq, zkv, zu, zg, a, mix_a, y, mix_c, x1, up))
        x = x2
    loss_part, dx = _loss_head(x, target)

    gs = {n: [None] * DEPTH for n in SMALL_NAMES if n != "rel_bias"}
    g_conv_w, dbias = [None] * DEPTH, [None] * DEPTH
    token = 0.0
    for l in reversed(range(DEPTH)):
        x0, zq, zkv, zu, zg, a, mix_a, y, mix_c, x1, up = saved[l]
        wt_in, w_out, wt_up, w_down = weights[l]
        dx1, dup, h2, gs["norm_mlp_g"][l], dxb = _mlp_bwd(dx, x1, up, row(small["norm_mlp_g"], l) + token, wt_up, w_down)
        g_up = _wgrad(dup, h2, "wgrad_up", chunked=True)
        g_down = _wgrad(up, dxb, "wgrad_down", chunked=True, square_relu=True)
        token = put_g(l, 1, (g_up, g_down))
        dma, dmc, dx1b = _bwd_out(dx1, w_out)
        g_out = jnp.concatenate([_wgrad(mix_a, dx1b, "wgrad_out_a"), _wgrad(mix_c, dx1b, "wgrad_out_c")], axis=0)
        du, dgt, pg = _conv_bwd(dmc, y, zu, zg, cw_pad[l], row(small["conv_ln_g"], l) + token,
                                row(small["conv_ln_b"], l), row(small["conv_out_g"], l))
        g_conv_w[l] = pg[:CONV_K]
        gs["conv_b"][l], gs["conv_ln_g"][l], gs["conv_ln_b"][l], gs["conv_out_g"][l] = pg[32], pg[33], pg[34], pg[35]
        dq, dkv, dbias[l], dog, dqg, dkg, dsk = _attn_bwd(
            dma, a, zq, zkv, bias, small["sinks"][l], row(small["q_norm_g"], l), row(small["k_norm_g"], l),
            row(small["attn_out_g"], l))
        gs["attn_out_g"][l], gs["q_norm_g"][l], gs["k_norm_g"][l], gs["sinks"][l] = dog, dqg, dkg, dsk[0, :N_HEADS]
        dx, h, gs["norm_mix_g"][l], dz = _bwd_in(dx1, x0, dq, dkv, du, dgt, row(small["norm_mix_g"], l), wt_in)
        g_in = _wgrad(dz, h, "wgrad_in", tr=256)
        token = put_g(l, 0, (g_in, g_out))
    small_grads = {n: jnp.stack([jnp.reshape(v, (-1,)) for v in vals]) for n, vals in gs.items()}
    small_grads["rel_bias"] = _bias_grad(jnp.stack(dbias), bucket)[:, :N_BUCKETS].T
    return loss_part, dx, small_grads, jnp.stack(g_conv_w)


def kernel(x, rel_bias, norm_mix_g, w_in, q_norm_g, k_norm_g, sinks, conv_w, conv_b, conv_ln_g, conv_ln_b, attn_out_g, conv_out_g, w_out, norm_mlp_g, w_mlp_up, w_mlp_down, loss_target, m_rel_bias, m_norm_mix_g, m_w_in, m_q_norm_g, m_k_norm_g, m_sinks, m_conv_w, m_conv_b, m_conv_ln_g, m_conv_ln_b, m_attn_out_g, m_conv_out_g, m_w_out, m_norm_mlp_g, m_w_mlp_up, m_w_mlp_down, v_rel_bias, v_norm_mix_g, v_w_in, v_q_norm_g, v_k_norm_g, v_sinks, v_conv_w, v_conv_b, v_conv_ln_g, v_conv_ln_b, v_attn_out_g, v_conv_out_g, v_w_out, v_norm_mlp_g, v_w_mlp_up, v_w_mlp_down):
    args = dict(locals())
    small = {n: args[n] for n in SMALL_NAMES}
    tr = lambda a: jnp.swapaxes(a, 1, 2)
    me = 4 * lax.axis_index("x") + 2 * lax.axis_index("y") + lax.axis_index("c")

    shards = (tr(w_in).astype(BF16), w_out.astype(BF16), tr(w_mlp_up).astype(BF16), w_mlp_down.astype(BF16))
    cw_sh = jnp.pad(tr(conv_w), ((0, 0), (0, 0), (0, HALO - CONV_K))).reshape(DEPTH * (CONV_W // N_DEV), HALO)
    gathers = {}

    def start_gather(l, half, dep):
        arrays = [s[l] for s in shards[2 * half:2 * half + 2]]
        if (l, half) == (0, 0):
            arrays = [cw_sh] + arrays
        if dep is not None:
            arrays = [a + dep[0, 0].astype(a.dtype) for a in arrays]
        gathers[l, half], tok = _exchange_start(arrays, f"gather_{l}{'ab'[half]}_start", scatter=False)
        return tok

    landed = {}

    def get_w(l, half, after):
        if (l, half) not in landed:
            _, lands, tok = _exchange_wait(gathers[l, half], after, f"gather_{l}{'ab'[half]}_wait", scatter=False)
            if l + 1 < DEPTH:
                tok = tok + start_gather(l + 1, half, tok)
            landed[l, half] = (*lands, tok[0, 0])
        return landed[l, half]

    token = start_gather(0, 0, None)
    token = token + start_gather(0, 1, token)
    cw_all, *first = get_w(0, 0, token)
    landed[0, 0] = tuple(first)
    conv_w_full = jnp.transpose(cw_all.reshape(N_DEV, DEPTH, CONV_W // N_DEV, HALO), (1, 3, 0, 2))
    conv_w_full = conv_w_full.reshape(DEPTH, HALO, CONV_W)[:, :CONV_K, :]

    scatters = {}

    last = {}

    def put_g(l, half, grads):
        scatters[l, half], last["token"] = _exchange_start(list(grads), f"scatter_{l}{'ab'[half]}_start", scatter=True)
        return last["token"][0, 0]

    loss_part, dx, small_grads, g_conv_w = _local_step(
        x[0], loss_target[0], small, conv_w_full, get_w, put_g, token[0, 0])

    big_w = (tr(w_in), w_out, w_mlp_up, w_mlp_down)
    big_m = (tr(m_w_in), m_w_out, m_w_mlp_up, m_w_mlp_down)
    big_v = (tr(v_w_in), v_w_out, v_w_mlp_up, v_w_mlp_down)
    big_out = [None] * 4

    def update_half(half, after):
        parts = [[None] * DEPTH, [None] * DEPTH]
        for l in reversed(range(DEPTH)):
            _, lands, _ = _exchange_wait(scatters[l, half], after, f"scatter_{l}{'ab'[half]}_wait", scatter=True)
            for k, land in enumerate(lands):
                parts[k][l] = land.reshape(N_DEV, land.shape[0] // N_DEV, land.shape[1])
        for k in range(2):
            n = 2 * half + k
            big_out[n] = _adam_big(parts[k], big_w[n], big_m[n], big_v[n], transposed=(n == 2))

    update_half(1, last["token"])

    order = [n for n in SMALL_NAMES]
    packed = _pack([small_grads[n] for n in order] + [g_conv_w])
    slots, = _all_gather((packed[None],))
    summed = _sum_slots(slots[0].reshape(N_DEV, packed.shape[0], 128))
    shapes = [small[n].shape for n in order] + [(DEPTH, CONV_K, CONV_W)]
    sg = _unpack(summed, shapes)
    g_small = dict(zip(order, sg[:-1]))
    g_small["conv_w"] = lax.dynamic_slice_in_dim(sg[-1], me * (CONV_W // N_DEV), CONV_W // N_DEV, axis=2)
    names = order + ["conv_w"]
    deltas, new_m, new_v = _adam_small([args[n] for n in names], [g_small[n] for n in names],
                                       [args["m_" + n] for n in names], [args["v_" + n] for n in names])
    res = {"grad": g_small, "delta": dict(zip(names, deltas)), "new_m": dict(zip(names, new_m)),
           "new_v": dict(zip(names, new_v))}
    update_half(0, big_out[3][0][0, :8, :8] + deltas[0][:8, :8])
    big_out[0] = [tr(o) for o in big_out[0]]
    for k, n in enumerate(("w_in", "w_out", "w_mlp_up", "w_mlp_down")):
        for kind, val in zip(("grad", "delta", "new_m", "new_v"), big_out[k]):
            res[kind][n] = val

    loss = lax.psum(loss_part[0, 0], ("x", "y", "c"))
    weights = ("rel_bias", "norm_mix_g", "w_in", "q_norm_g", "k_norm_g", "sinks", "conv_w", "conv_b", "conv_ln_g",
               "conv_ln_b", "attn_out_g", "conv_out_g", "w_out", "norm_mlp_g", "w_mlp_up", "w_mlp_down")
    return (loss, dx[None], *[res[kind][n] for kind in ("grad", "delta", "new_m", "new_v") for n in weights])
```

```python
import math

import numpy as np
import jax
import jax.numpy as jnp
from jax import lax
from jax.experimental import pallas as pl
from jax.experimental.pallas import tpu as pltpu

F32, BF16 = jnp.float32, jnp.bfloat16
D_MODEL = 1024
DEPTH = 4
HEAD_DIM = 64
N_HEADS = 8
N_KV = 2
GQA = N_HEADS // N_KV
ATTN_W = N_HEADS * HEAD_DIM
KV_W = N_KV * HEAD_DIM
CONV_W = D_MODEL - ATTN_W
IN_W = ATTN_W + 2 * KV_W + 2 * CONV_W
BLOCK = 128
CONV_K = 31
HALO = 32
N_BUCKETS = 32
MAX_DIST = 128
D_FF = 4 * D_MODEL
FF_CHUNK = 512
N_FF = D_FF // FF_CHUNK
EPS = 1e-6
NEG = -1e30
N_DEV = 8
ADAM_LR, ADAM_B1, ADAM_B2, ADAM_EPS, ADAM_WD, ADAM_STEP = 0.001, 0.9, 0.999, 1e-08, 0.01, 10
VMEM_LIMIT = 56 * 1024 * 1024
MESH = pl.DeviceIdType.MESH

RESIDENT = pl.BlockSpec(memory_space=pltpu.VMEM)
IN_SMEM = pl.BlockSpec(memory_space=pltpu.SMEM)
ANY = pl.BlockSpec(memory_space=pl.ANY)


def _call(body, **kw):
    return pl.pallas_call(body, **kw)


def _params(*sem):
    return pltpu.CompilerParams(dimension_semantics=sem, vmem_limit_bytes=VMEM_LIMIT)


def _dot(a, b):
    return lax.dot_general(a, b, (((1,), (0,)), ((), ())), preferred_element_type=F32)


def _dot_nt(a, b):
    return lax.dot_general(a, b, (((1,), (1,)), ((), ())), preferred_element_type=F32)


def _dot_tn(a, b):
    return lax.dot_general(a, b, (((0,), (0,)), ((), ())), preferred_element_type=F32)


def _sig(x):
    return 1.0 / (1.0 + jnp.exp(-x))


def _rms(x):
    r = lax.rsqrt(jnp.mean(x * x, axis=-1, keepdims=True) + EPS)
    return x * r, r


def _rms_bwd(dy_g, xh, r):
    return r * (dy_g - xh * jnp.mean(dy_g * xh, axis=-1, keepdims=True))


def _rows(tm, w):
    return pl.BlockSpec((tm, w), lambda i: (i, 0))


def _acc_rows(w, rows=1):
    return pl.BlockSpec((rows, w), lambda i: (0, 0))


def _colsum(x):
    return jnp.sum(x, axis=0, keepdims=True)


def _fwd_in(x, g, wt):
    T = x.shape[0]
    tm = 512

    def body(x_ref, g_ref, w_ref, q_ref, kv_ref, u_ref, gt_ref):
        xh, _ = _rms(x_ref[...])
        h = (xh * g_ref[...]).astype(BF16)
        z = _dot_nt(h, w_ref[...])
        q_ref[...] = z[:, :ATTN_W]
        kv_ref[...] = z[:, ATTN_W:ATTN_W + 2 * KV_W]
        u_ref[...] = z[:, ATTN_W + 2 * KV_W:ATTN_W + 2 * KV_W + CONV_W]
        gt_ref[...] = z[:, ATTN_W + 2 * KV_W + CONV_W:]

    widths = (ATTN_W, 2 * KV_W, CONV_W, CONV_W)
    return _call(
        body, name="fwd_in", grid=(T // tm,),
        in_specs=[_rows(tm, D_MODEL), RESIDENT, RESIDENT],
        out_specs=[_rows(tm, w) for w in widths],
        out_shape=[jax.ShapeDtypeStruct((T, w), F32) for w in widths],
        compiler_params=_params("parallel"),
    )(x, g, wt)


ATTN_SUB = 16


def _rms0(x):
    r = lax.rsqrt(jnp.mean(x * x, axis=0, keepdims=True) + EPS)
    return x * r, r


def _rms0_bwd(dy_g, xh, r):
    return r * (dy_g - xh * jnp.mean(dy_g * xh, axis=0, keepdims=True))


def _tri_t():
    kj = lax.broadcasted_iota(jnp.int32, (BLOCK, BLOCK), 0)
    qi = lax.broadcasted_iota(jnp.int32, (BLOCK, BLOCK), 1)
    return jnp.concatenate([kj <= qi] * GQA, axis=1)


def _head_lanes(g):
    lane = lax.broadcasted_iota(jnp.int32, (1, GQA * BLOCK), 1)
    return (lane >= g * BLOCK) & (lane < (g + 1) * BLOCK)


def _heads_side_by_side(xt, kh):
    return jnp.concatenate(
        [xt[(kh * GQA + g) * HEAD_DIM:(kh * GQA + g + 1) * HEAD_DIM, :] for g in range(GQA)], axis=1)


def _kv_block(kv, kh, kg, kg_t):
    kvt = kv.T
    khat, rk = _rms(kv[:, kh * HEAD_DIM:(kh + 1) * HEAD_DIM])
    khat_t, _ = _rms0(kvt[kh * HEAD_DIM:(kh + 1) * HEAD_DIM, :])
    return dict(khat=khat, rk=rk, kn=(khat * kg).astype(BF16), kn_t=(khat_t * kg_t).astype(BF16),
                v=kv[:, KV_W + kh * HEAD_DIM:KV_W + (kh + 1) * HEAD_DIM].astype(BF16),
                v_t=kvt[KV_W + kh * HEAD_DIM:KV_W + (kh + 1) * HEAD_DIM, :].astype(BF16))


def _attn_group(zq_t, kp, kc, kh, bias_ref, sinks_ref, qg_t, tri, has_prev):
    scale = 1.0 / math.sqrt(HEAD_DIM)
    qhat, rq = _rms0(_heads_side_by_side(zq_t, kh))
    qf = qhat * qg_t
    qn = qf.astype(BF16)
    qs = (qf * scale).astype(BF16)
    s = jnp.where(tri, _dot(kc["kn"], qs), _dot(kp["kn"], qs)) + bias_ref[kh]
    if has_prev is not None:
        s = jnp.where(tri | has_prev, s, NEG)
    sink = jnp.zeros((1, GQA * BLOCK), F32)
    for g in range(GQA):
        sink = jnp.where(_head_lanes(g), sinks_ref[kh * GQA + g], sink)
    m = jnp.maximum(jnp.max(s, axis=0, keepdims=True), sink)
    p = jnp.exp(s - m)
    es = jnp.exp(sink - m)
    inv = 1.0 / (jnp.sum(p, axis=0, keepdims=True) + es)
    return dict(qhat=qhat, rq=rq, qn=qn, pn=p * inv, psink=es * inv)


def _split(pb, tri):
    zero = jnp.zeros_like(pb)
    return jnp.where(tri, pb, zero), jnp.where(tri, zero, pb)


def _attn_fwd(zq, zkv, bias, sinks, qg, kg, og):
    T = zq.shape[0]
    rows = ATTN_SUB * BLOCK

    def body(q_ref, kvc_ref, kvp_ref, bias_ref, sinks_ref, qgt_ref, kg_ref, kgt_ref, og_ref, a_ref, mix_ref):
        n = pl.program_id(0)
        tri = _tri_t()
        qgt, kgv, kgt = qgt_ref[...], kg_ref[...], kgt_ref[...]
        kvs = [kvp_ref[...]] + [kvc_ref[i * BLOCK:(i + 1) * BLOCK, :] for i in range(ATTN_SUB)]
        keys = [[_kv_block(kv, kh, kgv, kgt) for kh in range(N_KV)] for kv in kvs]
        for i in range(ATTN_SUB):
            zq_t = q_ref[i * BLOCK:(i + 1) * BLOCK, :].T
            has_prev = (n > 0) if i == 0 else None
            outs = []
            for kh in range(N_KV):
                kp, kc = keys[i][kh], keys[i + 1][kh]
                c = _attn_group(zq_t, kp, kc, kh, bias_ref, sinks_ref, qgt, tri, has_prev)
                p_c, p_p = _split(c["pn"].astype(BF16), tri)
                o_t = _dot(kc["v_t"], p_c) + _dot(kp["v_t"], p_p)
                outs += [o_t[:, g * BLOCK:(g + 1) * BLOCK] for g in range(GQA)]
            a = jnp.concatenate(outs, axis=0).T
            a_ref[i * BLOCK:(i + 1) * BLOCK, :] = a
            ah, _ = _rms(a)
            mix_ref[i * BLOCK:(i + 1) * BLOCK, :] = (ah * og_ref[...]).astype(BF16)

    return _call(
        body, name="attn_fwd", grid=(T // rows,),
        in_specs=[_rows(rows, ATTN_W), _rows(rows, 2 * KV_W),
                  pl.BlockSpec((BLOCK, 2 * KV_W), lambda n: (jnp.maximum(n * ATTN_SUB - 1, 0), 0)),
                  RESIDENT, IN_SMEM, RESIDENT, RESIDENT, RESIDENT, RESIDENT],
        out_specs=[_rows(rows, ATTN_W), _rows(rows, ATTN_W)],
        out_shape=[jax.ShapeDtypeStruct((T, ATTN_W), F32), jax.ShapeDtypeStruct((T, ATTN_W), BF16)],
        compiler_params=_params("parallel"),
    )(zq, zkv, zkv, bias, sinks, qg.reshape(HEAD_DIM, 1), kg, kg.reshape(HEAD_DIM, 1), og)


def _conv_post(y, lg, lb, og):
    mu = jnp.mean(y, axis=-1, keepdims=True)
    yc = y - mu
    rstd = lax.rsqrt(jnp.mean(yc * yc, axis=-1, keepdims=True) + EPS)
    yn = yc * rstd
    ln = yn * lg + lb
    sg = _sig(ln)
    c = ln * sg
    ch, r = _rms(c)
    return yn, rstd, ln, sg, ch, r


CONV_CHUNK = 64


SLAB_ROWS = CONV_CHUNK + 8 * ((CONV_K - 1) // 8)


def _shifted_taps(buf, slab, r0, base):
    for b in range(8):
        taps = range(b, CONV_K, 8)
        span = CONV_CHUNK + 8 * (len(taps) - 1)
        slab[0:span, :] = buf[r0 + base + b:r0 + base + b + span, :]
        for a, j in enumerate(taps):
            yield j, slab[8 * a:8 * a + CONV_CHUNK, :]


def _fold8(x):
    return jnp.sum(x.reshape(x.shape[0] // 8, 8, x.shape[1]), axis=0)


def _conv_fwd(zu, zg, cw, cb, lg, lb, og):
    T = zu.shape[0]
    tt = 512
    halo_spec = pl.BlockSpec((HALO, CONV_W), lambda i: (jnp.maximum(i * (tt // HALO) - 1, 0), 0))

    def body(u_ref, g_ref, uh_ref, gh_ref, cw_ref, cb_ref, lg_ref, lb_ref, og_ref, y_ref, mix_ref, buf, slab):
        i = pl.program_id(0)
        hal = uh_ref[...] * _sig(gh_ref[...])
        buf[0:HALO, :] = jnp.where(i > 0, hal, 0.0)
        buf[HALO:HALO + tt, :] = u_ref[...] * _sig(g_ref[...])
        cbv, lgv, lbv, ogv = cb_ref[...], lg_ref[...], lb_ref[...], og_ref[...]
        for r0 in range(0, tt, CONV_CHUNK):
            rs = slice(r0, r0 + CONV_CHUNK)
            acc = jnp.zeros((CONV_CHUNK, CONV_W), F32)
            for j, win in _shifted_taps(buf, slab, r0, HALO - (CONV_K - 1)):
                acc = acc + cw_ref[j:j + 1, :] * win
            y = acc + cbv
            y_ref[rs, :] = y
            mix_ref[rs, :] = (_conv_post(y, lgv, lbv, ogv)[4] * ogv).astype(BF16)

    return _call(
        body, name="conv_fwd", grid=(T // tt,),
        in_specs=[_rows(tt, CONV_W), _rows(tt, CONV_W), halo_spec, halo_spec] + [RESIDENT] * 5,
        out_specs=[_rows(tt, CONV_W), _rows(tt, CONV_W)],
        out_shape=[jax.ShapeDtypeStruct((T, CONV_W), F32), jax.ShapeDtypeStruct((T, CONV_W), BF16)],
        scratch_shapes=[pltpu.VMEM((HALO + tt, CONV_W), F32), pltpu.VMEM((SLAB_ROWS, CONV_W), F32)],
        compiler_params=_params("parallel"),
    )(zu, zg, zu, zg, cw, cb, lg, lb, og)


def _chunked(tm):
    return pl.BlockSpec((N_FF, tm, FF_CHUNK), lambda i: (0, i, 0))


def _fwd_out(x, mix_a, mix_c, w_out):
    T = x.shape[0]
    tm = 512

    def body(x_ref, a_ref, c_ref, w_ref, o_ref):
        o_ref[...] = (x_ref[...] + _dot(a_ref[...], w_ref[0:ATTN_W, :])
                      + _dot(c_ref[...], w_ref[ATTN_W:, :]))

    return _call(
        body, name="fwd_out", grid=(T // tm,),
        in_specs=[_rows(tm, D_MODEL), _rows(tm, ATTN_W), _rows(tm, CONV_W), RESIDENT],
        out_specs=_rows(tm, D_MODEL),
        out_shape=jax.ShapeDtypeStruct((T, D_MODEL), F32),
        compiler_params=_params("parallel"),
    )(x, mix_a, mix_c, w_out)


def _mlp_fwd(x, g, wup_t, wdown):
    T = x.shape[0]
    tm = 512

    def body(x_ref, g_ref, wu_ref, wd_ref, o_ref, up_ref):
        xv = x_ref[...]
        xh, _ = _rms(xv)
        h = (xh * g_ref[...]).astype(BF16)
        acc = xv
        for c in range(N_FF):
            rows = slice(c * FF_CHUNK, (c + 1) * FF_CHUNK)
            up = _dot_nt(h, wu_ref[rows, :])
            up_ref[c] = up.astype(BF16)
            act = jnp.square(jnp.maximum(up, 0.0))
            acc = acc + _dot(act.astype(BF16), wd_ref[rows, :])
        o_ref[...] = acc

    return _call(
        body, name="mlp_fwd", grid=(T // tm,),
        in_specs=[_rows(tm, D_MODEL), RESIDENT, RESIDENT, RESIDENT],
        out_specs=[_rows(tm, D_MODEL), _chunked(tm)],
        out_shape=[jax.ShapeDtypeStruct((T, D_MODEL), F32), jax.ShapeDtypeStruct((N_FF, T, FF_CHUNK), BF16)],
        compiler_params=_params("parallel"),
    )(x, g, wup_t, wdown)


def _loss_head(y, target):
    T = y.shape[0]
    tm = 512

    def body(y_ref, t_ref, l_ref, d_ref):
        @pl.when(pl.program_id(0) == 0)
        def _():
            l_ref[...] = jnp.zeros_like(l_ref)
        e = y_ref[...] - t_ref[...]
        d_ref[...] = e / D_MODEL
        l_ref[...] += 0.5 * jnp.sum(jnp.mean(e * e, axis=-1, keepdims=True))

    return _call(
        body, name="loss_head", grid=(T // tm,),
        in_specs=[_rows(tm, D_MODEL), _rows(tm, D_MODEL)],
        out_specs=[_acc_rows(128, 8), _rows(tm, D_MODEL)],
        out_shape=[jax.ShapeDtypeStruct((8, 128), F32), jax.ShapeDtypeStruct((T, D_MODEL), F32)],
        compiler_params=_params("arbitrary"),
    )(y, target)


def _bias_table(rel_bias, bucket):
    def body(rb_ref, bk_ref, o_ref):
        bk = bk_ref[...]
        for h in range(N_HEADS):
            acc = jnp.zeros((BLOCK, BLOCK), F32)
            for b in range(N_BUCKETS):
                acc = jnp.where(bk == b, rb_ref[b, h], acc)
            o_ref[h // GQA, :, (h % GQA) * BLOCK:(h % GQA + 1) * BLOCK] = acc

    return _call(
        body, name="bias_table", in_specs=[IN_SMEM, RESIDENT], out_specs=RESIDENT,
        out_shape=jax.ShapeDtypeStruct((N_KV, BLOCK, GQA * BLOCK), F32),
    )(rel_bias, bucket)


def _mlp_bwd(dx2, x1, up, g, wup_t, wdown):
    T = x1.shape[0]
    tm = 512

    def body(d_ref, x_ref, up_ref, g_ref, wu_ref, wd_ref, dx_ref, dup_ref, h_ref, dg_ref, db_ref, d1b_ref):
        @pl.when(pl.program_id(0) == 0)
        def _():
            dg_ref[...] = jnp.zeros_like(dg_ref)
        d2 = d_ref[...]
        d2b = d2.astype(BF16)
        db_ref[...] = d2b
        xh, r = _rms(x_ref[...])
        gv = g_ref[...]
        h_ref[...] = (xh * gv).astype(BF16)
        dh = jnp.zeros((tm, D_MODEL), F32)
        for c in range(N_FF):
            rows = slice(c * FF_CHUNK, (c + 1) * FF_CHUNK)
            dact = _dot_nt(d2b, wd_ref[rows, :])
            dup = (dact * (2.0 * jnp.maximum(up_ref[c].astype(F32), 0.0))).astype(BF16)
            dup_ref[c] = dup
            dh = dh + _dot(dup, wu_ref[rows, :])
        dg_ref[...] += _colsum(dh * xh)
        d1 = d2 + _rms_bwd(dh * gv, xh, r)
        dx_ref[...] = d1
        d1b_ref[...] = d1.astype(BF16)

    return _call(
        body, name="mlp_bwd", grid=(T // tm,),
        in_specs=[_rows(tm, D_MODEL), _rows(tm, D_MODEL), _chunked(tm), RESIDENT, RESIDENT, RESIDENT],
        out_specs=[_rows(tm, D_MODEL), _chunked(tm), _rows(tm, D_MODEL), _acc_rows(D_MODEL), _rows(tm, D_MODEL),
                   _rows(tm, D_MODEL)],
        out_shape=[jax.ShapeDtypeStruct((T, D_MODEL), F32), jax.ShapeDtypeStruct((N_FF, T, FF_CHUNK), BF16),
                   jax.ShapeDtypeStruct((T, D_MODEL), BF16), jax.ShapeDtypeStruct((1, D_MODEL), F32),
                   jax.ShapeDtypeStruct((T, D_MODEL), BF16), jax.ShapeDtypeStruct((T, D_MODEL), BF16)],
        compiler_params=_params("arbitrary"),
    )(dx2, x1, up, g, wup_t, wdown)


def _bwd_out(dx1b, w_out):
    T = dx1b.shape[0]
    tm = 512

    def body(d_ref, w_ref, da_ref, dc_ref):
        dm = _dot_nt(d_ref[...], w_ref[...])
        da_ref[...] = dm[:, :ATTN_W]
        dc_ref[...] = dm[:, ATTN_W:]

    return _call(
        body, name="bwd_out", grid=(T // tm,),
        in_specs=[_rows(tm, D_MODEL), RESIDENT],
        out_specs=[_rows(tm, ATTN_W), _rows(tm, CONV_W)],
        out_shape=[jax.ShapeDtypeStruct((T, ATTN_W), F32), jax.ShapeDtypeStruct((T, CONV_W), F32)],
        compiler_params=_params("parallel"),
    )(dx1b, w_out)


def _wgrad_one_block(a, b, name):
    T, tr = a.shape
    tk = min(T, 4096)
    nk = T // tk

    def body(a_ref, b_ref, o_ref, acc):
        k = pl.program_id(0)

        @pl.when(k == 0)
        def _():
            acc[...] = jnp.zeros_like(acc)
        acc[...] += _dot_tn(a_ref[...], b_ref[...])

        @pl.when(k == nk - 1)
        def _():
            o_ref[...] = acc[...].astype(BF16)

    return _call(
        body, name=name, grid=(nk,),
        in_specs=[_rows(tk, tr), _rows(tk, D_MODEL)],
        out_specs=pl.BlockSpec((tr, D_MODEL), lambda k: (0, 0)),
        out_shape=jax.ShapeDtypeStruct((tr, D_MODEL), BF16),
        scratch_shapes=[pltpu.VMEM((tr, D_MODEL), F32)],
        compiler_params=_params("arbitrary"),
    )(a, b)


def _wgrad(a, b, name, chunked=False, square_relu=False, tr=FF_CHUNK):
    if chunked:
        nr, T, tr = a.shape
    else:
        T, R = a.shape
        tr = min(R, tr)
        nr = R // tr
    piece = min(T, 2048)
    if nr == 1:
        return _wgrad_one_block(a, b, name)

    def body(a_ref, b_ref, o_ref):
        if square_relu:
            tot = jnp.zeros((tr, D_MODEL), F32)
            for k0 in range(0, T, piece):
                av = jnp.square(jnp.maximum(a_ref[k0:k0 + piece, :].astype(F32), 0.0)).astype(BF16)
                tot = tot + _dot_tn(av, b_ref[k0:k0 + piece, :])
        else:
            tot = _dot_tn(a_ref[...], b_ref[...])
        o_ref[...] = tot.astype(BF16)

    a_spec = (pl.BlockSpec((None, T, tr), lambda r: (r, 0, 0)) if chunked else pl.BlockSpec((T, tr), lambda r: (0, r)))
    return _call(
        body, name=name, grid=(nr,),
        in_specs=[a_spec, RESIDENT],
        out_specs=pl.BlockSpec((tr, D_MODEL), lambda r: (r, 0)),
        out_shape=jax.ShapeDtypeStruct((nr * tr, D_MODEL), BF16),
        compiler_params=_params("parallel"),
    )(a, b)


def _conv_bwd(dmix, y, zu, zg, cw, lg, lb, og):
    T = y.shape[0]
    tt = 512
    nt = T // tt
    per = tt // HALO
    prev_spec = pl.BlockSpec((HALO, CONV_W), lambda i: (jnp.maximum(i * per - 1, 0), 0))
    next_spec = pl.BlockSpec((HALO, CONV_W), lambda i: (jnp.minimum((i + 1) * per, nt * per - 1), 0))

    def body(dm_ref, dmn_ref, y_ref, yn_ref, u_ref, g_ref, uh_ref, gh_ref, cw_ref, lg_ref, lb_ref, og_ref,
             du_ref, dg_ref, pg_ref, hbuf, dybuf, dwacc, pacc, slab):
        i = pl.program_id(0)

        @pl.when(i == 0)
        def _():
            dwacc[...] = jnp.zeros_like(dwacc)
            pacc[...] = jnp.zeros_like(pacc)
        lgv, lbv, ogv = lg_ref[...], lb_ref[...], og_ref[...]

        def chain(yv, dm):
            yn, rstd, ln, sg, ch, r = _conv_post(yv, lgv, lbv, ogv)
            dc = _rms_bwd(dm * ogv, ch, r)
            dln = dc * sg * (1.0 + ln * (1.0 - sg))
            dyn = dln * lgv
            dy = rstd * (dyn - jnp.mean(dyn, axis=-1, keepdims=True)
                         - yn * jnp.mean(dyn * yn, axis=-1, keepdims=True))
            return dy, dm * ch, dln * yn, dln

        hbuf[0:HALO, :] = jnp.where(i > 0, uh_ref[...] * _sig(gh_ref[...]), 0.0)
        for r0 in range(0, tt, CONV_CHUNK):
            rs = slice(r0, r0 + CONV_CHUNK)
            hbuf[HALO + r0:HALO + r0 + CONV_CHUNK, :] = u_ref[rs, :] * _sig(g_ref[rs, :])
            dy, p_og, p_lg, p_lb = chain(y_ref[rs, :], dm_ref[rs, :])
            dybuf[rs, :] = dy
            for k, part in enumerate((dy, p_lg, p_lb, p_og)):
                pacc[8 * k:8 * k + 8, :] += _fold8(part)
        dyh, _, _, _ = chain(yn_ref[...], dmn_ref[...])
        dybuf[tt:tt + HALO, :] = jnp.where(i < nt - 1, dyh, 0.0)
        for r0 in range(0, tt, CONV_CHUNK):
            rs = slice(r0, r0 + CONV_CHUNK)
            dy = dybuf[rs, :]
            dh = jnp.zeros((CONV_CHUNK, CONV_W), F32)
            for j, win in _shifted_taps(dybuf, slab, r0, 0):
                dh = dh + cw_ref[CONV_K - 1 - j:CONV_K - j, :] * win
            for j, win in _shifted_taps(hbuf, slab, r0, HALO - (CONV_K - 1)):
                dwacc[8 * j:8 * j + 8, :] += _fold8(dy * win)
            sgt = _sig(g_ref[rs, :])
            du_ref[rs, :] = (dh * sgt).astype(BF16)
            dg_ref[rs, :] = (dh * u_ref[rs, :] * sgt * (1.0 - sgt)).astype(BF16)

        @pl.when(i == nt - 1)
        def _():
            pg_ref[...] = jnp.zeros_like(pg_ref)
            for j in range(CONV_K):
                pg_ref[j:j + 1, :] = _colsum(dwacc[8 * j:8 * j + 8, :])
            for k in range(4):
                pg_ref[32 + k:33 + k, :] = _colsum(pacc[8 * k:8 * k + 8, :])

    return _call(
        body, name="conv_bwd", grid=(nt,),
        in_specs=[_rows(tt, CONV_W), next_spec, _rows(tt, CONV_W), next_spec, _rows(tt, CONV_W), _rows(tt, CONV_W),
                  prev_spec, prev_spec] + [RESIDENT] * 4,
        out_specs=[_rows(tt, CONV_W), _rows(tt, CONV_W), _acc_rows(CONV_W, 40)],
        out_shape=[jax.ShapeDtypeStruct((T, CONV_W), BF16), jax.ShapeDtypeStruct((T, CONV_W), BF16),
                   jax.ShapeDtypeStruct((40, CONV_W), F32)],
        scratch_shapes=[pltpu.VMEM((HALO + tt, CONV_W), F32), pltpu.VMEM((tt + HALO, CONV_W), F32),
                        pltpu.VMEM((8 * HALO, CONV_W), F32), pltpu.VMEM((32, CONV_W), F32),
                        pltpu.VMEM((SLAB_ROWS, CONV_W), F32)],
        compiler_params=_params("arbitrary"),
    )(dmix, dmix, y, y, zu, zg, zu, zg, cw, lg, lb, og)


def _attn_bwd(dmix, a, zq, zkv, bias, sinks, qg, kg, og):
    T = zq.shape[0]
    rows = ATTN_SUB * BLOCK
    ns = T // rows
    nb = T // BLOCK
    cur = lambda w: pl.BlockSpec((rows, w), lambda n: (jnp.minimum(n, ns - 1), 0))
    scale = 1.0 / math.sqrt(HEAD_DIM)
    done = rows - BLOCK

    def body(dm_ref, a_ref, q_ref, kvc_ref, kvp_ref, bias_ref, sinks_ref, qgt_ref, kg_ref, kgt_ref, og_ref,
             dq_ref, dkv_ref, db_ref, dog_ref, dqg_ref, dkg_ref, dsk_ref, carry):
        n = pl.program_id(0)

        @pl.when(n == 0)
        def _():
            for ref in (db_ref, dog_ref, dqg_ref, dkg_ref, dsk_ref, carry):
                ref[...] = jnp.zeros_like(ref)

        @pl.when(n < ns)
        def _():
            tri = _tri_t()
            ogv, qgt, kgv, kgt = og_ref[...], qgt_ref[...], kg_ref[...], kgt_ref[...]
            lane = lax.broadcasted_iota(jnp.int32, (1, 128), 1)
            kvs = [kvp_ref[...]] + [kvc_ref[i * BLOCK:(i + 1) * BLOCK, :] for i in range(ATTN_SUB)]
            keys = [[_kv_block(kv, kh, kgv, kgt) for kh in range(N_KV)] for kv in kvs]
            dkn = [[jnp.zeros((BLOCK, HEAD_DIM), F32)] * N_KV for _ in kvs]
            dv = [[jnp.zeros((BLOCK, HEAD_DIM), F32)] * N_KV for _ in kvs]
            dsk = jnp.zeros((1, 128), F32)
            for i in range(ATTN_SUB):
                blk = slice(i * BLOCK, (i + 1) * BLOCK)
                ah, ra = _rms(a_ref[blk, :])
                dm = dm_ref[blk, :]
                dog_ref[...] += _colsum(dm * ah)
                da_t = _rms_bwd(dm * ogv, ah, ra).T
                zq_t = q_ref[blk, :].T
                has_prev = (n > 0) if i == 0 else None
                dqs = []
                for kh in range(N_KV):
                    kp, kc = keys[i][kh], keys[i + 1][kh]
                    c = _attn_group(zq_t, kp, kc, kh, bias_ref, sinks_ref, qgt, tri, has_prev)
                    dob = _heads_side_by_side(da_t, kh).astype(BF16)
                    pn = c["pn"]
                    p_c, p_p = _split(pn.astype(BF16), tri)
                    dv[i + 1][kh] = dv[i + 1][kh] + _dot_nt(p_c, dob)
                    dv[i][kh] = dv[i][kh] + _dot_nt(p_p, dob)
                    dp = jnp.where(tri, _dot(kc["v"], dob), _dot(kp["v"], dob))
                    dl = jnp.sum(pn * dp, axis=0, keepdims=True)
                    ds = pn * (dp - dl)
                    dsr = -c["psink"] * dl
                    for g in range(GQA):
                        dsk = dsk + jnp.where(lane == kh * GQA + g, jnp.sum(jnp.where(_head_lanes(g), dsr, 0.0)), 0.0)
                    db_ref[kh] += ds
                    ds_c, ds_p = _split((ds * scale).astype(BF16), tri)
                    dqn = _dot(kc["kn_t"], ds_c) + _dot(kp["kn_t"], ds_p)
                    dkn[i + 1][kh] = dkn[i + 1][kh] + _dot_nt(ds_c, c["qn"])
                    dkn[i][kh] = dkn[i][kh] + _dot_nt(ds_p, c["qn"])
                    dqg_ref[...] += jnp.sum(dqn * c["qhat"], axis=1, keepdims=True)
                    dq_t = _rms0_bwd(dqn * qgt, c["qhat"], c["rq"])
                    dqs += [dq_t[:, g * BLOCK:(g + 1) * BLOCK] for g in range(GQA)]
                dq_ref[blk, :] = jnp.concatenate(dqs, axis=0).T.astype(BF16)
            dsk_ref[...] += dsk
            dkv = []
            for j in range(ATTN_SUB + 1):
                dk = []
                for kh in range(N_KV):
                    key = keys[j][kh]
                    dkg_ref[...] += _colsum(dkn[j][kh] * key["khat"])
                    dk.append(_rms_bwd(dkn[j][kh] * kgv, key["khat"], key["rk"]))
                dkv.append(jnp.concatenate(dk + dv[j], axis=-1))
            if done:
                dkv_ref[0:done, :] = carry[0:done, :].astype(BF16)
            dkv_ref[done:rows, :] = (carry[done:rows, :] + dkv[0]).astype(BF16)
            for j in range(1, ATTN_SUB + 1):
                carry[(j - 1) * BLOCK:j * BLOCK, :] = dkv[j]

        @pl.when(n == ns)
        def _():
            dkv_ref[...] = carry[...].astype(BF16)

    small = lambda w: pl.BlockSpec((1, w), lambda n: (0, 0))
    return _call(
        body, name="attn_bwd", grid=(ns + 1,),
        in_specs=[cur(ATTN_W), cur(ATTN_W), cur(ATTN_W), cur(2 * KV_W),
                  pl.BlockSpec((BLOCK, 2 * KV_W), lambda n: (jnp.clip(n * ATTN_SUB - 1, 0, nb - 1), 0)),
                  RESIDENT, IN_SMEM, RESIDENT, RESIDENT, RESIDENT, RESIDENT],
        out_specs=[cur(ATTN_W), pl.BlockSpec((rows, 2 * KV_W), lambda n: (jnp.maximum(n - 1, 0), 0)),
                   pl.BlockSpec((N_KV, BLOCK, GQA * BLOCK), lambda n: (0, 0, 0)),
                   small(ATTN_W), pl.BlockSpec((HEAD_DIM, 1), lambda n: (0, 0)), small(HEAD_DIM), small(128)],
        out_shape=[jax.ShapeDtypeStruct((T, ATTN_W), BF16), jax.ShapeDtypeStruct((T, 2 * KV_W), BF16),
                   jax.ShapeDtypeStruct((N_KV, BLOCK, GQA * BLOCK), F32),
                   jax.ShapeDtypeStruct((1, ATTN_W), F32), jax.ShapeDtypeStruct((HEAD_DIM, 1), F32),
                   jax.ShapeDtypeStruct((1, HEAD_DIM), F32), jax.ShapeDtypeStruct((1, 128), F32)],
        scratch_shapes=[pltpu.VMEM((rows, 2 * KV_W), F32)],
        compiler_params=_params("arbitrary"),
    )(dmix, a, zq, zkv, zkv, bias, sinks, qg.reshape(HEAD_DIM, 1), kg, kg.reshape(HEAD_DIM, 1), og)


def _bwd_in(dx1, x, dq, dkv, du, dgt, g, wt):
    T = x.shape[0]
    tm = 512

    def body(d1_ref, x_ref, dq_ref, dkv_ref, du_ref, dgt_ref, g_ref, w_ref, dx_ref, h_ref, dg_ref, dz_ref):
        @pl.when(pl.program_id(0) == 0)
        def _():
            dg_ref[...] = jnp.zeros_like(dg_ref)
        xh, r = _rms(x_ref[...])
        gv = g_ref[...]
        h_ref[...] = (xh * gv).astype(BF16)
        dz = jnp.concatenate([dq_ref[...], dkv_ref[...], du_ref[...], dgt_ref[...]], axis=-1)
        dz_ref[...] = dz
        dh = _dot(dz, w_ref[...])
        dg_ref[...] += _colsum(dh * xh)
        dx_ref[...] = d1_ref[...] + _rms_bwd(dh * gv, xh, r)

    return _call(
        body, name="bwd_in", grid=(T // tm,),
        in_specs=[_rows(tm, D_MODEL), _rows(tm, D_MODEL), _rows(tm, ATTN_W), _rows(tm, 2 * KV_W),
                  _rows(tm, CONV_W), _rows(tm, CONV_W), RESIDENT, RESIDENT],
        out_specs=[_rows(tm, D_MODEL), _rows(tm, D_MODEL), _acc_rows(D_MODEL), _rows(tm, IN_W)],
        out_shape=[jax.ShapeDtypeStruct((T, D_MODEL), F32), jax.ShapeDtypeStruct((T, D_MODEL), BF16),
                   jax.ShapeDtypeStruct((1, D_MODEL), F32), jax.ShapeDtypeStruct((T, IN_W), BF16)],
        compiler_params=_params("arbitrary"),
    )(dx1, x, dq, dkv, du, dgt, g, wt)


def _bias_grad(db, bucket):
    def body(db_ref, bk_ref, o_ref):
        bk = bk_ref[...]
        lane = lax.broadcasted_iota(jnp.int32, (1, 128), 1)
        for h in range(N_HEADS):
            cols = slice((h % GQA) * BLOCK, (h % GQA + 1) * BLOCK)
            tot = db_ref[0, h // GQA, :, cols]
            for l in range(1, DEPTH):
                tot = tot + db_ref[l, h // GQA, :, cols]
            out = jnp.zeros((1, 128), F32)
            for b in range(N_BUCKETS):
                out = jnp.where(lane == b, jnp.sum(jnp.where(bk == b, tot, 0.0)), out)
            o_ref[h:h + 1, :] = out

    return _call(
        body, name="bias_grad", in_specs=[RESIDENT, RESIDENT], out_specs=RESIDENT,
        out_shape=jax.ShapeDtypeStruct((N_HEADS, 128), F32),
        compiler_params=pltpu.CompilerParams(vmem_limit_bytes=VMEM_LIMIT),
    )(db, bucket)


def _place():
    return lax.axis_index("x"), lax.axis_index("y"), lax.axis_index("c")


def _all_gather(shards):
    na = len(shards)

    def body(*refs):
        ins, outs = refs[:na], refs[na:2 * na]
        send_sems, recv_sems, local_sems = refs[2 * na:]
        x, y, c = _place()
        me, sibling = (x, y, c), (x, y, 1 - c)
        chips = [(1 - x, y), (x, 1 - y), (1 - x, 1 - y)]

        def rows(a, p):
            r = ins[a].shape[1]
            return outs[a].at[:, pl.ds((4 * p[0] + 2 * p[1] + p[2]) * r, r), :]

        def copy(a, k, block, to, src=None):
            return pltpu.make_async_remote_copy(
                src_ref=rows(a, block) if src is None else src, dst_ref=rows(a, block),
                send_sem=send_sems.at[a, k], recv_sem=recv_sems.at[a, k], device_id=to, device_id_type=MESH)

        mine = [pltpu.make_async_copy(ins[a], rows(a, me), local_sems.at[a]) for a in range(na)]
        for cp in mine:
            cp.start()
        first = []
        for a in range(na):
            first.append(copy(a, 0, me, sibling, src=ins[a]))
            first += [copy(a, 1 + j, me, (*chip, c), src=ins[a]) for j, chip in enumerate(chips)]
        for cp in first:
            cp.start()
        passed = []
        for j, chip in enumerate(chips):
            for a in range(na):
                copy(a, 1 + j, (*chip, c), me).wait_recv()
                cp = copy(a, 4 + j, (*chip, c), sibling)
                cp.start()
                passed.append(cp)
        for a in range(na):
            copy(a, 0, sibling, me).wait_recv()
            for j, chip in enumerate(chips):
                copy(a, 4 + j, (*chip, 1 - c), me).wait_recv()
        for cp in first + passed:
            cp.wait_send()
        for cp in mine:
            cp.wait()

    return _call(
        body, name="all_gather",
        in_specs=[ANY] * na, out_specs=[ANY] * na,
        out_shape=[jax.ShapeDtypeStruct((s.shape[0], N_DEV * s.shape[1], s.shape[2]), s.dtype) for s in shards],
        scratch_shapes=[pltpu.SemaphoreType.DMA((na, 7)), pltpu.SemaphoreType.DMA((na, 7)),
                        pltpu.SemaphoreType.DMA((na,))],
    )(*shards)


IN_HBM = pl.BlockSpec(memory_space=pltpu.HBM)
IN_SEM = pl.BlockSpec(memory_space=pltpu.SEMAPHORE)
DATAFLOW = pltpu.SideEffectType.DATAFLOW_SIDE_EFFECTING


def _exchange_copies(srcs, lands, send_sems, recv_sems, scatter):
    x, y, c = _place()
    me = 4 * x + 2 * y + c
    copies = []
    for k in range(1, N_DEV):
        p = (x ^ (k >> 2), y ^ ((k >> 1) & 1), c ^ (k & 1))
        for a, (src, land) in enumerate(zip(srcs, lands)):
            r = land.shape[0] // N_DEV
            if scatter:
                src = src.at[pl.ds((4 * p[0] + 2 * p[1] + p[2]) * r, r), :]
            copies.append(pltpu.make_async_remote_copy(
                src_ref=src, dst_ref=land.at[pl.ds(me * r, r), :], send_sem=send_sems.at[a * (N_DEV - 1) + k - 1],
                recv_sem=recv_sems.at[a * (N_DEV - 1) + k - 1], device_id=p, device_id_type=MESH))
    return copies


def _own_copies(srcs, lands, send_sems, scatter):
    x, y, c = _place()
    me = 4 * x + 2 * y + c
    copies = []
    for a, (src, land) in enumerate(zip(srcs, lands)):
        r = land.shape[0] // N_DEV
        if scatter:
            src = src.at[pl.ds(me * r, r), :]
        copies.append(pltpu.make_async_copy(src, land.at[pl.ds(me * r, r), :],
                                            send_sems.at[len(srcs) * (N_DEV - 1) + a]))
    return copies


def _exchange_start(srcs, name, scatter):
    na = len(srcs)
    lands = [lax.empty((s.shape[0] * (1 if scatter else N_DEV), s.shape[1]), s.dtype) for s in srcs]

    def body(*refs):
        ins, lnd = refs[:na], refs[na:2 * na]
        send_sems, recv_sems = refs[2 * na], refs[2 * na + 1]
        token = refs[-1]
        for cp in _exchange_copies(ins, lnd, send_sems, recv_sems, scatter) + _own_copies(ins, lnd, send_sems, scatter):
            cp.start()
        token[...] = jnp.zeros_like(token)

    hbm = lambda a: pltpu.with_memory_space_constraint(a, pltpu.HBM)
    out = _call(
        body, name=name,
        out_shape=(pltpu.SemaphoreType.DMA((na * N_DEV,)), pltpu.SemaphoreType.DMA((na * (N_DEV - 1),)),
                   *[pltpu.HBM(a.shape, a.dtype) for a in (*srcs, *lands)], jax.ShapeDtypeStruct((8, 128), F32)),
        in_specs=[IN_HBM] * (2 * na),
        out_specs=(IN_SEM, IN_SEM, *[IN_HBM] * (2 * na), RESIDENT),
        input_output_aliases={i: 2 + i for i in range(2 * na)},
        compiler_params=pltpu.CompilerParams(has_side_effects=DATAFLOW),
    )(*[hbm(a) for a in (*srcs, *lands)])
    return (out[0], out[1], out[2:2 + na], out[2 + na:2 + 2 * na]), out[-1]


def _exchange_wait(state, after, name, scatter):
    send_sems, recv_sems, srcs, lands = state
    na = len(srcs)

    def body(*refs):
        ins, lnd = refs[:na], refs[na:2 * na]
        for cp in _exchange_copies(ins, lnd, refs[2 * na], refs[2 * na + 1], scatter):
            cp.wait_send()
            cp.wait_recv()
        for cp in _own_copies(ins, lnd, refs[2 * na], scatter):
            cp.wait()
        refs[-1][...] = jnp.zeros_like(refs[-1])

    out = _call(
        body, name=name,
        out_shape=(*[pltpu.HBM(a.shape, a.dtype) for a in (*srcs, *lands)], jax.ShapeDtypeStruct((8, 128), F32)),
        in_specs=[IN_HBM] * (2 * na) + [IN_SEM, IN_SEM, ANY],
        out_specs=(*[IN_HBM] * (2 * na), RESIDENT),
        input_output_aliases={i: i for i in range(2 * na)},
        compiler_params=pltpu.CompilerParams(has_side_effects=DATAFLOW),
    )(*srcs, *lands, send_sems, recv_sems, after)
    return out[:na], out[na:2 * na], out[-1]


def _adam_math(w, g, m, v):
    m = ADAM_B1 * m + (1.0 - ADAM_B1) * g
    v = ADAM_B2 * v + (1.0 - ADAM_B2) * jnp.square(g)
    m_hat = m / (1.0 - ADAM_B1 ** ADAM_STEP)
    v_hat = v / (1.0 - ADAM_B2 ** ADAM_STEP)
    delta = -ADAM_LR * (m_hat / (jnp.sqrt(v_hat) + ADAM_EPS) + ADAM_WD * w)
    return delta, m, v


def _adam_big(parts, w, m, v, transposed=False):
    if transposed:
        nl, cdim, r = w.shape
    else:
        nl, r, cdim = w.shape
    tr = r if r <= 256 else 256
    ni = r // tr
    spec = (pl.BlockSpec((None, cdim, tr), lambda l, i: (l, 0, i)) if transposed
            else pl.BlockSpec((None, tr, cdim), lambda l, i: (l, i, 0)))

    def part_spec(layer):
        return pl.BlockSpec((N_DEV, tr, cdim),
                            lambda l, i: (0, jnp.where(l == layer, i, jnp.where(l < layer, 0, ni - 1)), 0))

    def body(*refs):
        p_refs = refs[:nl]
        w_ref, m_ref, v_ref, g_ref, d_ref, nm_ref, nv_ref = refs[nl:]
        for layer in range(nl):
            @pl.when(pl.program_id(0) == layer)
            def _(p_ref=p_refs[layer]):
                g = p_ref[0].astype(F32)
                for s in range(1, N_DEV):
                    g = g + p_ref[s].astype(F32)
                if transposed:
                    g = g.T
                d, nm, nv = _adam_math(w_ref[...], g, m_ref[...], v_ref[...])
                g_ref[...] = g
                d_ref[...] = d
                nm_ref[...] = nm
                nv_ref[...] = nv

    return _call(
        body, name="adam_big", grid=(nl, ni),
        in_specs=[part_spec(layer) for layer in range(nl)] + [spec, spec, spec],
        out_specs=[spec] * 4,
        out_shape=[jax.ShapeDtypeStruct(w.shape, F32)] * 4,
        compiler_params=_params("arbitrary", "arbitrary"),
    )(*parts, w, m, v)


def _sum_slots(parts):
    def body(p_ref, o_ref):
        g = p_ref[0]
        for s in range(1, N_DEV):
            g = g + p_ref[s]
        o_ref[...] = g

    return _call(body, name="sum_slots", in_specs=[RESIDENT], out_specs=RESIDENT,
                 out_shape=jax.ShapeDtypeStruct(parts.shape[1:], F32))(parts)


def _adam_small(ws, gs, ms, vs):
    n = len(ws)

    def body(*refs):
        ins, outs = refs[:4 * n], refs[4 * n:]
        for k in range(n):
            d, nm, nv = _adam_math(ins[k][...], ins[n + k][...], ins[2 * n + k][...], ins[3 * n + k][...])
            outs[k][...] = d
            outs[n + k][...] = nm
            outs[2 * n + k][...] = nv

    out = _call(body, name="adam_small", in_specs=[RESIDENT] * (4 * n), out_specs=[RESIDENT] * (3 * n),
                out_shape=[jax.ShapeDtypeStruct(w.shape, F32) for w in ws] * 3)(*ws, *gs, *ms, *vs)
    return out[:n], out[n:2 * n], out[2 * n:]


def _pack(arrays):
    parts = []
    for a in arrays:
        flat = a.reshape(-1)
        n = flat.shape[0]
        padded = -(-n // 1024) * 1024
        parts.append(jnp.pad(flat, (0, padded - n)).reshape(padded // 128, 128))
    return jnp.concatenate(parts, axis=0)


def _unpack(packed, shapes):
    out, row = [], 0
    for shp in shapes:
        n = int(np.prod(shp))
        rows = -(-n // 1024) * 8
        out.append(packed[row:row + rows].reshape(-1)[:n].reshape(shp))
        row += rows
    return out


def _t5_bucket_table():
    kj = np.arange(BLOCK)[:, None]
    qi = np.arange(BLOCK)[None, :]
    n = (qi - kj) % BLOCK
    max_exact = N_BUCKETS // 2
    large = max_exact + (np.log(np.maximum(n, 1) / max_exact) / np.log(MAX_DIST / max_exact)
                         * (N_BUCKETS - max_exact)).astype(np.int32)
    large = np.minimum(large, N_BUCKETS - 1)
    return np.where(n < max_exact, n, large).astype(np.int32)


SMALL_NAMES = ("rel_bias", "norm_mix_g", "q_norm_g", "k_norm_g", "sinks", "conv_b", "conv_ln_g", "conv_ln_b",
               "attn_out_g", "conv_out_g", "norm_mlp_g")


def _local_step(x, target, small, conv_w, get_w, put_g, token):
    bucket = jnp.asarray(_t5_bucket_table())
    bias = _bias_table(small["rel_bias"], bucket)
    row = lambda a, l: a[l][None, :]
    cw_pad = jnp.pad(conv_w, ((0, 0), (0, HALO - CONV_K), (0, 0)))
    saved, weights = [], []
    for l in range(DEPTH):
        wt_in, w_out, tok = get_w(l, 0, x)
        zq, zkv, zu, zg = _fwd_in(x, row(small["norm_mix_g"], l) + tok + (token if l == 0 else 0.0), wt_in)
        a, mix_a = _attn_fwd(zq, zkv, bias, small["sinks"][l], row(small["q_norm_g"], l), row(small["k_norm_g"], l),
                             row(small["attn_out_g"], l))
        y, mix_c = _conv_fwd(zu, zg, cw_pad[l], row(small["conv_b"], l), row(small["conv_ln_g"], l),
                             row(small["conv_ln_b"], l), row(small["conv_out_g"], l))
        x1 = _fwd_out(x, mix_a, mix_c, w_out)
        wt_up, w_down, tok = get_w(l, 1, x1)
        weights.append((wt_in, w_out, wt_up, w_down))
        x2, up = _mlp_fwd(x1, row(small["norm_mlp_g"], l) + tok, wt_up, w_down)
        saved.append((x, zq, zkv, zu, zg, a, mix_a, y, mix_c, x1, up))
        x = x2
    loss_part, dx = _loss_head(x, target)

    gs = {n: [None] * DEPTH for n in SMALL_NAMES if n != "rel_bias"}
    g_conv_w, dbias = [None] * DEPTH, [None] * DEPTH
    token = 0.0
    for l in reversed(range(DEPTH)):
        x0, zq, zkv, zu, zg, a, mix_a, y, mix_c, x1, up = saved[l]
        wt_in, w_out, wt_up, w_down = weights[l]
        dx1, dup, h2, gs["norm_mlp_g"][l], dxb, dx1b = _mlp_bwd(
            dx, x1, up, row(small["norm_mlp_g"], l) + token, wt_up, w_down)
        g_up = _wgrad(dup, h2, "wgrad_up", chunked=True)
        g_down = _wgrad(up, dxb, "wgrad_down", chunked=True, square_relu=True)
        token = put_g(l, 1, (g_up, g_down))
        dma, dmc = _bwd_out(dx1b, w_out)
        g_out = jnp.concatenate([_wgrad(mix_a, dx1b, "wgrad_out_a"), _wgrad(mix_c, dx1b, "wgrad_out_c")], axis=0)
        du, dgt, pg = _conv_bwd(dmc, y, zu, zg, cw_pad[l], row(small["conv_ln_g"], l) + token,
                                row(small["conv_ln_b"], l), row(small["conv_out_g"], l))
        g_conv_w[l] = pg[:CONV_K]
        gs["conv_b"][l], gs["conv_ln_g"][l], gs["conv_ln_b"][l], gs["conv_out_g"][l] = pg[32], pg[33], pg[34], pg[35]
        dq, dkv, dbias[l], dog, dqg, dkg, dsk = _attn_bwd(
            dma, a, zq, zkv, bias, small["sinks"][l], row(small["q_norm_g"], l), row(small["k_norm_g"], l),
            row(small["attn_out_g"], l))
        gs["attn_out_g"][l], gs["q_norm_g"][l], gs["k_norm_g"][l], gs["sinks"][l] = dog, dqg, dkg, dsk[0, :N_HEADS]
        dx, h, gs["norm_mix_g"][l], dz = _bwd_in(dx1, x0, dq, dkv, du, dgt, row(small["norm_mix_g"], l), wt_in)
        g_in = _wgrad(dz, h, "wgrad_in", tr=256)
        token = put_g(l, 0, (g_in, g_out))
    small_grads = {n: jnp.stack([jnp.reshape(v, (-1,)) for v in vals]) for n, vals in gs.items()}
    small_grads["rel_bias"] = _bias_grad(jnp.stack(dbias), bucket)[:, :N_BUCKETS].T
    return loss_part, dx, small_grads, jnp.stack(g_conv_w)


def kernel(x, rel_bias, norm_mix_g, w_in, q_norm_g, k_norm_g, sinks, conv_w, conv_b, conv_ln_g, conv_ln_b, attn_out_g, conv_out_g, w_out, norm_mlp_g, w_mlp_up, w_mlp_down, loss_target, m_rel_bias, m_norm_mix_g, m_w_in, m_q_norm_g, m_k_norm_g, m_sinks, m_conv_w, m_conv_b, m_conv_ln_g, m_conv_ln_b, m_attn_out_g, m_conv_out_g, m_w_out, m_norm_mlp_g, m_w_mlp_up, m_w_mlp_down, v_rel_bias, v_norm_mix_g, v_w_in, v_q_norm_g, v_k_norm_g, v_sinks, v_conv_w, v_conv_b, v_conv_ln_g, v_conv_ln_b, v_attn_out_g, v_conv_out_g, v_w_out, v_norm_mlp_g, v_w_mlp_up, v_w_mlp_down):
    args = dict(locals())
    small = {n: args[n] for n in SMALL_NAMES}
    tr = lambda a: jnp.swapaxes(a, 1, 2)
    me = 4 * lax.axis_index("x") + 2 * lax.axis_index("y") + lax.axis_index("c")

    shards = (tr(w_in).astype(BF16), w_out.astype(BF16), tr(w_mlp_up).astype(BF16), w_mlp_down.astype(BF16))
    cw_sh = jnp.pad(tr(conv_w), ((0, 0), (0, 0), (0, HALO - CONV_K))).reshape(DEPTH * (CONV_W // N_DEV), HALO)
    gathers = {}

    def start_gather(l, half, dep):
        arrays = [s[l] for s in shards[2 * half:2 * half + 2]]
        if (l, half) == (0, 0):
            arrays = [cw_sh] + arrays
        if dep is not None:
            arrays = [a + dep[0, 0].astype(a.dtype) for a in arrays]
        gathers[l, half], tok = _exchange_start(arrays, f"gather_{l}{'ab'[half]}_start", scatter=False)
        return tok

    landed = {}

    def get_w(l, half, after):
        if (l, half) not in landed:
            _, lands, tok = _exchange_wait(gathers[l, half], after, f"gather_{l}{'ab'[half]}_wait", scatter=False)
            if l + 1 < DEPTH:
                tok = tok + start_gather(l + 1, half, tok)
            landed[l, half] = (*lands, tok[0, 0])
        return landed[l, half]

    token = start_gather(0, 0, None)
    token = token + start_gather(0, 1, token)
    cw_all, *first = get_w(0, 0, token)
    landed[0, 0] = tuple(first)
    conv_w_full = jnp.transpose(cw_all.reshape(N_DEV, DEPTH, CONV_W // N_DEV, HALO), (1, 3, 0, 2))
    conv_w_full = conv_w_full.reshape(DEPTH, HALO, CONV_W)[:, :CONV_K, :]

    scatters = {}

    last = {}

    def put_g(l, half, grads):
        scatters[l, half], last["token"] = _exchange_start(list(grads), f"scatter_{l}{'ab'[half]}_start", scatter=True)
        return last["token"][0, 0]

    loss_part, dx, small_grads, g_conv_w = _local_step(
        x[0], loss_target[0], small, conv_w_full, get_w, put_g, token[0, 0])

    big_w = (tr(w_in), w_out, w_mlp_up, w_mlp_down)
    big_m = (tr(m_w_in), m_w_out, m_w_mlp_up, m_w_mlp_down)
    big_v = (tr(v_w_in), v_w_out, v_w_mlp_up, v_w_mlp_down)
    big_out = [None] * 4

    def update_half(half, after):
        parts = [[None] * DEPTH, [None] * DEPTH]
        for l in reversed(range(DEPTH)):
            _, lands, _ = _exchange_wait(scatters[l, half], after, f"scatter_{l}{'ab'[half]}_wait", scatter=True)
            for k, land in enumerate(lands):
                parts[k][l] = land.reshape(N_DEV, land.shape[0] // N_DEV, land.shape[1])
        for k in range(2):
            n = 2 * half + k
            big_out[n] = _adam_big(parts[k], big_w[n], big_m[n], big_v[n], transposed=(n == 2))

    update_half(1, last["token"])

    order = [n for n in SMALL_NAMES]
    packed = _pack([small_grads[n] for n in order] + [g_conv_w])
    slots, = _all_gather((packed[None],))
    summed = _sum_slots(slots[0].reshape(N_DEV, packed.shape[0], 128))
    shapes = [small[n].shape for n in order] + [(DEPTH, CONV_K, CONV_W)]
    sg = _unpack(summed, shapes)
    g_small = dict(zip(order, sg[:-1]))
    g_small["conv_w"] = lax.dynamic_slice_in_dim(sg[-1], me * (CONV_W // N_DEV), CONV_W // N_DEV, axis=2)
    names = order + ["conv_w"]
    deltas, new_m, new_v = _adam_small([args[n] for n in names], [g_small[n] for n in names],
                                       [args["m_" + n] for n in names], [args["v_" + n] for n in names])
    res = {"grad": g_small, "delta": dict(zip(names, deltas)), "new_m": dict(zip(names, new_m)),
           "new_v": dict(zip(names, new_v))}
    update_half(0, big_out[3][0][0, :8, :8] + deltas[0][:8, :8])
    big_out[0] = [tr(o) for o in big_out[0]]
    for k, n in enumerate(("w_in", "w_out", "w_mlp_up", "w_mlp_down")):
        for kind, val in zip(("grad", "delta", "new_m", "new_v"), big_out[k]):
            res[kind][n] = val

    loss = lax.psum(loss_part[0, 0], ("x", "y", "c"))
    weights = ("rel_bias", "norm_mix_g", "w_in", "q_norm_g", "k_norm_g", "sinks", "conv_w", "conv_b", "conv_ln_g",
               "conv_ln_b", "attn_out_g", "conv_out_g", "w_out", "norm_mlp_g", "w_mlp_up", "w_mlp_down")
    return (loss, dx[None], *[res[kind][n] for kind in ("grad", "delta", "new_m", "new_v") for n in weights])
```

```python
import math

import numpy as np
import jax
import jax.numpy as jnp
from jax import lax
from jax.experimental import pallas as pl
from jax.experimental.pallas import tpu as pltpu

F32, BF16 = jnp.float32, jnp.bfloat16
D_MODEL = 1024
DEPTH = 4
HEAD_DIM = 64
N_HEADS = 8
N_KV = 2
GQA = N_HEADS // N_KV
ATTN_W = N_HEADS * HEAD_DIM
KV_W = N_KV * HEAD_DIM
CONV_W = D_MODEL - ATTN_W
IN_W = ATTN_W + 2 * KV_W + 2 * CONV_W
BLOCK = 128
CONV_K = 31
HALO = 32
N_BUCKETS = 32
MAX_DIST = 128
D_FF = 4 * D_MODEL
FF_CHUNK = 512
N_FF = D_FF // FF_CHUNK
EPS = 1e-6
NEG = -1e30
N_DEV = 8
ADAM_LR, ADAM_B1, ADAM_B2, ADAM_EPS, ADAM_WD, ADAM_STEP = 0.001, 0.9, 0.999, 1e-08, 0.01, 10
VMEM_LIMIT = 56 * 1024 * 1024
MESH = pl.DeviceIdType.MESH

RESIDENT = pl.BlockSpec(memory_space=pltpu.VMEM)
IN_SMEM = pl.BlockSpec(memory_space=pltpu.SMEM)
ANY = pl.BlockSpec(memory_space=pl.ANY)


def _call(body, **kw):
    return pl.pallas_call(body, **kw)


def _params(*sem):
    return pltpu.CompilerParams(dimension_semantics=sem, vmem_limit_bytes=VMEM_LIMIT)


def _dot(a, b):
    return lax.dot_general(a, b, (((1,), (0,)), ((), ())), preferred_element_type=F32)


def _dot_nt(a, b):
    return lax.dot_general(a, b, (((1,), (1,)), ((), ())), preferred_element_type=F32)


def _dot_tn(a, b):
    return lax.dot_general(a, b, (((0,), (0,)), ((), ())), preferred_element_type=F32)


def _sig(x):
    return 1.0 / (1.0 + jnp.exp(-x))


def _rms(x):
    r = lax.rsqrt(jnp.mean(x * x, axis=-1, keepdims=True) + EPS)
    return x * r, r


def _rms_bwd(dy_g, xh, r):
    return r * (dy_g - xh * jnp.mean(dy_g * xh, axis=-1, keepdims=True))


def _rows(tm, w):
    return pl.BlockSpec((tm, w), lambda i: (i, 0))


def _acc_rows(w, rows=1):
    return pl.BlockSpec((rows, w), lambda i: (0, 0))


def _colsum(x):
    return jnp.sum(x, axis=0, keepdims=True)


def _fwd_in(x, g, wt):
    T = x.shape[0]
    tm = min(T, 1024)

    def body(x_ref, g_ref, w_ref, q_ref, kv_ref, u_ref, gt_ref):
        xh, _ = _rms(x_ref[...])
        h = (xh * g_ref[...]).astype(BF16)
        z = _dot_nt(h, w_ref[...])
        q_ref[...] = z[:, :ATTN_W]
        kv_ref[...] = z[:, ATTN_W:ATTN_W + 2 * KV_W]
        u_ref[...] = z[:, ATTN_W + 2 * KV_W:ATTN_W + 2 * KV_W + CONV_W]
        gt_ref[...] = z[:, ATTN_W + 2 * KV_W + CONV_W:]

    widths = (ATTN_W, 2 * KV_W, CONV_W, CONV_W)
    return _call(
        body, name="fwd_in", grid=(T // tm,),
        in_specs=[_rows(tm, D_MODEL), RESIDENT, RESIDENT],
        out_specs=[_rows(tm, w) for w in widths],
        out_shape=[jax.ShapeDtypeStruct((T, w), F32) for w in widths],
        compiler_params=_params("parallel"),
    )(x, g, wt)


ATTN_SUB = 16


def _rms0(x):
    r = lax.rsqrt(jnp.mean(x * x, axis=0, keepdims=True) + EPS)
    return x * r, r


def _rms0_bwd(dy_g, xh, r):
    return r * (dy_g - xh * jnp.mean(dy_g * xh, axis=0, keepdims=True))


def _tri_t():
    kj = lax.broadcasted_iota(jnp.int32, (BLOCK, BLOCK), 0)
    qi = lax.broadcasted_iota(jnp.int32, (BLOCK, BLOCK), 1)
    return jnp.concatenate([kj <= qi] * GQA, axis=1)


def _head_lanes(g):
    lane = lax.broadcasted_iota(jnp.int32, (1, GQA * BLOCK), 1)
    return (lane >= g * BLOCK) & (lane < (g + 1) * BLOCK)


def _heads_side_by_side(xt, kh):
    return jnp.concatenate(
        [xt[(kh * GQA + g) * HEAD_DIM:(kh * GQA + g + 1) * HEAD_DIM, :] for g in range(GQA)], axis=1)


def _kv_block(kv, kh, kg, kg_t):
    kvt = kv.T
    khat, rk = _rms(kv[:, kh * HEAD_DIM:(kh + 1) * HEAD_DIM])
    khat_t, _ = _rms0(kvt[kh * HEAD_DIM:(kh + 1) * HEAD_DIM, :])
    return dict(khat=khat, rk=rk, kn=(khat * kg).astype(BF16), kn_t=(khat_t * kg_t).astype(BF16),
                v=kv[:, KV_W + kh * HEAD_DIM:KV_W + (kh + 1) * HEAD_DIM].astype(BF16),
                v_t=kvt[KV_W + kh * HEAD_DIM:KV_W + (kh + 1) * HEAD_DIM, :].astype(BF16))


def _attn_group(zq_t, kp, kc, kh, bias_ref, sinks_ref, qg_t, tri, has_prev):
    scale = 1.0 / math.sqrt(HEAD_DIM)
    qhat, rq = _rms0(_heads_side_by_side(zq_t, kh))
    qf = qhat * qg_t
    qn = qf.astype(BF16)
    qs = (qf * scale).astype(BF16)
    s = jnp.where(tri, _dot(kc["kn"], qs), _dot(kp["kn"], qs)) + bias_ref[kh]
    if has_prev is not None:
        s = jnp.where(tri | has_prev, s, NEG)
    sink = jnp.zeros((1, GQA * BLOCK), F32)
    for g in range(GQA):
        sink = jnp.where(_head_lanes(g), sinks_ref[kh * GQA + g], sink)
    m = jnp.maximum(jnp.max(s, axis=0, keepdims=True), sink)
    p = jnp.exp(s - m)
    es = jnp.exp(sink - m)
    inv = 1.0 / (jnp.sum(p, axis=0, keepdims=True) + es)
    return dict(qhat=qhat, rq=rq, qn=qn, pn=p * inv, psink=es * inv)


def _split(pb, tri):
    zero = jnp.zeros_like(pb)
    return jnp.where(tri, pb, zero), jnp.where(tri, zero, pb)


def _attn_fwd(zq, zkv, bias, sinks, qg, kg, og):
    T = zq.shape[0]
    rows = ATTN_SUB * BLOCK

    def body(q_ref, kvc_ref, kvp_ref, bias_ref, sinks_ref, qgt_ref, kg_ref, kgt_ref, og_ref, a_ref, mix_ref):
        n = pl.program_id(0)
        tri = _tri_t()
        qgt, kgv, kgt = qgt_ref[...], kg_ref[...], kgt_ref[...]
        kvs = [kvp_ref[...]] + [kvc_ref[i * BLOCK:(i + 1) * BLOCK, :] for i in range(ATTN_SUB)]
        keys = [[_kv_block(kv, kh, kgv, kgt) for kh in range(N_KV)] for kv in kvs]
        for i in range(ATTN_SUB):
            zq_t = q_ref[i * BLOCK:(i + 1) * BLOCK, :].T
            has_prev = (n > 0) if i == 0 else None
            outs = []
            for kh in range(N_KV):
                kp, kc = keys[i][kh], keys[i + 1][kh]
                c = _attn_group(zq_t, kp, kc, kh, bias_ref, sinks_ref, qgt, tri, has_prev)
                p_c, p_p = _split(c["pn"].astype(BF16), tri)
                o_t = _dot(kc["v_t"], p_c) + _dot(kp["v_t"], p_p)
                outs += [o_t[:, g * BLOCK:(g + 1) * BLOCK] for g in range(GQA)]
            a = jnp.concatenate(outs, axis=0).T
            a_ref[i * BLOCK:(i + 1) * BLOCK, :] = a
            ah, _ = _rms(a)
            mix_ref[i * BLOCK:(i + 1) * BLOCK, :] = (ah * og_ref[...]).astype(BF16)

    return _call(
        body, name="attn_fwd", grid=(T // rows,),
        in_specs=[_rows(rows, ATTN_W), _rows(rows, 2 * KV_W),
                  pl.BlockSpec((BLOCK, 2 * KV_W), lambda n: (jnp.maximum(n * ATTN_SUB - 1, 0), 0)),
                  RESIDENT, IN_SMEM, RESIDENT, RESIDENT, RESIDENT, RESIDENT],
        out_specs=[_rows(rows, ATTN_W), _rows(rows, ATTN_W)],
        out_shape=[jax.ShapeDtypeStruct((T, ATTN_W), F32), jax.ShapeDtypeStruct((T, ATTN_W), BF16)],
        compiler_params=_params("parallel"),
    )(zq, zkv, zkv, bias, sinks, qg.reshape(HEAD_DIM, 1), kg, kg.reshape(HEAD_DIM, 1), og)


def _conv_post(y, lg, lb, og):
    mu = jnp.mean(y, axis=-1, keepdims=True)
    yc = y - mu
    rstd = lax.rsqrt(jnp.mean(yc * yc, axis=-1, keepdims=True) + EPS)
    yn = yc * rstd
    ln = yn * lg + lb
    sg = _sig(ln)
    c = ln * sg
    ch, r = _rms(c)
    return yn, rstd, ln, sg, ch, r


CONV_CHUNK = 64


SLAB_ROWS = CONV_CHUNK + 8 * ((CONV_K - 1) // 8)


def _shifted_taps(buf, slab, r0, base):
    for b in range(8):
        taps = range(b, CONV_K, 8)
        span = CONV_CHUNK + 8 * (len(taps) - 1)
        slab[0:span, :] = buf[r0 + base + b:r0 + base + b + span, :]
        for a, j in enumerate(taps):
            yield j, slab[8 * a:8 * a + CONV_CHUNK, :]


def _fold8(x):
    return jnp.sum(x.reshape(x.shape[0] // 8, 8, x.shape[1]), axis=0)


def _conv_fwd(zu, zg, cw, cb, lg, lb, og):
    T = zu.shape[0]
    tt = 512
    halo_spec = pl.BlockSpec((HALO, CONV_W), lambda i: (jnp.maximum(i * (tt // HALO) - 1, 0), 0))

    def body(u_ref, g_ref, uh_ref, gh_ref, cw_ref, cb_ref, lg_ref, lb_ref, og_ref, y_ref, mix_ref, buf, slab):
        i = pl.program_id(0)
        hal = uh_ref[...] * _sig(gh_ref[...])
        buf[0:HALO, :] = jnp.where(i > 0, hal, 0.0)
        buf[HALO:HALO + tt, :] = u_ref[...] * _sig(g_ref[...])
        cbv, lgv, lbv, ogv = cb_ref[...], lg_ref[...], lb_ref[...], og_ref[...]
        for r0 in range(0, tt, CONV_CHUNK):
            rs = slice(r0, r0 + CONV_CHUNK)
            acc = jnp.zeros((CONV_CHUNK, CONV_W), F32)
            for j, win in _shifted_taps(buf, slab, r0, HALO - (CONV_K - 1)):
                acc = acc + cw_ref[j:j + 1, :] * win
            y = acc + cbv
            y_ref[rs, :] = y
            mix_ref[rs, :] = (_conv_post(y, lgv, lbv, ogv)[4] * ogv).astype(BF16)

    return _call(
        body, name="conv_fwd", grid=(T // tt,),
        in_specs=[_rows(tt, CONV_W), _rows(tt, CONV_W), halo_spec, halo_spec] + [RESIDENT] * 5,
        out_specs=[_rows(tt, CONV_W), _rows(tt, CONV_W)],
        out_shape=[jax.ShapeDtypeStruct((T, CONV_W), F32), jax.ShapeDtypeStruct((T, CONV_W), BF16)],
        scratch_shapes=[pltpu.VMEM((HALO + tt, CONV_W), F32), pltpu.VMEM((SLAB_ROWS, CONV_W), F32)],
        compiler_params=_params("parallel"),
    )(zu, zg, zu, zg, cw, cb, lg, lb, og)


def _chunked(tm):
    return pl.BlockSpec((N_FF, tm, FF_CHUNK), lambda i: (0, i, 0))


def _fwd_out(x, mix_a, mix_c, w_out):
    T = x.shape[0]
    tm = min(T, 1024)

    def body(x_ref, a_ref, c_ref, w_ref, o_ref):
        o_ref[...] = (x_ref[...] + _dot(a_ref[...], w_ref[0:ATTN_W, :])
                      + _dot(c_ref[...], w_ref[ATTN_W:, :]))

    return _call(
        body, name="fwd_out", grid=(T // tm,),
        in_specs=[_rows(tm, D_MODEL), _rows(tm, ATTN_W), _rows(tm, CONV_W), RESIDENT],
        out_specs=_rows(tm, D_MODEL),
        out_shape=jax.ShapeDtypeStruct((T, D_MODEL), F32),
        compiler_params=_params("parallel"),
    )(x, mix_a, mix_c, w_out)


def _mlp_fwd(x, g, wup_t, wdown):
    T = x.shape[0]
    tm = 512

    def body(x_ref, g_ref, wu_ref, wd_ref, o_ref, up_ref):
        xv = x_ref[...]
        xh, _ = _rms(xv)
        h = (xh * g_ref[...]).astype(BF16)
        acc = xv
        for c in range(N_FF):
            rows = slice(c * FF_CHUNK, (c + 1) * FF_CHUNK)
            up = _dot_nt(h, wu_ref[rows, :])
            up_ref[c] = up.astype(BF16)
            act = jnp.square(jnp.maximum(up, 0.0))
            acc = acc + _dot(act.astype(BF16), wd_ref[rows, :])
        o_ref[...] = acc

    return _call(
        body, name="mlp_fwd", grid=(T // tm,),
        in_specs=[_rows(tm, D_MODEL), RESIDENT, RESIDENT, RESIDENT],
        out_specs=[_rows(tm, D_MODEL), _chunked(tm)],
        out_shape=[jax.ShapeDtypeStruct((T, D_MODEL), F32), jax.ShapeDtypeStruct((N_FF, T, FF_CHUNK), BF16)],
        compiler_params=_params("parallel"),
    )(x, g, wup_t, wdown)


def _loss_head(y, target):
    T = y.shape[0]
    tm = 512

    def body(y_ref, t_ref, l_ref, d_ref):
        @pl.when(pl.program_id(0) == 0)
        def _():
            l_ref[...] = jnp.zeros_like(l_ref)
        e = y_ref[...] - t_ref[...]
        d_ref[...] = e / D_MODEL
        l_ref[...] += 0.5 * jnp.sum(jnp.mean(e * e, axis=-1, keepdims=True))

    return _call(
        body, name="loss_head", grid=(T // tm,),
        in_specs=[_rows(tm, D_MODEL), _rows(tm, D_MODEL)],
        out_specs=[_acc_rows(128, 8), _rows(tm, D_MODEL)],
        out_shape=[jax.ShapeDtypeStruct((8, 128), F32), jax.ShapeDtypeStruct((T, D_MODEL), F32)],
        compiler_params=_params("arbitrary"),
    )(y, target)


def _bias_table(rel_bias, bucket):
    def body(rb_ref, bk_ref, o_ref):
        bk = bk_ref[...]
        for h in range(N_HEADS):
            acc = jnp.zeros((BLOCK, BLOCK), F32)
            for b in range(N_BUCKETS):
                acc = jnp.where(bk == b, rb_ref[b, h], acc)
            o_ref[h // GQA, :, (h % GQA) * BLOCK:(h % GQA + 1) * BLOCK] = acc

    return _call(
        body, name="bias_table", in_specs=[IN_SMEM, RESIDENT], out_specs=RESIDENT,
        out_shape=jax.ShapeDtypeStruct((N_KV, BLOCK, GQA * BLOCK), F32),
    )(rel_bias, bucket)


def _mlp_bwd(dx2, x1, up, g, wup_t, wdown):
    T = x1.shape[0]
    tm = 512

    def body(d_ref, x_ref, up_ref, g_ref, wu_ref, wd_ref, dx_ref, dup_ref, h_ref, dg_ref, db_ref, d1b_ref):
        @pl.when(pl.program_id(0) == 0)
        def _():
            dg_ref[...] = jnp.zeros_like(dg_ref)
        d2 = d_ref[...]
        d2b = d2.astype(BF16)
        db_ref[...] = d2b
        xh, r = _rms(x_ref[...])
        gv = g_ref[...]
        h_ref[...] = (xh * gv).astype(BF16)
        dh = jnp.zeros((tm, D_MODEL), F32)
        for c in range(N_FF):
            rows = slice(c * FF_CHUNK, (c + 1) * FF_CHUNK)
            dact = _dot_nt(d2b, wd_ref[rows, :])
            dup = (dact * (2.0 * jnp.maximum(up_ref[c].astype(F32), 0.0))).astype(BF16)
            dup_ref[c] = dup
            dh = dh + _dot(dup, wu_ref[rows, :])
        dg_ref[...] += _colsum(dh * xh)
        d1 = d2 + _rms_bwd(dh * gv, xh, r)
        dx_ref[...] = d1
        d1b_ref[...] = d1.astype(BF16)

    return _call(
        body, name="mlp_bwd", grid=(T // tm,),
        in_specs=[_rows(tm, D_MODEL), _rows(tm, D_MODEL), _chunked(tm), RESIDENT, RESIDENT, RESIDENT],
        out_specs=[_rows(tm, D_MODEL), _chunked(tm), _rows(tm, D_MODEL), _acc_rows(D_MODEL), _rows(tm, D_MODEL),
                   _rows(tm, D_MODEL)],
        out_shape=[jax.ShapeDtypeStruct((T, D_MODEL), F32), jax.ShapeDtypeStruct((N_FF, T, FF_CHUNK), BF16),
                   jax.ShapeDtypeStruct((T, D_MODEL), BF16), jax.ShapeDtypeStruct((1, D_MODEL), F32),
                   jax.ShapeDtypeStruct((T, D_MODEL), BF16), jax.ShapeDtypeStruct((T, D_MODEL), BF16)],
        compiler_params=_params("arbitrary"),
    )(dx2, x1, up, g, wup_t, wdown)


def _bwd_out(dx1b, w_out):
    T = dx1b.shape[0]
    tm = min(T, 1024)

    def body(d_ref, w_ref, da_ref, dc_ref):
        dm = _dot_nt(d_ref[...], w_ref[...])
        da_ref[...] = dm[:, :ATTN_W]
        dc_ref[...] = dm[:, ATTN_W:]

    return _call(
        body, name="bwd_out", grid=(T // tm,),
        in_specs=[_rows(tm, D_MODEL), RESIDENT],
        out_specs=[_rows(tm, ATTN_W), _rows(tm, CONV_W)],
        out_shape=[jax.ShapeDtypeStruct((T, ATTN_W), F32), jax.ShapeDtypeStruct((T, CONV_W), F32)],
        compiler_params=_params("parallel"),
    )(dx1b, w_out)


def _wgrad_one_block(a, b, name):
    T, tr = a.shape
    tk = min(T, 4096)
    nk = T // tk

    def body(a_ref, b_ref, o_ref, acc):
        k = pl.program_id(0)

        @pl.when(k == 0)
        def _():
            acc[...] = jnp.zeros_like(acc)
        acc[...] += _dot_tn(a_ref[...], b_ref[...])

        @pl.when(k == nk - 1)
        def _():
            o_ref[...] = acc[...].astype(BF16)

    return _call(
        body, name=name, grid=(nk,),
        in_specs=[_rows(tk, tr), _rows(tk, D_MODEL)],
        out_specs=pl.BlockSpec((tr, D_MODEL), lambda k: (0, 0)),
        out_shape=jax.ShapeDtypeStruct((tr, D_MODEL), BF16),
        scratch_shapes=[pltpu.VMEM((tr, D_MODEL), F32)],
        compiler_params=_params("arbitrary"),
    )(a, b)


def _wgrad(a, b, name, chunked=False, square_relu=False, tr=FF_CHUNK):
    if chunked:
        nr, T, tr = a.shape
    else:
        T, R = a.shape
        tr = min(R, tr)
        nr = R // tr
    piece = min(T, 2048)
    if nr == 1:
        return _wgrad_one_block(a, b, name)

    def body(a_ref, b_ref, o_ref):
        if square_relu:
            tot = jnp.zeros((tr, D_MODEL), F32)
            for k0 in range(0, T, piece):
                av = jnp.square(jnp.maximum(a_ref[k0:k0 + piece, :].astype(F32), 0.0)).astype(BF16)
                tot = tot + _dot_tn(av, b_ref[k0:k0 + piece, :])
        else:
            tot = _dot_tn(a_ref[...], b_ref[...])
        o_ref[...] = tot.astype(BF16)

    a_spec = (pl.BlockSpec((None, T, tr), lambda r: (r, 0, 0)) if chunked else pl.BlockSpec((T, tr), lambda r: (0, r)))
    return _call(
        body, name=name, grid=(nr,),
        in_specs=[a_spec, RESIDENT],
        out_specs=pl.BlockSpec((tr, D_MODEL), lambda r: (r, 0)),
        out_shape=jax.ShapeDtypeStruct((nr * tr, D_MODEL), BF16),
        compiler_params=_params("parallel"),
    )(a, b)


def _conv_bwd(dmix, y, zu, zg, cw, lg, lb, og):
    T = y.shape[0]
    tt = 512
    nt = T // tt
    per = tt // HALO
    prev_spec = pl.BlockSpec((HALO, CONV_W), lambda i: (jnp.maximum(i * per - 1, 0), 0))
    next_spec = pl.BlockSpec((HALO, CONV_W), lambda i: (jnp.minimum((i + 1) * per, nt * per - 1), 0))

    def body(dm_ref, dmn_ref, y_ref, yn_ref, u_ref, g_ref, uh_ref, gh_ref, cw_ref, lg_ref, lb_ref, og_ref,
             du_ref, dg_ref, pg_ref, hbuf, dybuf, dwacc, pacc, slab):
        i = pl.program_id(0)

        @pl.when(i == 0)
        def _():
            dwacc[...] = jnp.zeros_like(dwacc)
            pacc[...] = jnp.zeros_like(pacc)
        lgv, lbv, ogv = lg_ref[...], lb_ref[...], og_ref[...]

        def chain(yv, dm):
            yn, rstd, ln, sg, ch, r = _conv_post(yv, lgv, lbv, ogv)
            dc = _rms_bwd(dm * ogv, ch, r)
            dln = dc * sg * (1.0 + ln * (1.0 - sg))
            dyn = dln * lgv
            dy = rstd * (dyn - jnp.mean(dyn, axis=-1, keepdims=True)
                         - yn * jnp.mean(dyn * yn, axis=-1, keepdims=True))
            return dy, dm * ch, dln * yn, dln

        hbuf[0:HALO, :] = jnp.where(i > 0, uh_ref[...] * _sig(gh_ref[...]), 0.0)
        for r0 in range(0, tt, CONV_CHUNK):
            rs = slice(r0, r0 + CONV_CHUNK)
            hbuf[HALO + r0:HALO + r0 + CONV_CHUNK, :] = u_ref[rs, :] * _sig(g_ref[rs, :])
            dy, p_og, p_lg, p_lb = chain(y_ref[rs, :], dm_ref[rs, :])
            dybuf[rs, :] = dy
            for k, part in enumerate((dy, p_lg, p_lb, p_og)):
                pacc[8 * k:8 * k + 8, :] += _fold8(part)
        dyh, _, _, _ = chain(yn_ref[...], dmn_ref[...])
        dybuf[tt:tt + HALO, :] = jnp.where(i < nt - 1, dyh, 0.0)
        for r0 in range(0, tt, CONV_CHUNK):
            rs = slice(r0, r0 + CONV_CHUNK)
            dy = dybuf[rs, :]
            dh = jnp.zeros((CONV_CHUNK, CONV_W), F32)
            for j, win in _shifted_taps(dybuf, slab, r0, 0):
                dh = dh + cw_ref[CONV_K - 1 - j:CONV_K - j, :] * win
            for j, win in _shifted_taps(hbuf, slab, r0, HALO - (CONV_K - 1)):
                dwacc[8 * j:8 * j + 8, :] += _fold8(dy * win)
            sgt = _sig(g_ref[rs, :])
            du_ref[rs, :] = (dh * sgt).astype(BF16)
            dg_ref[rs, :] = (dh * u_ref[rs, :] * sgt * (1.0 - sgt)).astype(BF16)

        @pl.when(i == nt - 1)
        def _():
            pg_ref[...] = jnp.zeros_like(pg_ref)
            for j in range(CONV_K):
                pg_ref[j:j + 1, :] = _colsum(dwacc[8 * j:8 * j + 8, :])
            for k in range(4):
                pg_ref[32 + k:33 + k, :] = _colsum(pacc[8 * k:8 * k + 8, :])

    return _call(
        body, name="conv_bwd", grid=(nt,),
        in_specs=[_rows(tt, CONV_W), next_spec, _rows(tt, CONV_W), next_spec, _rows(tt, CONV_W), _rows(tt, CONV_W),
                  prev_spec, prev_spec] + [RESIDENT] * 4,
        out_specs=[_rows(tt, CONV_W), _rows(tt, CONV_W), _acc_rows(CONV_W, 40)],
        out_shape=[jax.ShapeDtypeStruct((T, CONV_W), BF16), jax.ShapeDtypeStruct((T, CONV_W), BF16),
                   jax.ShapeDtypeStruct((40, CONV_W), F32)],
        scratch_shapes=[pltpu.VMEM((HALO + tt, CONV_W), F32), pltpu.VMEM((tt + HALO, CONV_W), F32),
                        pltpu.VMEM((8 * HALO, CONV_W), F32), pltpu.VMEM((32, CONV_W), F32),
                        pltpu.VMEM((SLAB_ROWS, CONV_W), F32)],
        compiler_params=_params("arbitrary"),
    )(dmix, dmix, y, y, zu, zg, zu, zg, cw, lg, lb, og)


def _attn_bwd(dmix, a, zq, zkv, bias, sinks, qg, kg, og):
    T = zq.shape[0]
    rows = ATTN_SUB * BLOCK
    ns = T // rows
    nb = T // BLOCK
    cur = lambda w: pl.BlockSpec((rows, w), lambda n: (jnp.minimum(n, ns - 1), 0))
    scale = 1.0 / math.sqrt(HEAD_DIM)
    done = rows - BLOCK

    def body(dm_ref, a_ref, q_ref, kvc_ref, kvp_ref, bias_ref, sinks_ref, qgt_ref, kg_ref, kgt_ref, og_ref,
             dq_ref, dkv_ref, db_ref, dog_ref, dqg_ref, dkg_ref, dsk_ref, carry):
        n = pl.program_id(0)

        @pl.when(n == 0)
        def _():
            for ref in (db_ref, dog_ref, dqg_ref, dkg_ref, dsk_ref, carry):
                ref[...] = jnp.zeros_like(ref)

        @pl.when(n < ns)
        def _():
            tri = _tri_t()
            ogv, qgt, kgv, kgt = og_ref[...], qgt_ref[...], kg_ref[...], kgt_ref[...]
            lane = lax.broadcasted_iota(jnp.int32, (1, 128), 1)
            kvs = [kvp_ref[...]] + [kvc_ref[i * BLOCK:(i + 1) * BLOCK, :] for i in range(ATTN_SUB)]
            keys = [[_kv_block(kv, kh, kgv, kgt) for kh in range(N_KV)] for kv in kvs]
            dkn = [[jnp.zeros((BLOCK, HEAD_DIM), F32)] * N_KV for _ in kvs]
            dv = [[jnp.zeros((BLOCK, HEAD_DIM), F32)] * N_KV for _ in kvs]
            dsk = jnp.zeros((1, 128), F32)
            for i in range(ATTN_SUB):
                blk = slice(i * BLOCK, (i + 1) * BLOCK)
                ah, ra = _rms(a_ref[blk, :])
                dm = dm_ref[blk, :]
                dog_ref[...] += _colsum(dm * ah)
                da_t = _rms_bwd(dm * ogv, ah, ra).T
                zq_t = q_ref[blk, :].T
                has_prev = (n > 0) if i == 0 else None
                dqs = []
                for kh in range(N_KV):
                    kp, kc = keys[i][kh], keys[i + 1][kh]
                    c = _attn_group(zq_t, kp, kc, kh, bias_ref, sinks_ref, qgt, tri, has_prev)
                    dob = _heads_side_by_side(da_t, kh).astype(BF16)
                    pn = c["pn"]
                    p_c, p_p = _split(pn.astype(BF16), tri)
                    dv[i + 1][kh] = dv[i + 1][kh] + _dot_nt(p_c, dob)
                    dv[i][kh] = dv[i][kh] + _dot_nt(p_p, dob)
                    dp = jnp.where(tri, _dot(kc["v"], dob), _dot(kp["v"], dob))
                    dl = jnp.sum(pn * dp, axis=0, keepdims=True)
                    ds = pn * (dp - dl)
                    dsr = -c["psink"] * dl
                    for g in range(GQA):
                        dsk = dsk + jnp.where(lane == kh * GQA + g, jnp.sum(jnp.where(_head_lanes(g), dsr, 0.0)), 0.0)
                    db_ref[kh] += ds
                    ds_c, ds_p = _split((ds * scale).astype(BF16), tri)
                    dqn = _dot(kc["kn_t"], ds_c) + _dot(kp["kn_t"], ds_p)
                    dkn[i + 1][kh] = dkn[i + 1][kh] + _dot_nt(ds_c, c["qn"])
                    dkn[i][kh] = dkn[i][kh] + _dot_nt(ds_p, c["qn"])
                    dqg_ref[...] += jnp.sum(dqn * c["qhat"], axis=1, keepdims=True)
                    dq_t = _rms0_bwd(dqn * qgt, c["qhat"], c["rq"])
                    dqs += [dq_t[:, g * BLOCK:(g + 1) * BLOCK] for g in range(GQA)]
                dq_ref[blk, :] = jnp.concatenate(dqs, axis=0).T.astype(BF16)
            dsk_ref[...] += dsk
            dkv = []
            for j in range(ATTN_SUB + 1):
                dk = []
                for kh in range(N_KV):
                    key = keys[j][kh]
                    dkg_ref[...] += _colsum(dkn[j][kh] * key["khat"])
                    dk.append(_rms_bwd(dkn[j][kh] * kgv, key["khat"], key["rk"]))
                dkv.append(jnp.concatenate(dk + dv[j], axis=-1))
            if done:
                dkv_ref[0:done, :] = carry[0:done, :].astype(BF16)
            dkv_ref[done:rows, :] = (carry[done:rows, :] + dkv[0]).astype(BF16)
            for j in range(1, ATTN_SUB + 1):
                carry[(j - 1) * BLOCK:j * BLOCK, :] = dkv[j]

        @pl.when(n == ns)
        def _():
            dkv_ref[...] = carry[...].astype(BF16)

    small = lambda w: pl.BlockSpec((1, w), lambda n: (0, 0))
    return _call(
        body, name="attn_bwd", grid=(ns + 1,),
        in_specs=[cur(ATTN_W), cur(ATTN_W), cur(ATTN_W), cur(2 * KV_W),
                  pl.BlockSpec((BLOCK, 2 * KV_W), lambda n: (jnp.clip(n * ATTN_SUB - 1, 0, nb - 1), 0)),
                  RESIDENT, IN_SMEM, RESIDENT, RESIDENT, RESIDENT, RESIDENT],
        out_specs=[cur(ATTN_W), pl.BlockSpec((rows, 2 * KV_W), lambda n: (jnp.maximum(n - 1, 0), 0)),
                   pl.BlockSpec((N_KV, BLOCK, GQA * BLOCK), lambda n: (0, 0, 0)),
                   small(ATTN_W), pl.BlockSpec((HEAD_DIM, 1), lambda n: (0, 0)), small(HEAD_DIM), small(128)],
        out_shape=[jax.ShapeDtypeStruct((T, ATTN_W), BF16), jax.ShapeDtypeStruct((T, 2 * KV_W), BF16),
                   jax.ShapeDtypeStruct((N_KV, BLOCK, GQA * BLOCK), F32),
                   jax.ShapeDtypeStruct((1, ATTN_W), F32), jax.ShapeDtypeStruct((HEAD_DIM, 1), F32),
                   jax.ShapeDtypeStruct((1, HEAD_DIM), F32), jax.ShapeDtypeStruct((1, 128), F32)],
        scratch_shapes=[pltpu.VMEM((rows, 2 * KV_W), F32)],
        compiler_params=_params("arbitrary"),
    )(dmix, a, zq, zkv, zkv, bias, sinks, qg.reshape(HEAD_DIM, 1), kg, kg.reshape(HEAD_DIM, 1), og)


def _bwd_in(dx1, x, dq, dkv, du, dgt, g, wt):
    T = x.shape[0]
    tm = 512

    def body(d1_ref, x_ref, dq_ref, dkv_ref, du_ref, dgt_ref, g_ref, w_ref, dx_ref, h_ref, dg_ref, dz_ref):
        @pl.when(pl.program_id(0) == 0)
        def _():
            dg_ref[...] = jnp.zeros_like(dg_ref)
        xh, r = _rms(x_ref[...])
        gv = g_ref[...]
        h_ref[...] = (xh * gv).astype(BF16)
        dz = jnp.concatenate([dq_ref[...], dkv_ref[...], du_ref[...], dgt_ref[...]], axis=-1)
        dz_ref[...] = dz
        dh = _dot(dz, w_ref[...])
        dg_ref[...] += _colsum(dh * xh)
        dx_ref[...] = d1_ref[...] + _rms_bwd(dh * gv, xh, r)

    return _call(
        body, name="bwd_in", grid=(T // tm,),
        in_specs=[_rows(tm, D_MODEL), _rows(tm, D_MODEL), _rows(tm, ATTN_W), _rows(tm, 2 * KV_W),
                  _rows(tm, CONV_W), _rows(tm, CONV_W), RESIDENT, RESIDENT],
        out_specs=[_rows(tm, D_MODEL), _rows(tm, D_MODEL), _acc_rows(D_MODEL), _rows(tm, IN_W)],
        out_shape=[jax.ShapeDtypeStruct((T, D_MODEL), F32), jax.ShapeDtypeStruct((T, D_MODEL), BF16),
                   jax.ShapeDtypeStruct((1, D_MODEL), F32), jax.ShapeDtypeStruct((T, IN_W), BF16)],
        compiler_params=_params("arbitrary"),
    )(dx1, x, dq, dkv, du, dgt, g, wt)


def _bias_grad(db, bucket):
    def body(db_ref, bk_ref, o_ref):
        bk = bk_ref[...]
        lane = lax.broadcasted_iota(jnp.int32, (1, 128), 1)
        for h in range(N_HEADS):
            cols = slice((h % GQA) * BLOCK, (h % GQA + 1) * BLOCK)
            tot = db_ref[0, h // GQA, :, cols]
            for l in range(1, DEPTH):
                tot = tot + db_ref[l, h // GQA, :, cols]
            out = jnp.zeros((1, 128), F32)
            for b in range(N_BUCKETS):
                out = jnp.where(lane == b, jnp.sum(jnp.where(bk == b, tot, 0.0)), out)
            o_ref[h:h + 1, :] = out

    return _call(
        body, name="bias_grad", in_specs=[RESIDENT, RESIDENT], out_specs=RESIDENT,
        out_shape=jax.ShapeDtypeStruct((N_HEADS, 128), F32),
        compiler_params=pltpu.CompilerParams(vmem_limit_bytes=VMEM_LIMIT),
    )(db, bucket)


def _place():
    return lax.axis_index("x"), lax.axis_index("y"), lax.axis_index("c")


def _all_gather(shards):
    na = len(shards)

    def body(*refs):
        ins, outs = refs[:na], refs[na:2 * na]
        send_sems, recv_sems, local_sems = refs[2 * na:]
        x, y, c = _place()
        me, sibling = (x, y, c), (x, y, 1 - c)
        chips = [(1 - x, y), (x, 1 - y), (1 - x, 1 - y)]

        def rows(a, p):
            r = ins[a].shape[1]
            return outs[a].at[:, pl.ds((4 * p[0] + 2 * p[1] + p[2]) * r, r), :]

        def copy(a, k, block, to, src=None):
            return pltpu.make_async_remote_copy(
                src_ref=rows(a, block) if src is None else src, dst_ref=rows(a, block),
                send_sem=send_sems.at[a, k], recv_sem=recv_sems.at[a, k], device_id=to, device_id_type=MESH)

        mine = [pltpu.make_async_copy(ins[a], rows(a, me), local_sems.at[a]) for a in range(na)]
        for cp in mine:
            cp.start()
        first = []
        for a in range(na):
            first.append(copy(a, 0, me, sibling, src=ins[a]))
            first += [copy(a, 1 + j, me, (*chip, c), src=ins[a]) for j, chip in enumerate(chips)]
        for cp in first:
            cp.start()
        passed = []
        for j, chip in enumerate(chips):
            for a in range(na):
                copy(a, 1 + j, (*chip, c), me).wait_recv()
                cp = copy(a, 4 + j, (*chip, c), sibling)
                cp.start()
                passed.append(cp)
        for a in range(na):
            copy(a, 0, sibling, me).wait_recv()
            for j, chip in enumerate(chips):
                copy(a, 4 + j, (*chip, 1 - c), me).wait_recv()
        for cp in first + passed:
            cp.wait_send()
        for cp in mine:
            cp.wait()

    return _call(
        body, name="all_gather",
        in_specs=[ANY] * na, out_specs=[ANY] * na,
        out_shape=[jax.ShapeDtypeStruct((s.shape[0], N_DEV * s.shape[1], s.shape[2]), s.dtype) for s in shards],
        scratch_shapes=[pltpu.SemaphoreType.DMA((na, 7)), pltpu.SemaphoreType.DMA((na, 7)),
                        pltpu.SemaphoreType.DMA((na,))],
    )(*shards)


IN_HBM = pl.BlockSpec(memory_space=pltpu.HBM)
IN_SEM = pl.BlockSpec(memory_space=pltpu.SEMAPHORE)
DATAFLOW = pltpu.SideEffectType.DATAFLOW_SIDE_EFFECTING


def _exchange_copies(srcs, lands, send_sems, recv_sems, scatter):
    x, y, c = _place()
    me = 4 * x + 2 * y + c
    copies = []
    for k in range(1, N_DEV):
        p = (x ^ (k >> 2), y ^ ((k >> 1) & 1), c ^ (k & 1))
        for a, (src, land) in enumerate(zip(srcs, lands)):
            r = land.shape[0] // N_DEV
            if scatter:
                src = src.at[pl.ds((4 * p[0] + 2 * p[1] + p[2]) * r, r), :]
            copies.append(pltpu.make_async_remote_copy(
                src_ref=src, dst_ref=land.at[pl.ds(me * r, r), :], send_sem=send_sems.at[a * (N_DEV - 1) + k - 1],
                recv_sem=recv_sems.at[a * (N_DEV - 1) + k - 1], device_id=p, device_id_type=MESH))
    return copies


def _own_copies(srcs, lands, send_sems, scatter):
    x, y, c = _place()
    me = 4 * x + 2 * y + c
    copies = []
    for a, (src, land) in enumerate(zip(srcs, lands)):
        r = land.shape[0] // N_DEV
        if scatter:
            src = src.at[pl.ds(me * r, r), :]
        copies.append(pltpu.make_async_copy(src, land.at[pl.ds(me * r, r), :],
                                            send_sems.at[len(srcs) * (N_DEV - 1) + a]))
    return copies


def _exchange_start(srcs, name, scatter):
    na = len(srcs)
    lands = [lax.empty((s.shape[0] * (1 if scatter else N_DEV), s.shape[1]), s.dtype) for s in srcs]

    def body(*refs):
        ins, lnd = refs[:na], refs[na:2 * na]
        send_sems, recv_sems = refs[2 * na], refs[2 * na + 1]
        token = refs[-1]
        for cp in _exchange_copies(ins, lnd, send_sems, recv_sems, scatter) + _own_copies(ins, lnd, send_sems, scatter):
            cp.start()
        token[...] = jnp.zeros_like(token)

    hbm = lambda a: pltpu.with_memory_space_constraint(a, pltpu.HBM)
    out = _call(
        body, name=name,
        out_shape=(pltpu.SemaphoreType.DMA((na * N_DEV,)), pltpu.SemaphoreType.DMA((na * (N_DEV - 1),)),
                   *[pltpu.HBM(a.shape, a.dtype) for a in (*srcs, *lands)], jax.ShapeDtypeStruct((8, 128), F32)),
        in_specs=[IN_HBM] * (2 * na),
        out_specs=(IN_SEM, IN_SEM, *[IN_HBM] * (2 * na), RESIDENT),
        input_output_aliases={i: 2 + i for i in range(2 * na)},
        compiler_params=pltpu.CompilerParams(has_side_effects=DATAFLOW),
    )(*[hbm(a) for a in (*srcs, *lands)])
    return (out[0], out[1], out[2:2 + na], out[2 + na:2 + 2 * na]), out[-1]


def _exchange_wait(state, after, name, scatter):
    send_sems, recv_sems, srcs, lands = state
    na = len(srcs)

    def body(*refs):
        ins, lnd = refs[:na], refs[na:2 * na]
        for cp in _exchange_copies(ins, lnd, refs[2 * na], refs[2 * na + 1], scatter):
            cp.wait_send()
            cp.wait_recv()
        for cp in _own_copies(ins, lnd, refs[2 * na], scatter):
            cp.wait()
        refs[-1][...] = jnp.zeros_like(refs[-1])

    out = _call(
        body, name=name,
        out_shape=(*[pltpu.HBM(a.shape, a.dtype) for a in (*srcs, *lands)], jax.ShapeDtypeStruct((8, 128), F32)),
        in_specs=[IN_HBM] * (2 * na) + [IN_SEM, IN_SEM, ANY],
        out_specs=(*[IN_HBM] * (2 * na), RESIDENT),
        input_output_aliases={i: i for i in range(2 * na)},
        compiler_params=pltpu.CompilerParams(has_side_effects=DATAFLOW),
    )(*srcs, *lands, send_sems, recv_sems, after)
    return out[:na], out[na:2 * na], out[-1]


def _adam_math(w, g, m, v):
    m = ADAM_B1 * m + (1.0 - ADAM_B1) * g
    v = ADAM_B2 * v + (1.0 - ADAM_B2) * jnp.square(g)
    m_hat = m / (1.0 - ADAM_B1 ** ADAM_STEP)
    v_hat = v / (1.0 - ADAM_B2 ** ADAM_STEP)
    delta = -ADAM_LR * (m_hat / (jnp.sqrt(v_hat) + ADAM_EPS) + ADAM_WD * w)
    return delta, m, v


def _adam_big(parts, w, m, v, transposed=False):
    if transposed:
        nl, cdim, r = w.shape
    else:
        nl, r, cdim = w.shape
    tr = r if r <= 256 else 256
    ni = r // tr
    spec = (pl.BlockSpec((None, cdim, tr), lambda l, i: (l, 0, i)) if transposed
            else pl.BlockSpec((None, tr, cdim), lambda l, i: (l, i, 0)))

    def part_spec(layer):
        return pl.BlockSpec((N_DEV, tr, cdim),
                            lambda l, i: (0, jnp.where(l == layer, i, jnp.where(l < layer, 0, ni - 1)), 0))

    def body(*refs):
        p_refs = refs[:nl]
        w_ref, m_ref, v_ref, g_ref, d_ref, nm_ref, nv_ref = refs[nl:]
        for layer in range(nl):
            @pl.when(pl.program_id(0) == layer)
            def _(p_ref=p_refs[layer]):
                g = p_ref[0].astype(F32)
                for s in range(1, N_DEV):
                    g = g + p_ref[s].astype(F32)
                if transposed:
                    g = g.T
                d, nm, nv = _adam_math(w_ref[...], g, m_ref[...], v_ref[...])
                g_ref[...] = g
                d_ref[...] = d
                nm_ref[...] = nm
                nv_ref[...] = nv

    return _call(
        body, name="adam_big", grid=(nl, ni),
        in_specs=[part_spec(layer) for layer in range(nl)] + [spec, spec, spec],
        out_specs=[spec] * 4,
        out_shape=[jax.ShapeDtypeStruct(w.shape, F32)] * 4,
        compiler_params=_params("arbitrary", "arbitrary"),
    )(*parts, w, m, v)


def _sum_slots(parts):
    def body(p_ref, o_ref):
        g = p_ref[0]
        for s in range(1, N_DEV):
            g = g + p_ref[s]
        o_ref[...] = g

    return _call(body, name="sum_slots", in_specs=[RESIDENT], out_specs=RESIDENT,
                 out_shape=jax.ShapeDtypeStruct(parts.shape[1:], F32))(parts)


def _adam_small(ws, gs, ms, vs):
    n = len(ws)

    def body(*refs):
        ins, outs = refs[:4 * n], refs[4 * n:]
        for k in range(n):
            d, nm, nv = _adam_math(ins[k][...], ins[n + k][...], ins[2 * n + k][...], ins[3 * n + k][...])
            outs[k][...] = d
            outs[n + k][...] = nm
            outs[2 * n + k][...] = nv

    out = _call(body, name="adam_small", in_specs=[RESIDENT] * (4 * n), out_specs=[RESIDENT] * (3 * n),
                out_shape=[jax.ShapeDtypeStruct(w.shape, F32) for w in ws] * 3)(*ws, *gs, *ms, *vs)
    return out[:n], out[n:2 * n], out[2 * n:]


def _pack(arrays):
    parts = []
    for a in arrays:
        flat = a.reshape(-1)
        n = flat.shape[0]
        padded = -(-n // 1024) * 1024
        parts.append(jnp.pad(flat, (0, padded - n)).reshape(padded // 128, 128))
    return jnp.concatenate(parts, axis=0)


def _unpack(packed, shapes):
    out, row = [], 0
    for shp in shapes:
        n = int(np.prod(shp))
        rows = -(-n // 1024) * 8
        out.append(packed[row:row + rows].reshape(-1)[:n].reshape(shp))
        row += rows
    return out


def _t5_bucket_table():
    kj = np.arange(BLOCK)[:, None]
    qi = np.arange(BLOCK)[None, :]
    n = (qi - kj) % BLOCK
    max_exact = N_BUCKETS // 2
    large = max_exact + (np.log(np.maximum(n, 1) / max_exact) / np.log(MAX_DIST / max_exact)
                         * (N_BUCKETS - max_exact)).astype(np.int32)
    large = np.minimum(large, N_BUCKETS - 1)
    return np.where(n < max_exact, n, large).astype(np.int32)


SMALL_NAMES = ("rel_bias", "norm_mix_g", "q_norm_g", "k_norm_g", "sinks", "conv_b", "conv_ln_g", "conv_ln_b",
               "attn_out_g", "conv_out_g", "norm_mlp_g")


def _local_step(x, target, small, conv_w, get_w, put_g, token):
    bucket = jnp.asarray(_t5_bucket_table())
    bias = _bias_table(small["rel_bias"], bucket)
    row = lambda a, l: a[l][None, :]
    cw_pad = jnp.pad(conv_w, ((0, 0), (0, HALO - CONV_K), (0, 0)))
    saved, weights = [], []
    for l in range(DEPTH):
        wt_in, w_out, tok = get_w(l, 0, x)
        zq, zkv, zu, zg = _fwd_in(x, row(small["norm_mix_g"], l) + tok + (token if l == 0 else 0.0), wt_in)
        a, mix_a = _attn_fwd(zq, zkv, bias, small["sinks"][l], row(small["q_norm_g"], l), row(small["k_norm_g"], l),
                             row(small["attn_out_g"], l))
        y, mix_c = _conv_fwd(zu, zg, cw_pad[l], row(small["conv_b"], l), row(small["conv_ln_g"], l),
                             row(small["conv_ln_b"], l), row(small["conv_out_g"], l))
        x1 = _fwd_out(x, mix_a, mix_c, w_out)
        wt_up, w_down, tok = get_w(l, 1, x1)
        weights.append((wt_in, w_out, wt_up, w_down))
        x2, up = _mlp_fwd(x1, row(small["norm_mlp_g"], l) + tok, wt_up, w_down)
        saved.append((x, zq, zkv, zu, zg, a, mix_a, y, mix_c, x1, up))
        x = x2
    loss_part, dx = _loss_head(x, target)

    gs = {n: [None] * DEPTH for n in SMALL_NAMES if n != "rel_bias"}
    g_conv_w, dbias = [None] * DEPTH, [None] * DEPTH
    token = 0.0
    for l in reversed(range(DEPTH)):
        x0, zq, zkv, zu, zg, a, mix_a, y, mix_c, x1, up = saved[l]
        wt_in, w_out, wt_up, w_down = weights[l]
        dx1, dup, h2, gs["norm_mlp_g"][l], dxb, dx1b = _mlp_bwd(
            dx, x1, up, row(small["norm_mlp_g"], l) + token, wt_up, w_down)
        g_up = _wgrad(dup, h2, "wgrad_up", chunked=True)
        g_down = _wgrad(up, dxb, "wgrad_down", chunked=True, square_relu=True)
        token = put_g(l, 1, (g_up, g_down))
        dma, dmc = _bwd_out(dx1b, w_out)
        g_out = jnp.concatenate([_wgrad(mix_a, dx1b, "wgrad_out_a"), _wgrad(mix_c, dx1b, "wgrad_out_c")], axis=0)
        du, dgt, pg = _conv_bwd(dmc, y, zu, zg, cw_pad[l], row(small["conv_ln_g"], l) + token,
                                row(small["conv_ln_b"], l), row(small["conv_out_g"], l))
        g_conv_w[l] = pg[:CONV_K]
        gs["conv_b"][l], gs["conv_ln_g"][l], gs["conv_ln_b"][l], gs["conv_out_g"][l] = pg[32], pg[33], pg[34], pg[35]
        dq, dkv, dbias[l], dog, dqg, dkg, dsk = _attn_bwd(
            dma, a, zq, zkv, bias, small["sinks"][l], row(small["q_norm_g"], l), row(small["k_norm_g"], l),
            row(small["attn_out_g"], l))
        gs["attn_out_g"][l], gs["q_norm_g"][l], gs["k_norm_g"][l], gs["sinks"][l] = dog, dqg, dkg, dsk[0, :N_HEADS]
        dx, h, gs["norm_mix_g"][l], dz = _bwd_in(dx1, x0, dq, dkv, du, dgt, row(small["norm_mix_g"], l), wt_in)
        g_in = _wgrad(dz, h, "wgrad_in", tr=256)
        token = put_g(l, 0, (g_in, g_out))
    small_grads = {n: jnp.stack([jnp.reshape(v, (-1,)) for v in vals]) for n, vals in gs.items()}
    small_grads["rel_bias"] = _bias_grad(jnp.stack(dbias), bucket)[:, :N_BUCKETS].T
    return loss_part, dx, small_grads, jnp.stack(g_conv_w)


def kernel(x, rel_bias, norm_mix_g, w_in, q_norm_g, k_norm_g, sinks, conv_w, conv_b, conv_ln_g, conv_ln_b, attn_out_g, conv_out_g, w_out, norm_mlp_g, w_mlp_up, w_mlp_down, loss_target, m_rel_bias, m_norm_mix_g, m_w_in, m_q_norm_g, m_k_norm_g, m_sinks, m_conv_w, m_conv_b, m_conv_ln_g, m_conv_ln_b, m_attn_out_g, m_conv_out_g, m_w_out, m_norm_mlp_g, m_w_mlp_up, m_w_mlp_down, v_rel_bias, v_norm_mix_g, v_w_in, v_q_norm_g, v_k_norm_g, v_sinks, v_conv_w, v_conv_b, v_conv_ln_g, v_conv_ln_b, v_attn_out_g, v_conv_out_g, v_w_out, v_norm_mlp_g, v_w_mlp_up, v_w_mlp_down):
    args = dict(locals())
    small = {n: args[n] for n in SMALL_NAMES}
    tr = lambda a: jnp.swapaxes(a, 1, 2)
    me = 4 * lax.axis_index("x") + 2 * lax.axis_index("y") + lax.axis_index("c")

    shards = (tr(w_in).astype(BF16), w_out.astype(BF16), tr(w_mlp_up).astype(BF16), w_mlp_down.astype(BF16))
    cw_sh = jnp.pad(tr(conv_w), ((0, 0), (0, 0), (0, HALO - CONV_K))).reshape(DEPTH * (CONV_W // N_DEV), HALO)
    gathers = {}

    def start_gather(l, half, dep):
        arrays = [s[l] for s in shards[2 * half:2 * half + 2]]
        if (l, half) == (0, 0):
            arrays = [cw_sh] + arrays
        if dep is not None:
            arrays = [a + dep[0, 0].astype(a.dtype) for a in arrays]
        gathers[l, half], tok = _exchange_start(arrays, f"gather_{l}{'ab'[half]}_start", scatter=False)
        return tok

    landed = {}

    def get_w(l, half, after):
        if (l, half) not in landed:
            _, lands, tok = _exchange_wait(gathers[l, half], after, f"gather_{l}{'ab'[half]}_wait", scatter=False)
            if l + 1 < DEPTH:
                tok = tok + start_gather(l + 1, half, tok)
            landed[l, half] = (*lands, tok[0, 0])
        return landed[l, half]

    token = start_gather(0, 0, None)
    token = token + start_gather(0, 1, token)
    cw_all, *first = get_w(0, 0, token)
    landed[0, 0] = tuple(first)
    conv_w_full = jnp.transpose(cw_all.reshape(N_DEV, DEPTH, CONV_W // N_DEV, HALO), (1, 3, 0, 2))
    conv_w_full = conv_w_full.reshape(DEPTH, HALO, CONV_W)[:, :CONV_K, :]

    scatters = {}

    last = {}

    def put_g(l, half, grads):
        scatters[l, half], last["token"] = _exchange_start(list(grads), f"scatter_{l}{'ab'[half]}_start", scatter=True)
        return last["token"][0, 0]

    loss_part, dx, small_grads, g_conv_w = _local_step(
        x[0], loss_target[0], small, conv_w_full, get_w, put_g, token[0, 0])

    big_w = (tr(w_in), w_out, w_mlp_up, w_mlp_down)
    big_m = (tr(m_w_in), m_w_out, m_w_mlp_up, m_w_mlp_down)
    big_v = (tr(v_w_in), v_w_out, v_w_mlp_up, v_w_mlp_down)
    big_out = [None] * 4

    def update_half(half, after):
        parts = [[None] * DEPTH, [None] * DEPTH]
        for l in reversed(range(DEPTH)):
            _, lands, _ = _exchange_wait(scatters[l, half], after, f"scatter_{l}{'ab'[half]}_wait", scatter=True)
            for k, land in enumerate(lands):
                parts[k][l] = land.reshape(N_DEV, land.shape[0] // N_DEV, land.shape[1])
        for k in range(2):
            n = 2 * half + k
            big_out[n] = _adam_big(parts[k], big_w[n], big_m[n], big_v[n], transposed=(n == 2))

    update_half(1, last["token"])

    order = [n for n in SMALL_NAMES]
    packed = _pack([small_grads[n] for n in order] + [g_conv_w])
    slots, = _all_gather((packed[None],))
    summed = _sum_slots(slots[0].reshape(N_DEV, packed.shape[0], 128))
    shapes = [small[n].shape for n in order] + [(DEPTH, CONV_K, CONV_W)]
    sg = _unpack(summed, shapes)
    g_small = dict(zip(order, sg[:-1]))
    g_small["conv_w"] = lax.dynamic_slice_in_dim(sg[-1], me * (CONV_W // N_DEV), CONV_W // N_DEV, axis=2)
    names = order + ["conv_w"]
    deltas, new_m, new_v = _adam_small([args[n] for n in names], [g_small[n] for n in names],
                                       [args["m_" + n] for n in names], [args["v_" + n] for n in names])
    res = {"grad": g_small, "delta": dict(zip(names, deltas)), "new_m": dict(zip(names, new_m)),
           "new_v": dict(zip(names, new_v))}
    update_half(0, big_out[3][0][0, :8, :8] + deltas[0][:8, :8])
    big_out[0] = [tr(o) for o in big_out[0]]
    for k, n in enumerate(("w_in", "w_out", "w_mlp_up", "w_mlp_down")):
        for kind, val in zip(("grad", "delta", "new_m", "new_v"), big_out[k]):
            res[kind][n] = val

    loss = lax.psum(loss_part[0, 0], ("x", "y", "c"))
    weights = ("rel_bias", "norm_mix_g", "w_in", "q_norm_g", "k_norm_g", "sinks", "conv_w", "conv_b", "conv_ln_g",
               "conv_ln_b", "attn_out_g", "conv_out_g", "w_out", "norm_mlp_g", "w_mlp_up", "w_mlp_down")
    return (loss, dx[None], *[res[kind][n] for kind in ("grad", "delta", "new_m", "new_v") for n in weights])
```

```python
import math

import numpy as np
import jax
import jax.numpy as jnp
from jax import lax
from jax.experimental import pallas as pl
from jax.experimental.pallas import tpu as pltpu

F32, BF16 = jnp.float32, jnp.bfloat16
D_MODEL = 1024
DEPTH = 4
HEAD_DIM = 64
N_HEADS = 8
N_KV = 2
GQA = N_HEADS // N_KV
ATTN_W = N_HEADS * HEAD_DIM
KV_W = N_KV * HEAD_DIM
CONV_W = D_MODEL - ATTN_W
IN_W = ATTN_W + 2 * KV_W + 2 * CONV_W
BLOCK = 128
CONV_K = 31
HALO = 32
N_BUCKETS = 32
MAX_DIST = 128
D_FF = 4 * D_MODEL
FF_CHUNK = 512
N_FF = D_FF // FF_CHUNK
EPS = 1e-6
NEG = -1e30
N_DEV = 8
ADAM_LR, ADAM_B1, ADAM_B2, ADAM_EPS, ADAM_WD, ADAM_STEP = 0.001, 0.9, 0.999, 1e-08, 0.01, 10
VMEM_LIMIT = 56 * 1024 * 1024
MESH = pl.DeviceIdType.MESH

RESIDENT = pl.BlockSpec(memory_space=pltpu.VMEM)
IN_SMEM = pl.BlockSpec(memory_space=pltpu.SMEM)
ANY = pl.BlockSpec(memory_space=pl.ANY)


def _call(body, **kw):
    return pl.pallas_call(body, **kw)


def _params(*sem):
    return pltpu.CompilerParams(dimension_semantics=sem, vmem_limit_bytes=VMEM_LIMIT)


def _dot(a, b):
    return lax.dot_general(a, b, (((1,), (0,)), ((), ())), preferred_element_type=F32)


def _dot_nt(a, b):
    return lax.dot_general(a, b, (((1,), (1,)), ((), ())), preferred_element_type=F32)


def _dot_tn(a, b):
    return lax.dot_general(a, b, (((0,), (0,)), ((), ())), preferred_element_type=F32)


def _sig(x):
    return 1.0 / (1.0 + jnp.exp(-x))


def _rms(x):
    r = lax.rsqrt(jnp.mean(x * x, axis=-1, keepdims=True) + EPS)
    return x * r, r


def _rms_bwd(dy_g, xh, r):
    return r * (dy_g - xh * jnp.mean(dy_g * xh, axis=-1, keepdims=True))


def _rows(tm, w):
    return pl.BlockSpec((tm, w), lambda i: (i, 0))


def _acc_rows(w, rows=1):
    return pl.BlockSpec((rows, w), lambda i: (0, 0))


def _colsum(x):
    return jnp.sum(x, axis=0, keepdims=True)


def _fwd_in(x, g, wt):
    T = x.shape[0]
    tm = min(T, 1024)

    def body(x_ref, g_ref, w_ref, q_ref, kv_ref, u_ref, gt_ref):
        xh, _ = _rms(x_ref[...])
        h = (xh * g_ref[...]).astype(BF16)
        z = _dot_nt(h, w_ref[...])
        q_ref[...] = z[:, :ATTN_W]
        kv_ref[...] = z[:, ATTN_W:ATTN_W + 2 * KV_W]
        u_ref[...] = z[:, ATTN_W + 2 * KV_W:ATTN_W + 2 * KV_W + CONV_W]
        gt_ref[...] = z[:, ATTN_W + 2 * KV_W + CONV_W:]

    widths = (ATTN_W, 2 * KV_W, CONV_W, CONV_W)
    return _call(
        body, name="fwd_in", grid=(T // tm,),
        in_specs=[_rows(tm, D_MODEL), RESIDENT, RESIDENT],
        out_specs=[_rows(tm, w) for w in widths],
        out_shape=[jax.ShapeDtypeStruct((T, w), F32) for w in widths],
        compiler_params=_params("parallel"),
    )(x, g, wt)


ATTN_SUB = 16


def _rms0(x):
    r = lax.rsqrt(jnp.mean(x * x, axis=0, keepdims=True) + EPS)
    return x * r, r


def _rms0_bwd(dy_g, xh, r):
    return r * (dy_g - xh * jnp.mean(dy_g * xh, axis=0, keepdims=True))


def _tri_t():
    kj = lax.broadcasted_iota(jnp.int32, (BLOCK, BLOCK), 0)
    qi = lax.broadcasted_iota(jnp.int32, (BLOCK, BLOCK), 1)
    return jnp.concatenate([kj <= qi] * GQA, axis=1)


def _head_lanes(g):
    lane = lax.broadcasted_iota(jnp.int32, (1, GQA * BLOCK), 1)
    return (lane >= g * BLOCK) & (lane < (g + 1) * BLOCK)


def _heads_side_by_side(xt, kh):
    return jnp.concatenate(
        [xt[(kh * GQA + g) * HEAD_DIM:(kh * GQA + g + 1) * HEAD_DIM, :] for g in range(GQA)], axis=1)


def _kv_block(kv, kh, kg, kg_t):
    kvt = kv.T
    khat, rk = _rms(kv[:, kh * HEAD_DIM:(kh + 1) * HEAD_DIM])
    khat_t, _ = _rms0(kvt[kh * HEAD_DIM:(kh + 1) * HEAD_DIM, :])
    return dict(khat=khat, rk=rk, kn=(khat * kg).astype(BF16), kn_t=(khat_t * kg_t).astype(BF16),
                v=kv[:, KV_W + kh * HEAD_DIM:KV_W + (kh + 1) * HEAD_DIM].astype(BF16),
                v_t=kvt[KV_W + kh * HEAD_DIM:KV_W + (kh + 1) * HEAD_DIM, :].astype(BF16))


def _attn_group(zq_t, kp, kc, kh, bias_ref, sinks_ref, qg_t, tri, has_prev):
    scale = 1.0 / math.sqrt(HEAD_DIM)
    qhat, rq = _rms0(_heads_side_by_side(zq_t, kh))
    qf = qhat * qg_t
    qn = qf.astype(BF16)
    qs = (qf * scale).astype(BF16)
    s = jnp.where(tri, _dot(kc["kn"], qs), _dot(kp["kn"], qs)) + bias_ref[kh]
    if has_prev is not None:
        s = jnp.where(tri | has_prev, s, NEG)
    sink = jnp.zeros((1, GQA * BLOCK), F32)
    for g in range(GQA):
        sink = jnp.where(_head_lanes(g), sinks_ref[kh * GQA + g], sink)
    m = jnp.maximum(jnp.max(s, axis=0, keepdims=True), sink)
    p = jnp.exp(s - m)
    es = jnp.exp(sink - m)
    inv = 1.0 / (jnp.sum(p, axis=0, keepdims=True) + es)
    return dict(qhat=qhat, rq=rq, qn=qn, pn=p * inv, psink=es * inv)


def _split(pb, tri):
    zero = jnp.zeros_like(pb)
    return jnp.where(tri, pb, zero), jnp.where(tri, zero, pb)


def _attn_fwd(zq, zkv, bias, sinks, qg, kg, og):
    T = zq.shape[0]
    rows = ATTN_SUB * BLOCK

    def body(q_ref, kvc_ref, kvp_ref, bias_ref, sinks_ref, qgt_ref, kg_ref, kgt_ref, og_ref, a_ref, mix_ref):
        n = pl.program_id(0)
        tri = _tri_t()
        qgt, kgv, kgt = qgt_ref[...], kg_ref[...], kgt_ref[...]
        kvs = [kvp_ref[...]] + [kvc_ref[i * BLOCK:(i + 1) * BLOCK, :] for i in range(ATTN_SUB)]
        keys = [[_kv_block(kv, kh, kgv, kgt) for kh in range(N_KV)] for kv in kvs]
        for i in range(ATTN_SUB):
            zq_t = q_ref[i * BLOCK:(i + 1) * BLOCK, :].T
            has_prev = (n > 0) if i == 0 else None
            outs = []
            for kh in range(N_KV):
                kp, kc = keys[i][kh], keys[i + 1][kh]
                c = _attn_group(zq_t, kp, kc, kh, bias_ref, sinks_ref, qgt, tri, has_prev)
                p_c, p_p = _split(c["pn"].astype(BF16), tri)
                o_t = _dot(kc["v_t"], p_c) + _dot(kp["v_t"], p_p)
                outs += [o_t[:, g * BLOCK:(g + 1) * BLOCK] for g in range(GQA)]
            a = jnp.concatenate(outs, axis=0).T
            a_ref[i * BLOCK:(i + 1) * BLOCK, :] = a
            ah, _ = _rms(a)
            mix_ref[i * BLOCK:(i + 1) * BLOCK, :] = (ah * og_ref[...]).astype(BF16)

    return _call(
        body, name="attn_fwd", grid=(T // rows,),
        in_specs=[_rows(rows, ATTN_W), _rows(rows, 2 * KV_W),
                  pl.BlockSpec((BLOCK, 2 * KV_W), lambda n: (jnp.maximum(n * ATTN_SUB - 1, 0), 0)),
                  RESIDENT, IN_SMEM, RESIDENT, RESIDENT, RESIDENT, RESIDENT],
        out_specs=[_rows(rows, ATTN_W), _rows(rows, ATTN_W)],
        out_shape=[jax.ShapeDtypeStruct((T, ATTN_W), F32), jax.ShapeDtypeStruct((T, ATTN_W), BF16)],
        compiler_params=_params("parallel"),
    )(zq, zkv, zkv, bias, sinks, qg.reshape(HEAD_DIM, 1), kg, kg.reshape(HEAD_DIM, 1), og)


def _conv_post(y, lg, lb, og):
    mu = jnp.mean(y, axis=-1, keepdims=True)
    yc = y - mu
    rstd = lax.rsqrt(jnp.mean(yc * yc, axis=-1, keepdims=True) + EPS)
    yn = yc * rstd
    ln = yn * lg + lb
    sg = _sig(ln)
    c = ln * sg
    ch, r = _rms(c)
    return yn, rstd, ln, sg, ch, r


CONV_CHUNK = 64


SLAB_ROWS = CONV_CHUNK + 8 * ((CONV_K - 1) // 8)


def _shifted_taps(buf, slab, r0, base):
    for b in range(8):
        taps = range(b, CONV_K, 8)
        span = CONV_CHUNK + 8 * (len(taps) - 1)
        slab[0:span, :] = buf[r0 + base + b:r0 + base + b + span, :]
        for a, j in enumerate(taps):
            yield j, slab[8 * a:8 * a + CONV_CHUNK, :]


def _fold8(x):
    return jnp.sum(x.reshape(x.shape[0] // 8, 8, x.shape[1]), axis=0)


def _conv_fwd(zu, zg, cw, cb, lg, lb, og):
    T = zu.shape[0]
    tt = min(T, 1024)
    halo_spec = pl.BlockSpec((HALO, CONV_W), lambda i: (jnp.maximum(i * (tt // HALO) - 1, 0), 0))

    def body(u_ref, g_ref, uh_ref, gh_ref, cw_ref, cb_ref, lg_ref, lb_ref, og_ref, y_ref, mix_ref, buf, slab):
        i = pl.program_id(0)
        hal = uh_ref[...] * _sig(gh_ref[...])
        buf[0:HALO, :] = jnp.where(i > 0, hal, 0.0)
        buf[HALO:HALO + tt, :] = u_ref[...] * _sig(g_ref[...])
        cbv, lgv, lbv, ogv = cb_ref[...], lg_ref[...], lb_ref[...], og_ref[...]
        for r0 in range(0, tt, CONV_CHUNK):
            rs = slice(r0, r0 + CONV_CHUNK)
            acc = jnp.zeros((CONV_CHUNK, CONV_W), F32)
            for j, win in _shifted_taps(buf, slab, r0, HALO - (CONV_K - 1)):
                acc = acc + cw_ref[j:j + 1, :] * win
            y = acc + cbv
            y_ref[rs, :] = y
            mix_ref[rs, :] = (_conv_post(y, lgv, lbv, ogv)[4] * ogv).astype(BF16)

    return _call(
        body, name="conv_fwd", grid=(T // tt,),
        in_specs=[_rows(tt, CONV_W), _rows(tt, CONV_W), halo_spec, halo_spec] + [RESIDENT] * 5,
        out_specs=[_rows(tt, CONV_W), _rows(tt, CONV_W)],
        out_shape=[jax.ShapeDtypeStruct((T, CONV_W), F32), jax.ShapeDtypeStruct((T, CONV_W), BF16)],
        scratch_shapes=[pltpu.VMEM((HALO + tt, CONV_W), F32), pltpu.VMEM((SLAB_ROWS, CONV_W), F32)],
        compiler_params=_params("parallel"),
    )(zu, zg, zu, zg, cw, cb, lg, lb, og)


def _chunked(tm):
    return pl.BlockSpec((N_FF, tm, FF_CHUNK), lambda i: (0, i, 0))


def _fwd_out(x, mix_a, mix_c, w_out):
    T = x.shape[0]
    tm = min(T, 1024)

    def body(x_ref, a_ref, c_ref, w_ref, o_ref):
        o_ref[...] = (x_ref[...] + _dot(a_ref[...], w_ref[0:ATTN_W, :])
                      + _dot(c_ref[...], w_ref[ATTN_W:, :]))

    return _call(
        body, name="fwd_out", grid=(T // tm,),
        in_specs=[_rows(tm, D_MODEL), _rows(tm, ATTN_W), _rows(tm, CONV_W), RESIDENT],
        out_specs=_rows(tm, D_MODEL),
        out_shape=jax.ShapeDtypeStruct((T, D_MODEL), F32),
        compiler_params=_params("parallel"),
    )(x, mix_a, mix_c, w_out)


def _mlp_fwd(x, g, wup_t, wdown):
    T = x.shape[0]
    tm = 512

    def body(x_ref, g_ref, wu_ref, wd_ref, o_ref, up_ref):
        xv = x_ref[...]
        xh, _ = _rms(xv)
        h = (xh * g_ref[...]).astype(BF16)
        acc = xv
        for c in range(N_FF):
            rows = slice(c * FF_CHUNK, (c + 1) * FF_CHUNK)
            up = _dot_nt(h, wu_ref[rows, :])
            up_ref[c] = up.astype(BF16)
            act = jnp.square(jnp.maximum(up, 0.0))
            acc = acc + _dot(act.astype(BF16), wd_ref[rows, :])
        o_ref[...] = acc

    return _call(
        body, name="mlp_fwd", grid=(T // tm,),
        in_specs=[_rows(tm, D_MODEL), RESIDENT, RESIDENT, RESIDENT],
        out_specs=[_rows(tm, D_MODEL), _chunked(tm)],
        out_shape=[jax.ShapeDtypeStruct((T, D_MODEL), F32), jax.ShapeDtypeStruct((N_FF, T, FF_CHUNK), BF16)],
        compiler_params=_params("parallel"),
    )(x, g, wup_t, wdown)


def _loss_head(y, target):
    T = y.shape[0]
    tm = 512

    def body(y_ref, t_ref, l_ref, d_ref):
        @pl.when(pl.program_id(0) == 0)
        def _():
            l_ref[...] = jnp.zeros_like(l_ref)
        e = y_ref[...] - t_ref[...]
        d_ref[...] = e / D_MODEL
        l_ref[...] += 0.5 * jnp.sum(jnp.mean(e * e, axis=-1, keepdims=True))

    return _call(
        body, name="loss_head", grid=(T // tm,),
        in_specs=[_rows(tm, D_MODEL), _rows(tm, D_MODEL)],
        out_specs=[_acc_rows(128, 8), _rows(tm, D_MODEL)],
        out_shape=[jax.ShapeDtypeStruct((8, 128), F32), jax.ShapeDtypeStruct((T, D_MODEL), F32)],
        compiler_params=_params("arbitrary"),
    )(y, target)


def _bias_table(rel_bias, bucket):
    def body(rb_ref, bk_ref, o_ref):
        bk = bk_ref[...]
        for h in range(N_HEADS):
            acc = jnp.zeros((BLOCK, BLOCK), F32)
            for b in range(N_BUCKETS):
                acc = jnp.where(bk == b, rb_ref[b, h], acc)
            o_ref[h // GQA, :, (h % GQA) * BLOCK:(h % GQA + 1) * BLOCK] = acc

    return _call(
        body, name="bias_table", in_specs=[IN_SMEM, RESIDENT], out_specs=RESIDENT,
        out_shape=jax.ShapeDtypeStruct((N_KV, BLOCK, GQA * BLOCK), F32),
    )(rel_bias, bucket)


def _mlp_bwd(dx2, x1, up, g, wup_t, wdown):
    T = x1.shape[0]
    tm = 512

    def body(d_ref, x_ref, up_ref, g_ref, wu_ref, wd_ref, dx_ref, dup_ref, h_ref, dg_ref, db_ref, d1b_ref):
        @pl.when(pl.program_id(0) == 0)
        def _():
            dg_ref[...] = jnp.zeros_like(dg_ref)
        d2 = d_ref[...]
        d2b = d2.astype(BF16)
        db_ref[...] = d2b
        xh, r = _rms(x_ref[...])
        gv = g_ref[...]
        h_ref[...] = (xh * gv).astype(BF16)
        dh = jnp.zeros((tm, D_MODEL), F32)
        for c in range(N_FF):
            rows = slice(c * FF_CHUNK, (c + 1) * FF_CHUNK)
            dact = _dot_nt(d2b, wd_ref[rows, :])
            dup = (dact * (2.0 * jnp.maximum(up_ref[c].astype(F32), 0.0))).astype(BF16)
            dup_ref[c] = dup
            dh = dh + _dot(dup, wu_ref[rows, :])
        dg_ref[...] += _colsum(dh * xh)
        d1 = d2 + _rms_bwd(dh * gv, xh, r)
        dx_ref[...] = d1
        d1b_ref[...] = d1.astype(BF16)

    return _call(
        body, name="mlp_bwd", grid=(T // tm,),
        in_specs=[_rows(tm, D_MODEL), _rows(tm, D_MODEL), _chunked(tm), RESIDENT, RESIDENT, RESIDENT],
        out_specs=[_rows(tm, D_MODEL), _chunked(tm), _rows(tm, D_MODEL), _acc_rows(D_MODEL), _rows(tm, D_MODEL),
                   _rows(tm, D_MODEL)],
        out_shape=[jax.ShapeDtypeStruct((T, D_MODEL), F32), jax.ShapeDtypeStruct((N_FF, T, FF_CHUNK), BF16),
                   jax.ShapeDtypeStruct((T, D_MODEL), BF16), jax.ShapeDtypeStruct((1, D_MODEL), F32),
                   jax.ShapeDtypeStruct((T, D_MODEL), BF16), jax.ShapeDtypeStruct((T, D_MODEL), BF16)],
        compiler_params=_params("arbitrary"),
    )(dx2, x1, up, g, wup_t, wdown)


def _bwd_out(dx1b, w_out):
    T = dx1b.shape[0]
    tm = min(T, 1024)

    def body(d_ref, w_ref, da_ref, dc_ref):
        dm = _dot_nt(d_ref[...], w_ref[...])
        da_ref[...] = dm[:, :ATTN_W]
        dc_ref[...] = dm[:, ATTN_W:]

    return _call(
        body, name="bwd_out", grid=(T // tm,),
        in_specs=[_rows(tm, D_MODEL), RESIDENT],
        out_specs=[_rows(tm, ATTN_W), _rows(tm, CONV_W)],
        out_shape=[jax.ShapeDtypeStruct((T, ATTN_W), F32), jax.ShapeDtypeStruct((T, CONV_W), F32)],
        compiler_params=_params("parallel"),
    )(dx1b, w_out)


def _wgrad_one_block(a, b, name):
    T, tr = a.shape
    tk = min(T, 4096)
    nk = T // tk

    def body(a_ref, b_ref, o_ref, acc):
        k = pl.program_id(0)

        @pl.when(k == 0)
        def _():
            acc[...] = jnp.zeros_like(acc)
        acc[...] += _dot_tn(a_ref[...], b_ref[...])

        @pl.when(k == nk - 1)
        def _():
            o_ref[...] = acc[...].astype(BF16)

    return _call(
        body, name=name, grid=(nk,),
        in_specs=[_rows(tk, tr), _rows(tk, D_MODEL)],
        out_specs=pl.BlockSpec((tr, D_MODEL), lambda k: (0, 0)),
        out_shape=jax.ShapeDtypeStruct((tr, D_MODEL), BF16),
        scratch_shapes=[pltpu.VMEM((tr, D_MODEL), F32)],
        compiler_params=_params("arbitrary"),
    )(a, b)


def _wgrad(a, b, name, chunked=False, square_relu=False, tr=FF_CHUNK):
    if chunked:
        nr, T, tr = a.shape
    else:
        T, R = a.shape
        tr = min(R, tr)
        nr = R // tr
    piece = min(T, 2048)
    if nr == 1:
        return _wgrad_one_block(a, b, name)

    def body(a_ref, b_ref, o_ref):
        if square_relu:
            tot = jnp.zeros((tr, D_MODEL), F32)
            for k0 in range(0, T, piece):
                av = jnp.square(jnp.maximum(a_ref[k0:k0 + piece, :].astype(F32), 0.0)).astype(BF16)
                tot = tot + _dot_tn(av, b_ref[k0:k0 + piece, :])
        else:
            tot = _dot_tn(a_ref[...], b_ref[...])
        o_ref[...] = tot.astype(BF16)

    a_spec = (pl.BlockSpec((None, T, tr), lambda r: (r, 0, 0)) if chunked else pl.BlockSpec((T, tr), lambda r: (0, r)))
    return _call(
        body, name=name, grid=(nr,),
        in_specs=[a_spec, RESIDENT],
        out_specs=pl.BlockSpec((tr, D_MODEL), lambda r: (r, 0)),
        out_shape=jax.ShapeDtypeStruct((nr * tr, D_MODEL), BF16),
        compiler_params=_params("parallel"),
    )(a, b)


def _conv_bwd(dmix, y, zu, zg, cw, lg, lb, og):
    T = y.shape[0]
    tt = min(T, 1024)
    nt = T // tt
    per = tt // HALO
    prev_spec = pl.BlockSpec((HALO, CONV_W), lambda i: (jnp.maximum(i * per - 1, 0), 0))
    next_spec = pl.BlockSpec((HALO, CONV_W), lambda i: (jnp.minimum((i + 1) * per, nt * per - 1), 0))

    def body(dm_ref, dmn_ref, y_ref, yn_ref, u_ref, g_ref, uh_ref, gh_ref, cw_ref, lg_ref, lb_ref, og_ref,
             du_ref, dg_ref, pg_ref, hbuf, dybuf, dwacc, pacc, slab):
        i = pl.program_id(0)

        @pl.when(i == 0)
        def _():
            dwacc[...] = jnp.zeros_like(dwacc)
            pacc[...] = jnp.zeros_like(pacc)
        lgv, lbv, ogv = lg_ref[...], lb_ref[...], og_ref[...]

        def chain(yv, dm):
            yn, rstd, ln, sg, ch, r = _conv_post(yv, lgv, lbv, ogv)
            dc = _rms_bwd(dm * ogv, ch, r)
            dln = dc * sg * (1.0 + ln * (1.0 - sg))
            dyn = dln * lgv
            dy = rstd * (dyn - jnp.mean(dyn, axis=-1, keepdims=True)
                         - yn * jnp.mean(dyn * yn, axis=-1, keepdims=True))
            return dy, dm * ch, dln * yn, dln

        hbuf[0:HALO, :] = jnp.where(i > 0, uh_ref[...] * _sig(gh_ref[...]), 0.0)
        for r0 in range(0, tt, CONV_CHUNK):
            rs = slice(r0, r0 + CONV_CHUNK)
            hbuf[HALO + r0:HALO + r0 + CONV_CHUNK, :] = u_ref[rs, :] * _sig(g_ref[rs, :])
            dy, p_og, p_lg, p_lb = chain(y_ref[rs, :], dm_ref[rs, :])
            dybuf[rs, :] = dy
            for k, part in enumerate((dy, p_lg, p_lb, p_og)):
                pacc[8 * k:8 * k + 8, :] += _fold8(part)
        dyh, _, _, _ = chain(yn_ref[...], dmn_ref[...])
        dybuf[tt:tt + HALO, :] = jnp.where(i < nt - 1, dyh, 0.0)
        for r0 in range(0, tt, CONV_CHUNK):
            rs = slice(r0, r0 + CONV_CHUNK)
            dy = dybuf[rs, :]
            dh = jnp.zeros((CONV_CHUNK, CONV_W), F32)
            for j, win in _shifted_taps(dybuf, slab, r0, 0):
                dh = dh + cw_ref[CONV_K - 1 - j:CONV_K - j, :] * win
            for j, win in _shifted_taps(hbuf, slab, r0, HALO - (CONV_K - 1)):
                dwacc[8 * j:8 * j + 8, :] += _fold8(dy * win)
            sgt = _sig(g_ref[rs, :])
            du_ref[rs, :] = (dh * sgt).astype(BF16)
            dg_ref[rs, :] = (dh * u_ref[rs, :] * sgt * (1.0 - sgt)).astype(BF16)

        @pl.when(i == nt - 1)
        def _():
            pg_ref[...] = jnp.zeros_like(pg_ref)
            for j in range(CONV_K):
                pg_ref[j:j + 1, :] = _colsum(dwacc[8 * j:8 * j + 8, :])
            for k in range(4):
                pg_ref[32 + k:33 + k, :] = _colsum(pacc[8 * k:8 * k + 8, :])

    return _call(
        body, name="conv_bwd", grid=(nt,),
        in_specs=[_rows(tt, CONV_W), next_spec, _rows(tt, CONV_W), next_spec, _rows(tt, CONV_W), _rows(tt, CONV_W),
                  prev_spec, prev_spec] + [RESIDENT] * 4,
        out_specs=[_rows(tt, CONV_W), _rows(tt, CONV_W), _acc_rows(CONV_W, 40)],
        out_shape=[jax.ShapeDtypeStruct((T, CONV_W), BF16), jax.ShapeDtypeStruct((T, CONV_W), BF16),
                   jax.ShapeDtypeStruct((40, CONV_W), F32)],
        scratch_shapes=[pltpu.VMEM((HALO + tt, CONV_W), F32), pltpu.VMEM((tt + HALO, CONV_W), F32),
                        pltpu.VMEM((8 * HALO, CONV_W), F32), pltpu.VMEM((32, CONV_W), F32),
                        pltpu.VMEM((SLAB_ROWS, CONV_W), F32)],
        compiler_params=_params("arbitrary"),
    )(dmix, dmix, y, y, zu, zg, zu, zg, cw, lg, lb, og)


def _attn_bwd(dmix, a, zq, zkv, bias, sinks, qg, kg, og):
    T = zq.shape[0]
    rows = ATTN_SUB * BLOCK
    ns = T // rows
    nb = T // BLOCK
    cur = lambda w: pl.BlockSpec((rows, w), lambda n: (jnp.minimum(n, ns - 1), 0))
    scale = 1.0 / math.sqrt(HEAD_DIM)
    done = rows - BLOCK

    def body(dm_ref, a_ref, q_ref, kvc_ref, kvp_ref, bias_ref, sinks_ref, qgt_ref, kg_ref, kgt_ref, og_ref,
             dq_ref, dkv_ref, db_ref, dog_ref, dqg_ref, dkg_ref, dsk_ref, carry):
        n = pl.program_id(0)

        @pl.when(n == 0)
        def _():
            for ref in (db_ref, dog_ref, dqg_ref, dkg_ref, dsk_ref, carry):
                ref[...] = jnp.zeros_like(ref)

        @pl.when(n < ns)
        def _():
            tri = _tri_t()
            ogv, qgt, kgv, kgt = og_ref[...], qgt_ref[...], kg_ref[...], kgt_ref[...]
            lane = lax.broadcasted_iota(jnp.int32, (1, 128), 1)
            kvs = [kvp_ref[...]] + [kvc_ref[i * BLOCK:(i + 1) * BLOCK, :] for i in range(ATTN_SUB)]
            keys = [[_kv_block(kv, kh, kgv, kgt) for kh in range(N_KV)] for kv in kvs]
            dkn = [[jnp.zeros((BLOCK, HEAD_DIM), F32)] * N_KV for _ in kvs]
            dv = [[jnp.zeros((BLOCK, HEAD_DIM), F32)] * N_KV for _ in kvs]
            dsk = jnp.zeros((1, 128), F32)
            for i in range(ATTN_SUB):
                blk = slice(i * BLOCK, (i + 1) * BLOCK)
                ah, ra = _rms(a_ref[blk, :])
                dm = dm_ref[blk, :]
                dog_ref[...] += _colsum(dm * ah)
                da_t = _rms_bwd(dm * ogv, ah, ra).T
                zq_t = q_ref[blk, :].T
                has_prev = (n > 0) if i == 0 else None
                dqs = []
                for kh in range(N_KV):
                    kp, kc = keys[i][kh], keys[i + 1][kh]
                    c = _attn_group(zq_t, kp, kc, kh, bias_ref, sinks_ref, qgt, tri, has_prev)
                    dob = _heads_side_by_side(da_t, kh).astype(BF16)
                    pn = c["pn"]
                    p_c, p_p = _split(pn.astype(BF16), tri)
                    dv[i + 1][kh] = dv[i + 1][kh] + _dot_nt(p_c, dob)
                    dv[i][kh] = dv[i][kh] + _dot_nt(p_p, dob)
                    dp = jnp.where(tri, _dot(kc["v"], dob), _dot(kp["v"], dob))
                    dl = jnp.sum(pn * dp, axis=0, keepdims=True)
                    ds = pn * (dp - dl)
                    dsr = -c["psink"] * dl
                    for g in range(GQA):
                        dsk = dsk + jnp.where(lane == kh * GQA + g, jnp.sum(jnp.where(_head_lanes(g), dsr, 0.0)), 0.0)
                    db_ref[kh] += ds
                    ds_c, ds_p = _split((ds * scale).astype(BF16), tri)
                    dqn = _dot(kc["kn_t"], ds_c) + _dot(kp["kn_t"], ds_p)
                    dkn[i + 1][kh] = dkn[i + 1][kh] + _dot_nt(ds_c, c["qn"])
                    dkn[i][kh] = dkn[i][kh] + _dot_nt(ds_p, c["qn"])
                    dqg_ref[...] += jnp.sum(dqn * c["qhat"], axis=1, keepdims=True)
                    dq_t = _rms0_bwd(dqn * qgt, c["qhat"], c["rq"])
                    dqs += [dq_t[:, g * BLOCK:(g + 1) * BLOCK] for g in range(GQA)]
                dq_ref[blk, :] = jnp.concatenate(dqs, axis=0).T.astype(BF16)
            dsk_ref[...] += dsk
            dkv = []
            for j in range(ATTN_SUB + 1):
                dk = []
                for kh in range(N_KV):
                    key = keys[j][kh]
                    dkg_ref[...] += _colsum(dkn[j][kh] * key["khat"])
                    dk.append(_rms_bwd(dkn[j][kh] * kgv, key["khat"], key["rk"]))
                dkv.append(jnp.concatenate(dk + dv[j], axis=-1))
            if done:
                dkv_ref[0:done, :] = carry[0:done, :].astype(BF16)
            dkv_ref[done:rows, :] = (carry[done:rows, :] + dkv[0]).astype(BF16)
            for j in range(1, ATTN_SUB + 1):
                carry[(j - 1) * BLOCK:j * BLOCK, :] = dkv[j]

        @pl.when(n == ns)
        def _():
            dkv_ref[...] = carry[...].astype(BF16)

    small = lambda w: pl.BlockSpec((1, w), lambda n: (0, 0))
    return _call(
        body, name="attn_bwd", grid=(ns + 1,),
        in_specs=[cur(ATTN_W), cur(ATTN_W), cur(ATTN_W), cur(2 * KV_W),
                  pl.BlockSpec((BLOCK, 2 * KV_W), lambda n: (jnp.clip(n * ATTN_SUB - 1, 0, nb - 1), 0)),
                  RESIDENT, IN_SMEM, RESIDENT, RESIDENT, RESIDENT, RESIDENT],
        out_specs=[cur(ATTN_W), pl.BlockSpec((rows, 2 * KV_W), lambda n: (jnp.maximum(n - 1, 0), 0)),
                   pl.BlockSpec((N_KV, BLOCK, GQA * BLOCK), lambda n: (0, 0, 0)),
                   small(ATTN_W), pl.BlockSpec((HEAD_DIM, 1), lambda n: (0, 0)), small(HEAD_DIM), small(128)],
        out_shape=[jax.ShapeDtypeStruct((T, ATTN_W), BF16), jax.ShapeDtypeStruct((T, 2 * KV_W), BF16),
                   jax.ShapeDtypeStruct((N_KV, BLOCK, GQA * BLOCK), F32),
                   jax.ShapeDtypeStruct((1, ATTN_W), F32), jax.ShapeDtypeStruct((HEAD_DIM, 1), F32),
                   jax.ShapeDtypeStruct((1, HEAD_DIM), F32), jax.ShapeDtypeStruct((1, 128), F32)],
        scratch_shapes=[pltpu.VMEM((rows, 2 * KV_W), F32)],
        compiler_params=_params("arbitrary"),
    )(dmix, a, zq, zkv, zkv, bias, sinks, qg.reshape(HEAD_DIM, 1), kg, kg.reshape(HEAD_DIM, 1), og)


def _bwd_in(dx1, x, dq, dkv, du, dgt, g, wt):
    T = x.shape[0]
    tm = 512

    def body(d1_ref, x_ref, dq_ref, dkv_ref, du_ref, dgt_ref, g_ref, w_ref, dx_ref, h_ref, dg_ref, dz_ref):
        @pl.when(pl.program_id(0) == 0)
        def _():
            dg_ref[...] = jnp.zeros_like(dg_ref)
        xh, r = _rms(x_ref[...])
        gv = g_ref[...]
        h_ref[...] = (xh * gv).astype(BF16)
        dz = jnp.concatenate([dq_ref[...], dkv_ref[...], du_ref[...], dgt_ref[...]], axis=-1)
        dz_ref[...] = dz
        dh = _dot(dz, w_ref[...])
        dg_ref[...] += _colsum(dh * xh)
        dx_ref[...] = d1_ref[...] + _rms_bwd(dh * gv, xh, r)

    return _call(
        body, name="bwd_in", grid=(T // tm,),
        in_specs=[_rows(tm, D_MODEL), _rows(tm, D_MODEL), _rows(tm, ATTN_W), _rows(tm, 2 * KV_W),
                  _rows(tm, CONV_W), _rows(tm, CONV_W), RESIDENT, RESIDENT],
        out_specs=[_rows(tm, D_MODEL), _rows(tm, D_MODEL), _acc_rows(D_MODEL), _rows(tm, IN_W)],
        out_shape=[jax.ShapeDtypeStruct((T, D_MODEL), F32), jax.ShapeDtypeStruct((T, D_MODEL), BF16),
                   jax.ShapeDtypeStruct((1, D_MODEL), F32), jax.ShapeDtypeStruct((T, IN_W), BF16)],
        compiler_params=_params("arbitrary"),
    )(dx1, x, dq, dkv, du, dgt, g, wt)


def _bias_grad(db, bucket):
    def body(db_ref, bk_ref, o_ref):
        bk = bk_ref[...]
        lane = lax.broadcasted_iota(jnp.int32, (1, 128), 1)
        for h in range(N_HEADS):
            cols = slice((h % GQA) * BLOCK, (h % GQA + 1) * BLOCK)
            tot = db_ref[0, h // GQA, :, cols]
            for l in range(1, DEPTH):
                tot = tot + db_ref[l, h // GQA, :, cols]
            out = jnp.zeros((1, 128), F32)
            for b in range(N_BUCKETS):
                out = jnp.where(lane == b, jnp.sum(jnp.where(bk == b, tot, 0.0)), out)
            o_ref[h:h + 1, :] = out

    return _call(
        body, name="bias_grad", in_specs=[RESIDENT, RESIDENT], out_specs=RESIDENT,
        out_shape=jax.ShapeDtypeStruct((N_HEADS, 128), F32),
        compiler_params=pltpu.CompilerParams(vmem_limit_bytes=VMEM_LIMIT),
    )(db, bucket)


def _place():
    return lax.axis_index("x"), lax.axis_index("y"), lax.axis_index("c")


def _all_gather(shards):
    na = len(shards)

    def body(*refs):
        ins, outs = refs[:na], refs[na:2 * na]
        send_sems, recv_sems, local_sems = refs[2 * na:]
        x, y, c = _place()
        me, sibling = (x, y, c), (x, y, 1 - c)
        chips = [(1 - x, y), (x, 1 - y), (1 - x, 1 - y)]

        def rows(a, p):
            r = ins[a].shape[1]
            return outs[a].at[:, pl.ds((4 * p[0] + 2 * p[1] + p[2]) * r, r), :]

        def copy(a, k, block, to, src=None):
            return pltpu.make_async_remote_copy(
                src_ref=rows(a, block) if src is None else src, dst_ref=rows(a, block),
                send_sem=send_sems.at[a, k], recv_sem=recv_sems.at[a, k], device_id=to, device_id_type=MESH)

        mine = [pltpu.make_async_copy(ins[a], rows(a, me), local_sems.at[a]) for a in range(na)]
        for cp in mine:
            cp.start()
        first = []
        for a in range(na):
            first.append(copy(a, 0, me, sibling, src=ins[a]))
            first += [copy(a, 1 + j, me, (*chip, c), src=ins[a]) for j, chip in enumerate(chips)]
        for cp in first:
            cp.start()
        passed = []
        for j, chip in enumerate(chips):
            for a in range(na):
                copy(a, 1 + j, (*chip, c), me).wait_recv()
                cp = copy(a, 4 + j, (*chip, c), sibling)
                cp.start()
                passed.append(cp)
        for a in range(na):
            copy(a, 0, sibling, me).wait_recv()
            for j, chip in enumerate(chips):
                copy(a, 4 + j, (*chip, 1 - c), me).wait_recv()
        for cp in first + passed:
            cp.wait_send()
        for cp in mine:
            cp.wait()

    return _call(
        body, name="all_gather",
        in_specs=[ANY] * na, out_specs=[ANY] * na,
        out_shape=[jax.ShapeDtypeStruct((s.shape[0], N_DEV * s.shape[1], s.shape[2]), s.dtype) for s in shards],
        scratch_shapes=[pltpu.SemaphoreType.DMA((na, 7)), pltpu.SemaphoreType.DMA((na, 7)),
                        pltpu.SemaphoreType.DMA((na,))],
    )(*shards)


IN_HBM = pl.BlockSpec(memory_space=pltpu.HBM)
IN_SEM = pl.BlockSpec(memory_space=pltpu.SEMAPHORE)
DATAFLOW = pltpu.SideEffectType.DATAFLOW_SIDE_EFFECTING


def _exchange_copies(srcs, lands, send_sems, recv_sems, scatter):
    x, y, c = _place()
    me = 4 * x + 2 * y + c
    copies = []
    for k in range(1, N_DEV):
        p = (x ^ (k >> 2), y ^ ((k >> 1) & 1), c ^ (k & 1))
        for a, (src, land) in enumerate(zip(srcs, lands)):
            r = land.shape[0] // N_DEV
            if scatter:
                src = src.at[pl.ds((4 * p[0] + 2 * p[1] + p[2]) * r, r), :]
            copies.append(pltpu.make_async_remote_copy(
                src_ref=src, dst_ref=land.at[pl.ds(me * r, r), :], send_sem=send_sems.at[a * (N_DEV - 1) + k - 1],
                recv_sem=recv_sems.at[a * (N_DEV - 1) + k - 1], device_id=p, device_id_type=MESH))
    return copies


def _own_copies(srcs, lands, send_sems, scatter):
    x, y, c = _place()
    me = 4 * x + 2 * y + c
    copies = []
    for a, (src, land) in enumerate(zip(srcs, lands)):
        r = land.shape[0] // N_DEV
        if scatter:
            src = src.at[pl.ds(me * r, r), :]
        copies.append(pltpu.make_async_copy(src, land.at[pl.ds(me * r, r), :],
                                            send_sems.at[len(srcs) * (N_DEV - 1) + a]))
    return copies


def _exchange_start(srcs, name, scatter):
    na = len(srcs)
    lands = [lax.empty((s.shape[0] * (1 if scatter else N_DEV), s.shape[1]), s.dtype) for s in srcs]

    def body(*refs):
        ins, lnd = refs[:na], refs[na:2 * na]
        send_sems, recv_sems = refs[2 * na], refs[2 * na + 1]
        token = refs[-1]
        for cp in _exchange_copies(ins, lnd, send_sems, recv_sems, scatter) + _own_copies(ins, lnd, send_sems, scatter):
            cp.start()
        token[...] = jnp.zeros_like(token)

    hbm = lambda a: pltpu.with_memory_space_constraint(a, pltpu.HBM)
    out = _call(
        body, name=name,
        out_shape=(pltpu.SemaphoreType.DMA((na * N_DEV,)), pltpu.SemaphoreType.DMA((na * (N_DEV - 1),)),
                   *[pltpu.HBM(a.shape, a.dtype) for a in (*srcs, *lands)], jax.ShapeDtypeStruct((8, 128), F32)),
        in_specs=[IN_HBM] * (2 * na),
        out_specs=(IN_SEM, IN_SEM, *[IN_HBM] * (2 * na), RESIDENT),
        input_output_aliases={i: 2 + i for i in range(2 * na)},
        compiler_params=pltpu.CompilerParams(has_side_effects=DATAFLOW),
    )(*[hbm(a) for a in (*srcs, *lands)])
    return (out[0], out[1], out[2:2 + na], out[2 + na:2 + 2 * na]), out[-1]


def _exchange_wait(state, after, name, scatter):
    send_sems, recv_sems, srcs, lands = state
    na = len(srcs)

    def body(*refs):
        ins, lnd = refs[:na], refs[na:2 * na]
        for cp in _exchange_copies(ins, lnd, refs[2 * na], refs[2 * na + 1], scatter):
            cp.wait_send()
            cp.wait_recv()
        for cp in _own_copies(ins, lnd, refs[2 * na], scatter):
            cp.wait()
        refs[-1][...] = jnp.zeros_like(refs[-1])

    out = _call(
        body, name=name,
        out_shape=(*[pltpu.HBM(a.shape, a.dtype) for a in (*srcs, *lands)], jax.ShapeDtypeStruct((8, 128), F32)),
        in_specs=[IN_HBM] * (2 * na) + [IN_SEM, IN_SEM, ANY],
        out_specs=(*[IN_HBM] * (2 * na), RESIDENT),
        input_output_aliases={i: i for i in range(2 * na)},
        compiler_params=pltpu.CompilerParams(has_side_effects=DATAFLOW),
    )(*srcs, *lands, send_sems, recv_sems, after)
    return out[:na], out[na:2 * na], out[-1]


def _adam_math(w, g, m, v):
    m = ADAM_B1 * m + (1.0 - ADAM_B1) * g
    v = ADAM_B2 * v + (1.0 - ADAM_B2) * jnp.square(g)
    m_hat = m / (1.0 - ADAM_B1 ** ADAM_STEP)
    v_hat = v / (1.0 - ADAM_B2 ** ADAM_STEP)
    delta = -ADAM_LR * (m_hat / (jnp.sqrt(v_hat) + ADAM_EPS) + ADAM_WD * w)
    return delta, m, v


def _adam_big(parts, w, m, v, transposed=False):
    if transposed:
        nl, cdim, r = w.shape
    else:
        nl, r, cdim = w.shape
    tr = r if r <= 256 else 256
    ni = r // tr
    spec = (pl.BlockSpec((None, cdim, tr), lambda l, i: (l, 0, i)) if transposed
            else pl.BlockSpec((None, tr, cdim), lambda l, i: (l, i, 0)))

    def part_spec(layer):
        return pl.BlockSpec((N_DEV, tr, cdim),
                            lambda l, i: (0, jnp.where(l == layer, i, jnp.where(l < layer, 0, ni - 1)), 0))

    def body(*refs):
        p_refs = refs[:nl]
        w_ref, m_ref, v_ref, g_ref, d_ref, nm_ref, nv_ref = refs[nl:]
        for layer in range(nl):
            @pl.when(pl.program_id(0) == layer)
            def _(p_ref=p_refs[layer]):
                g = p_ref[0].astype(F32)
                for s in range(1, N_DEV):
                    g = g + p_ref[s].astype(F32)
                if transposed:
                    g = g.T
                d, nm, nv = _adam_math(w_ref[...], g, m_ref[...], v_ref[...])
                g_ref[...] = g
                d_ref[...] = d
                nm_ref[...] = nm
                nv_ref[...] = nv

    return _call(
        body, name="adam_big", grid=(nl, ni),
        in_specs=[part_spec(layer) for layer in range(nl)] + [spec, spec, spec],
        out_specs=[spec] * 4,
        out_shape=[jax.ShapeDtypeStruct(w.shape, F32)] * 4,
        compiler_params=_params("arbitrary", "arbitrary"),
    )(*parts, w, m, v)


def _sum_slots(parts):
    def body(p_ref, o_ref):
        g = p_ref[0]
        for s in range(1, N_DEV):
            g = g + p_ref[s]
        o_ref[...] = g

    return _call(body, name="sum_slots", in_specs=[RESIDENT], out_specs=RESIDENT,
                 out_shape=jax.ShapeDtypeStruct(parts.shape[1:], F32))(parts)


def _adam_small(ws, gs, ms, vs):
    n = len(ws)

    def body(*refs):
        ins, outs = refs[:4 * n], refs[4 * n:]
        for k in range(n):
            d, nm, nv = _adam_math(ins[k][...], ins[n + k][...], ins[2 * n + k][...], ins[3 * n + k][...])
            outs[k][...] = d
            outs[n + k][...] = nm
            outs[2 * n + k][...] = nv

    out = _call(body, name="adam_small", in_specs=[RESIDENT] * (4 * n), out_specs=[RESIDENT] * (3 * n),
                out_shape=[jax.ShapeDtypeStruct(w.shape, F32) for w in ws] * 3)(*ws, *gs, *ms, *vs)
    return out[:n], out[n:2 * n], out[2 * n:]


def _pack(arrays):
    parts = []
    for a in arrays:
        flat = a.reshape(-1)
        n = flat.shape[0]
        padded = -(-n // 1024) * 1024
        parts.append(jnp.pad(flat, (0, padded - n)).reshape(padded // 128, 128))
    return jnp.concatenate(parts, axis=0)


def _unpack(packed, shapes):
    out, row = [], 0
    for shp in shapes:
        n = int(np.prod(shp))
        rows = -(-n // 1024) * 8
        out.append(packed[row:row + rows].reshape(-1)[:n].reshape(shp))
        row += rows
    return out


def _t5_bucket_table():
    kj = np.arange(BLOCK)[:, None]
    qi = np.arange(BLOCK)[None, :]
    n = (qi - kj) % BLOCK
    max_exact = N_BUCKETS // 2
    large = max_exact + (np.log(np.maximum(n, 1) / max_exact) / np.log(MAX_DIST / max_exact)
                         * (N_BUCKETS - max_exact)).astype(np.int32)
    large = np.minimum(large, N_BUCKETS - 1)
    return np.where(n < max_exact, n, large).astype(np.int32)


SMALL_NAMES = ("rel_bias", "norm_mix_g", "q_norm_g", "k_norm_g", "sinks", "conv_b", "conv_ln_g", "conv_ln_b",
               "attn_out_g", "conv_out_g", "norm_mlp_g")


def _local_step(x, target, small, conv_w, get_w, put_g, token):
    bucket = jnp.asarray(_t5_bucket_table())
    bias = _bias_table(small["rel_bias"], bucket)
    row = lambda a, l: a[l][None, :]
    cw_pad = jnp.pad(conv_w, ((0, 0), (0, HALO - CONV_K), (0, 0)))
    saved, weights = [], []
    for l in range(DEPTH):
        wt_in, w_out, tok = get_w(l, 0, x)
        zq, zkv, zu, zg = _fwd_in(x, row(small["norm_mix_g"], l) + tok + (token if l == 0 else 0.0), wt_in)
        a, mix_a = _attn_fwd(zq, zkv, bias, small["sinks"][l], row(small["q_norm_g"], l), row(small["k_norm_g"], l),
                             row(small["attn_out_g"], l))
        y, mix_c = _conv_fwd(zu, zg, cw_pad[l], row(small["conv_b"], l), row(small["conv_ln_g"], l),
                             row(small["conv_ln_b"], l), row(small["conv_out_g"], l))
        x1 = _fwd_out(x, mix_a, mix_c, w_out)
        wt_up, w_down, tok = get_w(l, 1, x1)
        weights.append((wt_in, w_out, wt_up, w_down))
        x2, up = _mlp_fwd(x1, row(small["norm_mlp_g"], l) + tok, wt_up, w_down)
        saved.append((x, zq, zkv, zu, zg, a, mix_a, y, mix_c, x1, up))
        x = x2
    loss_part, dx = _loss_head(x, target)

    gs = {n: [None] * DEPTH for n in SMALL_NAMES if n != "rel_bias"}
    g_conv_w, dbias = [None] * DEPTH, [None] * DEPTH
    token = 0.0
    for l in reversed(range(DEPTH)):
        x0, zq, zkv, zu, zg, a, mix_a, y, mix_c, x1, up = saved[l]
        wt_in, w_out, wt_up, w_down = weights[l]
        dx1, dup, h2, gs["norm_mlp_g"][l], dxb, dx1b = _mlp_bwd(
            dx, x1, up, row(small["norm_mlp_g"], l) + token, wt_up, w_down)
        g_up = _wgrad(dup, h2, "wgrad_up", chunked=True)
        g_down = _wgrad(up, dxb, "wgrad_down", chunked=True, square_relu=True)
        token = put_g(l, 1, (g_up, g_down))
        dma, dmc = _bwd_out(dx1b, w_out)
        g_out = jnp.concatenate([_wgrad(mix_a, dx1b, "wgrad_out_a"), _wgrad(mix_c, dx1b, "wgrad_out_c")], axis=0)
        du, dgt, pg = _conv_bwd(dmc, y, zu, zg, cw_pad[l], row(small["conv_ln_g"], l) + token,
                                row(small["conv_ln_b"], l), row(small["conv_out_g"], l))
        g_conv_w[l] = pg[:CONV_K]
        gs["conv_b"][l], gs["conv_ln_g"][l], gs["conv_ln_b"][l], gs["conv_out_g"][l] = pg[32], pg[33], pg[34], pg[35]
        dq, dkv, dbias[l], dog, dqg, dkg, dsk = _attn_bwd(
            dma, a, zq, zkv, bias, small["sinks"][l], row(small["q_norm_g"], l), row(small["k_norm_g"], l),
            row(small["attn_out_g"], l))
        gs["attn_out_g"][l], gs["q_norm_g"][l], gs["k_norm_g"][l], gs["sinks"][l] = dog, dqg, dkg, dsk[0, :N_HEADS]
        dx, h, gs["norm_mix_g"][l], dz = _bwd_in(dx1, x0, dq, dkv, du, dgt, row(small["norm_mix_g"], l), wt_in)
        g_in = _wgrad(dz, h, "wgrad_in", tr=256)
        token = put_g(l, 0, (g_in, g_out))
    small_grads = {n: jnp.stack([jnp.reshape(v, (-1,)) for v in vals]) for n, vals in gs.items()}
    small_grads["rel_bias"] = _bias_grad(jnp.stack(dbias), bucket)[:, :N_BUCKETS].T
    return loss_part, dx, small_grads, jnp.stack(g_conv_w)


def kernel(x, rel_bias, norm_mix_g, w_in, q_norm_g, k_norm_g, sinks, conv_w, conv_b, conv_ln_g, conv_ln_b, attn_out_g, conv_out_g, w_out, norm_mlp_g, w_mlp_up, w_mlp_down, loss_target, m_rel_bias, m_norm_mix_g, m_w_in, m_q_norm_g, m_k_norm_g, m_sinks, m_conv_w, m_conv_b, m_conv_ln_g, m_conv_ln_b, m_attn_out_g, m_conv_out_g, m_w_out, m_norm_mlp_g, m_w_mlp_up, m_w_mlp_down, v_rel_bias, v_norm_mix_g, v_w_in, v_q_norm_g, v_k_norm_g, v_sinks, v_conv_w, v_conv_b, v_conv_ln_g, v_conv_ln_b, v_attn_out_g, v_conv_out_g, v_w_out, v_norm_mlp_g, v_w_mlp_up, v_w_mlp_down):
    args = dict(locals())
    small = {n: args[n] for n in SMALL_NAMES}
    tr = lambda a: jnp.swapaxes(a, 1, 2)
    me = 4 * lax.axis_index("x") + 2 * lax.axis_index("y") + lax.axis_index("c")

    shards = (tr(w_in).astype(BF16), w_out.astype(BF16), tr(w_mlp_up).astype(BF16), w_mlp_down.astype(BF16))
    cw_sh = jnp.pad(tr(conv_w), ((0, 0), (0, 0), (0, HALO - CONV_K))).reshape(DEPTH * (CONV_W // N_DEV), HALO)
    gathers = {}

    def start_gather(l, half, dep):
        arrays = [s[l] for s in shards[2 * half:2 * half + 2]]
        if (l, half) == (0, 0):
            arrays = [cw_sh] + arrays
        if dep is not None:
            arrays = [a + dep[0, 0].astype(a.dtype) for a in arrays]
        gathers[l, half], tok = _exchange_start(arrays, f"gather_{l}{'ab'[half]}_start", scatter=False)
        return tok

    landed = {}

    def get_w(l, half, after):
        if (l, half) not in landed:
            _, lands, tok = _exchange_wait(gathers[l, half], after, f"gather_{l}{'ab'[half]}_wait", scatter=False)
            if l + 1 < DEPTH:
                tok = tok + start_gather(l + 1, half, tok)
            landed[l, half] = (*lands, tok[0, 0])
        return landed[l, half]

    token = start_gather(0, 0, None)
    token = token + start_gather(0, 1, token)
    cw_all, *first = get_w(0, 0, token)
    landed[0, 0] = tuple(first)
    conv_w_full = jnp.transpose(cw_all.reshape(N_DEV, DEPTH, CONV_W // N_DEV, HALO), (1, 3, 0, 2))
    conv_w_full = conv_w_full.reshape(DEPTH, HALO, CONV_W)[:, :CONV_K, :]

    scatters = {}

    last = {}

    def put_g(l, half, grads):
        scatters[l, half], last["token"] = _exchange_start(list(grads), f"scatter_{l}{'ab'[half]}_start", scatter=True)
        return last["token"][0, 0]

    loss_part, dx, small_grads, g_conv_w = _local_step(
        x[0], loss_target[0], small, conv_w_full, get_w, put_g, token[0, 0])

    big_w = (tr(w_in), w_out, w_mlp_up, w_mlp_down)
    big_m = (tr(m_w_in), m_w_out, m_w_mlp_up, m_w_mlp_down)
    big_v = (tr(v_w_in), v_w_out, v_w_mlp_up, v_w_mlp_down)
    big_out = [None] * 4

    def update_half(half, after):
        parts = [[None] * DEPTH, [None] * DEPTH]
        for l in reversed(range(DEPTH)):
            _, lands, _ = _exchange_wait(scatters[l, half], after, f"scatter_{l}{'ab'[half]}_wait", scatter=True)
            for k, land in enumerate(lands):
                parts[k][l] = land.reshape(N_DEV, land.shape[0] // N_DEV, land.shape[1])
        for k in range(2):
            n = 2 * half + k
            big_out[n] = _adam_big(parts[k], big_w[n], big_m[n], big_v[n], transposed=(n == 2))

    update_half(1, last["token"])

    order = [n for n in SMALL_NAMES]
    packed = _pack([small_grads[n] for n in order] + [g_conv_w])
    slots, = _all_gather((packed[None],))
    summed = _sum_slots(slots[0].reshape(N_DEV, packed.shape[0], 128))
    shapes = [small[n].shape for n in order] + [(DEPTH, CONV_K, CONV_W)]
    sg = _unpack(summed, shapes)
    g_small = dict(zip(order, sg[:-1]))
    g_small["conv_w"] = lax.dynamic_slice_in_dim(sg[-1], me * (CONV_W // N_DEV), CONV_W // N_DEV, axis=2)
    names = order + ["conv_w"]
    deltas, new_m, new_v = _adam_small([args[n] for n in names], [g_small[n] for n in names],
                                       [args["m_" + n] for n in names], [args["v_" + n] for n in names])
    res = {"grad": g_small, "delta": dict(zip(names, deltas)), "new_m": dict(zip(names, new_m)),
           "new_v": dict(zip(names, new_v))}
    update_half(0, big_out[3][0][0, :8, :8] + deltas[0][:8, :8])
    big_out[0] = [tr(o) for o in big_out[0]]
    for k, n in enumerate(("w_in", "w_out", "w_mlp_up", "w_mlp_down")):
        for kind, val in zip(("grad", "delta", "new_m", "new_v"), big_out[k]):
            res[kind][n] = val

    loss = lax.psum(loss_part[0, 0], ("x", "y", "c"))
    weights = ("rel_bias", "norm_mix_g", "w_in", "q_norm_g", "k_norm_g", "sinks", "conv_w", "conv_b", "conv_ln_g",
               "conv_ln_b", "attn_out_g", "conv_out_g", "w_out", "norm_mlp_g", "w_mlp_up", "w_mlp_down")
    return (loss, dx[None], *[res[kind][n] for kind in ("grad", "delta", "new_m", "new_v") for n in weights])
```
